```python
import jax
import jax.numpy as jnp
from jax import lax
import numpy as np

D_MODEL = 2048
BATCH = 2
SEQ = 16384
DEPTH = 1

CHUNK = 64
MIX_WIDTH = D_MODEL
RWKV_HEAD_DIM = 64
RWKV_WIDTH = MIX_WIDTH // 2
RWKV_HEADS = RWKV_WIDTH // RWKV_HEAD_DIM
SB_HEAD_DIM = 64
SB_WIDTH = MIX_WIDTH - RWKV_WIDTH
SB_HEADS = SB_WIDTH // SB_HEAD_DIM
DECAY_LORA = 64
ICLR_LORA = 64
GATE_LORA = 160
RWKV_COLS = 3 * RWKV_WIDTH + DECAY_LORA + ICLR_LORA + GATE_LORA
IN_COLS = RWKV_COLS + 3 * SB_WIDTH
SB_BLOCK = 128
N_GROUPS = 8
EXPERTS_PER_GROUP = 8
N_EXPERTS = N_GROUPS * EXPERTS_PER_GROUP
TOP_K_IN_GROUP = 2
D_EXPERT = D_MODEL // 2
MOE_BLOCK = 128
RMS_EPS = 1e-6
GN_EPS = 64e-5
L2_EPS = 1e-12

kernel_name = "hymba_rwkv7_stickbreak_hmoe_adaln"


def rms_norm(x, gain):
    xf = x.astype(jnp.float32)
    y = xf * lax.rsqrt(jnp.mean(xf * xf, axis=-1, keepdims=True) + RMS_EPS)
    return (y * gain.astype(jnp.float32)).astype(x.dtype)


def rwkv7_time_mix(p_rwkv, shift_mu, w0, w_decay_up, a0, w_iclr_up, w_gate_up,
                   k_k, k_a, r_k, ln_gain, ln_bias):
    out_dtype = p_rwkv.dtype
    B, S, _ = p_rwkv.shape
    H, N = RWKV_HEADS, RWKV_HEAD_DIM
    p = p_rwkv.astype(jnp.float32)
    prev = jnp.pad(p[:, :-1], ((0, 0), (1, 0), (0, 0)))
    p = p + (prev - p) * shift_mu.astype(jnp.float32)
    cuts = [RWKV_WIDTH, 2 * RWKV_WIDTH, 3 * RWKV_WIDTH,
            3 * RWKV_WIDTH + DECAY_LORA, 3 * RWKV_WIDTH + DECAY_LORA + ICLR_LORA]
    r, k, v, xw, xa, xg = jnp.split(p, cuts, axis=-1)
    f = lambda t: t.astype(jnp.float32)
    w = -jax.nn.softplus(-(f(w0) + jnp.tanh(xw) @ f(w_decay_up))) - 0.5
    decay = jnp.exp(-jnp.exp(w))
    a = jax.nn.sigmoid(f(a0) + xa @ f(w_iclr_up))
    g = jax.nn.sigmoid(xg) @ f(w_gate_up)
    heads = lambda t: t.reshape(B, S, H, N)
    kk = heads(k * f(k_k))
    kk = kk / jnp.maximum(jnp.linalg.norm(kk, axis=-1, keepdims=True), L2_EPS)
    k = k * (1.0 + (a - 1.0) * f(k_a))
    rh, kh, vh, wh, ah = heads(r), heads(k), heads(v), heads(decay), heads(a)

    def step(state, inp):
        r_t, w_t, k_t, v_t, ia_t, ib_t = inp
        sa = jnp.einsum('bhvk,bhk->bhv', state, ia_t)
        state = (state * w_t[:, :, None, :]
                 + sa[..., None] * ib_t[:, :, None, :]
                 + v_t[..., None] * k_t[:, :, None, :])
        return state, jnp.einsum('bhvk,bhk->bhv', state, r_t)

    tm = lambda t: jnp.moveaxis(t, 1, 0)
    state0 = jnp.zeros((B, H, N, N), jnp.float32)
    _, ys = lax.scan(step, state0, (tm(rh), tm(wh), tm(kh), tm(vh), tm(-kk), tm(kk * ah)))
    y = jnp.moveaxis(ys, 0, 1)
    mu = jnp.mean(y, axis=-1, keepdims=True)
    var = jnp.mean(jnp.square(y - mu), axis=-1, keepdims=True)
    y = ((y - mu) * lax.rsqrt(var + GN_EPS)).reshape(B, S, RWKV_WIDTH)
    y = y * f(ln_gain) + f(ln_bias)
    bonus = jnp.sum(rh * kh * f(r_k), axis=-1, keepdims=True) * vh
    y = (y + bonus.reshape(B, S, RWKV_WIDTH)) * g
    return y.astype(out_dtype)


def stick_breaking_attention(q, k, v, norm_gain):
    out_dtype = q.dtype
    B, S, _ = q.shape
    H, dh = SB_HEADS, SB_HEAD_DIM
    to_heads = lambda t: t.reshape(B, S, H, dh).transpose(0, 2, 1, 3).astype(jnp.float32)
    qh = to_heads(q) * (dh ** -0.5)
    kh, vh = to_heads(k), to_heads(v)
    n_blocks = S // SB_BLOCK
    offs = jnp.arange(SB_BLOCK)

    def query_block(qb):
        q0 = qb * SB_BLOCK
        q_blk = lax.dynamic_slice_in_dim(qh, q0, SB_BLOCK, axis=2)
        t_idx = q0 + offs

        def key_step(i, carry):
            acc, log_rest = carry
            k0 = (qb - i) * SB_BLOCK
            k_blk = lax.dynamic_slice_in_dim(kh, k0, SB_BLOCK, axis=2)
            v_blk = lax.dynamic_slice_in_dim(vh, k0, SB_BLOCK, axis=2)
            causal = (k0 + offs)[None, :] < t_idx[:, None]
            z = jnp.einsum('bhqd,bhkd->bhqk', q_blk, k_blk)
            log_beta = jax.nn.log_sigmoid(z)
            log_1m = jnp.where(causal, jax.nn.log_sigmoid(-z), 0.0)
            between = lax.cumsum(log_1m, axis=3, reverse=True) - log_1m
            weight = jnp.where(causal, jnp.exp(log_beta + between + log_rest[..., None]), 0.0)
            acc = acc + jnp.einsum('bhqk,bhkd->bhqd', weight, v_blk)
            return acc, log_rest + jnp.sum(log_1m, axis=-1)

        init = (jnp.zeros((B, H, SB_BLOCK, dh), jnp.float32),
                jnp.zeros((B, H, SB_BLOCK), jnp.float32))
        acc, _ = lax.fori_loop(0, qb + 1, key_step, init)
        return acc

    o = lax.map(query_block, jnp.arange(n_blocks))
    o = o.transpose(1, 0, 3, 2, 4).reshape(B, S, H, dh)
    o = o * lax.rsqrt(jnp.mean(o * o, axis=-1, keepdims=True) + RMS_EPS)
    o = o * norm_gain.astype(jnp.float32).reshape(H, dh)
    return o.reshape(B, S, SB_WIDTH).astype(out_dtype)


def hierarchical_moe(h, w_router_group, b_router_group, w_router_expert, b_router_expert,
                     w_exp_gate, w_exp_up, w_exp_down):
    T, D = h.shape
    gl = (h @ w_router_group + b_router_group).astype(jnp.float32)
    pg = jax.nn.softmax(gl, axis=-1)
    g_sel = jnp.argmax(gl, axis=-1)
    p_sel = jnp.take_along_axis(pg, g_sel[:, None], axis=1)[:, 0]
    el = (h @ w_router_expert + b_router_expert).astype(jnp.float32)
    el = el.reshape(T, N_GROUPS, EXPERTS_PER_GROUP)
    el_g = jnp.take_along_axis(el, g_sel[:, None, None], axis=1)[:, 0]
    top_v, top_i = lax.top_k(el_g, TOP_K_IN_GROUP)
    pair_w = jax.nn.softmax(top_v, axis=-1) * p_sel[:, None]
    e_id = (g_sel[:, None] * EXPERTS_PER_GROUP + top_i).astype(jnp.int32)
    flat_e = e_id.reshape(-1)
    flat_tok = jnp.repeat(jnp.arange(T, dtype=jnp.int32), TOP_K_IN_GROUP)
    flat_w = pair_w.reshape(-1)
    M = flat_e.shape[0]
    order = jnp.argsort(flat_e)
    se, stok, sw = flat_e[order], flat_tok[order], flat_w[order]
    counts = jax.ops.segment_sum(jnp.ones_like(flat_e), flat_e, num_segments=N_EXPERTS)
    starts = jnp.cumsum(counts) - counts
    padded = (counts + MOE_BLOCK - 1) // MOE_BLOCK * MOE_BLOCK
    pends = jnp.cumsum(padded)
    pstarts = pends - padded
    dest = pstarts[se] + jnp.arange(M, dtype=jnp.int32) - starts[se]
    n_blocks = (M + N_EXPERTS * (MOE_BLOCK - 1) + MOE_BLOCK - 1) // MOE_BLOCK
    n_rows = n_blocks * MOE_BLOCK
    row_tok = jnp.zeros((n_rows,), jnp.int32).at[dest].set(stok)
    row_w = jnp.zeros((n_rows,), h.dtype).at[dest].set(sw.astype(h.dtype))
    block_start = jnp.arange(n_blocks, dtype=jnp.int32) * MOE_BLOCK
    block_e = jnp.minimum(jnp.searchsorted(pends, block_start, side='right'), N_EXPERTS - 1)

    def run_block(args):
        toks, wts, e = args
        xb = h[toks]
        hid = jax.nn.silu(xb @ w_exp_gate[e]) * (xb @ w_exp_up[e])
        return (hid @ w_exp_down[e]) * wts[:, None]

    y_rows = lax.map(run_block, (row_tok.reshape(n_blocks, MOE_BLOCK),
                                 row_w.reshape(n_blocks, MOE_BLOCK), block_e))
    return jnp.zeros_like(h).at[row_tok].add(y_rows.reshape(n_rows, D))


def setup_inputs(seed: int = 0) -> dict:
    key = jax.random.key(seed)
    ks = jax.random.split(key, 32)
    L, D, f32 = DEPTH, D_MODEL, jnp.float32
    nrm = lambda k, shape, s: jax.random.normal(k, shape, f32) * s
    return {
        "x": nrm(ks[0], (BATCH, SEQ, D), 1.0),
        "c": nrm(ks[1], (BATCH, D), 1.0),
        "w_ada": nrm(ks[2], (L, D, 6 * D), D ** -0.5),
        "b_ada": nrm(ks[3], (L, 6 * D), 0.01),
        "norm1_gain": 1.0 + nrm(ks[4], (L, D), 0.1),
        "w_in": nrm(ks[5], (L, D, IN_COLS), D ** -0.5),
        "shift_mu": jax.random.uniform(ks[6], (L, RWKV_COLS), f32),
        "w0": jax.random.uniform(ks[7], (L, RWKV_WIDTH), f32, minval=-4.0, maxval=1.0),
        "w_decay_up": nrm(ks[8], (L, DECAY_LORA, RWKV_WIDTH), 0.5 * DECAY_LORA ** -0.5),
        "a0": nrm(ks[9], (L, RWKV_WIDTH), 0.5),
        "w_iclr_up": nrm(ks[10], (L, ICLR_LORA, RWKV_WIDTH), 0.5 * ICLR_LORA ** -0.5),
        "w_gate_up": nrm(ks[11], (L, GATE_LORA, RWKV_WIDTH), GATE_LORA ** -0.5),
        "k_k": 0.85 + nrm(ks[12], (L, RWKV_WIDTH), 0.05),
        "k_a": 1.0 + nrm(ks[13], (L, RWKV_WIDTH), 0.05),
        "r_k": nrm(ks[14], (L, RWKV_HEADS, RWKV_HEAD_DIM), 0.1),
        "ln_x_gain": 1.0 + nrm(ks[15], (L, RWKV_WIDTH), 0.1),
        "ln_x_bias": nrm(ks[16], (L, RWKV_WIDTH), 0.01),
        "sb_norm_gain": 1.0 + nrm(ks[17], (L, SB_WIDTH), 0.1),
        "w_out": nrm(ks[18], (L, MIX_WIDTH, D), MIX_WIDTH ** -0.5),
        "norm2_gain": 1.0 + nrm(ks[19], (L, D), 0.1),
        "w_router_group": nrm(ks[20], (L, D, N_GROUPS), D ** -0.5),
        "b_router_group": nrm(ks[21], (L, N_GROUPS), 0.01),
        "w_router_expert": nrm(ks[22], (L, D, N_EXPERTS), D ** -0.5),
        "b_router_expert": nrm(ks[23], (L, N_EXPERTS), 0.01),
        "w_exp_gate": nrm(ks[24], (L, N_EXPERTS, D, D_EXPERT), D ** -0.5),
        "w_exp_up": nrm(ks[25], (L, N_EXPERTS, D, D_EXPERT), D ** -0.5),
        "w_exp_down": nrm(ks[26], (L, N_EXPERTS, D_EXPERT, D), D_EXPERT ** -0.5),
        "final_norm_gain": 1.0 + nrm(ks[27], (D,), 0.1),
    }


def reference(x, c, w_ada, b_ada, norm1_gain, w_in, shift_mu, w0, w_decay_up, a0, w_iclr_up,
              w_gate_up, k_k, k_a, r_k, ln_x_gain, ln_x_bias, sb_norm_gain, w_out, norm2_gain,
              w_router_group, b_router_group, w_router_expert, b_router_expert,
              w_exp_gate, w_exp_up, w_exp_down, final_norm_gain):
    B, S, D = x.shape
    assert S % CHUNK == 0 and S % SB_BLOCK == 0
    sb_cuts = [RWKV_COLS, RWKV_COLS + SB_WIDTH, RWKV_COLS + 2 * SB_WIDTH]
    for l in range(DEPTH):
        mod = (jax.nn.silu(c) @ w_ada[l] + b_ada[l])[:, None, :]
        sh1, sc1, g1, sh2, sc2, g2 = jnp.split(mod, 6, axis=-1)
        h = rms_norm(x, norm1_gain[l]) * (1.0 + sc1) + sh1
        p = h @ w_in[l]
        p_rwkv, q_sb, k_sb, v_sb = jnp.split(p, sb_cuts, axis=-1)
        y_a = rwkv7_time_mix(p_rwkv, shift_mu[l], w0[l], w_decay_up[l], a0[l], w_iclr_up[l],
                             w_gate_up[l], k_k[l], k_a[l], r_k[l], ln_x_gain[l], ln_x_bias[l])
        y_b = stick_breaking_attention(q_sb, k_sb, v_sb, sb_norm_gain[l])
        x = x + g1 * (jnp.concatenate([y_a, y_b], axis=-1) @ w_out[l])
        h = rms_norm(x, norm2_gain[l]) * (1.0 + sc2) + sh2
        y_m = hierarchical_moe(h.reshape(B * S, D), w_router_group[l], b_router_group[l],
                               w_router_expert[l], b_router_expert[l],
                               w_exp_gate[l], w_exp_up[l], w_exp_down[l])
        x = x + g2 * y_m.reshape(B, S, D)
    return rms_norm(x, final_norm_gain)
```

```python
import functools

import jax
import jax.numpy as jnp
import numpy as np
from jax import lax
from jax.experimental import pallas as pl
from jax.experimental.pallas import tpu as pltpu

F32 = jnp.float32
BF16 = jnp.bfloat16

RMS_EPS = 1e-6
GN_EPS = 64e-5
L2_EPS = 1e-12

HEAD_DIM = 64
PAIR = 2 * HEAD_DIM
RWKV_CHUNK = 64
SB_BLOCK = 128
N_GROUPS = 8
EXPERTS_PER_GROUP = 8
N_EXPERTS = N_GROUPS * EXPERTS_PER_GROUP
TOP_K_IN_GROUP = 2
MOE_BLOCK = 128
DECAY_LORA = 64
ICLR_LORA = 64
GATE_LORA = 160
LANE = 128
VMEM_LIMIT = 48 * 1024 * 1024
SB_UNDERFLOW_LOG = -104.0


def _dot(a, b):
    return lax.dot_general(a, b, (((1,), (0,)), ((), ())), preferred_element_type=F32)


def _dot_nt(a, b):
    return lax.dot_general(a, b, (((1,), (1,)), ((), ())), preferred_element_type=F32)


def _split2(x):
    hi = x.astype(BF16)
    lo = (x - hi.astype(F32)).astype(BF16)
    return hi, lo


def _split3(x):
    a1 = x.astype(BF16)
    r1 = x - a1.astype(F32)
    a2 = r1.astype(BF16)
    a3 = (r1 - a2.astype(F32)).astype(BF16)
    return a1, a2, a3


def _mm(a, b, passes, nt=False):
    d = _dot_nt if nt else _dot
    if passes == 1:
        return d(a.astype(BF16), b.astype(BF16))
    ah, al = _split2(a)
    bh, bl = _split2(b)
    return d(ah, bh) + d(ah, bl) + d(al, bh)


def _mm_exact_rhs(a, b_exact):
    a1, a2, a3 = _split3(a)
    return _dot(a1, b_exact) + _dot(a2, b_exact) + _dot(a3, b_exact)


def _mm_exact_lhs(a_exact, b):
    b1, b2, b3 = _split3(b)
    return _dot(a_exact, b1) + _dot(a_exact, b2) + _dot(a_exact, b3)


def _sigmoid(x):
    return 1.0 / (1.0 + jnp.exp(-x))


def _softplus(x):
    return jnp.maximum(x, 0.0) + jnp.log(1.0 + jnp.exp(-jnp.abs(x)))


def _params(sem, vmem=VMEM_LIMIT):
    return pltpu.CompilerParams(dimension_semantics=sem, vmem_limit_bytes=vmem)


def _ada_kernel(c_ref, w_ref, b_ref, o_ref):
    c = c_ref[...]
    s = c * _sigmoid(c)
    o_ref[...] = _mm(s, w_ref[...], 3) + b_ref[...]


def _ada_mod(c, w, b):
    bsz, d = c.shape
    n = w.shape[1]
    rows = 8
    cp = jnp.zeros((rows, d), F32).at[:bsz].set(c)
    tn = 1024
    out = pl.pallas_call(
        _ada_kernel,
        out_shape=jax.ShapeDtypeStruct((rows, n), F32),
        grid=(n // tn,),
        in_specs=[pl.BlockSpec((rows, d), lambda j: (0, 0)),
                  pl.BlockSpec((d, tn), lambda j: (0, j)),
                  pl.BlockSpec((1, tn), lambda j: (0, j))],
        out_specs=pl.BlockSpec((rows, tn), lambda j: (0, j)),
        compiler_params=_params(("arbitrary",)),
        name="ada_mod",
    )(cp, w, b.reshape(1, n))
    return out[:bsz]


def _normmod_mm_kernel(x_ref, gain_ref, sc_ref, sh_ref, w_ref, o_ref, h_scr):
    @pl.when(pl.program_id(1) == 0)
    def _():
        xf = x_ref[...]
        ms = jnp.mean(xf * xf, axis=-1, keepdims=True)
        y = xf * lax.rsqrt(ms + RMS_EPS) * gain_ref[...]
        h_scr[...] = (y * (1.0 + sc_ref[0]) + sh_ref[0]).astype(BF16)

    o_ref[...] = _dot(h_scr[...], w_ref[...]).astype(o_ref.dtype)


def _normmod_matmul(x2, gain, sc, sh, w, out_dtype, seq, tm, tn):
    t, d = x2.shape
    n = w.shape[1]
    per_b = seq // tm
    return pl.pallas_call(
        _normmod_mm_kernel,
        out_shape=jax.ShapeDtypeStruct((t, n), out_dtype),
        grid=(t // tm, n // tn),
        in_specs=[pl.BlockSpec((tm, d), lambda i, j: (i, 0)),
                  pl.BlockSpec((1, d), lambda i, j: (0, 0)),
                  pl.BlockSpec((1, 1, d), lambda i, j: (i // per_b, 0, 0)),
                  pl.BlockSpec((1, 1, d), lambda i, j: (i // per_b, 0, 0)),
                  pl.BlockSpec((d, tn), lambda i, j: (0, j))],
        out_specs=pl.BlockSpec((tm, tn), lambda i, j: (i, j)),
        scratch_shapes=[pltpu.VMEM((tm, d), BF16)],
        compiler_params=_params(("parallel", "arbitrary")),
        name="normmod_proj",
    )(x2, gain.reshape(1, d), sc, sh, w)


_V_W0, _V_A0, _V_KK, _V_KA, _V_RK, _V_LNG, _V_LNB = range(7)
_M_STRICT, _M_INCL, _M_BD8, _M_OFF8, _M_OFF16, _M_OFF32, _M_EYE = range(7)

RWKV_PASSES = 3


def _rwkv_masks():
    i = np.arange(PAIR)[:, None]
    j = np.arange(PAIR)[None, :]
    strict = (j < i)
    incl = (j <= i)
    bd8 = (i // 8 == j // 8)
    def off(b):
        return (i // (2 * b) == j // (2 * b)) & (i // b > j // b)
    eye = (i == j)
    m = np.stack([strict, incl, bd8 & strict, off(8), off(16), off(32), eye]).astype(np.float32)
    return jnp.asarray(m)


def _tri_inverse(lmat, masks_ref, passes):
    eye = masks_ref[_M_EYE]
    dblk = lmat * masks_ref[_M_BD8]
    x = eye + dblk
    p = _mm(dblk, dblk, passes)
    x = x + _mm(p, x, passes)
    p = _mm(p, p, passes)
    x = x + _mm(p, x, passes)
    for plane in (_M_OFF8, _M_OFF16, _M_OFF32):
        off = lmat * masks_ref[plane]
        x = x + _mm(_mm(x, off, passes), x, passes)
    return x


def _rwkv_kernel(p_ref, mu_ref, vec_ref, wd_ref, wa_ref, wg_ref, e_ref, tri_ref, blk_ref,
                 masks_ref, o_ref,
                 rt_s, at_s, bt_s, kt_s, v_s, bd_s, kd_s, gc_s, y_s, g_s, bon_s,
                 state_s, carry_s, *, ts, n_pairs, passes):
    s_idx = pl.program_id(1)

    @pl.when(s_idx == 0)
    def _():
        state_s[...] = jnp.zeros_like(state_s)
        carry_s[...] = jnp.zeros_like(carry_s)

    width = n_pairs * PAIR
    pt = p_ref[0]
    row = lax.broadcasted_iota(jnp.int32, pt.shape, 0)
    prev = jnp.where(row == 0, carry_s[...], pltpu.roll(pt, 1, 0))
    carry_s[...] = pt[ts - 1:ts, :]
    pm = pt + (prev - pt) * mu_ref[...]

    r = pm[:, 0:width]
    k = pm[:, width:2 * width]
    v = pm[:, 2 * width:3 * width]
    o3 = 3 * width
    xw = pm[:, o3:o3 + LANE]
    xa = pm[:, o3 + LANE:o3 + 2 * LANE]
    xg = pm[:, o3 + 2 * LANE:o3 + 4 * LANE]

    vec = lambda i: vec_ref[i:i + 1, :]
    e_mat = e_ref[...]
    wlog = -_softplus(-(vec(_V_W0) + _mm(jnp.tanh(xw), wd_ref[...], 3))) - 0.5
    logw = -jnp.exp(wlog)
    a = _sigmoid(vec(_V_A0) + _mm(xa, wa_ref[...], 3))
    g_s[...] = _mm(_sigmoid(xg), wg_ref[...], 3)
    kk = k * vec(_V_KK)
    ss = _mm_exact_rhs(kk * kk, e_mat)
    kk = kk / jnp.maximum(jnp.sqrt(ss), L2_EPS)
    kp = k * (1.0 + (a - 1.0) * vec(_V_KA))
    ib = kk * a
    bon_s[...] = _mm_exact_rhs(r * kp * vec(_V_RK), e_mat) * v
    cum = _mm_exact_lhs(tri_ref[...], logw)
    tot = _mm_exact_lhs(blk_ref[...], logw)
    rt_s[...] = r * jnp.exp(cum)
    at_s[...] = -kk * jnp.exp(cum - logw)
    dec_in = jnp.exp(-cum)
    bt_s[...] = ib * dec_in
    kt_s[...] = kp * dec_in
    v_s[...] = v
    dec_out = jnp.exp(tot - cum)
    bd_s[...] = ib * dec_out
    kd_s[...] = kp * dec_out
    gc_s[...] = jnp.exp(tot)

    lane = lax.broadcasted_iota(jnp.int32, (RWKV_CHUNK, PAIR), 1)
    m0 = (lane < HEAD_DIM).astype(F32)
    m1 = 1.0 - m0
    strict = masks_ref[_M_STRICT]
    incl = masks_ref[_M_INCL]

    def chunk_body(c, carry):
        rows = pl.ds(pl.multiple_of(c * RWKV_CHUNK, RWKV_CHUNK), RWKV_CHUNK)
        for j in range(n_pairs):
            lanes = slice(j * PAIR, (j + 1) * PAIR)

            def stacked(ref):
                xblk = ref[rows, lanes]
                return jnp.concatenate([xblk * m0, xblk * m1], axis=0)

            rh, ah, bh, kh = stacked(rt_s), stacked(at_s), stacked(bt_s), stacked(kt_s)
            vh, bdh, kdh = stacked(v_s), stacked(bd_s), stacked(kd_s)
            gc = gc_s[pl.ds(pl.multiple_of(c * RWKV_CHUNK, RWKV_CHUNK), 1), lanes]
            st = state_s[j]

            bk = jnp.concatenate([bh, kh], axis=0)
            ga = _mm(ah, bk, passes, nt=True)
            gr = _mm(rh, bk, passes, nt=True)
            a_ab = ga[:, :PAIR] * strict
            a_ak = ga[:, PAIR:] * strict
            a_rb = gr[:, :PAIR] * incl
            a_rk = gr[:, PAIR:] * incl
            tinv = _tri_inverse(a_ab, masks_ref, passes)
            x0 = _mm(ah, st, passes, nt=True) + _mm(a_ak, vh, passes)
            u = _mm(tinv, x0, passes)
            y = _mm(rh, st, passes, nt=True) + _mm(a_rb, u, passes) + _mm(a_rk, vh, passes)
            zt = jnp.concatenate([u, vh], axis=0).T
            bkd = jnp.concatenate([bdh, kdh], axis=0)
            state_s[j] = st * gc + _mm(zt, bkd, passes)
            y_s[rows, lanes] = y[:RWKV_CHUNK] + y[RWKV_CHUNK:]
        return carry

    lax.fori_loop(0, ts // RWKV_CHUNK, chunk_body, 0)

    y = y_s[...]
    inv_n = 1.0 / HEAD_DIM
    mean = _mm_exact_rhs(y, e_mat) * inv_n
    dlt = y - mean
    var = _mm_exact_rhs(dlt * dlt, e_mat) * inv_n
    yn = dlt * lax.rsqrt(var + GN_EPS) * vec(_V_LNG) + vec(_V_LNB)
    o_ref[0] = ((yn + bon_s[...]) * g_s[...]).astype(o_ref.dtype)


def _rwkv_time_mix(p, mu, vecs, wd, wa, wg, *, ts, passes=RWKV_PASSES):
    bsz, seq, cols = p.shape
    width = vecs.shape[1]
    n_pairs = width // PAIR
    heads = np.arange(width) // HEAD_DIM
    e_mat = jnp.asarray((heads[:, None] == heads[None, :]).astype(np.float32), dtype=BF16)
    tok = np.arange(ts)
    same = (tok[:, None] // RWKV_CHUNK) == (tok[None, :] // RWKV_CHUNK)
    tri = jnp.asarray((same & (tok[None, :] <= tok[:, None])).astype(np.float32), dtype=BF16)
    blk = jnp.asarray(same.astype(np.float32), dtype=BF16)
    masks = _rwkv_masks()
    full = lambda shape: pl.BlockSpec(shape, lambda b, s: (0,) * len(shape))
    big = lambda: pltpu.VMEM((ts, width), F32)
    kern = functools.partial(_rwkv_kernel, ts=ts, n_pairs=n_pairs, passes=passes)
    return pl.pallas_call(
        kern,
        out_shape=jax.ShapeDtypeStruct((bsz, seq, width), BF16),
        grid=(bsz, seq // ts),
        in_specs=[pl.BlockSpec((1, ts, cols), lambda b, s: (b, s, 0)),
                  full((1, cols)), full(vecs.shape), full(wd.shape), full(wa.shape),
                  full(wg.shape), full(e_mat.shape), full(tri.shape), full(blk.shape),
                  full(masks.shape)],
        out_specs=pl.BlockSpec((1, ts, width), lambda b, s: (b, s, 0)),
        scratch_shapes=[big() for _ in range(11)]
        + [pltpu.VMEM((n_pairs, PAIR, PAIR), F32), pltpu.VMEM((1, cols), F32)],
        compiler_params=_params(("parallel", "arbitrary")),
        name="rwkv7_scan",
    )(p, mu, vecs, wd, wa, wg, e_mat, tri, blk, masks)


def _sb_kernel(q_ref, k_ref, v_ref, gain_ref, cum_ref, e2_ref, o_ref):
    qb = pl.program_id(2)
    q = q_ref[0] * jnp.asarray(HEAD_DIM ** -0.5, BF16)
    lane = lax.broadcasted_iota(jnp.int32, (SB_BLOCK, PAIR), 1)
    h0 = lane < HEAD_DIM
    zero = jnp.zeros((), BF16)
    qi = lax.broadcasted_iota(jnp.int32, (SB_BLOCK, 2 * SB_BLOCK), 0)
    ki = lax.broadcasted_iota(jnp.int32, (SB_BLOCK, 2 * SB_BLOCK), 1) % SB_BLOCK
    causal = ki < qi
    cum_mat = cum_ref[...]

    def block(kb, acc, rest, diag):
        rows = pl.ds(pl.multiple_of(kb * SB_BLOCK, SB_BLOCK), SB_BLOCK)
        kblk = k_ref[0, rows, :]
        vblk = v_ref[0, rows, :]
        khat = jnp.concatenate([jnp.where(h0, kblk, zero), jnp.where(h0, zero, kblk)], axis=0)
        vhat = jnp.concatenate([jnp.where(h0, vblk, zero), jnp.where(h0, zero, vblk)], axis=0)
        z = _dot_nt(q, khat)
        log_beta = jnp.minimum(z, 0.0) - jnp.log(1.0 + jnp.exp(-jnp.abs(z)))
        log_1m = log_beta - z
        if diag:
            log_1m = jnp.where(causal, log_1m, 0.0)
        hi, lo = _split2(log_1m)
        parts = []
        for h in range(2):
            sl = slice(h * SB_BLOCK, (h + 1) * SB_BLOCK)
            parts.append(_dot(jnp.concatenate([hi[:, sl], lo[:, sl]], axis=1), cum_mat))
        between = jnp.concatenate([parts[0][:, :SB_BLOCK], parts[1][:, :SB_BLOCK]], axis=1)
        total = jnp.concatenate([parts[0][:, SB_BLOCK:], parts[1][:, SB_BLOCK:]], axis=1)
        w = jnp.exp(log_beta + between + rest)
        if diag:
            w = jnp.where(causal, w, 0.0)
        acc = acc + _dot(w.astype(BF16), vhat)
        return acc, rest + total

    acc0 = jnp.zeros((SB_BLOCK, PAIR), F32)
    rest0 = jnp.zeros((SB_BLOCK, 2 * SB_BLOCK), F32)
    acc, rest = block(qb, acc0, rest0, True)

    def cond(carry):
        i, _, _, live = carry
        return jnp.logical_and(i <= qb, live > SB_UNDERFLOW_LOG)

    def body(carry):
        i, acc, rest, _ = carry
        acc, rest = block(qb - i, acc, rest, False)
        return i + 1, acc, rest, jnp.max(rest)

    _, acc, _, _ = lax.while_loop(cond, body, (jnp.int32(1), acc, rest, jnp.max(rest)))

    ms = _mm_exact_rhs(acc * acc, e2_ref[...]) * (1.0 / HEAD_DIM)
    o_ref[0] = (acc * lax.rsqrt(ms + RMS_EPS) * gain_ref[...]).astype(o_ref.dtype)


def _sb_attention(qkv, gain, width):
    bsz, seq, _ = qkv.shape
    n_pairs = width // PAIR
    nq = seq // SB_BLOCK
    i = np.arange(SB_BLOCK)
    upper = (i[:, None] > i[None, :]).astype(np.float32)
    half = np.concatenate([upper, np.ones_like(upper)], axis=1)
    cum_mat = jnp.asarray(np.concatenate([half, half], axis=0), dtype=BF16)
    hd = np.arange(PAIR) // HEAD_DIM
    e2 = jnp.asarray((hd[:, None] == hd[None, :]).astype(np.float32), dtype=BF16)
    return pl.pallas_call(
        _sb_kernel,
        out_shape=jax.ShapeDtypeStruct((bsz, seq, width), BF16),
        grid=(bsz, n_pairs, nq),
        in_specs=[pl.BlockSpec((1, SB_BLOCK, PAIR), lambda b, j, t: (b, t, j)),
                  pl.BlockSpec((1, seq, PAIR), lambda b, j, t: (b, 0, n_pairs + j)),
                  pl.BlockSpec((1, seq, PAIR), lambda b, j, t: (b, 0, 2 * n_pairs + j)),
                  pl.BlockSpec((1, PAIR), lambda b, j, t: (0, j)),
                  pl.BlockSpec(cum_mat.shape, lambda b, j, t: (0, 0)),
                  pl.BlockSpec(e2.shape, lambda b, j, t: (0, 0))],
        out_specs=pl.BlockSpec((1, SB_BLOCK, PAIR), lambda b, j, t: (b, t, j)),
        compiler_params=_params(("parallel", "parallel", "arbitrary")),
        name="stickbreak_attn",
    )(qkv, qkv, qkv, gain.reshape(1, width), cum_mat, e2)


def _outproj_kernel(x_ref, ya_ref, yb_ref, wa_ref, wb_ref, g1_ref, gain_ref, sc_ref, sh_ref,
                    wr_ref, br_ref, x1_ref, h_ref, lg_ref):
    mix = _dot(ya_ref[...], wa_ref[...]) + _dot(yb_ref[...], wb_ref[...])
    x1 = x_ref[...] + g1_ref[0] * mix
    x1_ref[...] = x1
    ms = jnp.mean(x1 * x1, axis=-1, keepdims=True)
    h = x1 * lax.rsqrt(ms + RMS_EPS) * gain_ref[...] * (1.0 + sc_ref[0]) + sh_ref[0]
    h_ref[...] = h.astype(h_ref.dtype)
    lg_ref[...] = _mm(h, wr_ref[...], 3) + br_ref[...]


def _outproj_norm_router(x2, ya, yb, w_a, w_b, g1, gain, sc, sh, w_r, b_r, seq, tm):
    t, d = x2.shape
    half = ya.shape[1]
    nr = w_r.shape[1]
    per_b = seq // tm
    row = lambda w: pl.BlockSpec((tm, w), lambda i: (i, 0))
    full = lambda shape: pl.BlockSpec(shape, lambda i: (0,) * len(shape))
    per_batch = pl.BlockSpec((1, 1, d), lambda i: (i // per_b, 0, 0))
    return pl.pallas_call(
        _outproj_kernel,
        out_shape=(jax.ShapeDtypeStruct((t, d), F32), jax.ShapeDtypeStruct((t, d), BF16),
                   jax.ShapeDtypeStruct((t, nr), F32)),
        grid=(t // tm,),
        in_specs=[row(d), row(half), row(half), full((half, d)), full((half, d)), per_batch,
                  full((1, d)), per_batch, per_batch, full((d, nr)), full((1, nr))],
        out_specs=(row(d), row(d), row(nr)),
        compiler_params=_params(("parallel",)),
        name="outproj_norm_router",
    )(x2, ya, yb, w_a, w_b, g1, gain.reshape(1, d), sc, sh, w_r, b_r)


def _moe_kernel(be_ref, nused_ref, x_ref, wt_ref, wg_ref, wu_ref, wd_ref, o_ref):
    i = pl.program_id(0)

    @pl.when(i < nused_ref[0])
    def _():
        xb = x_ref[...]
        hg = _dot(xb, wg_ref[0])
        hu = _dot(xb, wu_ref[0])
        hid = hg * _sigmoid(hg) * hu
        o_ref[...] = _dot(hid.astype(BF16), wd_ref[0]) * wt_ref[...]

    @pl.when(i >= nused_ref[0])
    def _():
        o_ref[...] = jnp.zeros_like(o_ref)


def _moe_experts(xg, row_w, block_e, n_used, w_gate, w_up, w_down):
    n_rows, d = xg.shape
    de = w_gate.shape[2]
    n_blocks = n_rows // MOE_BLOCK
    grid_spec = pltpu.PrefetchScalarGridSpec(
        num_scalar_prefetch=2,
        grid=(n_blocks,),
        in_specs=[pl.BlockSpec((MOE_BLOCK, d), lambda i, be, nu: (i, 0)),
                  pl.BlockSpec((MOE_BLOCK, 1), lambda i, be, nu: (i, 0)),
                  pl.BlockSpec((1, d, de), lambda i, be, nu: (be[i], 0, 0)),
                  pl.BlockSpec((1, d, de), lambda i, be, nu: (be[i], 0, 0)),
                  pl.BlockSpec((1, de, d), lambda i, be, nu: (be[i], 0, 0))],
        out_specs=pl.BlockSpec((MOE_BLOCK, d), lambda i, be, nu: (i, 0)),
    )
    return pl.pallas_call(
        _moe_kernel,
        out_shape=jax.ShapeDtypeStruct((n_rows, d), F32),
        grid_spec=grid_spec,
        compiler_params=_params(("arbitrary",)),
        name="moe_experts",
    )(block_e, n_used, xg, row_w.reshape(n_rows, 1), w_gate, w_up, w_down)


def _final_kernel(x1_ref, ym_ref, g2_ref, gain_ref, o_ref):
    x2 = x1_ref[...] + g2_ref[0] * ym_ref[...]
    ms = jnp.mean(x2 * x2, axis=-1, keepdims=True)
    o_ref[...] = x2 * lax.rsqrt(ms + RMS_EPS) * gain_ref[...]


def _final_norm(x1, ym, g2, gain, seq, tm):
    t, d = x1.shape
    per_b = seq // tm
    row = pl.BlockSpec((tm, d), lambda i: (i, 0))
    return pl.pallas_call(
        _final_kernel,
        out_shape=jax.ShapeDtypeStruct((t, d), F32),
        grid=(t // tm,),
        in_specs=[row, row, pl.BlockSpec((1, 1, d), lambda i: (i // per_b, 0, 0)),
                  pl.BlockSpec((1, d), lambda i: (0, 0))],
        out_specs=row,
        compiler_params=_params(("parallel",)),
        name="final_norm",
    )(x1, ym, g2, gain.reshape(1, d))


def _route(logits):
    t = logits.shape[0]
    gl = logits[:, :N_GROUPS]
    el = logits[:, N_GROUPS:N_GROUPS + N_EXPERTS].reshape(t, N_GROUPS, EXPERTS_PER_GROUP)
    pg = jax.nn.softmax(gl, axis=-1)
    g_sel = jnp.argmax(gl, axis=-1)
    p_sel = jnp.take_along_axis(pg, g_sel[:, None], axis=1)[:, 0]
    el_g = jnp.take_along_axis(el, g_sel[:, None, None], axis=1)[:, 0]
    top_v, top_i = lax.top_k(el_g, TOP_K_IN_GROUP)
    pair_w = jax.nn.softmax(top_v, axis=-1) * p_sel[:, None]
    e_id = (g_sel[:, None] * EXPERTS_PER_GROUP + top_i).astype(jnp.int32)
    flat_e = e_id.reshape(-1)
    flat_tok = jnp.repeat(jnp.arange(t, dtype=jnp.int32), TOP_K_IN_GROUP)
    flat_w = pair_w.reshape(-1)
    m = flat_e.shape[0]
    order = jnp.argsort(flat_e)
    se, stok, sw = flat_e[order], flat_tok[order], flat_w[order]
    counts = jax.ops.segment_sum(jnp.ones_like(flat_e), flat_e, num_segments=N_EXPERTS)
    starts = jnp.cumsum(counts) - counts
    padded = (counts + MOE_BLOCK - 1) // MOE_BLOCK * MOE_BLOCK
    pends = jnp.cumsum(padded)
    pstarts = pends - padded
    dest = pstarts[se] + jnp.arange(m, dtype=jnp.int32) - starts[se]
    n_blocks = (m + N_EXPERTS * (MOE_BLOCK - 1) + MOE_BLOCK - 1) // MOE_BLOCK
    n_rows = n_blocks * MOE_BLOCK
    row_tok = jnp.zeros((n_rows,), jnp.int32).at[dest].set(stok)
    row_w = jnp.zeros((n_rows,), F32).at[dest].set(sw)
    block_start = jnp.arange(n_blocks, dtype=jnp.int32) * MOE_BLOCK
    block_e = jnp.minimum(jnp.searchsorted(pends, block_start, side='right'),
                          N_EXPERTS - 1).astype(jnp.int32)
    n_used = (pends[-1] // MOE_BLOCK).astype(jnp.int32).reshape(1)
    return row_tok, row_w, block_e, n_used


def _pad_cols(w, n):
    return jnp.pad(w, ((0, 0), (0, n - w.shape[1])))


def _pad_rows(w, n):
    return jnp.pad(w, ((0, n - w.shape[0]), (0, 0)))


def _layer(x, mod, norm1_gain, w_in, shift_mu, w0, w_decay_up, a0, w_iclr_up, w_gate_up,
           k_k, k_a, r_k, ln_x_gain, ln_x_bias, sb_norm_gain, w_out, norm2_gain,
           w_router_group, b_router_group, w_router_expert, b_router_expert,
           w_exp_gate, w_exp_up, w_exp_down, *, rwkv_ts, tm_in, tm_out, tm_fin):
    bsz, seq, d = x.shape
    t = bsz * seq
    rw = w0.shape[0]
    sbw = sb_norm_gain.shape[0]
    sh1, sc1, g1, sh2, sc2, g2 = [m.reshape(bsz, 1, d) for m in jnp.split(mod, 6, axis=-1)]

    o = 3 * rw
    seg = lambda a, b: w_in[:, a:b]
    w_rwkv = jnp.concatenate([
        seg(0, o),
        _pad_cols(seg(o, o + DECAY_LORA), LANE),
        _pad_cols(seg(o + DECAY_LORA, o + DECAY_LORA + ICLR_LORA), LANE),
        _pad_cols(seg(o + DECAY_LORA + ICLR_LORA, o + DECAY_LORA + ICLR_LORA + GATE_LORA),
                  2 * LANE)], axis=1).astype(BF16)
    rcols = o + DECAY_LORA + ICLR_LORA + GATE_LORA
    w_sb = w_in[:, rcols:].astype(BF16)
    mseg = lambda a, b: shift_mu[a:b][None, :]
    mu = jnp.concatenate([
        mseg(0, o),
        _pad_cols(mseg(o, o + DECAY_LORA), LANE),
        _pad_cols(mseg(o + DECAY_LORA, o + DECAY_LORA + ICLR_LORA), LANE),
        _pad_cols(mseg(o + DECAY_LORA + ICLR_LORA, rcols), 2 * LANE)], axis=1)

    x2 = x.reshape(t, d)
    p_rwkv = _normmod_matmul(x2, norm1_gain, sc1, sh1, w_rwkv, F32, seq, tm_in, 512)
    qkv = _normmod_matmul(x2, norm1_gain, sc1, sh1, w_sb, BF16, seq, tm_in, 512)

    vecs = jnp.stack([w0, a0, k_k, k_a, r_k.reshape(-1), ln_x_gain, ln_x_bias,
                      jnp.zeros_like(w0)])
    y_a = _rwkv_time_mix(p_rwkv.reshape(bsz, seq, -1), mu, vecs,
                         _pad_rows(w_decay_up, LANE), _pad_rows(w_iclr_up, LANE),
                         _pad_rows(w_gate_up, 2 * LANE), ts=rwkv_ts)
    y_b = _sb_attention(qkv.reshape(bsz, seq, -1), sb_norm_gain, sbw)

    w_r = _pad_cols(jnp.concatenate([w_router_group, w_router_expert], axis=1), LANE)
    b_r = _pad_cols(jnp.concatenate([b_router_group, b_router_expert])[None, :], LANE)
    w_o = w_out.astype(BF16)
    x1, h2, logits = _outproj_norm_router(
        x2, y_a.reshape(t, rw), y_b.reshape(t, sbw), w_o[:rw], w_o[rw:], g1, norm2_gain,
        sc2, sh2, w_r, b_r, seq, tm_out)

    row_tok, row_w, block_e, n_used = _route(logits)
    y_rows = _moe_experts(h2[row_tok], row_w, block_e, n_used, w_exp_gate.astype(BF16),
                          w_exp_up.astype(BF16), w_exp_down.astype(BF16))
    y_m = jnp.zeros((t, d), F32).at[row_tok].add(y_rows)
    return x1, y_m, g2


def kernel(x, c, w_ada, b_ada, norm1_gain, w_in, shift_mu, w0, w_decay_up, a0, w_iclr_up, w_gate_up, k_k, k_a, r_k, ln_x_gain, ln_x_bias, sb_norm_gain, w_out, norm2_gain, w_router_group, b_router_group, w_router_expert, b_router_expert, w_exp_gate, w_exp_up, w_exp_down, final_norm_gain):
    bsz, seq, d = x.shape
    depth = w_ada.shape[0]
    tiles = dict(rwkv_ts=min(256, seq), tm_in=min(1024, seq), tm_out=min(256, seq),
                 tm_fin=min(512, seq))
    for l in range(depth):
        mod = _ada_mod(c, w_ada[l], b_ada[l])
        x1, y_m, g2 = _layer(
            x, mod, norm1_gain[l], w_in[l], shift_mu[l], w0[l], w_decay_up[l], a0[l],
            w_iclr_up[l], w_gate_up[l], k_k[l], k_a[l], r_k[l], ln_x_gain[l], ln_x_bias[l],
            sb_norm_gain[l], w_out[l], norm2_gain[l], w_router_group[l], b_router_group[l],
            w_router_expert[l], b_router_expert[l], w_exp_gate[l], w_exp_up[l],
            w_exp_down[l], **tiles)
        last = l == depth - 1
        gain = final_norm_gain if last else jnp.ones((d,), F32)
        out = _final_norm(x1, y_m, g2, gain, seq, tiles["tm_fin"])
        if not last:
            raise NotImplementedError("kernel supports the problem's DEPTH == 1")
        x = out.reshape(bsz, seq, d)
    return x
```

```python
import functools

import jax
import jax.numpy as jnp
import numpy as np
from jax import lax
from jax.experimental import pallas as pl
from jax.experimental.pallas import tpu as pltpu

F32 = jnp.float32
BF16 = jnp.bfloat16

RMS_EPS = 1e-6
GN_EPS = 64e-5
L2_EPS = 1e-12

HEAD_DIM = 64
PAIR = 2 * HEAD_DIM
RWKV_CHUNK = 64
SB_BLOCK = 128
N_GROUPS = 8
EXPERTS_PER_GROUP = 8
N_EXPERTS = N_GROUPS * EXPERTS_PER_GROUP
TOP_K_IN_GROUP = 2
MOE_BLOCK = 128
DECAY_LORA = 64
ICLR_LORA = 64
GATE_LORA = 160
LANE = 128
MXU_TILE = 256
VMEM_LIMIT = 48 * 1024 * 1024
SB_UNDERFLOW_LOG = -104.0


def _dot(a, b):
    return lax.dot_general(a, b, (((1,), (0,)), ((), ())), preferred_element_type=F32)


def _dot_nt(a, b):
    return lax.dot_general(a, b, (((1,), (1,)), ((), ())), preferred_element_type=F32)


def _split2(x):
    hi = x.astype(BF16)
    lo = (x - hi.astype(F32)).astype(BF16)
    return hi, lo


def _mm(a, b, passes, nt=False):
    d = _dot_nt if nt else _dot
    if passes == 1:
        return d(a.astype(BF16), b.astype(BF16))
    ah, al = _split2(a)
    bh, bl = _split2(b)
    return d(ah, bh) + d(ah, bl) + d(al, bh)


def _mm_exact_rhs(a, b_exact):
    hi, lo = _split2(a)
    return _dot(hi, b_exact) + _dot(lo, b_exact)


def _mm_exact_lhs(a_exact, b):
    hi, lo = _split2(b)
    return _dot(a_exact, hi) + _dot(a_exact, lo)


def _group_sum(x, e_blk):
    w = e_blk.shape[0]
    return jnp.concatenate([_mm_exact_rhs(x[:, g * w:(g + 1) * w], e_blk)
                            for g in range(x.shape[1] // w)], axis=1)


def _sigmoid(x):
    return 1.0 / (1.0 + jnp.exp(-x))


def _softplus(x):
    return jnp.maximum(x, 0.0) + jnp.log(1.0 + jnp.exp(-jnp.abs(x)))


def _params(sem, vmem=VMEM_LIMIT):
    return pltpu.CompilerParams(dimension_semantics=sem, vmem_limit_bytes=vmem)


def _ada_kernel(c_ref, w_ref, b_ref, o_ref):
    c = c_ref[...]
    s = c * _sigmoid(c)
    o_ref[...] = _mm(s, w_ref[...], 3) + b_ref[...]


def _ada_mod(c, w, b):
    bsz, d = c.shape
    n = w.shape[1]
    rows = 8
    cp = jnp.zeros((rows, d), F32).at[:bsz].set(c)
    tn = 1024
    out = pl.pallas_call(
        _ada_kernel,
        out_shape=jax.ShapeDtypeStruct((rows, n), F32),
        grid=(n // tn,),
        in_specs=[pl.BlockSpec((rows, d), lambda j: (0, 0)),
                  pl.BlockSpec((d, tn), lambda j: (0, j)),
                  pl.BlockSpec((1, tn), lambda j: (0, j))],
        out_specs=pl.BlockSpec((rows, tn), lambda j: (0, j)),
        compiler_params=_params(("arbitrary",)),
        name="ada_mod",
    )(cp, w, b.reshape(1, n))
    return out[:bsz]


def _normmod_mm_kernel(x_ref, gain_ref, sc_ref, sh_ref, w_ref, o_ref, h_scr):
    @pl.when(pl.program_id(1) == 0)
    def _():
        xf = x_ref[...]
        ms = jnp.mean(xf * xf, axis=-1, keepdims=True)
        y = xf * lax.rsqrt(ms + RMS_EPS) * gain_ref[...]
        h_scr[...] = (y * (1.0 + sc_ref[0]) + sh_ref[0]).astype(BF16)

    o_ref[...] = _dot(h_scr[...], w_ref[...]).astype(o_ref.dtype)


def _normmod_matmul(x2, gain, sc, sh, w, out_dtype, seq, tm, tn):
    t, d = x2.shape
    n = w.shape[1]
    per_b = seq // tm
    return pl.pallas_call(
        _normmod_mm_kernel,
        out_shape=jax.ShapeDtypeStruct((t, n), out_dtype),
        grid=(t // tm, n // tn),
        in_specs=[pl.BlockSpec((tm, d), lambda i, j: (i, 0)),
                  pl.BlockSpec((1, d), lambda i, j: (0, 0)),
                  pl.BlockSpec((1, 1, d), lambda i, j: (i // per_b, 0, 0)),
                  pl.BlockSpec((1, 1, d), lambda i, j: (i // per_b, 0, 0)),
                  pl.BlockSpec((d, tn), lambda i, j: (0, j))],
        out_specs=pl.BlockSpec((tm, tn), lambda i, j: (i, j)),
        scratch_shapes=[pltpu.VMEM((tm, d), BF16)],
        compiler_params=_params(("parallel", "arbitrary")),
        name="normmod_proj",
    )(x2, gain.reshape(1, d), sc, sh, w)


_V_W0, _V_A0, _V_KK, _V_KA, _V_RK, _V_LNG, _V_LNB = range(7)
_M_STRICT, _M_INCL, _M_BD8, _M_OFF8, _M_OFF16, _M_OFF32, _M_EYE = range(7)

RWKV_PASSES = 1


def _rwkv_masks():
    i = np.arange(PAIR)[:, None]
    j = np.arange(PAIR)[None, :]
    strict = (j < i)
    incl = (j <= i)
    bd8 = (i // 8 == j // 8)
    def off(b):
        return (i // (2 * b) == j // (2 * b)) & (i // b > j // b)
    eye = (i == j)
    m = np.stack([strict, incl, bd8 & strict, off(8), off(16), off(32), eye]).astype(np.float32)
    return jnp.asarray(m)


def _tri_inverse(lmat, masks_ref, passes):
    eye = masks_ref[_M_EYE]
    dblk = lmat * masks_ref[_M_BD8]
    x = eye + dblk
    p = _mm(dblk, dblk, passes)
    x = x + _mm(p, x, passes)
    p = _mm(p, p, passes)
    x = x + _mm(p, x, passes)
    for plane in (_M_OFF8, _M_OFF16, _M_OFF32):
        off = lmat * masks_ref[plane]
        x = x + _mm(_mm(x, off, passes), x, passes)
    return x


def _rwkv_kernel(p_ref, mu_ref, vec_ref, wd_ref, wa_ref, wg_ref, e_ref, tri_ref,
                 masks_ref, o_ref,
                 rt_s, at_s, bt_s, kt_s, v_s, bd_s, kd_s, gc_s, y_s, g_s, bon_s,
                 state_s, carry_s, *, ts, n_pairs, passes):
    s_idx = pl.program_id(1)

    @pl.when(s_idx == 0)
    def _():
        state_s[...] = jnp.zeros_like(state_s)
        carry_s[...] = jnp.zeros_like(carry_s)

    width = n_pairs * PAIR
    pt = p_ref[0]
    row = lax.broadcasted_iota(jnp.int32, pt.shape, 0)
    prev = jnp.where(row == 0, carry_s[...], pltpu.roll(pt, 1, 0))
    carry_s[...] = pt[ts - 1:ts, :]
    pm = pt + (prev - pt) * mu_ref[...]

    r = pm[:, 0:width]
    k = pm[:, width:2 * width]
    v = pm[:, 2 * width:3 * width]
    o3 = 3 * width
    xw = pm[:, o3:o3 + LANE]
    xa = pm[:, o3 + LANE:o3 + 2 * LANE]
    xg = pm[:, o3 + 2 * LANE:o3 + 4 * LANE]

    vec = lambda i: vec_ref[i:i + 1, :]
    e_mat = e_ref[...]
    wlog = -_softplus(-(vec(_V_W0) + _mm(jnp.tanh(xw), wd_ref[...], 3))) - 0.5
    logw = -jnp.exp(wlog)
    a = _sigmoid(vec(_V_A0) + _mm(xa, wa_ref[...], 1))
    g_s[...] = _mm(_sigmoid(xg), wg_ref[...], 1)
    kk = k * vec(_V_KK)
    ss = _group_sum(kk * kk, e_mat)
    kk = kk / jnp.maximum(jnp.sqrt(ss), L2_EPS)
    kp = k * (1.0 + (a - 1.0) * vec(_V_KA))
    ib = kk * a
    bon_s[...] = _group_sum(r * kp * vec(_V_RK), e_mat) * v
    cum = _mm_exact_lhs(tri_ref[...], logw)
    tot = jnp.concatenate(
        [jnp.broadcast_to(cum[c * RWKV_CHUNK + RWKV_CHUNK - 1:(c + 1) * RWKV_CHUNK, :],
                          (RWKV_CHUNK, width)) for c in range(ts // RWKV_CHUNK)], axis=0)
    rt_s[...] = r * jnp.exp(cum)
    at_s[...] = -kk * jnp.exp(cum - logw)
    dec_in = jnp.exp(-cum)
    bt_s[...] = ib * dec_in
    kt_s[...] = kp * dec_in
    v_s[...] = v
    dec_out = jnp.exp(tot - cum)
    bd_s[...] = ib * dec_out
    kd_s[...] = kp * dec_out
    gc_s[...] = jnp.exp(tot)

    lane = lax.broadcasted_iota(jnp.int32, (RWKV_CHUNK, PAIR), 1)
    m0 = (lane < HEAD_DIM).astype(F32)
    m1 = 1.0 - m0
    strict = masks_ref[_M_STRICT]
    incl = masks_ref[_M_INCL]

    def chunk_body(c, carry):
        rows = pl.ds(pl.multiple_of(c * RWKV_CHUNK, RWKV_CHUNK), RWKV_CHUNK)
        for j in range(n_pairs):
            lanes = slice(j * PAIR, (j + 1) * PAIR)

            def stacked(ref):
                xblk = ref[rows, lanes]
                return jnp.concatenate([xblk * m0, xblk * m1], axis=0)

            rh, ah, bh, kh = stacked(rt_s), stacked(at_s), stacked(bt_s), stacked(kt_s)
            vh, bdh, kdh = stacked(v_s), stacked(bd_s), stacked(kd_s)
            gc = gc_s[pl.ds(pl.multiple_of(c * RWKV_CHUNK, RWKV_CHUNK), 1), lanes]
            st = state_s[j]

            bk = jnp.concatenate([bh, kh], axis=0)
            ga = _mm(ah, bk, passes, nt=True)
            gr = _mm(rh, bk, passes, nt=True)
            a_ab = ga[:, :PAIR] * strict
            a_ak = ga[:, PAIR:] * strict
            a_rb = gr[:, :PAIR] * incl
            a_rk = gr[:, PAIR:] * incl
            tinv = _tri_inverse(a_ab, masks_ref, passes)
            x0 = _mm(ah, st, passes, nt=True) + _mm(a_ak, vh, passes)
            u = _mm(tinv, x0, passes)
            y = _mm(rh, st, passes, nt=True) + _mm(a_rb, u, passes) + _mm(a_rk, vh, passes)
            zt = jnp.concatenate([u, vh], axis=0).T
            bkd = jnp.concatenate([bdh, kdh], axis=0)
            state_s[j] = st * gc + _mm(zt, bkd, passes)
            y_s[rows, lanes] = y[:RWKV_CHUNK] + y[RWKV_CHUNK:]
        return carry

    lax.fori_loop(0, ts // RWKV_CHUNK, chunk_body, 0)

    y = y_s[...]
    inv_n = 1.0 / HEAD_DIM
    mean = _group_sum(y, e_mat) * inv_n
    dlt = y - mean
    var = _group_sum(dlt * dlt, e_mat) * inv_n
    yn = dlt * lax.rsqrt(var + GN_EPS) * vec(_V_LNG) + vec(_V_LNB)
    o_ref[0] = ((yn + bon_s[...]) * g_s[...]).astype(o_ref.dtype)


def _rwkv_time_mix(p, mu, vecs, wd, wa, wg, *, ts, passes=RWKV_PASSES):
    bsz, seq, cols = p.shape
    width = vecs.shape[1]
    n_pairs = width // PAIR
    heads = np.arange(MXU_TILE) // HEAD_DIM
    e_mat = jnp.asarray((heads[:, None] == heads[None, :]).astype(np.float32), dtype=BF16)
    tok = np.arange(ts)
    same = (tok[:, None] // RWKV_CHUNK) == (tok[None, :] // RWKV_CHUNK)
    tri = jnp.asarray((same & (tok[None, :] <= tok[:, None])).astype(np.float32), dtype=BF16)
    masks = _rwkv_masks()
    full = lambda shape: pl.BlockSpec(shape, lambda b, s: (0,) * len(shape))
    big = lambda: pltpu.VMEM((ts, width), F32)
    kern = functools.partial(_rwkv_kernel, ts=ts, n_pairs=n_pairs, passes=passes)
    return pl.pallas_call(
        kern,
        out_shape=jax.ShapeDtypeStruct((bsz, seq, width), BF16),
        grid=(bsz, seq // ts),
        in_specs=[pl.BlockSpec((1, ts, cols), lambda b, s: (b, s, 0)),
                  full((1, cols)), full(vecs.shape), full(wd.shape), full(wa.shape),
                  full(wg.shape), full(e_mat.shape), full(tri.shape), full(masks.shape)],
        out_specs=pl.BlockSpec((1, ts, width), lambda b, s: (b, s, 0)),
        scratch_shapes=[big() for _ in range(11)]
        + [pltpu.VMEM((n_pairs, PAIR, PAIR), F32), pltpu.VMEM((1, cols), F32)],
        compiler_params=_params(("parallel", "arbitrary")),
        name="rwkv7_scan",
    )(p, mu, vecs, wd, wa, wg, e_mat, tri, masks)


SB_WINDOW = 3
SB_Q_PER_STEP = 4


def _sb_kernel(q_ref, k_ref, v_ref, gain_ref, cum_ref, e2_ref, o_ref, *, q_per_step):
    lane = lax.broadcasted_iota(jnp.int32, (SB_BLOCK, PAIR), 1)
    h0 = lane < HEAD_DIM
    zero = jnp.zeros((), BF16)
    qi = lax.broadcasted_iota(jnp.int32, (SB_BLOCK, 2 * SB_BLOCK), 0)
    ki = lax.broadcasted_iota(jnp.int32, (SB_BLOCK, 2 * SB_BLOCK), 1) % SB_BLOCK
    causal = ki < qi
    cum_mat = cum_ref[...]
    scale = jnp.asarray(HEAD_DIM ** -0.5, BF16)
    width2 = 2 * SB_BLOCK

    def sweep(q, qb, offsets, acc, rest, first_is_diag):
        khats, vhats, valid = [], [], []
        for off in offsets:
            kb = qb - off
            valid.append(kb >= 0)
            rows = pl.ds(pl.multiple_of(jnp.maximum(kb, 0) * SB_BLOCK, SB_BLOCK), SB_BLOCK)
            kblk = k_ref[0, rows, :]
            vblk = v_ref[0, rows, :]
            khats += [jnp.where(h0, kblk, zero), jnp.where(h0, zero, kblk)]
            vhats += [jnp.where(h0, vblk, zero), jnp.where(h0, zero, vblk)]
        z_all = _dot_nt(q, jnp.concatenate(khats, axis=0))
        ws = []
        for n, off in enumerate(offsets):
            z = z_all[:, n * width2:(n + 1) * width2]
            log_beta = jnp.minimum(z, 0.0) - jnp.log(1.0 + jnp.exp(-jnp.abs(z)))
            log_1m = log_beta - z
            if first_is_diag and n == 0:
                keep = causal
            elif first_is_diag:
                keep = jnp.broadcast_to(valid[n], causal.shape)
            else:
                keep = None
            if keep is not None:
                log_1m = jnp.where(keep, log_1m, 0.0)
            hi, lo = _split2(log_1m)
            parts = []
            for h in range(2):
                sl = slice(h * SB_BLOCK, (h + 1) * SB_BLOCK)
                parts.append(_dot(jnp.concatenate([hi[:, sl], lo[:, sl]], axis=1), cum_mat))
            between = jnp.concatenate([parts[0][:, :SB_BLOCK], parts[1][:, :SB_BLOCK]], axis=1)
            total = jnp.concatenate([parts[0][:, SB_BLOCK:], parts[1][:, SB_BLOCK:]], axis=1)
            w = jnp.exp(log_beta + between + rest)
            if keep is not None:
                w = jnp.where(keep, w, 0.0)
            ws.append(w.astype(BF16))
            rest = rest + total
        acc = acc + _dot(jnp.concatenate(ws, axis=1), jnp.concatenate(vhats, axis=0))
        return acc, rest

    def q_block(sub, carry):
        qb = pl.program_id(2) * q_per_step + sub
        qrows = pl.ds(pl.multiple_of(sub * SB_BLOCK, SB_BLOCK), SB_BLOCK)
        q = q_ref[0, qrows, :] * scale
        acc = jnp.zeros((SB_BLOCK, PAIR), F32)
        rest = jnp.zeros((SB_BLOCK, width2), F32)
        acc, rest = sweep(q, qb, tuple(range(SB_WINDOW)), acc, rest, True)

        def cond(c):
            i, _, _, live = c
            return jnp.logical_and(i <= qb, live > SB_UNDERFLOW_LOG)

        def body(c):
            i, acc, rest, _ = c
            acc, rest = sweep(q, qb - i, (0,), acc, rest, False)
            return i + 1, acc, rest, jnp.max(rest)

        _, acc, _, _ = lax.while_loop(cond, body,
                                      (jnp.int32(SB_WINDOW), acc, rest, jnp.max(rest)))
        ms = _mm_exact_rhs(acc * acc, e2_ref[...]) * (1.0 / HEAD_DIM)
        o_ref[0, qrows, :] = (acc * lax.rsqrt(ms + RMS_EPS) * gain_ref[...]).astype(o_ref.dtype)
        return carry

    lax.fori_loop(0, q_per_step, q_block, 0)


def _sb_attention(qkv, gain, width):
    bsz, seq, _ = qkv.shape
    n_pairs = width // PAIR
    nq = seq // SB_BLOCK
    i = np.arange(SB_BLOCK)
    upper = (i[:, None] > i[None, :]).astype(np.float32)
    half = np.concatenate([upper, np.ones_like(upper)], axis=1)
    cum_mat = jnp.asarray(np.concatenate([half, half], axis=0), dtype=BF16)
    hd = np.arange(PAIR) // HEAD_DIM
    e2 = jnp.asarray((hd[:, None] == hd[None, :]).astype(np.float32), dtype=BF16)
    qps = min(SB_Q_PER_STEP, nq)
    qrows = qps * SB_BLOCK
    return pl.pallas_call(
        functools.partial(_sb_kernel, q_per_step=qps),
        out_shape=jax.ShapeDtypeStruct((bsz, seq, width), BF16),
        grid=(bsz, n_pairs, nq // qps),
        in_specs=[pl.BlockSpec((1, qrows, PAIR), lambda b, j, t: (b, t, j)),
                  pl.BlockSpec((1, seq, PAIR), lambda b, j, t: (b, 0, n_pairs + j)),
                  pl.BlockSpec((1, seq, PAIR), lambda b, j, t: (b, 0, 2 * n_pairs + j)),
                  pl.BlockSpec((1, PAIR), lambda b, j, t: (0, j)),
                  pl.BlockSpec(cum_mat.shape, lambda b, j, t: (0, 0)),
                  pl.BlockSpec(e2.shape, lambda b, j, t: (0, 0))],
        out_specs=pl.BlockSpec((1, qrows, PAIR), lambda b, j, t: (b, t, j)),
        compiler_params=_params(("parallel", "parallel", "arbitrary")),
        name="stickbreak_attn",
    )(qkv, qkv, qkv, gain.reshape(1, width), cum_mat, e2)


def _outproj_kernel(x_ref, ya_ref, yb_ref, wa_ref, wb_ref, g1_ref, gain_ref, sc_ref, sh_ref,
                    wr_ref, br_ref, x1_ref, h_ref, lg_ref):
    mix = _dot(ya_ref[...], wa_ref[...]) + _dot(yb_ref[...], wb_ref[...])
    x1 = x_ref[...] + g1_ref[0] * mix
    x1_ref[...] = x1
    ms = jnp.mean(x1 * x1, axis=-1, keepdims=True)
    h = x1 * lax.rsqrt(ms + RMS_EPS) * gain_ref[...] * (1.0 + sc_ref[0]) + sh_ref[0]
    h_ref[...] = h.astype(h_ref.dtype)
    lg_ref[...] = _mm(h, wr_ref[...], 3) + br_ref[...]


def _outproj_norm_router(x2, ya, yb, w_a, w_b, g1, gain, sc, sh, w_r, b_r, seq, tm):
    t, d = x2.shape
    half = ya.shape[1]
    nr = w_r.shape[1]
    per_b = seq // tm
    row = lambda w: pl.BlockSpec((tm, w), lambda i: (i, 0))
    full = lambda shape: pl.BlockSpec(shape, lambda i: (0,) * len(shape))
    per_batch = pl.BlockSpec((1, 1, d), lambda i: (i // per_b, 0, 0))
    return pl.pallas_call(
        _outproj_kernel,
        out_shape=(jax.ShapeDtypeStruct((t, d), F32), jax.ShapeDtypeStruct((t, d), BF16),
                   jax.ShapeDtypeStruct((t, nr), F32)),
        grid=(t // tm,),
        in_specs=[row(d), row(half), row(half), full((half, d)), full((half, d)), per_batch,
                  full((1, d)), per_batch, per_batch, full((d, nr)), full((1, nr))],
        out_specs=(row(d), row(d), row(nr)),
        compiler_params=_params(("parallel",)),
        name="outproj_norm_router",
    )(x2, ya, yb, w_a, w_b, g1, gain.reshape(1, d), sc, sh, w_r, b_r)


def _moe_kernel(be_ref, nused_ref, x_ref, wt_ref, wg_ref, wu_ref, wd_ref, o_ref):
    i = pl.program_id(0)

    @pl.when(i < nused_ref[0])
    def _():
        xb = x_ref[...]
        hg = _dot(xb, wg_ref[0])
        hu = _dot(xb, wu_ref[0])
        hid = hg * _sigmoid(hg) * hu
        o_ref[...] = _dot(hid.astype(BF16), wd_ref[0]) * wt_ref[...]

    @pl.when(i >= nused_ref[0])
    def _():
        o_ref[...] = jnp.zeros_like(o_ref)


def _moe_experts(xg, row_w, block_e, n_used, w_gate, w_up, w_down):
    n_rows, d = xg.shape
    de = w_gate.shape[2]
    n_blocks = n_rows // MOE_BLOCK
    grid_spec = pltpu.PrefetchScalarGridSpec(
        num_scalar_prefetch=2,
        grid=(n_blocks,),
        in_specs=[pl.BlockSpec((MOE_BLOCK, d), lambda i, be, nu: (i, 0)),
                  pl.BlockSpec((MOE_BLOCK, 1), lambda i, be, nu: (i, 0)),
                  pl.BlockSpec((1, d, de), lambda i, be, nu: (be[i], 0, 0)),
                  pl.BlockSpec((1, d, de), lambda i, be, nu: (be[i], 0, 0)),
                  pl.BlockSpec((1, de, d), lambda i, be, nu: (be[i], 0, 0))],
        out_specs=pl.BlockSpec((MOE_BLOCK, d), lambda i, be, nu: (i, 0)),
    )
    return pl.pallas_call(
        _moe_kernel,
        out_shape=jax.ShapeDtypeStruct((n_rows, d), F32),
        grid_spec=grid_spec,
        compiler_params=_params(("arbitrary",)),
        name="moe_experts",
    )(block_e, n_used, xg, row_w.reshape(n_rows, 1), w_gate, w_up, w_down)


def _final_kernel(x1_ref, ym_ref, g2_ref, gain_ref, o_ref):
    x2 = x1_ref[...] + g2_ref[0] * ym_ref[...]
    ms = jnp.mean(x2 * x2, axis=-1, keepdims=True)
    o_ref[...] = x2 * lax.rsqrt(ms + RMS_EPS) * gain_ref[...]


def _final_norm(x1, ym, g2, gain, seq, tm):
    t, d = x1.shape
    per_b = seq // tm
    row = pl.BlockSpec((tm, d), lambda i: (i, 0))
    return pl.pallas_call(
        _final_kernel,
        out_shape=jax.ShapeDtypeStruct((t, d), F32),
        grid=(t // tm,),
        in_specs=[row, row, pl.BlockSpec((1, 1, d), lambda i: (i // per_b, 0, 0)),
                  pl.BlockSpec((1, d), lambda i: (0, 0))],
        out_specs=row,
        compiler_params=_params(("parallel",)),
        name="final_norm",
    )(x1, ym, g2, gain.reshape(1, d))


def _route(logits):
    t = logits.shape[0]
    gl = logits[:, :N_GROUPS]
    el = logits[:, N_GROUPS:N_GROUPS + N_EXPERTS].reshape(t, N_GROUPS, EXPERTS_PER_GROUP)
    pg = jax.nn.softmax(gl, axis=-1)
    g_sel = jnp.argmax(gl, axis=-1)
    p_sel = jnp.take_along_axis(pg, g_sel[:, None], axis=1)[:, 0]
    el_g = jnp.take_along_axis(el, g_sel[:, None, None], axis=1)[:, 0]
    top_v, top_i = lax.top_k(el_g, TOP_K_IN_GROUP)
    pair_w = jax.nn.softmax(top_v, axis=-1) * p_sel[:, None]
    e_id = (g_sel[:, None] * EXPERTS_PER_GROUP + top_i).astype(jnp.int32)
    flat_e = e_id.reshape(-1)
    flat_tok = jnp.repeat(jnp.arange(t, dtype=jnp.int32), TOP_K_IN_GROUP)
    flat_w = pair_w.reshape(-1)
    m = flat_e.shape[0]
    order = jnp.argsort(flat_e)
    se, stok, sw = flat_e[order], flat_tok[order], flat_w[order]
    counts = jax.ops.segment_sum(jnp.ones_like(flat_e), flat_e, num_segments=N_EXPERTS)
    starts = jnp.cumsum(counts) - counts
    padded = (counts + MOE_BLOCK - 1) // MOE_BLOCK * MOE_BLOCK
    pends = jnp.cumsum(padded)
    pstarts = pends - padded
    dest = pstarts[se] + jnp.arange(m, dtype=jnp.int32) - starts[se]
    n_blocks = (m + N_EXPERTS * (MOE_BLOCK - 1) + MOE_BLOCK - 1) // MOE_BLOCK
    n_rows = n_blocks * MOE_BLOCK
    row_tok = jnp.zeros((n_rows,), jnp.int32).at[dest].set(stok)
    row_w = jnp.zeros((n_rows,), F32).at[dest].set(sw)
    block_start = jnp.arange(n_blocks, dtype=jnp.int32) * MOE_BLOCK
    block_e = jnp.minimum(jnp.searchsorted(pends, block_start, side='right'),
                          N_EXPERTS - 1).astype(jnp.int32)
    n_used = (pends[-1] // MOE_BLOCK).astype(jnp.int32).reshape(1)
    return row_tok, row_w, block_e, n_used


def _pad_cols(w, n):
    return jnp.pad(w, ((0, 0), (0, n - w.shape[1])))


def _pad_rows(w, n):
    return jnp.pad(w, ((0, n - w.shape[0]), (0, 0)))


def _layer(x, mod, norm1_gain, w_in, shift_mu, w0, w_decay_up, a0, w_iclr_up, w_gate_up,
           k_k, k_a, r_k, ln_x_gain, ln_x_bias, sb_norm_gain, w_out, norm2_gain,
           w_router_group, b_router_group, w_router_expert, b_router_expert,
           w_exp_gate, w_exp_up, w_exp_down, *, rwkv_ts, tm_in, tm_out, tm_fin):
    bsz, seq, d = x.shape
    t = bsz * seq
    rw = w0.shape[0]
    sbw = sb_norm_gain.shape[0]
    sh1, sc1, g1, sh2, sc2, g2 = [m.reshape(bsz, 1, d) for m in jnp.split(mod, 6, axis=-1)]

    o = 3 * rw
    seg = lambda a, b: w_in[:, a:b]
    w_rwkv = jnp.concatenate([
        seg(0, o),
        _pad_cols(seg(o, o + DECAY_LORA), LANE),
        _pad_cols(seg(o + DECAY_LORA, o + DECAY_LORA + ICLR_LORA), LANE),
        _pad_cols(seg(o + DECAY_LORA + ICLR_LORA, o + DECAY_LORA + ICLR_LORA + GATE_LORA),
                  2 * LANE)], axis=1).astype(BF16)
    rcols = o + DECAY_LORA + ICLR_LORA + GATE_LORA
    w_sb = w_in[:, rcols:].astype(BF16)
    mseg = lambda a, b: shift_mu[a:b][None, :]
    mu = jnp.concatenate([
        mseg(0, o),
        _pad_cols(mseg(o, o + DECAY_LORA), LANE),
        _pad_cols(mseg(o + DECAY_LORA, o + DECAY_LORA + ICLR_LORA), LANE),
        _pad_cols(mseg(o + DECAY_LORA + ICLR_LORA, rcols), 2 * LANE)], axis=1)

    x2 = x.reshape(t, d)
    p_rwkv = _normmod_matmul(x2, norm1_gain, sc1, sh1, w_rwkv, F32, seq, tm_in, 512)
    qkv = _normmod_matmul(x2, norm1_gain, sc1, sh1, w_sb, BF16, seq, tm_in, 512)

    vecs = jnp.stack([w0, a0, k_k, k_a, r_k.reshape(-1), ln_x_gain, ln_x_bias,
                      jnp.zeros_like(w0)])
    y_a = _rwkv_time_mix(p_rwkv.reshape(bsz, seq, -1), mu, vecs,
                         _pad_rows(w_decay_up, LANE), _pad_rows(w_iclr_up, LANE),
                         _pad_rows(w_gate_up, 2 * LANE), ts=rwkv_ts)
    y_b = _sb_attention(qkv.reshape(bsz, seq, -1), sb_norm_gain, sbw)

    w_r = _pad_cols(jnp.concatenate([w_router_group, w_router_expert], axis=1), LANE)
    b_r = _pad_cols(jnp.concatenate([b_router_group, b_router_expert])[None, :], LANE)
    w_o = w_out.astype(BF16)
    x1, h2, logits = _outproj_norm_router(
        x2, y_a.reshape(t, rw), y_b.reshape(t, sbw), w_o[:rw], w_o[rw:], g1, norm2_gain,
        sc2, sh2, w_r, b_r, seq, tm_out)

    row_tok, row_w, block_e, n_used = _route(logits)
    y_rows = _moe_experts(h2[row_tok], row_w, block_e, n_used, w_exp_gate.astype(BF16),
                          w_exp_up.astype(BF16), w_exp_down.astype(BF16))
    y_m = jnp.zeros((t, d), F32).at[row_tok].add(y_rows)
    return x1, y_m, g2


def kernel(x, c, w_ada, b_ada, norm1_gain, w_in, shift_mu, w0, w_decay_up, a0, w_iclr_up, w_gate_up, k_k, k_a, r_k, ln_x_gain, ln_x_bias, sb_norm_gain, w_out, norm2_gain, w_router_group, b_router_group, w_router_expert, b_router_expert, w_exp_gate, w_exp_up, w_exp_down, final_norm_gain):
    bsz, seq, d = x.shape
    depth = w_ada.shape[0]
    tiles = dict(rwkv_ts=min(256, seq), tm_in=min(1024, seq), tm_out=min(256, seq),
                 tm_fin=min(512, seq))
    for l in range(depth):
        mod = _ada_mod(c, w_ada[l], b_ada[l])
        x1, y_m, g2 = _layer(
            x, mod, norm1_gain[l], w_in[l], shift_mu[l], w0[l], w_decay_up[l], a0[l],
            w_iclr_up[l], w_gate_up[l], k_k[l], k_a[l], r_k[l], ln_x_gain[l], ln_x_bias[l],
            sb_norm_gain[l], w_out[l], norm2_gain[l], w_router_group[l], b_router_group[l],
            w_router_expert[l], b_router_expert[l], w_exp_gate[l], w_exp_up[l],
            w_exp_down[l], **tiles)
        last = l == depth - 1
        gain = final_norm_gain if last else jnp.ones((d,), F32)
        out = _final_norm(x1, y_m, g2, gain, seq, tiles["tm_fin"])
        if not last:
            raise NotImplementedError("kernel supports the problem's DEPTH == 1")
        x = out.reshape(bsz, seq, d)
    return x
```

```python
import functools

import jax
import jax.numpy as jnp
import numpy as np
from jax import lax
from jax.experimental import pallas as pl
from jax.experimental.pallas import tpu as pltpu

F32 = jnp.float32
BF16 = jnp.bfloat16

RMS_EPS = 1e-6
GN_EPS = 64e-5
L2_EPS = 1e-12

HEAD_DIM = 64
PAIR = 2 * HEAD_DIM
RWKV_CHUNK = 64
SB_BLOCK = 128
N_GROUPS = 8
EXPERTS_PER_GROUP = 8
N_EXPERTS = N_GROUPS * EXPERTS_PER_GROUP
TOP_K_IN_GROUP = 2
MOE_BLOCK = 128
DECAY_LORA = 64
ICLR_LORA = 64
GATE_LORA = 160
LANE = 128
MXU_TILE = 256
VMEM_LIMIT = 48 * 1024 * 1024
SB_UNDERFLOW_LOG = -104.0


def _dot(a, b):
    return lax.dot_general(a, b, (((1,), (0,)), ((), ())), preferred_element_type=F32)


def _dot_nt(a, b):
    return lax.dot_general(a, b, (((1,), (1,)), ((), ())), preferred_element_type=F32)


def _split2(x):
    hi = x.astype(BF16)
    lo = (x - hi.astype(F32)).astype(BF16)
    return hi, lo


def _mm(a, b, passes, nt=False):
    d = _dot_nt if nt else _dot
    if passes == 1:
        return d(a.astype(BF16), b.astype(BF16))
    ah, al = _split2(a)
    bh, bl = _split2(b)
    return d(ah, bh) + d(ah, bl) + d(al, bh)


def _mm_exact_rhs(a, b_exact):
    hi, lo = _split2(a)
    return _dot(hi, b_exact) + _dot(lo, b_exact)


def _mm_exact_lhs(a_exact, b):
    hi, lo = _split2(b)
    return _dot(a_exact, hi) + _dot(a_exact, lo)


def _group_sum(x, e_blk):
    w = e_blk.shape[0]
    return jnp.concatenate([_mm_exact_rhs(x[:, g * w:(g + 1) * w], e_blk)
                            for g in range(x.shape[1] // w)], axis=1)


def _sigmoid(x):
    return 1.0 / (1.0 + jnp.exp(-x))


def _softplus(x):
    return jnp.maximum(x, 0.0) + jnp.log(1.0 + jnp.exp(-jnp.abs(x)))


def _params(sem, vmem=VMEM_LIMIT):
    return pltpu.CompilerParams(dimension_semantics=sem, vmem_limit_bytes=vmem)


def _ada_kernel(c_ref, w_ref, b_ref, o_ref):
    c = c_ref[...]
    s = c * _sigmoid(c)
    o_ref[...] = _mm(s, w_ref[...], 3) + b_ref[...]


def _ada_mod(c, w, b):
    bsz, d = c.shape
    n = w.shape[1]
    rows = 8
    cp = jnp.zeros((rows, d), F32).at[:bsz].set(c)
    tn = 1024
    out = pl.pallas_call(
        _ada_kernel,
        out_shape=jax.ShapeDtypeStruct((rows, n), F32),
        grid=(n // tn,),
        in_specs=[pl.BlockSpec((rows, d), lambda j: (0, 0)),
                  pl.BlockSpec((d, tn), lambda j: (0, j)),
                  pl.BlockSpec((1, tn), lambda j: (0, j))],
        out_specs=pl.BlockSpec((rows, tn), lambda j: (0, j)),
        compiler_params=_params(("arbitrary",)),
        name="ada_mod",
    )(cp, w, b.reshape(1, n))
    return out[:bsz]


def _normmod_mm_kernel(x_ref, gain_ref, sc_ref, sh_ref, w_ref, o_ref, h_scr):
    @pl.when(pl.program_id(1) == 0)
    def _():
        xf = x_ref[...]
        ms = jnp.mean(xf * xf, axis=-1, keepdims=True)
        y = xf * lax.rsqrt(ms + RMS_EPS) * gain_ref[...]
        h_scr[...] = (y * (1.0 + sc_ref[0]) + sh_ref[0]).astype(BF16)

    o_ref[...] = _dot(h_scr[...], w_ref[...]).astype(o_ref.dtype)


def _normmod_matmul(x2, gain, sc, sh, w, out_dtype, seq, tm, tn):
    t, d = x2.shape
    n = w.shape[1]
    per_b = seq // tm
    return pl.pallas_call(
        _normmod_mm_kernel,
        out_shape=jax.ShapeDtypeStruct((t, n), out_dtype),
        grid=(t // tm, n // tn),
        in_specs=[pl.BlockSpec((tm, d), lambda i, j: (i, 0)),
                  pl.BlockSpec((1, d), lambda i, j: (0, 0)),
                  pl.BlockSpec((1, 1, d), lambda i, j: (i // per_b, 0, 0)),
                  pl.BlockSpec((1, 1, d), lambda i, j: (i // per_b, 0, 0)),
                  pl.BlockSpec((d, tn), lambda i, j: (0, j))],
        out_specs=pl.BlockSpec((tm, tn), lambda i, j: (i, j)),
        scratch_shapes=[pltpu.VMEM((tm, d), BF16)],
        compiler_params=_params(("parallel", "arbitrary")),
        name="normmod_proj",
    )(x2, gain.reshape(1, d), sc, sh, w)


_V_W0, _V_A0, _V_KK, _V_KA, _V_RK, _V_LNG, _V_LNB = range(7)
_M_STRICT, _M_INCL, _M_BD8, _M_OFF8, _M_OFF16, _M_OFF32, _M_EYE = range(7)

RWKV_PASSES = 1


def _rwkv_masks():
    i = np.arange(PAIR)[:, None]
    j = np.arange(PAIR)[None, :]
    strict = (j < i)
    incl = (j <= i)
    bd8 = (i // 8 == j // 8)
    def off(b):
        return (i // (2 * b) == j // (2 * b)) & (i // b > j // b)
    eye = (i == j)
    m = np.stack([strict, incl, bd8 & strict, off(8), off(16), off(32), eye]).astype(np.float32)
    return jnp.asarray(m)


def _tri_inverse(lmats, masks_ref, passes):
    eye = masks_ref[_M_EYE]
    bd8 = masks_ref[_M_BD8]
    mm = lambda a, b: _mm(a, b, passes)
    dblk = [l * bd8 for l in lmats]
    x = [eye + d for d in dblk]
    p = [mm(d, d) for d in dblk]
    x = [xi + mm(pi, xi) for xi, pi in zip(x, p)]
    p = [mm(pi, pi) for pi in p]
    x = [xi + mm(pi, xi) for xi, pi in zip(x, p)]
    for plane in (_M_OFF8, _M_OFF16, _M_OFF32):
        mask = masks_ref[plane]
        t = [mm(xi, l * mask) for xi, l in zip(x, lmats)]
        x = [xi + mm(ti, xi) for xi, ti in zip(x, t)]
    return x


def _rwkv_kernel(p_ref, mu_ref, vec_ref, wd_ref, wa_ref, wg_ref, e_ref, tri_ref,
                 masks_ref, o_ref,
                 rt_s, at_s, bt_s, kt_s, v_s, bd_s, kd_s, gc_s, y_s, g_s, bon_s,
                 state_s, carry_s, *, ts, n_pairs, passes):
    s_idx = pl.program_id(1)

    @pl.when(s_idx == 0)
    def _():
        state_s[...] = jnp.zeros_like(state_s)
        carry_s[...] = jnp.zeros_like(carry_s)

    width = n_pairs * PAIR
    pt = p_ref[0]
    row = lax.broadcasted_iota(jnp.int32, pt.shape, 0)
    prev = jnp.where(row == 0, carry_s[...], pltpu.roll(pt, 1, 0))
    carry_s[...] = pt[ts - 1:ts, :]
    pm = pt + (prev - pt) * mu_ref[...]

    r = pm[:, 0:width]
    k = pm[:, width:2 * width]
    v = pm[:, 2 * width:3 * width]
    o3 = 3 * width
    xw = pm[:, o3:o3 + LANE]
    xa = pm[:, o3 + LANE:o3 + 2 * LANE]
    xg = pm[:, o3 + 2 * LANE:o3 + 4 * LANE]

    vec = lambda i: vec_ref[i:i + 1, :]
    e_mat = e_ref[...]
    wlog = -_softplus(-(vec(_V_W0) + _mm(jnp.tanh(xw), wd_ref[...], 3))) - 0.5
    logw = -jnp.exp(wlog)
    a = _sigmoid(vec(_V_A0) + _mm(xa, wa_ref[...], 1))
    g_s[...] = _mm(_sigmoid(xg), wg_ref[...], 1)
    kk = k * vec(_V_KK)
    ss = _group_sum(kk * kk, e_mat)
    kk = kk / jnp.maximum(jnp.sqrt(ss), L2_EPS)
    kp = k * (1.0 + (a - 1.0) * vec(_V_KA))
    ib = kk * a
    bon_s[...] = _group_sum(r * kp * vec(_V_RK), e_mat) * v
    cum = _mm_exact_lhs(tri_ref[...], logw)
    tot = jnp.concatenate(
        [jnp.broadcast_to(cum[c * RWKV_CHUNK + RWKV_CHUNK - 1:(c + 1) * RWKV_CHUNK, :],
                          (RWKV_CHUNK, width)) for c in range(ts // RWKV_CHUNK)], axis=0)
    rt_s[...] = r * jnp.exp(cum)
    at_s[...] = -kk * jnp.exp(cum - logw)
    dec_in = jnp.exp(-cum)
    bt_s[...] = ib * dec_in
    kt_s[...] = kp * dec_in
    v_s[...] = v
    dec_out = jnp.exp(tot - cum)
    bd_s[...] = ib * dec_out
    kd_s[...] = kp * dec_out
    gc_s[...] = jnp.exp(tot)

    lane = lax.broadcasted_iota(jnp.int32, (RWKV_CHUNK, PAIR), 1)
    m0 = (lane < HEAD_DIM).astype(F32)
    m1 = 1.0 - m0
    strict = masks_ref[_M_STRICT]
    incl = masks_ref[_M_INCL]

    def chunk_body(c, carry):
        row0 = pl.multiple_of(c * RWKV_CHUNK, RWKV_CHUNK)
        rows = pl.ds(row0, RWKV_CHUNK)
        pairs = range(n_pairs)
        lanes = [slice(j * PAIR, (j + 1) * PAIR) for j in pairs]
        mm = lambda x, w: _mm(x, w, passes)
        mm_nt = lambda x, w: _mm(x, w, passes, nt=True)

        def stacked(ref):
            blks = [ref[rows, lanes[j]] for j in pairs]
            return [jnp.concatenate([b * m0, b * m1], axis=0) for b in blks]

        rh, ah, bh, kh = stacked(rt_s), stacked(at_s), stacked(bt_s), stacked(kt_s)
        vh, bdh, kdh = stacked(v_s), stacked(bd_s), stacked(kd_s)
        st = [state_s[j] for j in pairs]
        bk = [jnp.concatenate([bh[j], kh[j]], axis=0) for j in pairs]
        ga = [mm_nt(ah[j], bk[j]) for j in pairs]
        gr = [mm_nt(rh[j], bk[j]) for j in pairs]
        tinv = _tri_inverse([ga[j][:, :PAIR] * strict for j in pairs], masks_ref, passes)
        x0 = [mm_nt(ah[j], st[j]) + mm(ga[j][:, PAIR:] * strict, vh[j]) for j in pairs]
        u = [mm(tinv[j], x0[j]) for j in pairs]
        y = [mm_nt(rh[j], st[j]) + mm(gr[j][:, :PAIR] * incl, u[j])
             + mm(gr[j][:, PAIR:] * incl, vh[j]) for j in pairs]
        for j in pairs:
            zt = jnp.concatenate([u[j], vh[j]], axis=0).T
            bkd = jnp.concatenate([bdh[j], kdh[j]], axis=0)
            gc = gc_s[pl.ds(row0, 1), lanes[j]]
            state_s[j] = st[j] * gc + mm(zt, bkd)
            y_s[rows, lanes[j]] = y[j][:RWKV_CHUNK] + y[j][RWKV_CHUNK:]
        return carry

    lax.fori_loop(0, ts // RWKV_CHUNK, chunk_body, 0)

    y = y_s[...]
    inv_n = 1.0 / HEAD_DIM
    mean = _group_sum(y, e_mat) * inv_n
    dlt = y - mean
    var = _group_sum(dlt * dlt, e_mat) * inv_n
    yn = dlt * lax.rsqrt(var + GN_EPS) * vec(_V_LNG) + vec(_V_LNB)
    o_ref[0] = ((yn + bon_s[...]) * g_s[...]).astype(o_ref.dtype)


def _rwkv_time_mix(p, mu, vecs, wd, wa, wg, *, ts, passes=RWKV_PASSES):
    bsz, seq, cols = p.shape
    width = vecs.shape[1]
    n_pairs = width // PAIR
    heads = np.arange(MXU_TILE) // HEAD_DIM
    e_mat = jnp.asarray((heads[:, None] == heads[None, :]).astype(np.float32), dtype=BF16)
    tok = np.arange(ts)
    same = (tok[:, None] // RWKV_CHUNK) == (tok[None, :] // RWKV_CHUNK)
    tri = jnp.asarray((same & (tok[None, :] <= tok[:, None])).astype(np.float32), dtype=BF16)
    masks = _rwkv_masks()
    full = lambda shape: pl.BlockSpec(shape, lambda b, s: (0,) * len(shape))
    big = lambda: pltpu.VMEM((ts, width), F32)
    kern = functools.partial(_rwkv_kernel, ts=ts, n_pairs=n_pairs, passes=passes)
    return pl.pallas_call(
        kern,
        out_shape=jax.ShapeDtypeStruct((bsz, seq, width), BF16),
        grid=(bsz, seq // ts),
        in_specs=[pl.BlockSpec((1, ts, cols), lambda b, s: (b, s, 0)),
                  full((1, cols)), full(vecs.shape), full(wd.shape), full(wa.shape),
                  full(wg.shape), full(e_mat.shape), full(tri.shape), full(masks.shape)],
        out_specs=pl.BlockSpec((1, ts, width), lambda b, s: (b, s, 0)),
        scratch_shapes=[big() for _ in range(11)]
        + [pltpu.VMEM((n_pairs, PAIR, PAIR), F32), pltpu.VMEM((1, cols), F32)],
        compiler_params=_params(("parallel", "arbitrary")),
        name="rwkv7_scan",
    )(p, mu, vecs, wd, wa, wg, e_mat, tri, masks)


SB_WINDOW = 6
SB_TAIL = 2
SB_Q_PER_STEP = 4


def _sb_kernel(q_ref, k_ref, v_ref, gain_ref, cum_ref, e2_ref, o_ref, *, q_per_step):
    lane = lax.broadcasted_iota(jnp.int32, (SB_BLOCK, PAIR), 1)
    h0 = lane < HEAD_DIM
    zero = jnp.zeros((), BF16)
    qi = lax.broadcasted_iota(jnp.int32, (SB_BLOCK, 2 * SB_BLOCK), 0)
    ki = lax.broadcasted_iota(jnp.int32, (SB_BLOCK, 2 * SB_BLOCK), 1) % SB_BLOCK
    causal = ki < qi
    cum_mat = cum_ref[...]
    scale = jnp.asarray(HEAD_DIM ** -0.5, BF16)
    width2 = 2 * SB_BLOCK

    def sweep(q, qb, offsets, acc, rest, first_is_diag):
        khats, vhats, valid = [], [], []
        for off in offsets:
            kb = qb - off
            valid.append(kb >= 0)
            rows = pl.ds(pl.multiple_of(jnp.maximum(kb, 0) * SB_BLOCK, SB_BLOCK), SB_BLOCK)
            kblk = k_ref[0, rows, :]
            vblk = v_ref[0, rows, :]
            khats += [jnp.where(h0, kblk, zero), jnp.where(h0, zero, kblk)]
            vhats += [jnp.where(h0, vblk, zero), jnp.where(h0, zero, vblk)]
        z_all = _dot_nt(q, jnp.concatenate(khats, axis=0))
        nblk = range(len(offsets))
        keep = [causal if first_is_diag else None]
        keep += [jnp.broadcast_to(valid[n], causal.shape) for n in nblk[1:]]
        zs = [z_all[:, n * width2:(n + 1) * width2] for n in nblk]
        log_beta = [jnp.minimum(z, 0.0) - jnp.log(1.0 + jnp.exp(-jnp.abs(z))) for z in zs]
        log_1m = [lb - z for lb, z in zip(log_beta, zs)]
        log_1m = [l if m is None else jnp.where(m, l, 0.0) for l, m in zip(log_1m, keep)]
        splits = [_split2(l) for l in log_1m]
        heads = [slice(h * SB_BLOCK, (h + 1) * SB_BLOCK) for h in range(2)]
        parts = [[_dot(jnp.concatenate([hi[:, sl], lo[:, sl]], axis=1), cum_mat) for sl in heads]
                 for hi, lo in splits]
        ws = []
        for n in nblk:
            between = jnp.concatenate([parts[n][0][:, :SB_BLOCK], parts[n][1][:, :SB_BLOCK]], axis=1)
            total = jnp.concatenate([parts[n][0][:, SB_BLOCK:], parts[n][1][:, SB_BLOCK:]], axis=1)
            w = jnp.exp(log_beta[n] + between + rest)
            if keep[n] is not None:
                w = jnp.where(keep[n], w, 0.0)
            ws.append(w.astype(BF16))
            rest = rest + total
        acc = acc + _dot(jnp.concatenate(ws, axis=1), jnp.concatenate(vhats, axis=0))
        return acc, rest

    def q_block(sub, carry):
        qb = pl.program_id(2) * q_per_step + sub
        qrows = pl.ds(pl.multiple_of(sub * SB_BLOCK, SB_BLOCK), SB_BLOCK)
        q = q_ref[0, qrows, :] * scale
        acc = jnp.zeros((SB_BLOCK, PAIR), F32)
        rest = jnp.zeros((SB_BLOCK, width2), F32)
        acc, rest = sweep(q, qb, tuple(range(SB_WINDOW)), acc, rest, True)

        def cond(c):
            i, _, _, live = c
            return jnp.logical_and(i <= qb, live > SB_UNDERFLOW_LOG)

        def body(c):
            i, acc, rest, _ = c
            acc, rest = sweep(q, qb - i, tuple(range(SB_TAIL)), acc, rest, False)
            return i + SB_TAIL, acc, rest, jnp.max(rest)

        _, acc, _, _ = lax.while_loop(cond, body,
                                      (jnp.int32(SB_WINDOW), acc, rest, jnp.max(rest)))
        ms = _mm_exact_rhs(acc * acc, e2_ref[...]) * (1.0 / HEAD_DIM)
        o_ref[0, qrows, :] = (acc * lax.rsqrt(ms + RMS_EPS) * gain_ref[...]).astype(o_ref.dtype)
        return carry

    lax.fori_loop(0, q_per_step, q_block, 0)


def _sb_attention(qkv, gain, width):
    bsz, seq, _ = qkv.shape
    n_pairs = width // PAIR
    nq = seq // SB_BLOCK
    i = np.arange(SB_BLOCK)
    upper = (i[:, None] > i[None, :]).astype(np.float32)
    half = np.concatenate([upper, np.ones_like(upper)], axis=1)
    cum_mat = jnp.asarray(np.concatenate([half, half], axis=0), dtype=BF16)
    hd = np.arange(PAIR) // HEAD_DIM
    e2 = jnp.asarray((hd[:, None] == hd[None, :]).astype(np.float32), dtype=BF16)
    qps = min(SB_Q_PER_STEP, nq)
    qrows = qps * SB_BLOCK
    return pl.pallas_call(
        functools.partial(_sb_kernel, q_per_step=qps),
        out_shape=jax.ShapeDtypeStruct((bsz, seq, width), BF16),
        grid=(bsz, n_pairs, nq // qps),
        in_specs=[pl.BlockSpec((1, qrows, PAIR), lambda b, j, t: (b, t, j)),
                  pl.BlockSpec((1, seq, PAIR), lambda b, j, t: (b, 0, n_pairs + j)),
                  pl.BlockSpec((1, seq, PAIR), lambda b, j, t: (b, 0, 2 * n_pairs + j)),
                  pl.BlockSpec((1, PAIR), lambda b, j, t: (0, j)),
                  pl.BlockSpec(cum_mat.shape, lambda b, j, t: (0, 0)),
                  pl.BlockSpec(e2.shape, lambda b, j, t: (0, 0))],
        out_specs=pl.BlockSpec((1, qrows, PAIR), lambda b, j, t: (b, t, j)),
        compiler_params=_params(("parallel", "parallel", "arbitrary")),
        name="stickbreak_attn",
    )(qkv, qkv, qkv, gain.reshape(1, width), cum_mat, e2)


def _outproj_kernel(x_ref, ya_ref, yb_ref, wa_ref, wb_ref, g1_ref, gain_ref, sc_ref, sh_ref,
                    wr_ref, br_ref, x1_ref, h_ref, lg_ref):
    mix = _dot(ya_ref[...], wa_ref[...]) + _dot(yb_ref[...], wb_ref[...])
    x1 = x_ref[...] + g1_ref[0] * mix
    x1_ref[...] = x1
    ms = jnp.mean(x1 * x1, axis=-1, keepdims=True)
    h = x1 * lax.rsqrt(ms + RMS_EPS) * gain_ref[...] * (1.0 + sc_ref[0]) + sh_ref[0]
    h_ref[...] = h.astype(h_ref.dtype)
    lg_ref[...] = _mm(h, wr_ref[...], 3) + br_ref[...]


def _outproj_norm_router(x2, ya, yb, w_a, w_b, g1, gain, sc, sh, w_r, b_r, seq, tm):
    t, d = x2.shape
    half = ya.shape[1]
    nr = w_r.shape[1]
    per_b = seq // tm
    row = lambda w: pl.BlockSpec((tm, w), lambda i: (i, 0))
    full = lambda shape: pl.BlockSpec(shape, lambda i: (0,) * len(shape))
    per_batch = pl.BlockSpec((1, 1, d), lambda i: (i // per_b, 0, 0))
    return pl.pallas_call(
        _outproj_kernel,
        out_shape=(jax.ShapeDtypeStruct((t, d), F32), jax.ShapeDtypeStruct((t, d), F32),
                   jax.ShapeDtypeStruct((t, nr), F32)),
        grid=(t // tm,),
        in_specs=[row(d), row(half), row(half), full((half, d)), full((half, d)), per_batch,
                  full((1, d)), per_batch, per_batch, full((d, nr)), full((1, nr))],
        out_specs=(row(d), row(d), row(nr)),
        compiler_params=_params(("parallel",)),
        name="outproj_norm_router",
    )(x2, ya, yb, w_a, w_b, g1, gain.reshape(1, d), sc, sh, w_r, b_r)


MOE_ROWS = 256
MOE_VMEM_LIMIT = 58 * 1024 * 1024
_DMA_UNROLL = 8


def _moe_kernel(be_ref, nused_ref, nvalid_ref, gcur_ref, gnext_ref, sidx_ref, wt_ref, wg_ref,
                wu_ref, wd_ref, h_hbm, y_hbm, xbuf, ybuf, wg_b, wu_b, wd_b, gsem, ssem):
    i = pl.program_id(0)
    n_used = nused_ref[0]
    last = pl.num_programs(0) - 1
    slot = i % 2

    def gather(idx_ref, r, s):
        return pltpu.make_async_copy(h_hbm.at[pl.ds(idx_ref[0, 0, r], 1), :],
                                     xbuf.at[s, pl.ds(r, 1), :], gsem.at[s])

    def scatter(idx_ref, r, s):
        return pltpu.make_async_copy(ybuf.at[s, pl.ds(r, 1), :],
                                     y_hbm.at[pl.ds(idx_ref[0, 0, r], 1), :], ssem.at[s])

    def for_rows(fn, n=None):
        def body(r, c):
            fn(r)
            return c
        if n is None:
            lax.fori_loop(0, MOE_ROWS, body, 0, unroll=_DMA_UNROLL)
        else:
            lax.fori_loop(0, n, body, 0)

    def scatter_wait(step):
        for_rows(lambda r: scatter(sidx_ref, r, step % 2).wait(), nvalid_ref[step])

    @pl.when(jnp.logical_and(i == 0, n_used > 0))
    def _():
        for_rows(lambda r: gather(gcur_ref, r, 0).start())

    @pl.when(i + 1 < n_used)
    def _():
        for_rows(lambda r: gather(gnext_ref, r, 1 - slot).start())

    @pl.when(i < n_used)
    def _():
        for_rows(lambda r: gather(gcur_ref, r, slot).wait())

        @pl.when(jnp.logical_or(i == 0, be_ref[i] != be_ref[jnp.maximum(i - 1, 0)]))
        def _():
            wg_b[...] = wg_ref[0].astype(BF16)
            wu_b[...] = wu_ref[0].astype(BF16)
            wd_b[...] = wd_ref[0].astype(BF16)

        @pl.when(i >= 2)
        def _():
            scatter_wait(i - 2)

        xb = xbuf[slot].astype(BF16)
        hg = _dot(xb, wg_b[...])
        hu = _dot(xb, wu_b[...])
        hid = hg * _sigmoid(hg) * hu
        ybuf[slot] = _dot(hid.astype(BF16), wd_b[...]) * wt_ref[...]
        for_rows(lambda r: scatter(sidx_ref, r, slot).start(), nvalid_ref[i])

    @pl.when(i == last)
    def _():
        for back in (2, 1):
            @pl.when(n_used >= back)
            def _():
                scatter_wait(n_used - back)


def _moe_experts(h2, gidx, sidx, row_w, block_e, n_used, n_valid, w_gate, w_up, w_down):
    t, d = h2.shape
    de = w_gate.shape[2]
    n_blocks = gidx.shape[0]
    once = pl.Buffered(1)
    idx_spec = lambda f: pl.BlockSpec((1, 1, MOE_ROWS), f, memory_space=pltpu.SMEM)
    grid_spec = pltpu.PrefetchScalarGridSpec(
        num_scalar_prefetch=3,
        grid=(n_blocks,),
        in_specs=[idx_spec(lambda i, be, nu, nv: (i, 0, 0)),
                  idx_spec(lambda i, be, nu, nv: (jnp.minimum(i + 1, n_blocks - 1), 0, 0)),
                  idx_spec(lambda i, be, nu, nv: (i, 0, 0)),
                  pl.BlockSpec((MOE_ROWS, 1), lambda i, be, nu, nv: (i, 0)),
                  pl.BlockSpec((1, d, de), lambda i, be, nu, nv: (be[i], 0, 0), pipeline_mode=once),
                  pl.BlockSpec((1, d, de), lambda i, be, nu, nv: (be[i], 0, 0), pipeline_mode=once),
                  pl.BlockSpec((1, de, d), lambda i, be, nu, nv: (be[i], 0, 0), pipeline_mode=once),
                  pl.BlockSpec(memory_space=pl.ANY)],
        out_specs=pl.BlockSpec(memory_space=pl.ANY),
        scratch_shapes=[pltpu.VMEM((2, MOE_ROWS, d), F32), pltpu.VMEM((2, MOE_ROWS, d), F32),
                        pltpu.VMEM((d, de), BF16), pltpu.VMEM((d, de), BF16),
                        pltpu.VMEM((de, d), BF16),
                        pltpu.SemaphoreType.DMA((2,)), pltpu.SemaphoreType.DMA((2,))],
    )
    return pl.pallas_call(
        _moe_kernel,
        out_shape=jax.ShapeDtypeStruct((TOP_K_IN_GROUP * t, d), F32),
        grid_spec=grid_spec,
        compiler_params=_params(("arbitrary",), MOE_VMEM_LIMIT),
        name="moe_experts",
    )(block_e, n_used, n_valid, gidx, gidx, sidx, row_w.reshape(-1, 1), w_gate, w_up, w_down, h2)


def _final_kernel(x1_ref, ya_ref, yb_ref, g2_ref, gain_ref, o_ref):
    x2 = x1_ref[...] + g2_ref[0] * (ya_ref[...] + yb_ref[...])
    ms = jnp.mean(x2 * x2, axis=-1, keepdims=True)
    o_ref[...] = x2 * lax.rsqrt(ms + RMS_EPS) * gain_ref[...]


def _final_norm(x1, y2, g2, gain, seq, tm):
    t, d = x1.shape
    per_b = seq // tm
    nt = t // tm
    row = pl.BlockSpec((tm, d), lambda i: (i, 0))
    return pl.pallas_call(
        _final_kernel,
        out_shape=jax.ShapeDtypeStruct((t, d), F32),
        grid=(nt,),
        in_specs=[row, row, pl.BlockSpec((tm, d), lambda i: (i + nt, 0)),
                  pl.BlockSpec((1, 1, d), lambda i: (i // per_b, 0, 0)),
                  pl.BlockSpec((1, d), lambda i: (0, 0))],
        out_specs=row,
        compiler_params=_params(("parallel",)),
        name="final_norm",
    )(x1, y2, y2, g2, gain.reshape(1, d))


def _route(logits):
    t = logits.shape[0]
    gl = logits[:, :N_GROUPS]
    el = logits[:, N_GROUPS:N_GROUPS + N_EXPERTS].reshape(t, N_GROUPS, EXPERTS_PER_GROUP)
    pg = jax.nn.softmax(gl, axis=-1)
    g_sel = jnp.argmax(gl, axis=-1)
    p_sel = jnp.take_along_axis(pg, g_sel[:, None], axis=1)[:, 0]
    el_g = jnp.take_along_axis(el, g_sel[:, None, None], axis=1)[:, 0]
    top_v, top_i = lax.top_k(el_g, TOP_K_IN_GROUP)
    pair_w = jax.nn.softmax(top_v, axis=-1) * p_sel[:, None]
    e_id = (g_sel[:, None] * EXPERTS_PER_GROUP + top_i).astype(jnp.int32)
    flat_e = e_id.reshape(-1)
    flat_w = pair_w.reshape(-1)
    m = flat_e.shape[0]
    order = jnp.argsort(flat_e).astype(jnp.int32)
    experts = jnp.arange(N_EXPERTS, dtype=jnp.int32)
    counts = jnp.sum((flat_e[:, None] == experts[None, :]).astype(jnp.int32), axis=0)
    starts = jnp.cumsum(counts) - counts
    padded = (counts + MOE_ROWS - 1) // MOE_ROWS * MOE_ROWS
    pends = jnp.cumsum(padded)
    pstarts = pends - padded
    n_blocks = (m + N_EXPERTS * (MOE_ROWS - 1) + MOE_ROWS - 1) // MOE_ROWS
    block_start = jnp.arange(n_blocks, dtype=jnp.int32) * MOE_ROWS
    block_e = jnp.minimum(jnp.sum((block_start[:, None] >= pends[None, :]).astype(jnp.int32),
                                  axis=1), N_EXPERTS - 1)
    blk = jnp.arange(n_blocks, dtype=jnp.int32)[:, None]
    rin = jnp.arange(MOE_ROWS, dtype=jnp.int32)[None, :]
    off = blk * MOE_ROWS + rin - pstarts[block_e][:, None]
    valid = off < counts[block_e][:, None]
    src = jnp.clip(starts[block_e][:, None] + off, 0, m - 1)
    assign = order[src]
    tok = assign // TOP_K_IN_GROUP
    gidx = jnp.where(valid, tok, 0)
    sidx = jnp.where(valid, (assign % TOP_K_IN_GROUP) * t + tok, 0)
    row_w = jnp.where(valid, flat_w[assign], 0.0)
    n_used = (pends[-1] // MOE_ROWS).astype(jnp.int32).reshape(1)
    n_valid = jnp.sum(valid.astype(jnp.int32), axis=1)
    shape3 = (n_blocks, 1, MOE_ROWS)
    return gidx.reshape(shape3), sidx.reshape(shape3), row_w, block_e, n_used, n_valid


def _pad_cols(w, n):
    return jnp.pad(w, ((0, 0), (0, n - w.shape[1])))


def _pad_rows(w, n):
    return jnp.pad(w, ((0, n - w.shape[0]), (0, 0)))


def _layer(x, mod, norm1_gain, w_in, shift_mu, w0, w_decay_up, a0, w_iclr_up, w_gate_up,
           k_k, k_a, r_k, ln_x_gain, ln_x_bias, sb_norm_gain, w_out, norm2_gain,
           w_router_group, b_router_group, w_router_expert, b_router_expert,
           w_exp_gate, w_exp_up, w_exp_down, *, rwkv_ts, tm_in, tm_out):
    bsz, seq, d = x.shape
    t = bsz * seq
    rw = w0.shape[0]
    sbw = sb_norm_gain.shape[0]
    sh1, sc1, g1, sh2, sc2, g2 = [m.reshape(bsz, 1, d) for m in jnp.split(mod, 6, axis=-1)]

    o = 3 * rw
    seg = lambda a, b: w_in[:, a:b]
    w_rwkv = jnp.concatenate([
        seg(0, o),
        _pad_cols(seg(o, o + DECAY_LORA), LANE),
        _pad_cols(seg(o + DECAY_LORA, o + DECAY_LORA + ICLR_LORA), LANE),
        _pad_cols(seg(o + DECAY_LORA + ICLR_LORA, o + DECAY_LORA + ICLR_LORA + GATE_LORA),
                  2 * LANE)], axis=1).astype(BF16)
    rcols = o + DECAY_LORA + ICLR_LORA + GATE_LORA
    w_sb = w_in[:, rcols:].astype(BF16)
    mseg = lambda a, b: shift_mu[a:b][None, :]
    mu = jnp.concatenate([
        mseg(0, o),
        _pad_cols(mseg(o, o + DECAY_LORA), LANE),
        _pad_cols(mseg(o + DECAY_LORA, o + DECAY_LORA + ICLR_LORA), LANE),
        _pad_cols(mseg(o + DECAY_LORA + ICLR_LORA, rcols), 2 * LANE)], axis=1)

    x2 = x.reshape(t, d)
    p_rwkv = _normmod_matmul(x2, norm1_gain, sc1, sh1, w_rwkv, F32, seq, tm_in, 512)
    qkv = _normmod_matmul(x2, norm1_gain, sc1, sh1, w_sb, BF16, seq, tm_in, 512)

    vecs = jnp.stack([w0, a0, k_k, k_a, r_k.reshape(-1), ln_x_gain, ln_x_bias,
                      jnp.zeros_like(w0)])
    y_a = _rwkv_time_mix(p_rwkv.reshape(bsz, seq, -1), mu, vecs,
                         _pad_rows(w_decay_up, LANE), _pad_rows(w_iclr_up, LANE),
                         _pad_rows(w_gate_up, 2 * LANE), ts=rwkv_ts)
    y_b = _sb_attention(qkv.reshape(bsz, seq, -1), sb_norm_gain, sbw)

    w_r = _pad_cols(jnp.concatenate([w_router_group, w_router_expert], axis=1), LANE)
    b_r = _pad_cols(jnp.concatenate([b_router_group, b_router_expert])[None, :], LANE)
    w_o = w_out.astype(BF16)
    x1, h2, logits = _outproj_norm_router(
        x2, y_a.reshape(t, rw), y_b.reshape(t, sbw), w_o[:rw], w_o[rw:], g1, norm2_gain,
        sc2, sh2, w_r, b_r, seq, tm_out)

    gidx, sidx, row_w, block_e, n_used, n_valid = _route(logits)
    y2 = _moe_experts(h2, gidx, sidx, row_w, block_e, n_used, n_valid,
                      w_exp_gate, w_exp_up, w_exp_down)
    return x1, y2, g2


def kernel(x, c, w_ada, b_ada, norm1_gain, w_in, shift_mu, w0, w_decay_up, a0, w_iclr_up, w_gate_up, k_k, k_a, r_k, ln_x_gain, ln_x_bias, sb_norm_gain, w_out, norm2_gain, w_router_group, b_router_group, w_router_expert, b_router_expert, w_exp_gate, w_exp_up, w_exp_down, final_norm_gain):
    bsz, seq, d = x.shape
    assert w_ada.shape[0] == 1, "the final norm is fused into the single layer's last kernel"
    l = 0
    tiles = dict(rwkv_ts=min(256, seq), tm_in=min(1024, seq), tm_out=min(256, seq))
    mod = _ada_mod(c, w_ada[l], b_ada[l])
    x1, y2, g2 = _layer(
        x, mod, norm1_gain[l], w_in[l], shift_mu[l], w0[l], w_decay_up[l], a0[l],
        w_iclr_up[l], w_gate_up[l], k_k[l], k_a[l], r_k[l], ln_x_gain[l], ln_x_bias[l],
        sb_norm_gain[l], w_out[l], norm2_gain[l], w_router_group[l], b_router_group[l],
        w_router_expert[l], b_router_expert[l], w_exp_gate[l], w_exp_up[l],
        w_exp_down[l], **tiles)
    out = _final_norm(x1, y2, g2, final_norm_gain, seq, min(512, seq))
    return out.reshape(bsz, seq, d)
```

```python
import functools

import jax
import jax.numpy as jnp
import numpy as np
from jax import lax
from jax.experimental import pallas as pl
from jax.experimental.pallas import tpu as pltpu

F32 = jnp.float32
BF16 = jnp.bfloat16

RMS_EPS = 1e-6
GN_EPS = 64e-5
L2_EPS = 1e-12

HEAD_DIM = 64
PAIR = 2 * HEAD_DIM
RWKV_CHUNK = 64
SB_BLOCK = 128
N_GROUPS = 8
EXPERTS_PER_GROUP = 8
N_EXPERTS = N_GROUPS * EXPERTS_PER_GROUP
TOP_K_IN_GROUP = 2
MOE_BLOCK = 128
DECAY_LORA = 64
ICLR_LORA = 64
GATE_LORA = 160
LANE = 128
MXU_TILE = 256
VMEM_LIMIT = 48 * 1024 * 1024
SB_UNDERFLOW_LOG = -104.0


def _dot(a, b):
    return lax.dot_general(a, b, (((1,), (0,)), ((), ())), preferred_element_type=F32)


def _dot_nt(a, b):
    return lax.dot_general(a, b, (((1,), (1,)), ((), ())), preferred_element_type=F32)


def _split2(x):
    hi = x.astype(BF16)
    lo = (x - hi.astype(F32)).astype(BF16)
    return hi, lo


def _mm(a, b, passes, nt=False):
    d = _dot_nt if nt else _dot
    if passes == 1:
        return d(a.astype(BF16), b.astype(BF16))
    ah, al = _split2(a)
    bh, bl = _split2(b)
    return d(ah, bh) + d(ah, bl) + d(al, bh)


def _mm_exact_rhs(a, b_exact):
    hi, lo = _split2(a)
    return _dot(hi, b_exact) + _dot(lo, b_exact)


def _mm_exact_lhs(a_exact, b):
    hi, lo = _split2(b)
    return _dot(a_exact, hi) + _dot(a_exact, lo)


def _group_sum(x, e_blk):
    w = e_blk.shape[0]
    return jnp.concatenate([_mm_exact_rhs(x[:, g * w:(g + 1) * w], e_blk)
                            for g in range(x.shape[1] // w)], axis=1)


def _sigmoid(x):
    return 1.0 / (1.0 + jnp.exp(-x))


def _softplus(x):
    return jnp.maximum(x, 0.0) + jnp.log(1.0 + jnp.exp(-jnp.abs(x)))


def _params(sem, vmem=VMEM_LIMIT):
    return pltpu.CompilerParams(dimension_semantics=sem, vmem_limit_bytes=vmem)


def _ada_kernel(c_ref, w_ref, b_ref, o_ref):
    c = c_ref[...]
    s = c * _sigmoid(c)
    o_ref[...] = _mm(s, w_ref[...], 3) + b_ref[...]


def _ada_mod(c, w, b):
    bsz, d = c.shape
    n = w.shape[1]
    rows = 8
    cp = jnp.zeros((rows, d), F32).at[:bsz].set(c)
    tn = 1024
    out = pl.pallas_call(
        _ada_kernel,
        out_shape=jax.ShapeDtypeStruct((rows, n), F32),
        grid=(n // tn,),
        in_specs=[pl.BlockSpec((rows, d), lambda j: (0, 0)),
                  pl.BlockSpec((d, tn), lambda j: (0, j)),
                  pl.BlockSpec((1, tn), lambda j: (0, j))],
        out_specs=pl.BlockSpec((rows, tn), lambda j: (0, j)),
        compiler_params=_params(("arbitrary",)),
        name="ada_mod",
    )(cp, w, b.reshape(1, n))
    return out[:bsz]


def _normmod_mm_kernel(x_ref, gain_ref, sc_ref, sh_ref, w_ref, o_ref, h_scr):
    @pl.when(pl.program_id(1) == 0)
    def _():
        xf = x_ref[...]
        ms = jnp.mean(xf * xf, axis=-1, keepdims=True)
        y = xf * lax.rsqrt(ms + RMS_EPS) * gain_ref[...]
        h_scr[...] = (y * (1.0 + sc_ref[0]) + sh_ref[0]).astype(BF16)

    o_ref[...] = _dot(h_scr[...], w_ref[...]).astype(o_ref.dtype)


def _normmod_matmul(x2, gain, sc, sh, w, out_dtype, seq, tm, tn):
    t, d = x2.shape
    n = w.shape[1]
    per_b = seq // tm
    return pl.pallas_call(
        _normmod_mm_kernel,
        out_shape=jax.ShapeDtypeStruct((t, n), out_dtype),
        grid=(t // tm, n // tn),
        in_specs=[pl.BlockSpec((tm, d), lambda i, j: (i, 0)),
                  pl.BlockSpec((1, d), lambda i, j: (0, 0)),
                  pl.BlockSpec((1, 1, d), lambda i, j: (i // per_b, 0, 0)),
                  pl.BlockSpec((1, 1, d), lambda i, j: (i // per_b, 0, 0)),
                  pl.BlockSpec((d, tn), lambda i, j: (0, j))],
        out_specs=pl.BlockSpec((tm, tn), lambda i, j: (i, j)),
        scratch_shapes=[pltpu.VMEM((tm, d), BF16)],
        compiler_params=_params(("parallel", "arbitrary")),
        name="normmod_proj",
    )(x2, gain.reshape(1, d), sc, sh, w)


_V_W0, _V_A0, _V_KK, _V_KA, _V_RK, _V_LNG, _V_LNB = range(7)
_M_STRICT, _M_INCL, _M_BD8, _M_OFF8, _M_OFF16, _M_OFF32, _M_EYE = range(7)

RWKV_PASSES = 1


def _rwkv_masks():
    i = np.arange(PAIR)[:, None]
    j = np.arange(PAIR)[None, :]
    strict = (j < i)
    incl = (j <= i)
    bd8 = (i // 8 == j // 8)
    def off(b):
        return (i // (2 * b) == j // (2 * b)) & (i // b > j // b)
    eye = (i == j)
    m = np.stack([strict, incl, bd8 & strict, off(8), off(16), off(32), eye]).astype(np.float32)
    return jnp.asarray(m)


def _tri_inverse(lmats, masks_ref, passes):
    eye = masks_ref[_M_EYE]
    bd8 = masks_ref[_M_BD8]
    mm = lambda a, b: _mm(a, b, passes)
    dblk = [l * bd8 for l in lmats]
    x = [eye + d for d in dblk]
    p = [mm(d, d) for d in dblk]
    x = [xi + mm(pi, xi) for xi, pi in zip(x, p)]
    p = [mm(pi, pi) for pi in p]
    x = [xi + mm(pi, xi) for xi, pi in zip(x, p)]
    for plane in (_M_OFF8, _M_OFF16, _M_OFF32):
        mask = masks_ref[plane]
        t = [mm(xi, l * mask) for xi, l in zip(x, lmats)]
        x = [xi + mm(ti, xi) for xi, ti in zip(x, t)]
    return x


def _rwkv_kernel(p_ref, mu_ref, vec_ref, wd_ref, wa_ref, wg_ref, e_ref, tri_ref,
                 masks_ref, o_ref,
                 rt_s, at_s, bt_s, kt_s, v_s, bd_s, kd_s, gc_s, y_s, g_s, bon_s,
                 state_s, carry_s, *, ts, n_pairs, passes):
    s_idx = pl.program_id(1)

    @pl.when(s_idx == 0)
    def _():
        state_s[...] = jnp.zeros_like(state_s)
        carry_s[...] = jnp.zeros_like(carry_s)

    width = n_pairs * PAIR
    pt = p_ref[0]
    row = lax.broadcasted_iota(jnp.int32, pt.shape, 0)
    prev = jnp.where(row == 0, carry_s[...], pltpu.roll(pt, 1, 0))
    carry_s[...] = pt[ts - 1:ts, :]
    pm = pt + (prev - pt) * mu_ref[...]

    r = pm[:, 0:width]
    k = pm[:, width:2 * width]
    v = pm[:, 2 * width:3 * width]
    o3 = 3 * width
    xw = pm[:, o3:o3 + LANE]
    xa = pm[:, o3 + LANE:o3 + 2 * LANE]
    xg = pm[:, o3 + 2 * LANE:o3 + 4 * LANE]

    vec = lambda i: vec_ref[i:i + 1, :]
    e_mat = e_ref[...]
    wlog = -_softplus(-(vec(_V_W0) + _mm(jnp.tanh(xw), wd_ref[...], 3))) - 0.5
    logw = -jnp.exp(wlog)
    a = _sigmoid(vec(_V_A0) + _mm(xa, wa_ref[...], 1))
    g_s[...] = _mm(_sigmoid(xg), wg_ref[...], 1)
    kk = k * vec(_V_KK)
    ss = _group_sum(kk * kk, e_mat)
    kk = kk / jnp.maximum(jnp.sqrt(ss), L2_EPS)
    kp = k * (1.0 + (a - 1.0) * vec(_V_KA))
    ib = kk * a
    bon_s[...] = _group_sum(r * kp * vec(_V_RK), e_mat) * v
    cum = _mm_exact_lhs(tri_ref[...], logw)
    tot = jnp.concatenate(
        [jnp.broadcast_to(cum[c * RWKV_CHUNK + RWKV_CHUNK - 1:(c + 1) * RWKV_CHUNK, :],
                          (RWKV_CHUNK, width)) for c in range(ts // RWKV_CHUNK)], axis=0)
    rt_s[...] = r * jnp.exp(cum)
    at_s[...] = -kk * jnp.exp(cum - logw)
    dec_in = jnp.exp(-cum)
    bt_s[...] = ib * dec_in
    kt_s[...] = kp * dec_in
    v_s[...] = v
    dec_out = jnp.exp(tot - cum)
    bd_s[...] = ib * dec_out
    kd_s[...] = kp * dec_out
    gc_s[...] = jnp.exp(tot)

    lane = lax.broadcasted_iota(jnp.int32, (RWKV_CHUNK, PAIR), 1)
    m0 = (lane < HEAD_DIM).astype(F32)
    m1 = 1.0 - m0
    strict = masks_ref[_M_STRICT]
    incl = masks_ref[_M_INCL]

    def chunk_body(c, carry):
        row0 = pl.multiple_of(c * RWKV_CHUNK, RWKV_CHUNK)
        rows = pl.ds(row0, RWKV_CHUNK)
        pairs = range(n_pairs)
        lanes = [slice(j * PAIR, (j + 1) * PAIR) for j in pairs]
        mm = lambda x, w: _mm(x, w, passes)
        mm_nt = lambda x, w: _mm(x, w, passes, nt=True)

        def stacked(ref):
            blks = [ref[rows, lanes[j]] for j in pairs]
            return [jnp.concatenate([b * m0, b * m1], axis=0) for b in blks]

        rh, ah, bh, kh = stacked(rt_s), stacked(at_s), stacked(bt_s), stacked(kt_s)
        vh, bdh, kdh = stacked(v_s), stacked(bd_s), stacked(kd_s)
        st = [state_s[j] for j in pairs]
        bk = [jnp.concatenate([bh[j], kh[j]], axis=0) for j in pairs]
        ga = [mm_nt(ah[j], bk[j]) for j in pairs]
        gr = [mm_nt(rh[j], bk[j]) for j in pairs]
        tinv = _tri_inverse([ga[j][:, :PAIR] * strict for j in pairs], masks_ref, passes)
        x0 = [mm_nt(ah[j], st[j]) + mm(ga[j][:, PAIR:] * strict, vh[j]) for j in pairs]
        u = [mm(tinv[j], x0[j]) for j in pairs]
        y = [mm_nt(rh[j], st[j]) + mm(gr[j][:, :PAIR] * incl, u[j])
             + mm(gr[j][:, PAIR:] * incl, vh[j]) for j in pairs]
        for j in pairs:
            zt = jnp.concatenate([u[j], vh[j]], axis=0).T
            bkd = jnp.concatenate([bdh[j], kdh[j]], axis=0)
            gc = gc_s[pl.ds(row0, 1), lanes[j]]
            state_s[j] = st[j] * gc + mm(zt, bkd)
            y_s[rows, lanes[j]] = y[j][:RWKV_CHUNK] + y[j][RWKV_CHUNK:]
        return carry

    lax.fori_loop(0, ts // RWKV_CHUNK, chunk_body, 0)

    y = y_s[...]
    inv_n = 1.0 / HEAD_DIM
    mean = _group_sum(y, e_mat) * inv_n
    dlt = y - mean
    var = _group_sum(dlt * dlt, e_mat) * inv_n
    yn = dlt * lax.rsqrt(var + GN_EPS) * vec(_V_LNG) + vec(_V_LNB)
    o_ref[0] = ((yn + bon_s[...]) * g_s[...]).astype(o_ref.dtype)


def _rwkv_time_mix(p, mu, vecs, wd, wa, wg, *, ts, passes=RWKV_PASSES):
    bsz, seq, cols = p.shape
    width = vecs.shape[1]
    n_pairs = width // PAIR
    heads = np.arange(MXU_TILE) // HEAD_DIM
    e_mat = jnp.asarray((heads[:, None] == heads[None, :]).astype(np.float32), dtype=BF16)
    tok = np.arange(ts)
    same = (tok[:, None] // RWKV_CHUNK) == (tok[None, :] // RWKV_CHUNK)
    tri = jnp.asarray((same & (tok[None, :] <= tok[:, None])).astype(np.float32), dtype=BF16)
    masks = _rwkv_masks()
    full = lambda shape: pl.BlockSpec(shape, lambda b, s: (0,) * len(shape))
    big = lambda: pltpu.VMEM((ts, width), F32)
    kern = functools.partial(_rwkv_kernel, ts=ts, n_pairs=n_pairs, passes=passes)
    return pl.pallas_call(
        kern,
        out_shape=jax.ShapeDtypeStruct((bsz, seq, width), BF16),
        grid=(bsz, seq // ts),
        in_specs=[pl.BlockSpec((1, ts, cols), lambda b, s: (b, s, 0)),
                  full((1, cols)), full(vecs.shape), full(wd.shape), full(wa.shape),
                  full(wg.shape), full(e_mat.shape), full(tri.shape), full(masks.shape)],
        out_specs=pl.BlockSpec((1, ts, width), lambda b, s: (b, s, 0)),
        scratch_shapes=[big() for _ in range(11)]
        + [pltpu.VMEM((n_pairs, PAIR, PAIR), F32), pltpu.VMEM((1, cols), F32)],
        compiler_params=_params(("parallel", "arbitrary")),
        name="rwkv7_scan",
    )(p, mu, vecs, wd, wa, wg, e_mat, tri, masks)


SB_WINDOW = 6
SB_TAIL = 2
SB_Q_PER_STEP = 4


def _sb_kernel(q_ref, k_ref, v_ref, gain_ref, cum_ref, e2_ref, o_ref, *, q_per_step):
    lane = lax.broadcasted_iota(jnp.int32, (SB_BLOCK, PAIR), 1)
    h0 = lane < HEAD_DIM
    zero = jnp.zeros((), BF16)
    qi = lax.broadcasted_iota(jnp.int32, (SB_BLOCK, 2 * SB_BLOCK), 0)
    ki = lax.broadcasted_iota(jnp.int32, (SB_BLOCK, 2 * SB_BLOCK), 1) % SB_BLOCK
    causal = ki < qi
    cum_mat = cum_ref[...]
    scale = jnp.asarray(HEAD_DIM ** -0.5, BF16)
    width2 = 2 * SB_BLOCK

    def sweep(q, qb, offsets, acc, rest, first_is_diag):
        khats, vhats, valid = [], [], []
        for off in offsets:
            kb = qb - off
            valid.append(kb >= 0)
            rows = pl.ds(pl.multiple_of(jnp.maximum(kb, 0) * SB_BLOCK, SB_BLOCK), SB_BLOCK)
            kblk = k_ref[0, rows, :]
            vblk = v_ref[0, rows, :]
            khats += [jnp.where(h0, kblk, zero), jnp.where(h0, zero, kblk)]
            vhats += [jnp.where(h0, vblk, zero), jnp.where(h0, zero, vblk)]
        z_all = _dot_nt(q, jnp.concatenate(khats, axis=0))
        nblk = range(len(offsets))
        keep = [causal if first_is_diag else None]
        keep += [jnp.broadcast_to(valid[n], causal.shape) for n in nblk[1:]]
        zs = [z_all[:, n * width2:(n + 1) * width2] for n in nblk]
        log_beta = [jnp.minimum(z, 0.0) - jnp.log(1.0 + jnp.exp(-jnp.abs(z))) for z in zs]
        log_1m = [lb - z for lb, z in zip(log_beta, zs)]
        log_1m = [l if m is None else jnp.where(m, l, 0.0) for l, m in zip(log_1m, keep)]
        splits = [_split2(l) for l in log_1m]
        heads = [slice(h * SB_BLOCK, (h + 1) * SB_BLOCK) for h in range(2)]
        parts = [[_dot(jnp.concatenate([hi[:, sl], lo[:, sl]], axis=1), cum_mat) for sl in heads]
                 for hi, lo in splits]
        ws = []
        for n in nblk:
            between = jnp.concatenate([parts[n][0][:, :SB_BLOCK], parts[n][1][:, :SB_BLOCK]], axis=1)
            total = jnp.concatenate([parts[n][0][:, SB_BLOCK:], parts[n][1][:, SB_BLOCK:]], axis=1)
            w = jnp.exp(log_beta[n] + between + rest)
            if keep[n] is not None:
                w = jnp.where(keep[n], w, 0.0)
            ws.append(w.astype(BF16))
            rest = rest + total
        acc = acc + _dot(jnp.concatenate(ws, axis=1), jnp.concatenate(vhats, axis=0))
        return acc, rest

    def q_block(sub, carry):
        qb = pl.program_id(2) * q_per_step + sub
        qrows = pl.ds(pl.multiple_of(sub * SB_BLOCK, SB_BLOCK), SB_BLOCK)
        q = q_ref[0, qrows, :] * scale
        acc = jnp.zeros((SB_BLOCK, PAIR), F32)
        rest = jnp.zeros((SB_BLOCK, width2), F32)
        acc, rest = sweep(q, qb, tuple(range(SB_WINDOW)), acc, rest, True)

        def cond(c):
            i, _, _, live = c
            return jnp.logical_and(i <= qb, live > SB_UNDERFLOW_LOG)

        def body(c):
            i, acc, rest, _ = c
            acc, rest = sweep(q, qb - i, tuple(range(SB_TAIL)), acc, rest, False)
            return i + SB_TAIL, acc, rest, jnp.max(rest)

        _, acc, _, _ = lax.while_loop(cond, body,
                                      (jnp.int32(SB_WINDOW), acc, rest, jnp.max(rest)))
        ms = _mm_exact_rhs(acc * acc, e2_ref[...]) * (1.0 / HEAD_DIM)
        o_ref[0, qrows, :] = (acc * lax.rsqrt(ms + RMS_EPS) * gain_ref[...]).astype(o_ref.dtype)
        return carry

    lax.fori_loop(0, q_per_step, q_block, 0)


def _sb_attention(qkv, gain, width):
    bsz, seq, _ = qkv.shape
    n_pairs = width // PAIR
    nq = seq // SB_BLOCK
    i = np.arange(SB_BLOCK)
    upper = (i[:, None] > i[None, :]).astype(np.float32)
    half = np.concatenate([upper, np.ones_like(upper)], axis=1)
    cum_mat = jnp.asarray(np.concatenate([half, half], axis=0), dtype=BF16)
    hd = np.arange(PAIR) // HEAD_DIM
    e2 = jnp.asarray((hd[:, None] == hd[None, :]).astype(np.float32), dtype=BF16)
    qps = min(SB_Q_PER_STEP, nq)
    qrows = qps * SB_BLOCK
    return pl.pallas_call(
        functools.partial(_sb_kernel, q_per_step=qps),
        out_shape=jax.ShapeDtypeStruct((bsz, seq, width), BF16),
        grid=(bsz, n_pairs, nq // qps),
        in_specs=[pl.BlockSpec((1, qrows, PAIR), lambda b, j, t: (b, t, j)),
                  pl.BlockSpec((1, seq, PAIR), lambda b, j, t: (b, 0, n_pairs + j)),
                  pl.BlockSpec((1, seq, PAIR), lambda b, j, t: (b, 0, 2 * n_pairs + j)),
                  pl.BlockSpec((1, PAIR), lambda b, j, t: (0, j)),
                  pl.BlockSpec(cum_mat.shape, lambda b, j, t: (0, 0)),
                  pl.BlockSpec(e2.shape, lambda b, j, t: (0, 0))],
        out_specs=pl.BlockSpec((1, qrows, PAIR), lambda b, j, t: (b, t, j)),
        compiler_params=_params(("parallel", "parallel", "arbitrary")),
        name="stickbreak_attn",
    )(qkv, qkv, qkv, gain.reshape(1, width), cum_mat, e2)


def _outproj_kernel(x_ref, ya_ref, yb_ref, wa_ref, wb_ref, g1_ref, gain_ref, sc_ref, sh_ref,
                    wr_ref, br_ref, x1_ref, h_ref, lg_ref):
    mix = _dot(ya_ref[...], wa_ref[...]) + _dot(yb_ref[...], wb_ref[...])
    x1 = x_ref[...] + g1_ref[0] * mix
    x1_ref[...] = x1
    ms = jnp.mean(x1 * x1, axis=-1, keepdims=True)
    h = x1 * lax.rsqrt(ms + RMS_EPS) * gain_ref[...] * (1.0 + sc_ref[0]) + sh_ref[0]
    h_ref[...] = h.astype(h_ref.dtype)
    lg_ref[...] = _mm(h, wr_ref[...], 3) + br_ref[...]


def _outproj_norm_router(x2, ya, yb, w_a, w_b, g1, gain, sc, sh, w_r, b_r, seq, tm):
    t, d = x2.shape
    half = ya.shape[1]
    nr = w_r.shape[1]
    per_b = seq // tm
    row = lambda w: pl.BlockSpec((tm, w), lambda i: (i, 0))
    full = lambda shape: pl.BlockSpec(shape, lambda i: (0,) * len(shape))
    per_batch = pl.BlockSpec((1, 1, d), lambda i: (i // per_b, 0, 0))
    return pl.pallas_call(
        _outproj_kernel,
        out_shape=(jax.ShapeDtypeStruct((t, d), F32), jax.ShapeDtypeStruct((t, d), F32),
                   jax.ShapeDtypeStruct((t, nr), F32)),
        grid=(t // tm,),
        in_specs=[row(d), row(half), row(half), full((half, d)), full((half, d)), per_batch,
                  full((1, d)), per_batch, per_batch, full((d, nr)), full((1, nr))],
        out_specs=(row(d), row(d), row(nr)),
        compiler_params=_params(("parallel",)),
        name="outproj_norm_router",
    )(x2, ya, yb, w_a, w_b, g1, gain.reshape(1, d), sc, sh, w_r, b_r)


MOE_ROWS = 256
MOE_VMEM_LIMIT = 58 * 1024 * 1024
_DMA_UNROLL = 8


def _moe_kernel(be_ref, nxt_ref, nused_ref, nvalid_ref, gcur_ref, gnext_ref, sidx_ref, wt_ref,
                h_hbm, wg_hbm, wu_hbm, wd_hbm, y_hbm,
                xbuf, ybuf, wg_f, wu_f, wd_f, wg_b, wu_b, wd_b, gsem, ssem, wsem):
    i = pl.program_id(0)
    n_used = nused_ref[0]
    last = pl.num_programs(0) - 1
    slot = i % 2
    de = wg_b.shape[1]
    d = wd_b.shape[1]

    def weight_copies(e):
        pairs = ((wg_hbm, wg_f), (wu_hbm, wu_f), (wd_hbm, wd_f))
        return [pltpu.make_async_copy(src.at[e], dst, wsem.at[k])
                for k, (src, dst) in enumerate(pairs)]

    def gather(idx_ref, r, s):
        return pltpu.make_async_copy(h_hbm.at[pl.ds(idx_ref[0, 0, r], 1), :],
                                     xbuf.at[s, pl.ds(r, 1), :], gsem.at[s])

    def scatter(idx_ref, r, s):
        return pltpu.make_async_copy(ybuf.at[s, pl.ds(r, 1), :],
                                     y_hbm.at[pl.ds(idx_ref[0, 0, r], 1), :], ssem.at[s])

    def for_rows(fn, n=None):
        def body(r, c):
            fn(r)
            return c
        if n is None:
            lax.fori_loop(0, MOE_ROWS, body, 0, unroll=_DMA_UNROLL)
        else:
            lax.fori_loop(0, n, body, 0)

    def scatter_wait(step):
        for_rows(lambda r: scatter(sidx_ref, r, step % 2).wait(), nvalid_ref[step])

    @pl.when(jnp.logical_and(i == 0, n_used > 0))
    def _():
        for_rows(lambda r: gather(gcur_ref, r, 0).start())
        for c in weight_copies(be_ref[0]):
            c.start()

    @pl.when(i < n_used)
    def _():
        e = be_ref[i]

        @pl.when(jnp.logical_or(i == 0, e != be_ref[jnp.maximum(i - 1, 0)]))
        def _():
            for c in weight_copies(e):
                c.wait()
            wg_b[...] = wg_f[...].astype(BF16)
            wu_b[...] = wu_f[...].astype(BF16)
            wd_b[...] = wd_f[...].astype(BF16)

            @pl.when(nxt_ref[i] != e)
            def _():
                for c in weight_copies(nxt_ref[i]):
                    c.start()

        for_rows(lambda r: gather(gcur_ref, r, slot).wait())

        @pl.when(i >= 2)
        def _():
            scatter_wait(i - 2)

        n_chunks = 2 * (de // MXU_TILE) + d // MXU_TILE
        per_chunk = MOE_ROWS // n_chunks
        issued = [0]

        def issue_gathers():
            for r in range(issued[0], issued[0] + per_chunk):
                gather(gnext_ref, r, 1 - slot).start()
            issued[0] += per_chunk

        xb = xbuf[slot].astype(BF16)
        hg, hu = [], []
        for n in range(de // MXU_TILE):
            cols = slice(n * MXU_TILE, (n + 1) * MXU_TILE)
            hg.append(_dot(xb, wg_b[:, cols]))
            issue_gathers()
            hu.append(_dot(xb, wu_b[:, cols]))
            issue_gathers()
        hg = jnp.concatenate(hg, axis=1)
        hid = (hg * _sigmoid(hg) * jnp.concatenate(hu, axis=1)).astype(BF16)
        wt = wt_ref[...]
        for n in range(d // MXU_TILE):
            cols = slice(n * MXU_TILE, (n + 1) * MXU_TILE)
            ybuf[slot, :, cols] = _dot(hid, wd_b[:, cols]) * wt
            issue_gathers()
        assert issued[0] == MOE_ROWS
        for_rows(lambda r: scatter(sidx_ref, r, slot).start(), nvalid_ref[i])

        @pl.when(i + 1 >= n_used)
        def _():
            for_rows(lambda r: gather(gnext_ref, r, 1 - slot).wait())

    @pl.when(i == last)
    def _():
        for back in (2, 1):
            @pl.when(n_used >= back)
            def _():
                scatter_wait(n_used - back)


def _moe_experts(h2, gidx, sidx, row_w, block_e, next_e, n_used, n_valid, w_gate, w_up, w_down):
    t, d = h2.shape
    de = w_gate.shape[2]
    n_blocks = gidx.shape[0]
    assert MOE_ROWS % (2 * (de // MXU_TILE) + d // MXU_TILE) == 0
    idx_spec = lambda f: pl.BlockSpec((1, 1, MOE_ROWS), f, memory_space=pltpu.SMEM)
    hbm = pl.BlockSpec(memory_space=pl.ANY)
    grid_spec = pltpu.PrefetchScalarGridSpec(
        num_scalar_prefetch=4,
        grid=(n_blocks,),
        in_specs=[idx_spec(lambda i, *_: (i, 0, 0)),
                  idx_spec(lambda i, *_: (jnp.minimum(i + 1, n_blocks - 1), 0, 0)),
                  idx_spec(lambda i, *_: (i, 0, 0)),
                  pl.BlockSpec((MOE_ROWS, 1), lambda i, *_: (i, 0)),
                  hbm, hbm, hbm, hbm],
        out_specs=hbm,
        scratch_shapes=[pltpu.VMEM((2, MOE_ROWS, d), F32), pltpu.VMEM((2, MOE_ROWS, d), F32),
                        pltpu.VMEM((d, de), F32), pltpu.VMEM((d, de), F32),
                        pltpu.VMEM((de, d), F32),
                        pltpu.VMEM((d, de), BF16), pltpu.VMEM((d, de), BF16),
                        pltpu.VMEM((de, d), BF16),
                        pltpu.SemaphoreType.DMA((2,)), pltpu.SemaphoreType.DMA((2,)),
                        pltpu.SemaphoreType.DMA((3,))],
    )
    return pl.pallas_call(
        _moe_kernel,
        out_shape=jax.ShapeDtypeStruct((TOP_K_IN_GROUP * t, d), F32),
        grid_spec=grid_spec,
        compiler_params=_params(("arbitrary",), MOE_VMEM_LIMIT),
        name="moe_experts",
    )(block_e, next_e, n_used, n_valid, gidx, gidx, sidx, row_w.reshape(-1, 1),
      h2, w_gate, w_up, w_down)


def _final_kernel(x1_ref, ya_ref, yb_ref, g2_ref, gain_ref, o_ref):
    x2 = x1_ref[...] + g2_ref[0] * (ya_ref[...] + yb_ref[...])
    ms = jnp.mean(x2 * x2, axis=-1, keepdims=True)
    o_ref[...] = x2 * lax.rsqrt(ms + RMS_EPS) * gain_ref[...]


def _final_norm(x1, y2, g2, gain, seq, tm):
    t, d = x1.shape
    per_b = seq // tm
    nt = t // tm
    row = pl.BlockSpec((tm, d), lambda i: (i, 0))
    return pl.pallas_call(
        _final_kernel,
        out_shape=jax.ShapeDtypeStruct((t, d), F32),
        grid=(nt,),
        in_specs=[row, row, pl.BlockSpec((tm, d), lambda i: (i + nt, 0)),
                  pl.BlockSpec((1, 1, d), lambda i: (i // per_b, 0, 0)),
                  pl.BlockSpec((1, d), lambda i: (0, 0))],
        out_specs=row,
        compiler_params=_params(("parallel",)),
        name="final_norm",
    )(x1, y2, y2, g2, gain.reshape(1, d))


def _route(logits):
    t = logits.shape[0]
    gl = logits[:, :N_GROUPS]
    el = logits[:, N_GROUPS:N_GROUPS + N_EXPERTS].reshape(t, N_GROUPS, EXPERTS_PER_GROUP)
    pg = jax.nn.softmax(gl, axis=-1)
    g_sel = jnp.argmax(gl, axis=-1)
    p_sel = jnp.take_along_axis(pg, g_sel[:, None], axis=1)[:, 0]
    el_g = jnp.take_along_axis(el, g_sel[:, None, None], axis=1)[:, 0]
    top_v, top_i = lax.top_k(el_g, TOP_K_IN_GROUP)
    pair_w = jax.nn.softmax(top_v, axis=-1) * p_sel[:, None]
    e_id = (g_sel[:, None] * EXPERTS_PER_GROUP + top_i).astype(jnp.int32)
    flat_e = e_id.reshape(-1)
    flat_w = pair_w.reshape(-1)
    m = flat_e.shape[0]
    order = jnp.argsort(flat_e).astype(jnp.int32)
    experts = jnp.arange(N_EXPERTS, dtype=jnp.int32)
    counts = jnp.sum((flat_e[:, None] == experts[None, :]).astype(jnp.int32), axis=0)
    starts = jnp.cumsum(counts) - counts
    padded = (counts + MOE_ROWS - 1) // MOE_ROWS * MOE_ROWS
    pends = jnp.cumsum(padded)
    pstarts = pends - padded
    n_blocks = (m + N_EXPERTS * (MOE_ROWS - 1) + MOE_ROWS - 1) // MOE_ROWS
    block_start = jnp.arange(n_blocks, dtype=jnp.int32) * MOE_ROWS
    block_e = jnp.minimum(jnp.sum((block_start[:, None] >= pends[None, :]).astype(jnp.int32),
                                  axis=1), N_EXPERTS - 1)
    blk = jnp.arange(n_blocks, dtype=jnp.int32)[:, None]
    rin = jnp.arange(MOE_ROWS, dtype=jnp.int32)[None, :]
    off = blk * MOE_ROWS + rin - pstarts[block_e][:, None]
    valid = off < counts[block_e][:, None]
    src = jnp.clip(starts[block_e][:, None] + off, 0, m - 1)
    assign = order[src]
    tok = assign // TOP_K_IN_GROUP
    gidx = jnp.where(valid, tok, 0)
    sidx = jnp.where(valid, (assign % TOP_K_IN_GROUP) * t + tok, 0)
    row_w = jnp.where(valid, flat_w[assign], 0.0)
    n_used = (pends[-1] // MOE_ROWS).astype(jnp.int32).reshape(1)
    n_valid = jnp.sum(valid.astype(jnp.int32), axis=1)
    later = (experts[None, :] > experts[:, None]) & (counts[None, :] > 0)
    next_expert = jnp.min(jnp.where(later, experts[None, :], N_EXPERTS), axis=1)
    next_expert = jnp.where(next_expert == N_EXPERTS, experts, next_expert)
    shape3 = (n_blocks, 1, MOE_ROWS)
    return (gidx.reshape(shape3), sidx.reshape(shape3), row_w, block_e, next_expert[block_e],
            n_used, n_valid)


def _pad_cols(w, n):
    return jnp.pad(w, ((0, 0), (0, n - w.shape[1])))


def _pad_rows(w, n):
    return jnp.pad(w, ((0, n - w.shape[0]), (0, 0)))


def _layer(x, mod, norm1_gain, w_in, shift_mu, w0, w_decay_up, a0, w_iclr_up, w_gate_up,
           k_k, k_a, r_k, ln_x_gain, ln_x_bias, sb_norm_gain, w_out, norm2_gain,
           w_router_group, b_router_group, w_router_expert, b_router_expert,
           w_exp_gate, w_exp_up, w_exp_down, *, rwkv_ts, tm_in, tm_out):
    bsz, seq, d = x.shape
    t = bsz * seq
    rw = w0.shape[0]
    sbw = sb_norm_gain.shape[0]
    sh1, sc1, g1, sh2, sc2, g2 = [m.reshape(bsz, 1, d) for m in jnp.split(mod, 6, axis=-1)]

    o = 3 * rw
    seg = lambda a, b: w_in[:, a:b]
    w_rwkv = jnp.concatenate([
        seg(0, o),
        _pad_cols(seg(o, o + DECAY_LORA), LANE),
        _pad_cols(seg(o + DECAY_LORA, o + DECAY_LORA + ICLR_LORA), LANE),
        _pad_cols(seg(o + DECAY_LORA + ICLR_LORA, o + DECAY_LORA + ICLR_LORA + GATE_LORA),
                  2 * LANE)], axis=1).astype(BF16)
    rcols = o + DECAY_LORA + ICLR_LORA + GATE_LORA
    w_sb = w_in[:, rcols:].astype(BF16)
    mseg = lambda a, b: shift_mu[a:b][None, :]
    mu = jnp.concatenate([
        mseg(0, o),
        _pad_cols(mseg(o, o + DECAY_LORA), LANE),
        _pad_cols(mseg(o + DECAY_LORA, o + DECAY_LORA + ICLR_LORA), LANE),
        _pad_cols(mseg(o + DECAY_LORA + ICLR_LORA, rcols), 2 * LANE)], axis=1)

    x2 = x.reshape(t, d)
    p_rwkv = _normmod_matmul(x2, norm1_gain, sc1, sh1, w_rwkv, F32, seq, tm_in, 512)
    qkv = _normmod_matmul(x2, norm1_gain, sc1, sh1, w_sb, BF16, seq, tm_in, 512)

    vecs = jnp.stack([w0, a0, k_k, k_a, r_k.reshape(-1), ln_x_gain, ln_x_bias,
                      jnp.zeros_like(w0)])
    y_a = _rwkv_time_mix(p_rwkv.reshape(bsz, seq, -1), mu, vecs,
                         _pad_rows(w_decay_up, LANE), _pad_rows(w_iclr_up, LANE),
                         _pad_rows(w_gate_up, 2 * LANE), ts=rwkv_ts)
    y_b = _sb_attention(qkv.reshape(bsz, seq, -1), sb_norm_gain, sbw)

    w_r = _pad_cols(jnp.concatenate([w_router_group, w_router_expert], axis=1), LANE)
    b_r = _pad_cols(jnp.concatenate([b_router_group, b_router_expert])[None, :], LANE)
    w_o = w_out.astype(BF16)
    x1, h2, logits = _outproj_norm_router(
        x2, y_a.reshape(t, rw), y_b.reshape(t, sbw), w_o[:rw], w_o[rw:], g1, norm2_gain,
        sc2, sh2, w_r, b_r, seq, tm_out)

    gidx, sidx, row_w, block_e, next_e, n_used, n_valid = _route(logits)
    y2 = _moe_experts(h2, gidx, sidx, row_w, block_e, next_e, n_used, n_valid,
                      w_exp_gate, w_exp_up, w_exp_down)
    return x1, y2, g2


def kernel(x, c, w_ada, b_ada, norm1_gain, w_in, shift_mu, w0, w_decay_up, a0, w_iclr_up, w_gate_up, k_k, k_a, r_k, ln_x_gain, ln_x_bias, sb_norm_gain, w_out, norm2_gain, w_router_group, b_router_group, w_router_expert, b_router_expert, w_exp_gate, w_exp_up, w_exp_down, final_norm_gain):
    bsz, seq, d = x.shape
    assert w_ada.shape[0] == 1, "the final norm is fused into the single layer's last kernel"
    l = 0
    tiles = dict(rwkv_ts=min(256, seq), tm_in=min(1024, seq), tm_out=min(256, seq))
    mod = _ada_mod(c, w_ada[l], b_ada[l])
    x1, y2, g2 = _layer(
        x, mod, norm1_gain[l], w_in[l], shift_mu[l], w0[l], w_decay_up[l], a0[l],
        w_iclr_up[l], w_gate_up[l], k_k[l], k_a[l], r_k[l], ln_x_gain[l], ln_x_bias[l],
        sb_norm_gain[l], w_out[l], norm2_gain[l], w_router_group[l], b_router_group[l],
        w_router_expert[l], b_router_expert[l], w_exp_gate[l], w_exp_up[l],
        w_exp_down[l], **tiles)
    out = _final_norm(x1, y2, g2, final_norm_gain, seq, min(512, seq))
    return out.reshape(bsz, seq, d)
```

```python
import functools

import jax
import jax.numpy as jnp
import numpy as np
from jax import lax
from jax.experimental import pallas as pl
from jax.experimental.pallas import tpu as pltpu

F32 = jnp.float32
BF16 = jnp.bfloat16

RMS_EPS = 1e-6
GN_EPS = 64e-5
L2_EPS = 1e-12

HEAD_DIM = 64
PAIR = 2 * HEAD_DIM
RWKV_CHUNK = 64
SB_BLOCK = 128
N_GROUPS = 8
EXPERTS_PER_GROUP = 8
N_EXPERTS = N_GROUPS * EXPERTS_PER_GROUP
TOP_K_IN_GROUP = 2
MOE_BLOCK = 128
DECAY_LORA = 64
ICLR_LORA = 64
GATE_LORA = 160
LANE = 128
MXU_TILE = 256
VMEM_LIMIT = 48 * 1024 * 1024
SB_UNDERFLOW_LOG = -104.0


def _dot(a, b):
    return lax.dot_general(a, b, (((1,), (0,)), ((), ())), preferred_element_type=F32)


def _dot_nt(a, b):
    return lax.dot_general(a, b, (((1,), (1,)), ((), ())), preferred_element_type=F32)


def _split2(x):
    hi = x.astype(BF16)
    lo = (x - hi.astype(F32)).astype(BF16)
    return hi, lo


def _mm(a, b, passes, nt=False):
    d = _dot_nt if nt else _dot
    if passes == 1:
        return d(a.astype(BF16), b.astype(BF16))
    ah, al = _split2(a)
    bh, bl = _split2(b)
    return d(ah, bh) + d(ah, bl) + d(al, bh)


def _mm_exact_rhs(a, b_exact):
    hi, lo = _split2(a)
    return _dot(hi, b_exact) + _dot(lo, b_exact)


def _mm_exact_lhs(a_exact, b):
    hi, lo = _split2(b)
    return _dot(a_exact, hi) + _dot(a_exact, lo)


def _group_sum(x, e_blk):
    w = e_blk.shape[0]
    return jnp.concatenate([_mm_exact_rhs(x[:, g * w:(g + 1) * w], e_blk)
                            for g in range(x.shape[1] // w)], axis=1)


def _sigmoid(x):
    return 1.0 / (1.0 + jnp.exp(-x))


def _softplus(x):
    return jnp.maximum(x, 0.0) + jnp.log(1.0 + jnp.exp(-jnp.abs(x)))


def _params(sem, vmem=VMEM_LIMIT):
    return pltpu.CompilerParams(dimension_semantics=sem, vmem_limit_bytes=vmem)


def _ada_kernel(c_ref, w_ref, b_ref, o_ref):
    c = c_ref[...]
    s = c * _sigmoid(c)
    o_ref[...] = _mm(s, w_ref[...], 3) + b_ref[...]


def _ada_mod(c, w, b):
    bsz, d = c.shape
    n = w.shape[1]
    rows = 8
    cp = jnp.zeros((rows, d), F32).at[:bsz].set(c)
    tn = 1024
    out = pl.pallas_call(
        _ada_kernel,
        out_shape=jax.ShapeDtypeStruct((rows, n), F32),
        grid=(n // tn,),
        in_specs=[pl.BlockSpec((rows, d), lambda j: (0, 0)),
                  pl.BlockSpec((d, tn), lambda j: (0, j)),
                  pl.BlockSpec((1, tn), lambda j: (0, j))],
        out_specs=pl.BlockSpec((rows, tn), lambda j: (0, j)),
        compiler_params=_params(("arbitrary",)),
        name="ada_mod",
    )(cp, w, b.reshape(1, n))
    return out[:bsz]


_NORM_SLAB = 128


def _normmod_mm_kernel(x_ref, gain_ref, sc_ref, sh_ref, w_ref, o_ref, h_scr):
    @pl.when(pl.program_id(1) == 0)
    def _():
        scale = gain_ref[...] * (1.0 + sc_ref[0])
        shift = sh_ref[0]
        slab = min(_NORM_SLAB, x_ref.shape[0])

        def norm_rows(k, c):
            rows = pl.ds(pl.multiple_of(k * slab, slab), slab)
            xf = x_ref[rows, :]
            ms = jnp.mean(xf * xf, axis=-1, keepdims=True)
            h_scr[rows, :] = (xf * lax.rsqrt(ms + RMS_EPS) * scale + shift).astype(BF16)
            return c
        lax.fori_loop(0, x_ref.shape[0] // slab, norm_rows, 0, unroll=2)

    o_ref[...] = _dot(h_scr[...], w_ref[...]).astype(o_ref.dtype)


def _normmod_matmul(x2, gain, sc, sh, w, out_dtype, seq, tm, tn):
    t, d = x2.shape
    n = w.shape[1]
    per_b = seq // tm
    return pl.pallas_call(
        _normmod_mm_kernel,
        out_shape=jax.ShapeDtypeStruct((t, n), out_dtype),
        grid=(t // tm, n // tn),
        in_specs=[pl.BlockSpec((tm, d), lambda i, j: (i, 0)),
                  pl.BlockSpec((1, d), lambda i, j: (0, 0)),
                  pl.BlockSpec((1, 1, d), lambda i, j: (i // per_b, 0, 0)),
                  pl.BlockSpec((1, 1, d), lambda i, j: (i // per_b, 0, 0)),
                  pl.BlockSpec((d, tn), lambda i, j: (0, j))],
        out_specs=pl.BlockSpec((tm, tn), lambda i, j: (i, j)),
        scratch_shapes=[pltpu.VMEM((tm, d), BF16)],
        compiler_params=_params(("parallel", "arbitrary")),
        name="normmod_proj",
    )(x2, gain.reshape(1, d), sc, sh, w)


_V_W0, _V_A0, _V_KK, _V_KA, _V_RK, _V_LNG, _V_LNB = range(7)
_M_STRICT, _M_INCL, _M_BD8, _M_OFF8, _M_OFF16, _M_OFF32, _M_EYE = range(7)

RWKV_PASSES = 1


def _rwkv_masks():
    i = np.arange(PAIR)[:, None]
    j = np.arange(PAIR)[None, :]
    strict = (j < i)
    incl = (j <= i)
    bd8 = (i // 8 == j // 8)
    def off(b):
        return (i // (2 * b) == j // (2 * b)) & (i // b > j // b)
    eye = (i == j)
    m = np.stack([strict, incl, bd8 & strict, off(8), off(16), off(32), eye]).astype(np.float32)
    return jnp.asarray(m)


def _tri_inverse(lmats, masks_ref, passes):
    eye = masks_ref[_M_EYE]
    bd8 = masks_ref[_M_BD8]
    mm = lambda a, b: _mm(a, b, passes)
    dblk = [l * bd8 for l in lmats]
    x = [eye + d for d in dblk]
    p = [mm(d, d) for d in dblk]
    x = [xi + mm(pi, xi) for xi, pi in zip(x, p)]
    p = [mm(pi, pi) for pi in p]
    x = [xi + mm(pi, xi) for xi, pi in zip(x, p)]
    for plane in (_M_OFF8, _M_OFF16, _M_OFF32):
        mask = masks_ref[plane]
        t = [mm(xi, l * mask) for xi, l in zip(x, lmats)]
        x = [xi + mm(ti, xi) for xi, ti in zip(x, t)]
    return x


def _rwkv_kernel(p_ref, mu_ref, vec_ref, wd_ref, wa_ref, wg_ref, e_ref, tri_ref,
                 masks_ref, o_ref,
                 rt_s, at_s, bt_s, kt_s, v_s, bd_s, kd_s, gc_s, y_s, g_s, bon_s,
                 state_s, carry_s, *, ts, n_pairs, passes):
    s_idx = pl.program_id(1)

    @pl.when(s_idx == 0)
    def _():
        state_s[...] = jnp.zeros_like(state_s)
        carry_s[...] = jnp.zeros_like(carry_s)

    width = n_pairs * PAIR
    pt = p_ref[0]
    row = lax.broadcasted_iota(jnp.int32, pt.shape, 0)
    prev = jnp.where(row == 0, carry_s[...], pltpu.roll(pt, 1, 0))
    carry_s[...] = pt[ts - 1:ts, :]
    pm = pt + (prev - pt) * mu_ref[...]

    r = pm[:, 0:width]
    k = pm[:, width:2 * width]
    v = pm[:, 2 * width:3 * width]
    o3 = 3 * width
    xw = pm[:, o3:o3 + LANE]
    xa = pm[:, o3 + LANE:o3 + 2 * LANE]
    xg = pm[:, o3 + 2 * LANE:o3 + 4 * LANE]

    vec = lambda i: vec_ref[i:i + 1, :]
    e_mat = e_ref[...]
    wlog = -_softplus(-(vec(_V_W0) + _mm(jnp.tanh(xw), wd_ref[...], 3))) - 0.5
    logw = -jnp.exp(wlog)
    a = _sigmoid(vec(_V_A0) + _mm(xa, wa_ref[...], 1))
    g_s[...] = _mm(_sigmoid(xg), wg_ref[...], 1)
    kk = k * vec(_V_KK)
    ss = _group_sum(kk * kk, e_mat)
    kk = kk / jnp.maximum(jnp.sqrt(ss), L2_EPS)
    kp = k * (1.0 + (a - 1.0) * vec(_V_KA))
    ib = kk * a
    bon_s[...] = _group_sum(r * kp * vec(_V_RK), e_mat) * v
    cum = _mm_exact_lhs(tri_ref[...], logw)
    tot = jnp.concatenate(
        [jnp.broadcast_to(cum[c * RWKV_CHUNK + RWKV_CHUNK - 1:(c + 1) * RWKV_CHUNK, :],
                          (RWKV_CHUNK, width)) for c in range(ts // RWKV_CHUNK)], axis=0)
    rt_s[...] = r * jnp.exp(cum)
    at_s[...] = -kk * jnp.exp(cum - logw)
    dec_in = jnp.exp(-cum)
    bt_s[...] = ib * dec_in
    kt_s[...] = kp * dec_in
    v_s[...] = v
    dec_out = jnp.exp(tot - cum)
    bd_s[...] = ib * dec_out
    kd_s[...] = kp * dec_out
    gc_s[...] = jnp.exp(tot)

    lane = lax.broadcasted_iota(jnp.int32, (RWKV_CHUNK, PAIR), 1)
    m0 = (lane < HEAD_DIM).astype(F32)
    m1 = 1.0 - m0
    strict = masks_ref[_M_STRICT]
    incl = masks_ref[_M_INCL]

    def chunk_body(c, carry):
        row0 = pl.multiple_of(c * RWKV_CHUNK, RWKV_CHUNK)
        rows = pl.ds(row0, RWKV_CHUNK)
        pairs = range(n_pairs)
        lanes = [slice(j * PAIR, (j + 1) * PAIR) for j in pairs]
        mm = lambda x, w: _mm(x, w, passes)
        mm_nt = lambda x, w: _mm(x, w, passes, nt=True)

        def stacked(ref):
            blks = [ref[rows, lanes[j]] for j in pairs]
            return [jnp.concatenate([b * m0, b * m1], axis=0) for b in blks]

        rh, ah, bh, kh = stacked(rt_s), stacked(at_s), stacked(bt_s), stacked(kt_s)
        vh, bdh, kdh = stacked(v_s), stacked(bd_s), stacked(kd_s)
        st = [state_s[j] for j in pairs]
        bk = [jnp.concatenate([bh[j], kh[j]], axis=0) for j in pairs]
        ga = [mm_nt(ah[j], bk[j]) for j in pairs]
        gr = [mm_nt(rh[j], bk[j]) for j in pairs]
        tinv = _tri_inverse([ga[j][:, :PAIR] * strict for j in pairs], masks_ref, passes)
        x0 = [mm_nt(ah[j], st[j]) + mm(ga[j][:, PAIR:] * strict, vh[j]) for j in pairs]
        u = [mm(tinv[j], x0[j]) for j in pairs]
        y = [mm_nt(rh[j], st[j]) + mm(gr[j][:, :PAIR] * incl, u[j])
             + mm(gr[j][:, PAIR:] * incl, vh[j]) for j in pairs]
        for j in pairs:
            zt = jnp.concatenate([u[j], vh[j]], axis=0).T
            bkd = jnp.concatenate([bdh[j], kdh[j]], axis=0)
            gc = gc_s[pl.ds(row0, 1), lanes[j]]
            state_s[j] = st[j] * gc + mm(zt, bkd)
            y_s[rows, lanes[j]] = y[j][:RWKV_CHUNK] + y[j][RWKV_CHUNK:]
        return carry

    lax.fori_loop(0, ts // RWKV_CHUNK, chunk_body, 0)

    y = y_s[...]
    inv_n = 1.0 / HEAD_DIM
    mean = _group_sum(y, e_mat) * inv_n
    dlt = y - mean
    var = _group_sum(dlt * dlt, e_mat) * inv_n
    yn = dlt * lax.rsqrt(var + GN_EPS) * vec(_V_LNG) + vec(_V_LNB)
    o_ref[0] = ((yn + bon_s[...]) * g_s[...]).astype(o_ref.dtype)


def _rwkv_time_mix(p, mu, vecs, wd, wa, wg, *, ts, passes=RWKV_PASSES):
    bsz, seq, cols = p.shape
    width = vecs.shape[1]
    n_pairs = width // PAIR
    heads = np.arange(MXU_TILE) // HEAD_DIM
    e_mat = jnp.asarray((heads[:, None] == heads[None, :]).astype(np.float32), dtype=BF16)
    tok = np.arange(ts)
    same = (tok[:, None] // RWKV_CHUNK) == (tok[None, :] // RWKV_CHUNK)
    tri = jnp.asarray((same & (tok[None, :] <= tok[:, None])).astype(np.float32), dtype=BF16)
    masks = _rwkv_masks()
    full = lambda shape: pl.BlockSpec(shape, lambda b, s: (0,) * len(shape))
    big = lambda: pltpu.VMEM((ts, width), F32)
    kern = functools.partial(_rwkv_kernel, ts=ts, n_pairs=n_pairs, passes=passes)
    return pl.pallas_call(
        kern,
        out_shape=jax.ShapeDtypeStruct((bsz, seq, width), BF16),
        grid=(bsz, seq // ts),
        in_specs=[pl.BlockSpec((1, ts, cols), lambda b, s: (b, s, 0)),
                  full((1, cols)), full(vecs.shape), full(wd.shape), full(wa.shape),
                  full(wg.shape), full(e_mat.shape), full(tri.shape), full(masks.shape)],
        out_specs=pl.BlockSpec((1, ts, width), lambda b, s: (b, s, 0)),
        scratch_shapes=[big() for _ in range(11)]
        + [pltpu.VMEM((n_pairs, PAIR, PAIR), F32), pltpu.VMEM((1, cols), F32)],
        compiler_params=_params(("parallel", "arbitrary")),
        name="rwkv7_scan",
    )(p, mu, vecs, wd, wa, wg, e_mat, tri, masks)


SB_WINDOW = 6
SB_TAIL = 2
SB_Q_PER_STEP = 4


def _sb_kernel(q_ref, k_ref, v_ref, gain_ref, cum_ref, e2_ref, o_ref, *, q_per_step):
    lane = lax.broadcasted_iota(jnp.int32, (SB_BLOCK, PAIR), 1)
    h0 = lane < HEAD_DIM
    zero = jnp.zeros((), BF16)
    qi = lax.broadcasted_iota(jnp.int32, (SB_BLOCK, 2 * SB_BLOCK), 0)
    ki = lax.broadcasted_iota(jnp.int32, (SB_BLOCK, 2 * SB_BLOCK), 1) % SB_BLOCK
    causal = ki < qi
    cum_mat = cum_ref[...]
    scale = jnp.asarray(HEAD_DIM ** -0.5, BF16)
    width2 = 2 * SB_BLOCK

    def sweep(q, qb, offsets, acc, rest, first_is_diag):
        khats, vhats, valid = [], [], []
        for off in offsets:
            kb = qb - off
            valid.append(kb >= 0)
            rows = pl.ds(pl.multiple_of(jnp.maximum(kb, 0) * SB_BLOCK, SB_BLOCK), SB_BLOCK)
            kblk = k_ref[0, rows, :]
            vblk = v_ref[0, rows, :]
            khats += [jnp.where(h0, kblk, zero), jnp.where(h0, zero, kblk)]
            vhats += [jnp.where(h0, vblk, zero), jnp.where(h0, zero, vblk)]
        z_all = _dot_nt(q, jnp.concatenate(khats, axis=0))
        nblk = range(len(offsets))
        keep = [causal if first_is_diag else None]
        keep += [jnp.broadcast_to(valid[n], causal.shape) for n in nblk[1:]]
        zs = [z_all[:, n * width2:(n + 1) * width2] for n in nblk]
        log_beta = [jnp.minimum(z, 0.0) - jnp.log(1.0 + jnp.exp(-jnp.abs(z))) for z in zs]
        log_1m = [lb - z for lb, z in zip(log_beta, zs)]
        log_1m = [l if m is None else jnp.where(m, l, 0.0) for l, m in zip(log_1m, keep)]
        splits = [_split2(l) for l in log_1m]
        heads = [slice(h * SB_BLOCK, (h + 1) * SB_BLOCK) for h in range(2)]
        parts = [[_dot(jnp.concatenate([hi[:, sl], lo[:, sl]], axis=1), cum_mat) for sl in heads]
                 for hi, lo in splits]
        ws = []
        for n in nblk:
            between = jnp.concatenate([parts[n][0][:, :SB_BLOCK], parts[n][1][:, :SB_BLOCK]], axis=1)
            total = jnp.concatenate([parts[n][0][:, SB_BLOCK:], parts[n][1][:, SB_BLOCK:]], axis=1)
            w = jnp.exp(log_beta[n] + between + rest)
            if keep[n] is not None:
                w = jnp.where(keep[n], w, 0.0)
            ws.append(w.astype(BF16))
            rest = rest + total
        acc = acc + _dot(jnp.concatenate(ws, axis=1), jnp.concatenate(vhats, axis=0))
        return acc, rest

    def q_block(sub, carry):
        qb = pl.program_id(2) * q_per_step + sub
        qrows = pl.ds(pl.multiple_of(sub * SB_BLOCK, SB_BLOCK), SB_BLOCK)
        q = q_ref[0, qrows, :] * scale
        acc = jnp.zeros((SB_BLOCK, PAIR), F32)
        rest = jnp.zeros((SB_BLOCK, width2), F32)
        acc, rest = sweep(q, qb, tuple(range(SB_WINDOW)), acc, rest, True)

        def cond(c):
            i, _, _, live = c
            return jnp.logical_and(i <= qb, live > SB_UNDERFLOW_LOG)

        def body(c):
            i, acc, rest, _ = c
            acc, rest = sweep(q, qb - i, tuple(range(SB_TAIL)), acc, rest, False)
            return i + SB_TAIL, acc, rest, jnp.max(rest)

        _, acc, _, _ = lax.while_loop(cond, body,
                                      (jnp.int32(SB_WINDOW), acc, rest, jnp.max(rest)))
        ms = _mm_exact_rhs(acc * acc, e2_ref[...]) * (1.0 / HEAD_DIM)
        o_ref[0, qrows, :] = (acc * lax.rsqrt(ms + RMS_EPS) * gain_ref[...]).astype(o_ref.dtype)
        return carry

    lax.fori_loop(0, q_per_step, q_block, 0)


def _sb_attention(qkv, gain, width):
    bsz, seq, _ = qkv.shape
    n_pairs = width // PAIR
    nq = seq // SB_BLOCK
    i = np.arange(SB_BLOCK)
    upper = (i[:, None] > i[None, :]).astype(np.float32)
    half = np.concatenate([upper, np.ones_like(upper)], axis=1)
    cum_mat = jnp.asarray(np.concatenate([half, half], axis=0), dtype=BF16)
    hd = np.arange(PAIR) // HEAD_DIM
    e2 = jnp.asarray((hd[:, None] == hd[None, :]).astype(np.float32), dtype=BF16)
    qps = min(SB_Q_PER_STEP, nq)
    qrows = qps * SB_BLOCK
    return pl.pallas_call(
        functools.partial(_sb_kernel, q_per_step=qps),
        out_shape=jax.ShapeDtypeStruct((bsz, seq, width), BF16),
        grid=(bsz, n_pairs, nq // qps),
        in_specs=[pl.BlockSpec((1, qrows, PAIR), lambda b, j, t: (b, t, j)),
                  pl.BlockSpec((1, seq, PAIR), lambda b, j, t: (b, 0, n_pairs + j)),
                  pl.BlockSpec((1, seq, PAIR), lambda b, j, t: (b, 0, 2 * n_pairs + j)),
                  pl.BlockSpec((1, PAIR), lambda b, j, t: (0, j)),
                  pl.BlockSpec(cum_mat.shape, lambda b, j, t: (0, 0)),
                  pl.BlockSpec(e2.shape, lambda b, j, t: (0, 0))],
        out_specs=pl.BlockSpec((1, qrows, PAIR), lambda b, j, t: (b, t, j)),
        compiler_params=_params(("parallel", "parallel", "arbitrary")),
        name="stickbreak_attn",
    )(qkv, qkv, qkv, gain.reshape(1, width), cum_mat, e2)


def _outproj_kernel(x_ref, ya_ref, yb_ref, wa_ref, wb_ref, g1_ref, gain_ref, sc_ref, sh_ref,
                    wr_ref, br_ref, x1_ref, h_ref, lg_ref):
    mix = _dot(ya_ref[...], wa_ref[...]) + _dot(yb_ref[...], wb_ref[...])
    x1 = x_ref[...] + g1_ref[0] * mix
    x1_ref[...] = x1
    ms = jnp.mean(x1 * x1, axis=-1, keepdims=True)
    h = x1 * lax.rsqrt(ms + RMS_EPS) * gain_ref[...] * (1.0 + sc_ref[0]) + sh_ref[0]
    h_ref[...] = h.astype(h_ref.dtype)
    lg_ref[...] = _mm(h, wr_ref[...], 3) + br_ref[...]


def _outproj_norm_router(x2, ya, yb, w_a, w_b, g1, gain, sc, sh, w_r, b_r, seq, tm):
    t, d = x2.shape
    half = ya.shape[1]
    nr = w_r.shape[1]
    per_b = seq // tm
    row = lambda w: pl.BlockSpec((tm, w), lambda i: (i, 0))
    full = lambda shape: pl.BlockSpec(shape, lambda i: (0,) * len(shape))
    per_batch = pl.BlockSpec((1, 1, d), lambda i: (i // per_b, 0, 0))
    return pl.pallas_call(
        _outproj_kernel,
        out_shape=(jax.ShapeDtypeStruct((t, d), F32), jax.ShapeDtypeStruct((t, d), F32),
                   jax.ShapeDtypeStruct((t, nr), F32)),
        grid=(t // tm,),
        in_specs=[row(d), row(half), row(half), full((half, d)), full((half, d)), per_batch,
                  full((1, d)), per_batch, per_batch, full((d, nr)), full((1, nr))],
        out_specs=(row(d), row(d), row(nr)),
        compiler_params=_params(("parallel",)),
        name="outproj_norm_router",
    )(x2, ya, yb, w_a, w_b, g1, gain.reshape(1, d), sc, sh, w_r, b_r)


MOE_ROWS = 256
MOE_VMEM_LIMIT = 58 * 1024 * 1024
_DMA_UNROLL = 8
_CAST_STEPS = 8


def _moe_kernel(be_ref, nxt_ref, nused_ref, nvalid_ref, gcur_ref, gnext_ref, sidx_ref, wt_ref,
                h_hbm, wg_hbm, wu_hbm, wd_hbm, y_hbm,
                xbuf, ybuf, wg_f, wu_f, wd_f, wg_b, wu_b, wd_b, gsem, ssem, wsem):
    i = pl.program_id(0)
    n_used = nused_ref[0]
    last = pl.num_programs(0) - 1
    slot = i % 2
    de = wg_b.shape[1]
    d = wd_b.shape[1]

    def weight_copies(e):
        pairs = ((wg_hbm, wg_f), (wu_hbm, wu_f), (wd_hbm, wd_f))
        return [pltpu.make_async_copy(src.at[e], dst, wsem.at[k])
                for k, (src, dst) in enumerate(pairs)]

    def gather(idx_ref, r, s):
        return pltpu.make_async_copy(h_hbm.at[pl.ds(idx_ref[0, 0, r], 1), :],
                                     xbuf.at[s, pl.ds(r, 1), :], gsem.at[s])

    def scatter(idx_ref, r, s):
        return pltpu.make_async_copy(ybuf.at[s, pl.ds(r, 1), :],
                                     y_hbm.at[pl.ds(idx_ref[0, 0, r], 1), :], ssem.at[s])

    def for_rows(fn, n=None):
        def body(r, c):
            fn(r)
            return c
        if n is None:
            lax.fori_loop(0, MOE_ROWS, body, 0, unroll=_DMA_UNROLL)
        else:
            lax.fori_loop(0, n, body, 0)

    def gather_wait(s):
        pltpu.make_async_copy(h_hbm.at[pl.ds(0, MOE_ROWS), :], xbuf.at[s], gsem.at[s]).wait()

    def scatter_wait(step):
        n = nvalid_ref[step]
        s = step % 2
        p = MOE_ROWS
        while p >= 1:
            @pl.when((n & p) != 0)
            def _():
                pltpu.make_async_copy(ybuf.at[s, pl.ds(0, p), :], y_hbm.at[pl.ds(0, p), :],
                                      ssem.at[s]).wait()
            p //= 2

    @pl.when(jnp.logical_and(i == 0, n_used > 0))
    def _():
        for_rows(lambda r: gather(gcur_ref, r, 0).start())
        for c in weight_copies(be_ref[0]):
            c.start()

    @pl.when(i < n_used)
    def _():
        e = be_ref[i]

        @pl.when(jnp.logical_or(i == 0, e != be_ref[jnp.maximum(i - 1, 0)]))
        def _():
            for c in weight_copies(e):
                c.wait()

            def cast_rows(k, c):
                for src, dst in ((wg_f, wg_b), (wu_f, wu_b), (wd_f, wd_b)):
                    nrow = src.shape[0] // _CAST_STEPS
                    rows = pl.ds(pl.multiple_of(k * nrow, nrow), nrow)
                    dst[rows, :] = src[rows, :].astype(BF16)
                return c
            lax.fori_loop(0, _CAST_STEPS, cast_rows, 0)

            @pl.when(nxt_ref[i] != e)
            def _():
                for c in weight_copies(nxt_ref[i]):
                    c.start()

        gather_wait(slot)

        @pl.when(i >= 2)
        def _():
            scatter_wait(i - 2)

        n_issue = 2 * (de // MXU_TILE)
        per_chunk = MOE_ROWS // n_issue
        issued = [0]

        def issue_gathers():
            for r in range(issued[0], issued[0] + per_chunk):
                gather(gnext_ref, r, 1 - slot).start()
            issued[0] += per_chunk

        xb = xbuf[slot].astype(BF16)
        hg, hu = [], []
        for n in range(de // MXU_TILE):
            cols = slice(n * MXU_TILE, (n + 1) * MXU_TILE)
            hg.append(_dot(xb, wg_b[:, cols]))
            issue_gathers()
            hu.append(_dot(xb, wu_b[:, cols]))
            issue_gathers()
        assert issued[0] == MOE_ROWS
        hg = jnp.concatenate(hg, axis=1)
        hid = (hg * _sigmoid(hg) * jnp.concatenate(hu, axis=1)).astype(BF16)
        ybuf[slot] = _dot(hid, wd_b[...]) * wt_ref[...]
        for_rows(lambda r: scatter(sidx_ref, r, slot).start(), nvalid_ref[i])

        @pl.when(i + 1 >= n_used)
        def _():
            gather_wait(1 - slot)

    @pl.when(i == last)
    def _():
        for back in (2, 1):
            @pl.when(n_used >= back)
            def _():
                scatter_wait(n_used - back)


def _moe_experts(h2, gidx, sidx, row_w, block_e, next_e, n_used, n_valid, w_gate, w_up, w_down):
    t, d = h2.shape
    de = w_gate.shape[2]
    n_blocks = gidx.shape[0]
    assert MOE_ROWS % (2 * (de // MXU_TILE)) == 0 and d % _CAST_STEPS == 0 == de % _CAST_STEPS
    idx_spec = lambda f: pl.BlockSpec((1, 1, MOE_ROWS), f, memory_space=pltpu.SMEM)
    hbm = pl.BlockSpec(memory_space=pl.ANY)
    grid_spec = pltpu.PrefetchScalarGridSpec(
        num_scalar_prefetch=4,
        grid=(n_blocks,),
        in_specs=[idx_spec(lambda i, *_: (i, 0, 0)),
                  idx_spec(lambda i, *_: (jnp.minimum(i + 1, n_blocks - 1), 0, 0)),
                  idx_spec(lambda i, *_: (i, 0, 0)),
                  pl.BlockSpec((MOE_ROWS, 1), lambda i, *_: (i, 0)),
                  hbm, hbm, hbm, hbm],
        out_specs=hbm,
        scratch_shapes=[pltpu.VMEM((2, MOE_ROWS, d), F32), pltpu.VMEM((2, MOE_ROWS, d), F32),
                        pltpu.VMEM((d, de), F32), pltpu.VMEM((d, de), F32),
                        pltpu.VMEM((de, d), F32),
                        pltpu.VMEM((d, de), BF16), pltpu.VMEM((d, de), BF16),
                        pltpu.VMEM((de, d), BF16),
                        pltpu.SemaphoreType.DMA((2,)), pltpu.SemaphoreType.DMA((2,)),
                        pltpu.SemaphoreType.DMA((3,))],
    )
    return pl.pallas_call(
        _moe_kernel,
        out_shape=jax.ShapeDtypeStruct((TOP_K_IN_GROUP * t, d), F32),
        grid_spec=grid_spec,
        compiler_params=_params(("arbitrary",), MOE_VMEM_LIMIT),
        name="moe_experts",
    )(block_e, next_e, n_used, n_valid, gidx, gidx, sidx, row_w.reshape(-1, 1),
      h2, w_gate, w_up, w_down)


def _final_kernel(x1_ref, ya_ref, yb_ref, g2_ref, gain_ref, o_ref):
    x2 = x1_ref[...] + g2_ref[0] * (ya_ref[...] + yb_ref[...])
    ms = jnp.mean(x2 * x2, axis=-1, keepdims=True)
    o_ref[...] = x2 * lax.rsqrt(ms + RMS_EPS) * gain_ref[...]


def _final_norm(x1, y2, g2, gain, seq, tm):
    t, d = x1.shape
    per_b = seq // tm
    nt = t // tm
    row = pl.BlockSpec((tm, d), lambda i: (i, 0))
    return pl.pallas_call(
        _final_kernel,
        out_shape=jax.ShapeDtypeStruct((t, d), F32),
        grid=(nt,),
        in_specs=[row, row, pl.BlockSpec((tm, d), lambda i: (i + nt, 0)),
                  pl.BlockSpec((1, 1, d), lambda i: (i // per_b, 0, 0)),
                  pl.BlockSpec((1, d), lambda i: (0, 0))],
        out_specs=row,
        compiler_params=_params(("parallel",)),
        name="final_norm",
    )(x1, y2, y2, g2, gain.reshape(1, d))


def _route(logits):
    t = logits.shape[0]
    gl = logits[:, :N_GROUPS]
    el = logits[:, N_GROUPS:N_GROUPS + N_EXPERTS].reshape(t, N_GROUPS, EXPERTS_PER_GROUP)
    pg = jax.nn.softmax(gl, axis=-1)
    g_sel = jnp.argmax(gl, axis=-1)
    p_sel = jnp.take_along_axis(pg, g_sel[:, None], axis=1)[:, 0]
    el_g = jnp.take_along_axis(el, g_sel[:, None, None], axis=1)[:, 0]
    top_v, top_i = lax.top_k(el_g, TOP_K_IN_GROUP)
    pair_w = jax.nn.softmax(top_v, axis=-1) * p_sel[:, None]
    e_id = (g_sel[:, None] * EXPERTS_PER_GROUP + top_i).astype(jnp.int32)
    flat_e = e_id.reshape(-1)
    flat_w = pair_w.reshape(-1)
    m = flat_e.shape[0]
    order = jnp.argsort(flat_e).astype(jnp.int32)
    experts = jnp.arange(N_EXPERTS, dtype=jnp.int32)
    counts = jnp.sum((flat_e[:, None] == experts[None, :]).astype(jnp.int32), axis=0)
    starts = jnp.cumsum(counts) - counts
    padded = (counts + MOE_ROWS - 1) // MOE_ROWS * MOE_ROWS
    pends = jnp.cumsum(padded)
    pstarts = pends - padded
    n_blocks = (m + N_EXPERTS * (MOE_ROWS - 1) + MOE_ROWS - 1) // MOE_ROWS
    block_start = jnp.arange(n_blocks, dtype=jnp.int32) * MOE_ROWS
    block_e = jnp.minimum(jnp.sum((block_start[:, None] >= pends[None, :]).astype(jnp.int32),
                                  axis=1), N_EXPERTS - 1)
    blk = jnp.arange(n_blocks, dtype=jnp.int32)[:, None]
    rin = jnp.arange(MOE_ROWS, dtype=jnp.int32)[None, :]
    off = blk * MOE_ROWS + rin - pstarts[block_e][:, None]
    valid = off < counts[block_e][:, None]
    src = jnp.clip(starts[block_e][:, None] + off, 0, m - 1)
    assign = order[src]
    tok = assign // TOP_K_IN_GROUP
    gidx = jnp.where(valid, tok, 0)
    sidx = jnp.where(valid, (assign % TOP_K_IN_GROUP) * t + tok, 0)
    row_w = jnp.where(valid, flat_w[assign], 0.0)
    n_used = (pends[-1] // MOE_ROWS).astype(jnp.int32).reshape(1)
    n_valid = jnp.sum(valid.astype(jnp.int32), axis=1)
    later = (experts[None, :] > experts[:, None]) & (counts[None, :] > 0)
    next_expert = jnp.min(jnp.where(later, experts[None, :], N_EXPERTS), axis=1)
    next_expert = jnp.where(next_expert == N_EXPERTS, experts, next_expert)
    shape3 = (n_blocks, 1, MOE_ROWS)
    return (gidx.reshape(shape3), sidx.reshape(shape3), row_w, block_e, next_expert[block_e],
            n_used, n_valid)


def _pad_cols(w, n):
    return jnp.pad(w, ((0, 0), (0, n - w.shape[1])))


def _pad_rows(w, n):
    return jnp.pad(w, ((0, n - w.shape[0]), (0, 0)))


def _layer(x, mod, norm1_gain, w_in, shift_mu, w0, w_decay_up, a0, w_iclr_up, w_gate_up,
           k_k, k_a, r_k, ln_x_gain, ln_x_bias, sb_norm_gain, w_out, norm2_gain,
           w_router_group, b_router_group, w_router_expert, b_router_expert,
           w_exp_gate, w_exp_up, w_exp_down, *, rwkv_ts, tm_in, tm_out):
    bsz, seq, d = x.shape
    t = bsz * seq
    rw = w0.shape[0]
    sbw = sb_norm_gain.shape[0]
    sh1, sc1, g1, sh2, sc2, g2 = [m.reshape(bsz, 1, d) for m in jnp.split(mod, 6, axis=-1)]

    o = 3 * rw
    seg = lambda a, b: w_in[:, a:b]
    w_rwkv = jnp.concatenate([
        seg(0, o),
        _pad_cols(seg(o, o + DECAY_LORA), LANE),
        _pad_cols(seg(o + DECAY_LORA, o + DECAY_LORA + ICLR_LORA), LANE),
        _pad_cols(seg(o + DECAY_LORA + ICLR_LORA, o + DECAY_LORA + ICLR_LORA + GATE_LORA),
                  2 * LANE)], axis=1).astype(BF16)
    rcols = o + DECAY_LORA + ICLR_LORA + GATE_LORA
    w_sb = w_in[:, rcols:].astype(BF16)
    mseg = lambda a, b: shift_mu[a:b][None, :]
    mu = jnp.concatenate([
        mseg(0, o),
        _pad_cols(mseg(o, o + DECAY_LORA), LANE),
        _pad_cols(mseg(o + DECAY_LORA, o + DECAY_LORA + ICLR_LORA), LANE),
        _pad_cols(mseg(o + DECAY_LORA + ICLR_LORA, rcols), 2 * LANE)], axis=1)

    x2 = x.reshape(t, d)
    p_rwkv = _normmod_matmul(x2, norm1_gain, sc1, sh1, w_rwkv, F32, seq, tm_in, 512)
    qkv = _normmod_matmul(x2, norm1_gain, sc1, sh1, w_sb, BF16, seq, tm_in, 512)

    vecs = jnp.stack([w0, a0, k_k, k_a, r_k.reshape(-1), ln_x_gain, ln_x_bias,
                      jnp.zeros_like(w0)])
    y_a = _rwkv_time_mix(p_rwkv.reshape(bsz, seq, -1), mu, vecs,
                         _pad_rows(w_decay_up, LANE), _pad_rows(w_iclr_up, LANE),
                         _pad_rows(w_gate_up, 2 * LANE), ts=rwkv_ts)
    y_b = _sb_attention(qkv.reshape(bsz, seq, -1), sb_norm_gain, sbw)

    w_r = _pad_cols(jnp.concatenate([w_router_group, w_router_expert], axis=1), LANE)
    b_r = _pad_cols(jnp.concatenate([b_router_group, b_router_expert])[None, :], LANE)
    w_o = w_out.astype(BF16)
    x1, h2, logits = _outproj_norm_router(
        x2, y_a.reshape(t, rw), y_b.reshape(t, sbw), w_o[:rw], w_o[rw:], g1, norm2_gain,
        sc2, sh2, w_r, b_r, seq, tm_out)

    gidx, sidx, row_w, block_e, next_e, n_used, n_valid = _route(logits)
    y2 = _moe_experts(h2, gidx, sidx, row_w, block_e, next_e, n_used, n_valid,
                      w_exp_gate, w_exp_up, w_exp_down)
    return x1, y2, g2


def kernel(x, c, w_ada, b_ada, norm1_gain, w_in, shift_mu, w0, w_decay_up, a0, w_iclr_up, w_gate_up, k_k, k_a, r_k, ln_x_gain, ln_x_bias, sb_norm_gain, w_out, norm2_gain, w_router_group, b_router_group, w_router_expert, b_router_expert, w_exp_gate, w_exp_up, w_exp_down, final_norm_gain):
    bsz, seq, d = x.shape
    assert w_ada.shape[0] == 1, "the final norm is fused into the single layer's last kernel"
    l = 0
    tiles = dict(rwkv_ts=min(256, seq), tm_in=min(1024, seq), tm_out=min(256, seq))
    mod = _ada_mod(c, w_ada[l], b_ada[l])
    x1, y2, g2 = _layer(
        x, mod, norm1_gain[l], w_in[l], shift_mu[l], w0[l], w_decay_up[l], a0[l],
        w_iclr_up[l], w_gate_up[l], k_k[l], k_a[l], r_k[l], ln_x_gain[l], ln_x_bias[l],
        sb_norm_gain[l], w_out[l], norm2_gain[l], w_router_group[l], b_router_group[l],
        w_router_expert[l], b_router_expert[l], w_exp_gate[l], w_exp_up[l],
        w_exp_down[l], **tiles)
    out = _final_norm(x1, y2, g2, final_norm_gain, seq, min(512, seq))
    return out.reshape(bsz, seq, d)
```

```python
import functools

import jax
import jax.numpy as jnp
import numpy as np
from jax import lax
from jax.experimental import pallas as pl
from jax.experimental.pallas import tpu as pltpu

F32 = jnp.float32
BF16 = jnp.bfloat16

RMS_EPS = 1e-6
GN_EPS = 64e-5
L2_EPS = 1e-12

HEAD_DIM = 64
PAIR = 2 * HEAD_DIM
RWKV_CHUNK = 64
SB_BLOCK = 128
N_GROUPS = 8
EXPERTS_PER_GROUP = 8
N_EXPERTS = N_GROUPS * EXPERTS_PER_GROUP
TOP_K_IN_GROUP = 2
MOE_BLOCK = 128
DECAY_LORA = 64
ICLR_LORA = 64
GATE_LORA = 160
LANE = 128
MXU_TILE = 256
VMEM_LIMIT = 48 * 1024 * 1024
SB_UNDERFLOW_LOG = -104.0


def _dot(a, b):
    return lax.dot_general(a, b, (((1,), (0,)), ((), ())), preferred_element_type=F32)


def _dot_nt(a, b):
    return lax.dot_general(a, b, (((1,), (1,)), ((), ())), preferred_element_type=F32)


def _split2(x):
    hi = x.astype(BF16)
    lo = (x - hi.astype(F32)).astype(BF16)
    return hi, lo


def _mm(a, b, passes, nt=False):
    d = _dot_nt if nt else _dot
    if passes == 1:
        return d(a.astype(BF16), b.astype(BF16))
    ah, al = _split2(a)
    bh, bl = _split2(b)
    return d(ah, bh) + d(ah, bl) + d(al, bh)


def _mm_exact_rhs(a, b_exact):
    hi, lo = _split2(a)
    return _dot(hi, b_exact) + _dot(lo, b_exact)


def _mm_exact_lhs(a_exact, b):
    hi, lo = _split2(b)
    return _dot(a_exact, hi) + _dot(a_exact, lo)


def _group_sum(x, e_blk):
    w = e_blk.shape[0]
    return jnp.concatenate([_mm_exact_rhs(x[:, g * w:(g + 1) * w], e_blk)
                            for g in range(x.shape[1] // w)], axis=1)


def _sigmoid(x):
    return 1.0 / (1.0 + jnp.exp(-x))


def _softplus(x):
    return jnp.maximum(x, 0.0) + jnp.log(1.0 + jnp.exp(-jnp.abs(x)))


def _params(sem, vmem=VMEM_LIMIT):
    return pltpu.CompilerParams(dimension_semantics=sem, vmem_limit_bytes=vmem)


def _ada_kernel(c_ref, w_ref, b_ref, o_ref):
    c = c_ref[...]
    s = c * _sigmoid(c)
    o_ref[...] = _mm(s, w_ref[...], 3) + b_ref[...]


def _ada_mod(c, w, b):
    bsz, d = c.shape
    n = w.shape[1]
    rows = 8
    cp = jnp.zeros((rows, d), F32).at[:bsz].set(c)
    tn = 1024
    out = pl.pallas_call(
        _ada_kernel,
        out_shape=jax.ShapeDtypeStruct((rows, n), F32),
        grid=(n // tn,),
        in_specs=[pl.BlockSpec((rows, d), lambda j: (0, 0)),
                  pl.BlockSpec((d, tn), lambda j: (0, j)),
                  pl.BlockSpec((1, tn), lambda j: (0, j))],
        out_specs=pl.BlockSpec((rows, tn), lambda j: (0, j)),
        compiler_params=_params(("arbitrary",)),
        name="ada_mod",
    )(cp, w, b.reshape(1, n))
    return out[:bsz]


_NORM_SLAB = 128


def _normmod_mm_kernel(x_ref, gain_ref, sc_ref, sh_ref, w_ref, o_ref, h_scr):
    @pl.when(pl.program_id(1) == 0)
    def _():
        scale = gain_ref[...] * (1.0 + sc_ref[0])
        shift = sh_ref[0]
        slab = min(_NORM_SLAB, x_ref.shape[0])

        def norm_rows(k, c):
            rows = pl.ds(pl.multiple_of(k * slab, slab), slab)
            xf = x_ref[rows, :]
            ms = jnp.mean(xf * xf, axis=-1, keepdims=True)
            h_scr[rows, :] = (xf * lax.rsqrt(ms + RMS_EPS) * scale + shift).astype(BF16)
            return c
        lax.fori_loop(0, x_ref.shape[0] // slab, norm_rows, 0, unroll=2)

    o_ref[...] = _dot(h_scr[...], w_ref[...]).astype(o_ref.dtype)


def _normmod_matmul(x2, gain, sc, sh, w, out_dtype, seq, tm, tn):
    t, d = x2.shape
    n = w.shape[1]
    per_b = seq // tm
    return pl.pallas_call(
        _normmod_mm_kernel,
        out_shape=jax.ShapeDtypeStruct((t, n), out_dtype),
        grid=(t // tm, n // tn),
        in_specs=[pl.BlockSpec((tm, d), lambda i, j: (i, 0)),
                  pl.BlockSpec((1, d), lambda i, j: (0, 0)),
                  pl.BlockSpec((1, 1, d), lambda i, j: (i // per_b, 0, 0)),
                  pl.BlockSpec((1, 1, d), lambda i, j: (i // per_b, 0, 0)),
                  pl.BlockSpec((d, tn), lambda i, j: (0, j))],
        out_specs=pl.BlockSpec((tm, tn), lambda i, j: (i, j)),
        scratch_shapes=[pltpu.VMEM((tm, d), BF16)],
        compiler_params=_params(("parallel", "arbitrary")),
        name="normmod_proj",
    )(x2, gain.reshape(1, d), sc, sh, w)


_V_W0, _V_A0, _V_KK, _V_KA, _V_RK, _V_LNG, _V_LNB = range(7)
_M_STRICT, _M_INCL, _M_BD8, _M_OFF8, _M_OFF16, _M_OFF32, _M_EYE = range(7)

RWKV_PASSES = 1


def _rwkv_masks():
    i = np.arange(PAIR)[:, None]
    j = np.arange(PAIR)[None, :]
    strict = (j < i)
    incl = (j <= i)
    bd8 = (i // 8 == j // 8)
    def off(b):
        return (i // (2 * b) == j // (2 * b)) & (i // b > j // b)
    eye = (i == j)
    m = np.stack([strict, incl, bd8 & strict, off(8), off(16), off(32), eye]).astype(np.float32)
    return jnp.asarray(m)


def _tri_inverse(lmats, masks_ref, passes):
    eye = masks_ref[_M_EYE]
    bd8 = masks_ref[_M_BD8]
    mm = lambda a, b: _mm(a, b, passes)
    dblk = [l * bd8 for l in lmats]
    x = [eye + d for d in dblk]
    p = [mm(d, d) for d in dblk]
    x = [xi + mm(pi, xi) for xi, pi in zip(x, p)]
    p = [mm(pi, pi) for pi in p]
    x = [xi + mm(pi, xi) for xi, pi in zip(x, p)]
    for plane in (_M_OFF8, _M_OFF16, _M_OFF32):
        mask = masks_ref[plane]
        t = [mm(xi, l * mask) for xi, l in zip(x, lmats)]
        x = [xi + mm(ti, xi) for xi, ti in zip(x, t)]
    return x


def _rwkv_kernel(p_ref, mu_ref, vec_ref, wd_ref, wa_ref, wg_ref, e_ref, tri_ref,
                 masks_ref, o_ref,
                 rt_s, at_s, bt_s, kt_s, v_s, bd_s, kd_s, gc_s, y_s, g_s, bon_s,
                 state_s, carry_s, *, ts, n_pairs, passes):
    s_idx = pl.program_id(1)

    @pl.when(s_idx == 0)
    def _():
        state_s[...] = jnp.zeros_like(state_s)
        carry_s[...] = jnp.zeros_like(carry_s)

    width = n_pairs * PAIR
    pt = p_ref[0]
    row = lax.broadcasted_iota(jnp.int32, pt.shape, 0)
    prev = jnp.where(row == 0, carry_s[...], pltpu.roll(pt, 1, 0))
    carry_s[...] = pt[ts - 1:ts, :]
    pm = pt + (prev - pt) * mu_ref[...]

    r = pm[:, 0:width]
    k = pm[:, width:2 * width]
    v = pm[:, 2 * width:3 * width]
    o3 = 3 * width
    xw = pm[:, o3:o3 + LANE]
    xa = pm[:, o3 + LANE:o3 + 2 * LANE]
    xg = pm[:, o3 + 2 * LANE:o3 + 4 * LANE]

    vec = lambda i: vec_ref[i:i + 1, :]
    e_mat = e_ref[...]
    wlog = -_softplus(-(vec(_V_W0) + _mm(jnp.tanh(xw), wd_ref[...], 3))) - 0.5
    logw = -jnp.exp(wlog)
    a = _sigmoid(vec(_V_A0) + _mm(xa, wa_ref[...], 1))
    g_s[...] = _mm(_sigmoid(xg), wg_ref[...], 1)
    kk = k * vec(_V_KK)
    ss = _group_sum(kk * kk, e_mat)
    kk = kk / jnp.maximum(jnp.sqrt(ss), L2_EPS)
    kp = k * (1.0 + (a - 1.0) * vec(_V_KA))
    ib = kk * a
    bon_s[...] = _group_sum(r * kp * vec(_V_RK), e_mat) * v
    cum = _mm_exact_lhs(tri_ref[...], logw)
    tot = jnp.concatenate(
        [jnp.broadcast_to(cum[c * RWKV_CHUNK + RWKV_CHUNK - 1:(c + 1) * RWKV_CHUNK, :],
                          (RWKV_CHUNK, width)) for c in range(ts // RWKV_CHUNK)], axis=0)
    rt_s[...] = r * jnp.exp(cum)
    at_s[...] = -kk * jnp.exp(cum - logw)
    dec_in = jnp.exp(-cum)
    bt_s[...] = ib * dec_in
    kt_s[...] = kp * dec_in
    v_s[...] = v
    dec_out = jnp.exp(tot - cum)
    bd_s[...] = ib * dec_out
    kd_s[...] = kp * dec_out
    gc_s[...] = jnp.exp(tot)

    lane = lax.broadcasted_iota(jnp.int32, (RWKV_CHUNK, PAIR), 1)
    m0 = (lane < HEAD_DIM).astype(F32)
    m1 = 1.0 - m0
    strict = masks_ref[_M_STRICT]
    incl = masks_ref[_M_INCL]

    def chunk_body(c, carry):
        row0 = pl.multiple_of(c * RWKV_CHUNK, RWKV_CHUNK)
        rows = pl.ds(row0, RWKV_CHUNK)
        pairs = range(n_pairs)
        lanes = [slice(j * PAIR, (j + 1) * PAIR) for j in pairs]
        mm = lambda x, w: _mm(x, w, passes)
        mm_nt = lambda x, w: _mm(x, w, passes, nt=True)

        def stacked(ref):
            blks = [ref[rows, lanes[j]] for j in pairs]
            return [jnp.concatenate([b * m0, b * m1], axis=0) for b in blks]

        rh, ah, bh, kh = stacked(rt_s), stacked(at_s), stacked(bt_s), stacked(kt_s)
        vh, bdh, kdh = stacked(v_s), stacked(bd_s), stacked(kd_s)
        st = [state_s[j] for j in pairs]
        bk = [jnp.concatenate([bh[j], kh[j]], axis=0) for j in pairs]
        ga = [mm_nt(ah[j], bk[j]) for j in pairs]
        gr = [mm_nt(rh[j], bk[j]) for j in pairs]
        tinv = _tri_inverse([ga[j][:, :PAIR] * strict for j in pairs], masks_ref, passes)
        x0 = [mm_nt(ah[j], st[j]) + mm(ga[j][:, PAIR:] * strict, vh[j]) for j in pairs]
        u = [mm(tinv[j], x0[j]) for j in pairs]
        y = [mm_nt(rh[j], st[j]) + mm(gr[j][:, :PAIR] * incl, u[j])
             + mm(gr[j][:, PAIR:] * incl, vh[j]) for j in pairs]
        for j in pairs:
            zt = jnp.concatenate([u[j], vh[j]], axis=0).T
            bkd = jnp.concatenate([bdh[j], kdh[j]], axis=0)
            gc = gc_s[pl.ds(row0, 1), lanes[j]]
            state_s[j] = st[j] * gc + mm(zt, bkd)
            y_s[rows, lanes[j]] = y[j][:RWKV_CHUNK] + y[j][RWKV_CHUNK:]
        return carry

    lax.fori_loop(0, ts // RWKV_CHUNK, chunk_body, 0)

    y = y_s[...]
    inv_n = 1.0 / HEAD_DIM
    mean = _group_sum(y, e_mat) * inv_n
    dlt = y - mean
    var = _group_sum(dlt * dlt, e_mat) * inv_n
    yn = dlt * lax.rsqrt(var + GN_EPS) * vec(_V_LNG) + vec(_V_LNB)
    o_ref[0] = ((yn + bon_s[...]) * g_s[...]).astype(o_ref.dtype)


def _rwkv_time_mix(p, mu, vecs, wd, wa, wg, *, ts, passes=RWKV_PASSES):
    bsz, seq, cols = p.shape
    width = vecs.shape[1]
    n_pairs = width // PAIR
    heads = np.arange(MXU_TILE) // HEAD_DIM
    e_mat = jnp.asarray((heads[:, None] == heads[None, :]).astype(np.float32), dtype=BF16)
    tok = np.arange(ts)
    same = (tok[:, None] // RWKV_CHUNK) == (tok[None, :] // RWKV_CHUNK)
    tri = jnp.asarray((same & (tok[None, :] <= tok[:, None])).astype(np.float32), dtype=BF16)
    masks = _rwkv_masks()
    full = lambda shape: pl.BlockSpec(shape, lambda b, s: (0,) * len(shape))
    big = lambda: pltpu.VMEM((ts, width), F32)
    kern = functools.partial(_rwkv_kernel, ts=ts, n_pairs=n_pairs, passes=passes)
    return pl.pallas_call(
        kern,
        out_shape=jax.ShapeDtypeStruct((bsz, seq, width), BF16),
        grid=(bsz, seq // ts),
        in_specs=[pl.BlockSpec((1, ts, cols), lambda b, s: (b, s, 0)),
                  full((1, cols)), full(vecs.shape), full(wd.shape), full(wa.shape),
                  full(wg.shape), full(e_mat.shape), full(tri.shape), full(masks.shape)],
        out_specs=pl.BlockSpec((1, ts, width), lambda b, s: (b, s, 0)),
        scratch_shapes=[big() for _ in range(11)]
        + [pltpu.VMEM((n_pairs, PAIR, PAIR), F32), pltpu.VMEM((1, cols), F32)],
        compiler_params=_params(("parallel", "arbitrary")),
        name="rwkv7_scan",
    )(p, mu, vecs, wd, wa, wg, e_mat, tri, masks)


SB_WINDOW = 6
SB_TAIL = 2
SB_Q_PER_STEP = 4


def _sb_kernel(q_ref, k_ref, v_ref, gain_ref, cum_ref, e2_ref, o_ref, *, q_per_step):
    lane = lax.broadcasted_iota(jnp.int32, (SB_BLOCK, PAIR), 1)
    h0 = lane < HEAD_DIM
    zero = jnp.zeros((), BF16)
    qi = lax.broadcasted_iota(jnp.int32, (SB_BLOCK, 2 * SB_BLOCK), 0)
    ki = lax.broadcasted_iota(jnp.int32, (SB_BLOCK, 2 * SB_BLOCK), 1) % SB_BLOCK
    causal = ki < qi
    cum_mat = cum_ref[...]
    scale = jnp.asarray(HEAD_DIM ** -0.5, BF16)
    width2 = 2 * SB_BLOCK

    heads = [slice(h * SB_BLOCK, (h + 1) * SB_BLOCK) for h in range(2)]

    def load_blocks(kb_top, n):
        khats, vhats, valid = [], [], []
        for j in range(n):
            kb = kb_top - j
            valid.append(kb >= 0)
            rows = pl.ds(pl.multiple_of(jnp.maximum(kb, 0) * SB_BLOCK, SB_BLOCK), SB_BLOCK)
            kblk = k_ref[0, rows, :]
            vblk = v_ref[0, rows, :]
            khats.append(jnp.concatenate([jnp.where(h0, kblk, zero), jnp.where(h0, zero, kblk)],
                                         axis=0))
            vhats.append(jnp.concatenate([jnp.where(h0, vblk, zero), jnp.where(h0, zero, vblk)],
                                         axis=0))
        return khats, vhats, valid

    def score_stage(zs, keeps):
        log_beta = [jnp.minimum(z, 0.0) - jnp.log(1.0 + jnp.exp(-jnp.abs(z))) for z in zs]
        log_1m = [lb - z for lb, z in zip(log_beta, zs)]
        log_1m = [l if m is None else jnp.where(m, l, 0.0) for l, m in zip(log_1m, keeps)]
        his = [l.astype(BF16) for l in log_1m]
        parts = [[_dot(hi[:, sl], cum_mat) for sl in heads] for hi in his]
        return log_beta, parts

    def weight_stage(log_beta, parts, keep, rest):
        between = jnp.concatenate([parts[0][:, :SB_BLOCK], parts[1][:, :SB_BLOCK]], axis=1)
        total = jnp.concatenate([parts[0][:, SB_BLOCK:], parts[1][:, SB_BLOCK:]], axis=1)
        w = jnp.exp(log_beta + between + rest)
        if keep is not None:
            w = jnp.where(keep, w, 0.0)
        return w.astype(BF16), rest + total

    def valid_mask(flag):
        return jnp.broadcast_to(flag, causal.shape)

    def sweep(q, kb_top, n, acc, rest):
        khats, vhats, valid = load_blocks(kb_top, n)
        z_all = _dot_nt(q, jnp.concatenate(khats, axis=0))
        keeps = [None] + [valid_mask(valid[j]) for j in range(1, n)]
        zs = [z_all[:, j * width2:(j + 1) * width2] for j in range(n)]
        log_beta, parts = score_stage(zs, keeps)
        ws = []
        for j in range(n):
            w, rest = weight_stage(log_beta[j], parts[j], keeps[j], rest)
            ws.append(w)
        acc = acc + _dot(jnp.concatenate(ws, axis=1), jnp.concatenate(vhats, axis=0))
        return acc, rest

    def window_pair(q2, qb_a, all_valid):
        n = SB_WINDOW
        khats, vhats, valid = load_blocks(qb_a + 1, n + 1)
        z_all = _dot_nt(q2, jnp.concatenate(khats, axis=0))
        off_diag = (lambda flag: None) if all_valid else valid_mask
        items = []
        for j in range(n):
            col = lambda jj: slice(jj * width2, (jj + 1) * width2)
            za = z_all[:SB_BLOCK, col(j + 1)]
            zb = z_all[SB_BLOCK:, col(j)]
            items.append((0, za, causal if j == 0 else off_diag(valid[j + 1])))
            items.append((1, zb, causal if j == 0 else off_diag(valid[j])))
        log_beta, parts = score_stage([it[1] for it in items], [it[2] for it in items])
        rest = [jnp.zeros((SB_BLOCK, width2), F32) for _ in range(2)]
        ws = [[], []]
        for idx, (s, _, keep) in enumerate(items):
            w, rest[s] = weight_stage(log_beta[idx], parts[idx], keep, rest[s])
            ws[s].append(w)
        acc_a = _dot(jnp.concatenate(ws[0], axis=1), jnp.concatenate(vhats[1:], axis=0))
        acc_b = _dot(jnp.concatenate(ws[1], axis=1), jnp.concatenate(vhats[:n], axis=0))
        return (acc_a, rest[0]), (acc_b, rest[1])

    def finish(q, qb, acc, rest, qrows):
        def cond(c):
            i, _, _, live = c
            return jnp.logical_and(i <= qb, live > SB_UNDERFLOW_LOG)

        def body(c):
            i, acc, rest, _ = c
            acc, rest = sweep(q, qb - i, SB_TAIL, acc, rest)
            return i + SB_TAIL, acc, rest, jnp.max(rest)

        _, acc, _, _ = lax.while_loop(cond, body,
                                      (jnp.int32(SB_WINDOW), acc, rest, jnp.max(rest)))
        ms = _mm_exact_rhs(acc * acc, e2_ref[...]) * (1.0 / HEAD_DIM)
        o_ref[0, qrows, :] = (acc * lax.rsqrt(ms + RMS_EPS) * gain_ref[...]).astype(o_ref.dtype)

    def q_pair(it, all_valid):
        qb_a = pl.program_id(2) * q_per_step + 2 * it
        row0 = pl.multiple_of(2 * it * SB_BLOCK, 2 * SB_BLOCK)
        q2 = q_ref[0, pl.ds(row0, 2 * SB_BLOCK), :] * scale
        (acc_a, rest_a), (acc_b, rest_b) = window_pair(q2, qb_a, all_valid)
        finish(q2[:SB_BLOCK], qb_a, acc_a, rest_a, pl.ds(row0, SB_BLOCK))
        finish(q2[SB_BLOCK:], qb_a + 1, acc_b, rest_b, pl.ds(row0 + SB_BLOCK, SB_BLOCK))

    def run(all_valid):
        def body(it, carry):
            q_pair(it, all_valid)
            return carry
        lax.fori_loop(0, q_per_step // 2, body, 0)

    first_full = -(-(SB_WINDOW - 1) // q_per_step)
    pl.when(pl.program_id(2) < first_full)(lambda: run(False))
    pl.when(pl.program_id(2) >= first_full)(lambda: run(True))


def _sb_attention(qkv, gain, width):
    bsz, seq, _ = qkv.shape
    n_pairs = width // PAIR
    nq = seq // SB_BLOCK
    i = np.arange(SB_BLOCK)
    upper = (i[:, None] > i[None, :]).astype(np.float32)
    half = np.concatenate([upper, np.ones_like(upper)], axis=1)
    cum_mat = jnp.asarray(half, dtype=BF16)
    hd = np.arange(PAIR) // HEAD_DIM
    e2 = jnp.asarray((hd[:, None] == hd[None, :]).astype(np.float32), dtype=BF16)
    qps = min(SB_Q_PER_STEP, nq)
    assert qps % 2 == 0 and nq % qps == 0, "query blocks are processed in adjacent pairs"
    qrows = qps * SB_BLOCK
    return pl.pallas_call(
        functools.partial(_sb_kernel, q_per_step=qps),
        out_shape=jax.ShapeDtypeStruct((bsz, seq, width), BF16),
        grid=(bsz, n_pairs, nq // qps),
        in_specs=[pl.BlockSpec((1, qrows, PAIR), lambda b, j, t: (b, t, j)),
                  pl.BlockSpec((1, seq, PAIR), lambda b, j, t: (b, 0, n_pairs + j)),
                  pl.BlockSpec((1, seq, PAIR), lambda b, j, t: (b, 0, 2 * n_pairs + j)),
                  pl.BlockSpec((1, PAIR), lambda b, j, t: (0, j)),
                  pl.BlockSpec(cum_mat.shape, lambda b, j, t: (0, 0)),
                  pl.BlockSpec(e2.shape, lambda b, j, t: (0, 0))],
        out_specs=pl.BlockSpec((1, qrows, PAIR), lambda b, j, t: (b, t, j)),
        compiler_params=_params(("parallel", "parallel", "arbitrary")),
        name="stickbreak_attn",
    )(qkv, qkv, qkv, gain.reshape(1, width), cum_mat, e2)


def _outproj_kernel(x_ref, ya_ref, yb_ref, wa_ref, wb_ref, g1_ref, gain_ref, sc_ref, sh_ref,
                    wr_ref, br_ref, x1_ref, h_ref, lg_ref):
    mix = _dot(ya_ref[...], wa_ref[...]) + _dot(yb_ref[...], wb_ref[...])
    x1 = x_ref[...] + g1_ref[0] * mix
    x1_ref[...] = x1
    ms = jnp.mean(x1 * x1, axis=-1, keepdims=True)
    h = x1 * lax.rsqrt(ms + RMS_EPS) * gain_ref[...] * (1.0 + sc_ref[0]) + sh_ref[0]
    h_ref[...] = h.astype(h_ref.dtype)
    lg_ref[...] = _mm(h, wr_ref[...], 3) + br_ref[...]


def _outproj_norm_router(x2, ya, yb, w_a, w_b, g1, gain, sc, sh, w_r, b_r, seq, tm):
    t, d = x2.shape
    half = ya.shape[1]
    nr = w_r.shape[1]
    per_b = seq // tm
    row = lambda w: pl.BlockSpec((tm, w), lambda i: (i, 0))
    full = lambda shape: pl.BlockSpec(shape, lambda i: (0,) * len(shape))
    per_batch = pl.BlockSpec((1, 1, d), lambda i: (i // per_b, 0, 0))
    return pl.pallas_call(
        _outproj_kernel,
        out_shape=(jax.ShapeDtypeStruct((t, d), F32), jax.ShapeDtypeStruct((t, d), F32),
                   jax.ShapeDtypeStruct((t, nr), F32)),
        grid=(t // tm,),
        in_specs=[row(d), row(half), row(half), full((half, d)), full((half, d)), per_batch,
                  full((1, d)), per_batch, per_batch, full((d, nr)), full((1, nr))],
        out_specs=(row(d), row(d), row(nr)),
        compiler_params=_params(("parallel",)),
        name="outproj_norm_router",
    )(x2, ya, yb, w_a, w_b, g1, gain.reshape(1, d), sc, sh, w_r, b_r)


MOE_ROWS = 256
MOE_VMEM_LIMIT = 58 * 1024 * 1024
_DMA_UNROLL = 8
_CAST_STEPS = 8


def _moe_kernel(be_ref, nxt_ref, nused_ref, gcur_ref, gnext_ref, scur_ref, sprev_ref, wt_ref,
                h_hbm, wg_hbm, wu_hbm, wd_hbm, y_hbm,
                xbuf, ybuf, wg_f, wu_f, wd_f, wg_b, wu_b, wd_b, gsem, ssem, wsem):
    i = pl.program_id(0)
    n_used = nused_ref[0]
    slot = i % 2
    de = wg_b.shape[1]
    d = wd_b.shape[1]

    def weight_copies(e):
        pairs = ((wg_hbm, wg_f), (wu_hbm, wu_f), (wd_hbm, wd_f))
        return [pltpu.make_async_copy(src.at[e], dst, wsem.at[k])
                for k, (src, dst) in enumerate(pairs)]

    def gather(idx_ref, r, s):
        return pltpu.make_async_copy(h_hbm.at[pl.ds(idx_ref[0, 0, r], 1), :],
                                     xbuf.at[s, pl.ds(r, 1), :], gsem.at[s])

    def scatter(idx_ref, r, s):
        return pltpu.make_async_copy(ybuf.at[s, pl.ds(r, 1), :],
                                     y_hbm.at[pl.ds(idx_ref[0, 0, r], 1), :], ssem.at[s])

    def for_rows(fn):
        def body(r, c):
            fn(r)
            return c
        lax.fori_loop(0, MOE_ROWS, body, 0, unroll=_DMA_UNROLL)

    def gather_wait(s):
        pltpu.make_async_copy(h_hbm.at[pl.ds(0, MOE_ROWS), :], xbuf.at[s], gsem.at[s]).wait()

    def scatter_wait(s):
        pltpu.make_async_copy(ybuf.at[s], y_hbm.at[pl.ds(0, MOE_ROWS), :], ssem.at[s]).wait()

    @pl.when(jnp.logical_and(i == 0, n_used > 0))
    def _():
        for_rows(lambda r: gather(gcur_ref, r, 0).start())
        for c in weight_copies(be_ref[0]):
            c.start()
        ybuf[1] = jnp.zeros(ybuf.shape[1:], ybuf.dtype)

    @pl.when(i < n_used)
    def _():
        e = be_ref[i]

        @pl.when(jnp.logical_or(i == 0, e != be_ref[jnp.maximum(i - 1, 0)]))
        def _():
            for c in weight_copies(e):
                c.wait()

            def cast_rows(k, c):
                for src, dst in ((wg_f, wg_b), (wu_f, wu_b), (wd_f, wd_b)):
                    nrow = src.shape[0] // _CAST_STEPS
                    rows = pl.ds(pl.multiple_of(k * nrow, nrow), nrow)
                    dst[rows, :] = src[rows, :].astype(BF16)
                return c
            lax.fori_loop(0, _CAST_STEPS, cast_rows, 0)

            @pl.when(nxt_ref[i] != e)
            def _():
                for c in weight_copies(nxt_ref[i]):
                    c.start()

        gather_wait(slot)

        def sliced_issue(n_slices, make_copy):
            per = MOE_ROWS // n_slices
            state = [0]

            def issue():
                for r in range(state[0], state[0] + per):
                    make_copy(r).start()
                state[0] += per
            return issue

        n_up = de // MXU_TILE
        n_down = d // MXU_TILE
        issue_gather = sliced_issue(2 * n_up, lambda r: gather(gnext_ref, r, 1 - slot))
        issue_scatter = sliced_issue(n_down, lambda r: scatter(sprev_ref, r, 1 - slot))

        xb = xbuf[slot].astype(BF16)
        hg, hu = [], []
        for n in range(n_up):
            cols = slice(n * MXU_TILE, (n + 1) * MXU_TILE)
            hg.append(_dot(xb, wg_b[:, cols]))
            issue_gather()
            hu.append(_dot(xb, wu_b[:, cols]))
            issue_gather()
        hg = jnp.concatenate(hg, axis=1)
        hid = (hg * _sigmoid(hg) * jnp.concatenate(hu, axis=1)).astype(BF16)

        @pl.when(i >= 1)
        def _():
            scatter_wait(slot)

        wt = wt_ref[...]
        for n in range(n_down):
            cols = slice(n * MXU_TILE, (n + 1) * MXU_TILE)
            ybuf[slot, :, cols] = _dot(hid, wd_b[:, cols]) * wt
            issue_scatter()

        @pl.when(i + 1 >= n_used)
        def _():
            gather_wait(1 - slot)
            scatter_wait(1 - slot)
            for_rows(lambda r: scatter(scur_ref, r, slot).start())
            scatter_wait(slot)


def _moe_experts(h2, gidx, sidx, row_w, block_e, next_e, n_used, w_gate, w_up, w_down):
    t, d = h2.shape
    de = w_gate.shape[2]
    n_blocks = gidx.shape[0]
    assert MOE_ROWS % (2 * (de // MXU_TILE)) == 0 == MOE_ROWS % (d // MXU_TILE)
    assert d % _CAST_STEPS == 0 == de % _CAST_STEPS and sidx.shape[0] == n_blocks + 1
    idx_spec = lambda f: pl.BlockSpec((1, 1, MOE_ROWS), f, memory_space=pltpu.SMEM)
    hbm = pl.BlockSpec(memory_space=pl.ANY)
    grid_spec = pltpu.PrefetchScalarGridSpec(
        num_scalar_prefetch=3,
        grid=(n_blocks,),
        in_specs=[idx_spec(lambda i, *_: (i, 0, 0)),
                  idx_spec(lambda i, *_: (jnp.minimum(i + 1, n_blocks - 1), 0, 0)),
                  idx_spec(lambda i, *_: (i, 0, 0)),
                  idx_spec(lambda i, *_: (jnp.where(i == 0, n_blocks, i - 1), 0, 0)),
                  pl.BlockSpec((MOE_ROWS, 1), lambda i, *_: (i, 0)),
                  hbm, hbm, hbm, hbm],
        out_specs=hbm,
        scratch_shapes=[pltpu.VMEM((2, MOE_ROWS, d), F32), pltpu.VMEM((2, MOE_ROWS, d), F32),
                        pltpu.VMEM((d, de), F32), pltpu.VMEM((d, de), F32),
                        pltpu.VMEM((de, d), F32),
                        pltpu.VMEM((d, de), BF16), pltpu.VMEM((d, de), BF16),
                        pltpu.VMEM((de, d), BF16),
                        pltpu.SemaphoreType.DMA((2,)), pltpu.SemaphoreType.DMA((2,)),
                        pltpu.SemaphoreType.DMA((3,))],
    )
    return pl.pallas_call(
        _moe_kernel,
        out_shape=jax.ShapeDtypeStruct((TOP_K_IN_GROUP * t + MOE_ROWS, d), F32),
        grid_spec=grid_spec,
        compiler_params=_params(("arbitrary",), MOE_VMEM_LIMIT),
        name="moe_experts",
    )(block_e, next_e, n_used, gidx, gidx, sidx, sidx, row_w.reshape(-1, 1),
      h2, w_gate, w_up, w_down)


def _final_kernel(x1_ref, ya_ref, yb_ref, g2_ref, gain_ref, o_ref):
    x2 = x1_ref[...] + g2_ref[0] * (ya_ref[...] + yb_ref[...])
    ms = jnp.mean(x2 * x2, axis=-1, keepdims=True)
    o_ref[...] = x2 * lax.rsqrt(ms + RMS_EPS) * gain_ref[...]


def _final_norm(x1, y2, g2, gain, seq, tm):
    t, d = x1.shape
    per_b = seq // tm
    nt = t // tm
    row = pl.BlockSpec((tm, d), lambda i: (i, 0))
    return pl.pallas_call(
        _final_kernel,
        out_shape=jax.ShapeDtypeStruct((t, d), F32),
        grid=(nt,),
        in_specs=[row, row, pl.BlockSpec((tm, d), lambda i: (i + nt, 0)),
                  pl.BlockSpec((1, 1, d), lambda i: (i // per_b, 0, 0)),
                  pl.BlockSpec((1, d), lambda i: (0, 0))],
        out_specs=row,
        compiler_params=_params(("parallel",)),
        name="final_norm",
    )(x1, y2, y2, g2, gain.reshape(1, d))


def _route(logits):
    t = logits.shape[0]
    gl = logits[:, :N_GROUPS]
    el = logits[:, N_GROUPS:N_GROUPS + N_EXPERTS].reshape(t, N_GROUPS, EXPERTS_PER_GROUP)
    pg = jax.nn.softmax(gl, axis=-1)
    g_sel = jnp.argmax(gl, axis=-1)
    p_sel = jnp.take_along_axis(pg, g_sel[:, None], axis=1)[:, 0]
    el_g = jnp.take_along_axis(el, g_sel[:, None, None], axis=1)[:, 0]
    top_v, top_i = lax.top_k(el_g, TOP_K_IN_GROUP)
    pair_w = jax.nn.softmax(top_v, axis=-1) * p_sel[:, None]
    e_id = (g_sel[:, None] * EXPERTS_PER_GROUP + top_i).astype(jnp.int32)
    flat_e = e_id.reshape(-1)
    flat_w = pair_w.reshape(-1)
    m = flat_e.shape[0]
    order = jnp.argsort(flat_e).astype(jnp.int32)
    experts = jnp.arange(N_EXPERTS, dtype=jnp.int32)
    counts = jnp.sum((flat_e[:, None] == experts[None, :]).astype(jnp.int32), axis=0)
    starts = jnp.cumsum(counts) - counts
    padded = (counts + MOE_ROWS - 1) // MOE_ROWS * MOE_ROWS
    pends = jnp.cumsum(padded)
    pstarts = pends - padded
    n_blocks = (m + N_EXPERTS * (MOE_ROWS - 1) + MOE_ROWS - 1) // MOE_ROWS
    block_start = jnp.arange(n_blocks, dtype=jnp.int32) * MOE_ROWS
    block_e = jnp.minimum(jnp.sum((block_start[:, None] >= pends[None, :]).astype(jnp.int32),
                                  axis=1), N_EXPERTS - 1)
    blk = jnp.arange(n_blocks, dtype=jnp.int32)[:, None]
    rin = jnp.arange(MOE_ROWS, dtype=jnp.int32)[None, :]
    off = blk * MOE_ROWS + rin - pstarts[block_e][:, None]
    valid = off < counts[block_e][:, None]
    src = jnp.clip(starts[block_e][:, None] + off, 0, m - 1)
    assign = order[src]
    tok = assign // TOP_K_IN_GROUP
    gidx = jnp.where(valid, tok, 0)
    spare = TOP_K_IN_GROUP * t + rin
    sidx = jnp.where(valid, (assign % TOP_K_IN_GROUP) * t + tok, spare)
    sidx = jnp.concatenate([sidx, spare], axis=0)
    row_w = jnp.where(valid, flat_w[assign], 0.0)
    n_used = (pends[-1] // MOE_ROWS).astype(jnp.int32).reshape(1)
    later = (experts[None, :] > experts[:, None]) & (counts[None, :] > 0)
    next_expert = jnp.min(jnp.where(later, experts[None, :], N_EXPERTS), axis=1)
    next_expert = jnp.where(next_expert == N_EXPERTS, experts, next_expert)
    return (gidx.reshape(n_blocks, 1, MOE_ROWS), sidx.reshape(n_blocks + 1, 1, MOE_ROWS), row_w,
            block_e, next_expert[block_e], n_used)


def _pad_cols(w, n):
    return jnp.pad(w, ((0, 0), (0, n - w.shape[1])))


def _pad_rows(w, n):
    return jnp.pad(w, ((0, n - w.shape[0]), (0, 0)))


def _layer(x, mod, norm1_gain, w_in, shift_mu, w0, w_decay_up, a0, w_iclr_up, w_gate_up,
           k_k, k_a, r_k, ln_x_gain, ln_x_bias, sb_norm_gain, w_out, norm2_gain,
           w_router_group, b_router_group, w_router_expert, b_router_expert,
           w_exp_gate, w_exp_up, w_exp_down, *, rwkv_ts, tm_in, tm_out):
    bsz, seq, d = x.shape
    t = bsz * seq
    rw = w0.shape[0]
    sbw = sb_norm_gain.shape[0]
    sh1, sc1, g1, sh2, sc2, g2 = [m.reshape(bsz, 1, d) for m in jnp.split(mod, 6, axis=-1)]

    o = 3 * rw
    seg = lambda a, b: w_in[:, a:b]
    w_rwkv = jnp.concatenate([
        seg(0, o),
        _pad_cols(seg(o, o + DECAY_LORA), LANE),
        _pad_cols(seg(o + DECAY_LORA, o + DECAY_LORA + ICLR_LORA), LANE),
        _pad_cols(seg(o + DECAY_LORA + ICLR_LORA, o + DECAY_LORA + ICLR_LORA + GATE_LORA),
                  2 * LANE)], axis=1).astype(BF16)
    rcols = o + DECAY_LORA + ICLR_LORA + GATE_LORA
    w_sb = w_in[:, rcols:].astype(BF16)
    mseg = lambda a, b: shift_mu[a:b][None, :]
    mu = jnp.concatenate([
        mseg(0, o),
        _pad_cols(mseg(o, o + DECAY_LORA), LANE),
        _pad_cols(mseg(o + DECAY_LORA, o + DECAY_LORA + ICLR_LORA), LANE),
        _pad_cols(mseg(o + DECAY_LORA + ICLR_LORA, rcols), 2 * LANE)], axis=1)

    x2 = x.reshape(t, d)
    p_rwkv = _normmod_matmul(x2, norm1_gain, sc1, sh1, w_rwkv, F32, seq, tm_in, 512)
    qkv = _normmod_matmul(x2, norm1_gain, sc1, sh1, w_sb, BF16, seq, tm_in, 512)

    vecs = jnp.stack([w0, a0, k_k, k_a, r_k.reshape(-1), ln_x_gain, ln_x_bias,
                      jnp.zeros_like(w0)])
    y_a = _rwkv_time_mix(p_rwkv.reshape(bsz, seq, -1), mu, vecs,
                         _pad_rows(w_decay_up, LANE), _pad_rows(w_iclr_up, LANE),
                         _pad_rows(w_gate_up, 2 * LANE), ts=rwkv_ts)
    y_b = _sb_attention(qkv.reshape(bsz, seq, -1), sb_norm_gain, sbw)

    w_r = _pad_cols(jnp.concatenate([w_router_group, w_router_expert], axis=1), LANE)
    b_r = _pad_cols(jnp.concatenate([b_router_group, b_router_expert])[None, :], LANE)
    w_o = w_out.astype(BF16)
    x1, h2, logits = _outproj_norm_router(
        x2, y_a.reshape(t, rw), y_b.reshape(t, sbw), w_o[:rw], w_o[rw:], g1, norm2_gain,
        sc2, sh2, w_r, b_r, seq, tm_out)

    gidx, sidx, row_w, block_e, next_e, n_used = _route(logits)
    y2 = _moe_experts(h2, gidx, sidx, row_w, block_e, next_e, n_used,
                      w_exp_gate, w_exp_up, w_exp_down)
    return x1, y2, g2


def kernel(x, c, w_ada, b_ada, norm1_gain, w_in, shift_mu, w0, w_decay_up, a0, w_iclr_up, w_gate_up, k_k, k_a, r_k, ln_x_gain, ln_x_bias, sb_norm_gain, w_out, norm2_gain, w_router_group, b_router_group, w_router_expert, b_router_expert, w_exp_gate, w_exp_up, w_exp_down, final_norm_gain):
    bsz, seq, d = x.shape
    assert w_ada.shape[0] == 1, "the final norm is fused into the single layer's last kernel"
    l = 0
    tiles = dict(rwkv_ts=min(256, seq), tm_in=min(1024, seq), tm_out=min(256, seq))
    mod = _ada_mod(c, w_ada[l], b_ada[l])
    x1, y2, g2 = _layer(
        x, mod, norm1_gain[l], w_in[l], shift_mu[l], w0[l], w_decay_up[l], a0[l],
        w_iclr_up[l], w_gate_up[l], k_k[l], k_a[l], r_k[l], ln_x_gain[l], ln_x_bias[l],
        sb_norm_gain[l], w_out[l], norm2_gain[l], w_router_group[l], b_router_group[l],
        w_router_expert[l], b_router_expert[l], w_exp_gate[l], w_exp_up[l],
        w_exp_down[l], **tiles)
    out = _final_norm(x1, y2, g2, final_norm_gain, seq, min(512, seq))
    return out.reshape(bsz, seq, d)
```

```python
import functools

import jax
import jax.numpy as jnp
import numpy as np
from jax import lax
from jax.experimental import pallas as pl
from jax.experimental.pallas import tpu as pltpu

F32 = jnp.float32
BF16 = jnp.bfloat16

RMS_EPS = 1e-6
GN_EPS = 64e-5
L2_EPS = 1e-12

HEAD_DIM = 64
PAIR = 2 * HEAD_DIM
RWKV_CHUNK = 64
SB_BLOCK = 128
N_GROUPS = 8
EXPERTS_PER_GROUP = 8
N_EXPERTS = N_GROUPS * EXPERTS_PER_GROUP
TOP_K_IN_GROUP = 2
MOE_BLOCK = 128
DECAY_LORA = 64
ICLR_LORA = 64
GATE_LORA = 160
LANE = 128
MXU_TILE = 256
VMEM_LIMIT = 48 * 1024 * 1024
SB_UNDERFLOW_LOG = -104.0


def _dot(a, b):
    return lax.dot_general(a, b, (((1,), (0,)), ((), ())), preferred_element_type=F32)


def _dot_nt(a, b):
    return lax.dot_general(a, b, (((1,), (1,)), ((), ())), preferred_element_type=F32)


def _split2(x):
    hi = x.astype(BF16)
    lo = (x - hi.astype(F32)).astype(BF16)
    return hi, lo


def _mm(a, b, passes, nt=False):
    d = _dot_nt if nt else _dot
    if passes == 1:
        return d(a.astype(BF16), b.astype(BF16))
    ah, al = _split2(a)
    bh, bl = _split2(b)
    return d(ah, bh) + d(ah, bl) + d(al, bh)


def _mm_exact_rhs(a, b_exact):
    hi, lo = _split2(a)
    return _dot(hi, b_exact) + _dot(lo, b_exact)


def _mm_exact_lhs(a_exact, b):
    hi, lo = _split2(b)
    return _dot(a_exact, hi) + _dot(a_exact, lo)


def _group_sum(x, e_blk):
    w = e_blk.shape[0]
    return jnp.concatenate([_mm_exact_rhs(x[:, g * w:(g + 1) * w], e_blk)
                            for g in range(x.shape[1] // w)], axis=1)


def _sigmoid(x):
    return 1.0 / (1.0 + jnp.exp(-x))


def _softplus(x):
    return jnp.maximum(x, 0.0) + jnp.log(1.0 + jnp.exp(-jnp.abs(x)))


def _params(sem, vmem=VMEM_LIMIT):
    return pltpu.CompilerParams(dimension_semantics=sem, vmem_limit_bytes=vmem)


def _ada_kernel(c_ref, w_ref, b_ref, o_ref):
    c = c_ref[...]
    s = c * _sigmoid(c)
    o_ref[...] = _mm(s, w_ref[...], 3) + b_ref[...]


def _ada_mod(c, w, b):
    bsz, d = c.shape
    n = w.shape[1]
    rows = 8
    cp = jnp.zeros((rows, d), F32).at[:bsz].set(c)
    tn = 1024
    out = pl.pallas_call(
        _ada_kernel,
        out_shape=jax.ShapeDtypeStruct((rows, n), F32),
        grid=(n // tn,),
        in_specs=[pl.BlockSpec((rows, d), lambda j: (0, 0)),
                  pl.BlockSpec((d, tn), lambda j: (0, j)),
                  pl.BlockSpec((1, tn), lambda j: (0, j))],
        out_specs=pl.BlockSpec((rows, tn), lambda j: (0, j)),
        compiler_params=_params(("arbitrary",)),
        name="ada_mod",
    )(cp, w, b.reshape(1, n))
    return out[:bsz]


_NORM_SLAB = 128


def _normmod_mm_kernel(x_ref, gain_ref, sc_ref, sh_ref, w_ref, o_ref, h_scr):
    @pl.when(pl.program_id(1) == 0)
    def _():
        scale = gain_ref[...] * (1.0 + sc_ref[0])
        shift = sh_ref[0]
        slab = min(_NORM_SLAB, x_ref.shape[0])

        def norm_rows(k, c):
            rows = pl.ds(pl.multiple_of(k * slab, slab), slab)
            xf = x_ref[rows, :]
            ms = jnp.mean(xf * xf, axis=-1, keepdims=True)
            h_scr[rows, :] = (xf * lax.rsqrt(ms + RMS_EPS) * scale + shift).astype(BF16)
            return c
        lax.fori_loop(0, x_ref.shape[0] // slab, norm_rows, 0, unroll=2)

    o_ref[...] = _dot(h_scr[...], w_ref[pl.program_id(1)]).astype(o_ref.dtype)


def _normmod_matmul(x2, gain, sc, sh, w, out_dtype, seq, tm, tn):
    t, d = x2.shape
    n = w.shape[1]
    per_b = seq // tm
    w_tiles = w.reshape(d, n // tn, tn).transpose(1, 0, 2)
    return pl.pallas_call(
        _normmod_mm_kernel,
        out_shape=jax.ShapeDtypeStruct((t, n), out_dtype),
        grid=(t // tm, n // tn),
        in_specs=[pl.BlockSpec((tm, d), lambda i, j: (i, 0)),
                  pl.BlockSpec((1, d), lambda i, j: (0, 0)),
                  pl.BlockSpec((1, 1, d), lambda i, j: (i // per_b, 0, 0)),
                  pl.BlockSpec((1, 1, d), lambda i, j: (i // per_b, 0, 0)),
                  pl.BlockSpec(w_tiles.shape, lambda i, j: (0, 0, 0),
                               pipeline_mode=pl.Buffered(1))],
        out_specs=pl.BlockSpec((tm, tn), lambda i, j: (i, j)),
        scratch_shapes=[pltpu.VMEM((tm, d), BF16)],
        compiler_params=_params(("parallel", "arbitrary")),
        name="normmod_proj",
    )(x2, gain.reshape(1, d), sc, sh, w_tiles)


_V_W0, _V_A0, _V_KK, _V_KA, _V_RK, _V_LNG, _V_LNB = range(7)
_M_STRICT, _M_INCL, _M_BD8, _M_OFF8, _M_OFF16, _M_OFF32, _M_EYE = range(7)

RWKV_PASSES = 1


def _rwkv_masks():
    i = np.arange(PAIR)[:, None]
    j = np.arange(PAIR)[None, :]
    strict = (j < i)
    incl = (j <= i)
    bd8 = (i // 8 == j // 8)
    def off(b):
        return (i // (2 * b) == j // (2 * b)) & (i // b > j // b)
    eye = (i == j)
    m = np.stack([strict, incl, bd8 & strict, off(8), off(16), off(32), eye]).astype(np.float32)
    return jnp.asarray(m)


def _tri_inverse(lmats, masks_ref, passes):
    eye = masks_ref[_M_EYE]
    bd8 = masks_ref[_M_BD8]
    mm = lambda a, b: _mm(a, b, passes)
    dblk = [l * bd8 for l in lmats]
    x = [eye + d for d in dblk]
    p = [mm(d, d) for d in dblk]
    x = [xi + mm(pi, xi) for xi, pi in zip(x, p)]
    p = [mm(pi, pi) for pi in p]
    x = [xi + mm(pi, xi) for xi, pi in zip(x, p)]
    for plane in (_M_OFF8, _M_OFF16, _M_OFF32):
        mask = masks_ref[plane]
        t = [mm(xi, l * mask) for xi, l in zip(x, lmats)]
        x = [xi + mm(ti, xi) for xi, ti in zip(x, t)]
    return x


def _rwkv_kernel(p_ref, mu_ref, vec_ref, wd_ref, wa_ref, wg_ref, e_ref, tri_ref,
                 masks_ref, o_ref,
                 rt_s, at_s, bt_s, kt_s, v_s, bd_s, kd_s, gc_s, y_s, g_s, bon_s,
                 state_s, carry_s, *, ts, n_pairs, passes):
    s_idx = pl.program_id(1)

    @pl.when(s_idx == 0)
    def _():
        state_s[...] = jnp.zeros_like(state_s)
        carry_s[...] = jnp.zeros_like(carry_s)

    width = n_pairs * PAIR
    pt = p_ref[0]
    row = lax.broadcasted_iota(jnp.int32, pt.shape, 0)
    prev = jnp.where(row == 0, carry_s[...], pltpu.roll(pt, 1, 0))
    carry_s[...] = pt[ts - 1:ts, :]
    pm = pt + (prev - pt) * mu_ref[...]

    r = pm[:, 0:width]
    k = pm[:, width:2 * width]
    v = pm[:, 2 * width:3 * width]
    o3 = 3 * width
    xw = pm[:, o3:o3 + LANE]
    xa = pm[:, o3 + LANE:o3 + 2 * LANE]
    xg = pm[:, o3 + 2 * LANE:o3 + 4 * LANE]

    vec = lambda i: vec_ref[i:i + 1, :]
    e_mat = e_ref[...]
    wlog = -_softplus(-(vec(_V_W0) + _mm(jnp.tanh(xw), wd_ref[...], 3))) - 0.5
    logw = -jnp.exp(wlog)
    a = _sigmoid(vec(_V_A0) + _mm(xa, wa_ref[...], 1))
    g_s[...] = _mm(_sigmoid(xg), wg_ref[...], 1)
    kk = k * vec(_V_KK)
    ss = _group_sum(kk * kk, e_mat)
    kk = kk / jnp.maximum(jnp.sqrt(ss), L2_EPS)
    kp = k * (1.0 + (a - 1.0) * vec(_V_KA))
    ib = kk * a
    bon_s[...] = _group_sum(r * kp * vec(_V_RK), e_mat) * v
    cum = _mm_exact_lhs(tri_ref[...], logw)
    tot = jnp.concatenate(
        [jnp.broadcast_to(cum[c * RWKV_CHUNK + RWKV_CHUNK - 1:(c + 1) * RWKV_CHUNK, :],
                          (RWKV_CHUNK, width)) for c in range(ts // RWKV_CHUNK)], axis=0)
    rt_s[...] = r * jnp.exp(cum)
    at_s[...] = -kk * jnp.exp(cum - logw)
    dec_in = jnp.exp(-cum)
    bt_s[...] = ib * dec_in
    kt_s[...] = kp * dec_in
    v_s[...] = v
    dec_out = jnp.exp(tot - cum)
    bd_s[...] = ib * dec_out
    kd_s[...] = kp * dec_out
    gc_s[...] = jnp.exp(tot)

    lane = lax.broadcasted_iota(jnp.int32, (RWKV_CHUNK, PAIR), 1)
    m0 = (lane < HEAD_DIM).astype(F32)
    m1 = 1.0 - m0
    strict = masks_ref[_M_STRICT]
    incl = masks_ref[_M_INCL]

    def chunk_body(c, carry):
        row0 = pl.multiple_of(c * RWKV_CHUNK, RWKV_CHUNK)
        rows = pl.ds(row0, RWKV_CHUNK)
        pairs = range(n_pairs)
        lanes = [slice(j * PAIR, (j + 1) * PAIR) for j in pairs]
        mm = lambda x, w: _mm(x, w, passes)
        mm_nt = lambda x, w: _mm(x, w, passes, nt=True)

        def stacked(ref):
            blks = [ref[rows, lanes[j]] for j in pairs]
            return [jnp.concatenate([b * m0, b * m1], axis=0) for b in blks]

        rh, ah, bh, kh = stacked(rt_s), stacked(at_s), stacked(bt_s), stacked(kt_s)
        vh, bdh, kdh = stacked(v_s), stacked(bd_s), stacked(kd_s)
        st = [state_s[j] for j in pairs]
        bk = [jnp.concatenate([bh[j], kh[j]], axis=0) for j in pairs]
        ga = [mm_nt(ah[j], bk[j]) for j in pairs]
        gr = [mm_nt(rh[j], bk[j]) for j in pairs]
        tinv = _tri_inverse([ga[j][:, :PAIR] * strict for j in pairs], masks_ref, passes)
        x0 = [mm_nt(ah[j], st[j]) + mm(ga[j][:, PAIR:] * strict, vh[j]) for j in pairs]
        u = [mm(tinv[j], x0[j]) for j in pairs]
        y = [mm_nt(rh[j], st[j]) + mm(gr[j][:, :PAIR] * incl, u[j])
             + mm(gr[j][:, PAIR:] * incl, vh[j]) for j in pairs]
        for j in pairs:
            zt = jnp.concatenate([u[j], vh[j]], axis=0).T
            bkd = jnp.concatenate([bdh[j], kdh[j]], axis=0)
            gc = gc_s[pl.ds(row0, 1), lanes[j]]
            state_s[j] = st[j] * gc + mm(zt, bkd)
            y_s[rows, lanes[j]] = y[j][:RWKV_CHUNK] + y[j][RWKV_CHUNK:]
        return carry

    lax.fori_loop(0, ts // RWKV_CHUNK, chunk_body, 0)

    y = y_s[...]
    inv_n = 1.0 / HEAD_DIM
    mean = _group_sum(y, e_mat) * inv_n
    dlt = y - mean
    var = _group_sum(dlt * dlt, e_mat) * inv_n
    yn = dlt * lax.rsqrt(var + GN_EPS) * vec(_V_LNG) + vec(_V_LNB)
    o_ref[0] = ((yn + bon_s[...]) * g_s[...]).astype(o_ref.dtype)


def _rwkv_time_mix(p, mu, vecs, wd, wa, wg, *, ts, passes=RWKV_PASSES):
    bsz, seq, cols = p.shape
    width = vecs.shape[1]
    n_pairs = width // PAIR
    heads = np.arange(MXU_TILE) // HEAD_DIM
    e_mat = jnp.asarray((heads[:, None] == heads[None, :]).astype(np.float32), dtype=BF16)
    tok = np.arange(ts)
    same = (tok[:, None] // RWKV_CHUNK) == (tok[None, :] // RWKV_CHUNK)
    tri = jnp.asarray((same & (tok[None, :] <= tok[:, None])).astype(np.float32), dtype=BF16)
    masks = _rwkv_masks()
    full = lambda shape: pl.BlockSpec(shape, lambda b, s: (0,) * len(shape))
    big = lambda: pltpu.VMEM((ts, width), F32)
    kern = functools.partial(_rwkv_kernel, ts=ts, n_pairs=n_pairs, passes=passes)
    return pl.pallas_call(
        kern,
        out_shape=jax.ShapeDtypeStruct((bsz, seq, width), BF16),
        grid=(bsz, seq // ts),
        in_specs=[pl.BlockSpec((1, ts, cols), lambda b, s: (b, s, 0)),
                  full((1, cols)), full(vecs.shape), full(wd.shape), full(wa.shape),
                  full(wg.shape), full(e_mat.shape), full(tri.shape), full(masks.shape)],
        out_specs=pl.BlockSpec((1, ts, width), lambda b, s: (b, s, 0)),
        scratch_shapes=[big() for _ in range(11)]
        + [pltpu.VMEM((n_pairs, PAIR, PAIR), F32), pltpu.VMEM((1, cols), F32)],
        compiler_params=_params(("parallel", "arbitrary")),
        name="rwkv7_scan",
    )(p, mu, vecs, wd, wa, wg, e_mat, tri, masks)


SB_WINDOW = 6
SB_TAIL = 2
SB_Q_PER_STEP = 4


def _sb_kernel(q_ref, k_ref, v_ref, gain_ref, cum_ref, e2_ref, o_ref, *, q_per_step):
    lane = lax.broadcasted_iota(jnp.int32, (SB_BLOCK, PAIR), 1)
    h0 = lane < HEAD_DIM
    zero = jnp.zeros((), BF16)
    qi = lax.broadcasted_iota(jnp.int32, (SB_BLOCK, 2 * SB_BLOCK), 0)
    ki = lax.broadcasted_iota(jnp.int32, (SB_BLOCK, 2 * SB_BLOCK), 1) % SB_BLOCK
    causal = ki < qi
    cum_mat = cum_ref[...]
    scale = jnp.asarray(HEAD_DIM ** -0.5, BF16)
    width2 = 2 * SB_BLOCK

    heads = [slice(h * SB_BLOCK, (h + 1) * SB_BLOCK) for h in range(2)]

    def load_blocks(kb_top, n):
        khats, vhats, valid = [], [], []
        for j in range(n):
            kb = kb_top - j
            valid.append(kb >= 0)
            rows = pl.ds(pl.multiple_of(jnp.maximum(kb, 0) * SB_BLOCK, SB_BLOCK), SB_BLOCK)
            kblk = k_ref[0, rows, :]
            vblk = v_ref[0, rows, :]
            khats.append(jnp.concatenate([jnp.where(h0, kblk, zero), jnp.where(h0, zero, kblk)],
                                         axis=0))
            vhats.append(jnp.concatenate([jnp.where(h0, vblk, zero), jnp.where(h0, zero, vblk)],
                                         axis=0))
        return khats, vhats, valid

    def score_stage(zs, keeps):
        log_beta = [jnp.minimum(z, 0.0) - jnp.log(1.0 + jnp.exp(-jnp.abs(z))) for z in zs]
        log_1m = [lb - z for lb, z in zip(log_beta, zs)]
        log_1m = [l if m is None else jnp.where(m, l, 0.0) for l, m in zip(log_1m, keeps)]
        his = [l.astype(BF16) for l in log_1m]
        parts = [[_dot(hi[:, sl], cum_mat) for sl in heads] for hi in his]
        return log_beta, parts

    def weight_stage(log_beta, parts, keep, rest):
        between = jnp.concatenate([parts[0][:, :SB_BLOCK], parts[1][:, :SB_BLOCK]], axis=1)
        total = jnp.concatenate([parts[0][:, SB_BLOCK:], parts[1][:, SB_BLOCK:]], axis=1)
        w = jnp.exp(log_beta + between + rest)
        if keep is not None:
            w = jnp.where(keep, w, 0.0)
        return w.astype(BF16), rest + total

    def valid_mask(flag):
        return jnp.broadcast_to(flag, causal.shape)

    def sweep(q, kb_top, n, acc, rest):
        khats, vhats, valid = load_blocks(kb_top, n)
        z_all = _dot_nt(q, jnp.concatenate(khats, axis=0))
        keeps = [None] + [valid_mask(valid[j]) for j in range(1, n)]
        zs = [z_all[:, j * width2:(j + 1) * width2] for j in range(n)]
        log_beta, parts = score_stage(zs, keeps)
        ws = []
        for j in range(n):
            w, rest = weight_stage(log_beta[j], parts[j], keeps[j], rest)
            ws.append(w)
        acc = acc + _dot(jnp.concatenate(ws, axis=1), jnp.concatenate(vhats, axis=0))
        return acc, rest

    def window_pair(q2, qb_a, all_valid):
        n = SB_WINDOW
        khats, vhats, valid = load_blocks(qb_a + 1, n + 1)
        z_all = _dot_nt(q2, jnp.concatenate(khats, axis=0))
        off_diag = (lambda flag: None) if all_valid else valid_mask
        items = []
        for j in range(n):
            col = lambda jj: slice(jj * width2, (jj + 1) * width2)
            za = z_all[:SB_BLOCK, col(j + 1)]
            zb = z_all[SB_BLOCK:, col(j)]
            items.append((0, za, causal if j == 0 else off_diag(valid[j + 1])))
            items.append((1, zb, causal if j == 0 else off_diag(valid[j])))
        log_beta, parts = score_stage([it[1] for it in items], [it[2] for it in items])
        rest = [jnp.zeros((SB_BLOCK, width2), F32) for _ in range(2)]
        ws = [[], []]
        for idx, (s, _, keep) in enumerate(items):
            w, rest[s] = weight_stage(log_beta[idx], parts[idx], keep, rest[s])
            ws[s].append(w)
        acc_a = _dot(jnp.concatenate(ws[0], axis=1), jnp.concatenate(vhats[1:], axis=0))
        acc_b = _dot(jnp.concatenate(ws[1], axis=1), jnp.concatenate(vhats[:n], axis=0))
        return (acc_a, rest[0]), (acc_b, rest[1])

    def finish(q, qb, acc, rest, qrows):
        def cond(c):
            i, _, _, live = c
            return jnp.logical_and(i <= qb, live > SB_UNDERFLOW_LOG)

        def body(c):
            i, acc, rest, _ = c
            acc, rest = sweep(q, qb - i, SB_TAIL, acc, rest)
            return i + SB_TAIL, acc, rest, jnp.max(rest)

        _, acc, _, _ = lax.while_loop(cond, body,
                                      (jnp.int32(SB_WINDOW), acc, rest, jnp.max(rest)))
        ms = _mm_exact_rhs(acc * acc, e2_ref[...]) * (1.0 / HEAD_DIM)
        o_ref[0, qrows, :] = (acc * lax.rsqrt(ms + RMS_EPS) * gain_ref[...]).astype(o_ref.dtype)

    def q_pair(it, all_valid):
        qb_a = pl.program_id(2) * q_per_step + 2 * it
        row0 = pl.multiple_of(2 * it * SB_BLOCK, 2 * SB_BLOCK)
        q2 = q_ref[0, pl.ds(row0, 2 * SB_BLOCK), :] * scale
        (acc_a, rest_a), (acc_b, rest_b) = window_pair(q2, qb_a, all_valid)
        finish(q2[:SB_BLOCK], qb_a, acc_a, rest_a, pl.ds(row0, SB_BLOCK))
        finish(q2[SB_BLOCK:], qb_a + 1, acc_b, rest_b, pl.ds(row0 + SB_BLOCK, SB_BLOCK))

    def run(all_valid):
        def body(it, carry):
            q_pair(it, all_valid)
            return carry
        lax.fori_loop(0, q_per_step // 2, body, 0)

    first_full = -(-(SB_WINDOW - 1) // q_per_step)
    pl.when(pl.program_id(2) < first_full)(lambda: run(False))
    pl.when(pl.program_id(2) >= first_full)(lambda: run(True))


def _sb_attention(qkv, gain, width):
    bsz, seq, _ = qkv.shape
    n_pairs = width // PAIR
    nq = seq // SB_BLOCK
    i = np.arange(SB_BLOCK)
    upper = (i[:, None] > i[None, :]).astype(np.float32)
    half = np.concatenate([upper, np.ones_like(upper)], axis=1)
    cum_mat = jnp.asarray(half, dtype=BF16)
    hd = np.arange(PAIR) // HEAD_DIM
    e2 = jnp.asarray((hd[:, None] == hd[None, :]).astype(np.float32), dtype=BF16)
    qps = min(SB_Q_PER_STEP, nq)
    assert qps % 2 == 0 and nq % qps == 0, "query blocks are processed in adjacent pairs"
    qrows = qps * SB_BLOCK
    return pl.pallas_call(
        functools.partial(_sb_kernel, q_per_step=qps),
        out_shape=jax.ShapeDtypeStruct((bsz, seq, width), BF16),
        grid=(bsz, n_pairs, nq // qps),
        in_specs=[pl.BlockSpec((1, qrows, PAIR), lambda b, j, t: (b, t, j)),
                  pl.BlockSpec((1, seq, PAIR), lambda b, j, t: (b, 0, n_pairs + j)),
                  pl.BlockSpec((1, seq, PAIR), lambda b, j, t: (b, 0, 2 * n_pairs + j)),
                  pl.BlockSpec((1, PAIR), lambda b, j, t: (0, j)),
                  pl.BlockSpec(cum_mat.shape, lambda b, j, t: (0, 0)),
                  pl.BlockSpec(e2.shape, lambda b, j, t: (0, 0))],
        out_specs=pl.BlockSpec((1, qrows, PAIR), lambda b, j, t: (b, t, j)),
        compiler_params=_params(("parallel", "parallel", "arbitrary")),
        name="stickbreak_attn",
    )(qkv, qkv, qkv, gain.reshape(1, width), cum_mat, e2)


def _outproj_kernel(x_ref, ya_ref, yb_ref, wa_ref, wb_ref, g1_ref, gain_ref, sc_ref, sh_ref,
                    wr_ref, br_ref, x1_ref, h_ref, lg_ref):
    mix = _dot(ya_ref[...], wa_ref[...]) + _dot(yb_ref[...], wb_ref[...])
    x1 = x_ref[...] + g1_ref[0] * mix
    x1_ref[...] = x1
    ms = jnp.mean(x1 * x1, axis=-1, keepdims=True)
    h = x1 * lax.rsqrt(ms + RMS_EPS) * gain_ref[...] * (1.0 + sc_ref[0]) + sh_ref[0]
    h_ref[...] = h.astype(h_ref.dtype)
    lg_ref[...] = _mm(h, wr_ref[...], 3) + br_ref[...]


def _outproj_norm_router(x2, ya, yb, w_a, w_b, g1, gain, sc, sh, w_r, b_r, seq, tm):
    t, d = x2.shape
    half = ya.shape[1]
    nr = w_r.shape[1]
    per_b = seq // tm
    row = lambda w: pl.BlockSpec((tm, w), lambda i: (i, 0))
    full = lambda shape: pl.BlockSpec(shape, lambda i: (0,) * len(shape))
    per_batch = pl.BlockSpec((1, 1, d), lambda i: (i // per_b, 0, 0))
    return pl.pallas_call(
        _outproj_kernel,
        out_shape=(jax.ShapeDtypeStruct((t, d), F32), jax.ShapeDtypeStruct((t, d), F32),
                   jax.ShapeDtypeStruct((t, nr), F32)),
        grid=(t // tm,),
        in_specs=[row(d), row(half), row(half), full((half, d)), full((half, d)), per_batch,
                  full((1, d)), per_batch, per_batch, full((d, nr)), full((1, nr))],
        out_specs=(row(d), row(d), row(nr)),
        compiler_params=_params(("parallel",)),
        name="outproj_norm_router",
    )(x2, ya, yb, w_a, w_b, g1, gain.reshape(1, d), sc, sh, w_r, b_r)


MOE_ROWS = 256
MOE_VMEM_LIMIT = 58 * 1024 * 1024
_DMA_UNROLL = 8
_CAST_STEPS = 8


def _moe_kernel(be_ref, nxt_ref, nused_ref, gcur_ref, gnext_ref, scur_ref, sprev_ref, wt_ref,
                h_hbm, wg_hbm, wu_hbm, wd_hbm, y_hbm,
                xbuf, ybuf, wg_f, wu_f, wd_f, wg_b, wu_b, wd_b, gsem, ssem, wsem):
    i = pl.program_id(0)
    n_used = nused_ref[0]
    slot = i % 2
    de = wg_b.shape[1]
    d = wd_b.shape[1]

    def weight_copies(e):
        pairs = ((wg_hbm, wg_f), (wu_hbm, wu_f), (wd_hbm, wd_f))
        return [pltpu.make_async_copy(src.at[e], dst, wsem.at[k])
                for k, (src, dst) in enumerate(pairs)]

    def gather(idx_ref, r, s):
        return pltpu.make_async_copy(h_hbm.at[pl.ds(idx_ref[0, 0, r], 1), :],
                                     xbuf.at[s, pl.ds(r, 1), :], gsem.at[s])

    def scatter(idx_ref, r, s):
        return pltpu.make_async_copy(ybuf.at[s, pl.ds(r, 1), :],
                                     y_hbm.at[pl.ds(idx_ref[0, 0, r], 1), :], ssem.at[s])

    def for_rows(fn):
        def body(r, c):
            fn(r)
            return c
        lax.fori_loop(0, MOE_ROWS, body, 0, unroll=_DMA_UNROLL)

    def gather_wait(s):
        pltpu.make_async_copy(h_hbm.at[pl.ds(0, MOE_ROWS), :], xbuf.at[s], gsem.at[s]).wait()

    def scatter_wait(s):
        pltpu.make_async_copy(ybuf.at[s], y_hbm.at[pl.ds(0, MOE_ROWS), :], ssem.at[s]).wait()

    @pl.when(jnp.logical_and(i == 0, n_used > 0))
    def _():
        for_rows(lambda r: gather(gcur_ref, r, 0).start())
        for c in weight_copies(be_ref[0]):
            c.start()
        ybuf[1] = jnp.zeros(ybuf.shape[1:], ybuf.dtype)

    @pl.when(i < n_used)
    def _():
        e = be_ref[i]

        @pl.when(jnp.logical_or(i == 0, e != be_ref[jnp.maximum(i - 1, 0)]))
        def _():
            for c in weight_copies(e):
                c.wait()

            def cast_rows(k, c):
                for src, dst in ((wg_f, wg_b), (wu_f, wu_b), (wd_f, wd_b)):
                    nrow = src.shape[0] // _CAST_STEPS
                    rows = pl.ds(pl.multiple_of(k * nrow, nrow), nrow)
                    dst[rows, :] = src[rows, :].astype(BF16)
                return c
            lax.fori_loop(0, _CAST_STEPS, cast_rows, 0)

            @pl.when(nxt_ref[i] != e)
            def _():
                for c in weight_copies(nxt_ref[i]):
                    c.start(priority=1)

        gather_wait(slot)

        def sliced_issue(n_slices, make_copy, priorities):
            per = MOE_ROWS // n_slices
            state = [0]

            def issue():
                for r in range(state[0], state[0] + per):
                    make_copy(r).start(priority=r % priorities)
                state[0] += per
            return issue

        n_up = de // MXU_TILE
        n_down = d // MXU_TILE
        issue_gather = sliced_issue(2 * n_up, lambda r: gather(gnext_ref, r, 1 - slot), 1)
        issue_scatter = sliced_issue(n_down, lambda r: scatter(sprev_ref, r, 1 - slot), 2)

        xb = xbuf[slot].astype(BF16)
        hg, hu = [], []
        for n in range(n_up):
            cols = slice(n * MXU_TILE, (n + 1) * MXU_TILE)
            hg.append(_dot(xb, wg_b[:, cols]))
            issue_gather()
            hu.append(_dot(xb, wu_b[:, cols]))
            issue_gather()
        hg = jnp.concatenate(hg, axis=1)
        hid = (hg * _sigmoid(hg) * jnp.concatenate(hu, axis=1)).astype(BF16)

        @pl.when(i >= 1)
        def _():
            scatter_wait(slot)

        wt = wt_ref[...]
        for n in range(n_down):
            cols = slice(n * MXU_TILE, (n + 1) * MXU_TILE)
            ybuf[slot, :, cols] = _dot(hid, wd_b[:, cols]) * wt
            issue_scatter()

        @pl.when(i + 1 >= n_used)
        def _():
            gather_wait(1 - slot)
            scatter_wait(1 - slot)
            for_rows(lambda r: scatter(scur_ref, r, slot).start())
            scatter_wait(slot)


def _moe_experts(h2, gidx, sidx, row_w, block_e, next_e, n_used, w_gate, w_up, w_down):
    t, d = h2.shape
    de = w_gate.shape[2]
    n_blocks = gidx.shape[0]
    assert MOE_ROWS % (2 * (de // MXU_TILE)) == 0 == MOE_ROWS % (d // MXU_TILE)
    assert d % _CAST_STEPS == 0 == de % _CAST_STEPS and sidx.shape[0] == n_blocks + 1
    idx_spec = lambda f: pl.BlockSpec((1, 1, MOE_ROWS), f, memory_space=pltpu.SMEM)
    hbm = pl.BlockSpec(memory_space=pl.ANY)
    grid_spec = pltpu.PrefetchScalarGridSpec(
        num_scalar_prefetch=3,
        grid=(n_blocks,),
        in_specs=[idx_spec(lambda i, *_: (i, 0, 0)),
                  idx_spec(lambda i, *_: (jnp.minimum(i + 1, n_blocks - 1), 0, 0)),
                  idx_spec(lambda i, *_: (i, 0, 0)),
                  idx_spec(lambda i, *_: (jnp.where(i == 0, n_blocks, i - 1), 0, 0)),
                  pl.BlockSpec((MOE_ROWS, 1), lambda i, *_: (i, 0)),
                  hbm, hbm, hbm, hbm],
        out_specs=hbm,
        scratch_shapes=[pltpu.VMEM((2, MOE_ROWS, d), F32), pltpu.VMEM((2, MOE_ROWS, d), F32),
                        pltpu.VMEM((d, de), F32), pltpu.VMEM((d, de), F32),
                        pltpu.VMEM((de, d), F32),
                        pltpu.VMEM((d, de), BF16), pltpu.VMEM((d, de), BF16),
                        pltpu.VMEM((de, d), BF16),
                        pltpu.SemaphoreType.DMA((2,)), pltpu.SemaphoreType.DMA((2,)),
                        pltpu.SemaphoreType.DMA((3,))],
    )
    return pl.pallas_call(
        _moe_kernel,
        out_shape=jax.ShapeDtypeStruct((TOP_K_IN_GROUP * t + MOE_ROWS, d), F32),
        grid_spec=grid_spec,
        compiler_params=_params(("arbitrary",), MOE_VMEM_LIMIT),
        name="moe_experts",
    )(block_e, next_e, n_used, gidx, gidx, sidx, sidx, row_w.reshape(-1, 1),
      h2, w_gate, w_up, w_down)


def _final_kernel(x1_ref, ya_ref, yb_ref, g2_ref, gain_ref, o_ref):
    x2 = x1_ref[...] + g2_ref[0] * (ya_ref[...] + yb_ref[...])
    ms = jnp.mean(x2 * x2, axis=-1, keepdims=True)
    o_ref[...] = x2 * lax.rsqrt(ms + RMS_EPS) * gain_ref[...]


def _final_norm(x1, y2, g2, gain, seq, tm):
    t, d = x1.shape
    per_b = seq // tm
    nt = t // tm
    row = pl.BlockSpec((tm, d), lambda i: (i, 0))
    return pl.pallas_call(
        _final_kernel,
        out_shape=jax.ShapeDtypeStruct((t, d), F32),
        grid=(nt,),
        in_specs=[row, row, pl.BlockSpec((tm, d), lambda i: (i + nt, 0)),
                  pl.BlockSpec((1, 1, d), lambda i: (i // per_b, 0, 0)),
                  pl.BlockSpec((1, d), lambda i: (0, 0))],
        out_specs=row,
        compiler_params=_params(("parallel",)),
        name="final_norm",
    )(x1, y2, y2, g2, gain.reshape(1, d))


def _route(logits):
    t = logits.shape[0]
    gl = logits[:, :N_GROUPS]
    el = logits[:, N_GROUPS:N_GROUPS + N_EXPERTS].reshape(t, N_GROUPS, EXPERTS_PER_GROUP)
    pg = jax.nn.softmax(gl, axis=-1)
    g_sel = jnp.argmax(gl, axis=-1)
    p_sel = jnp.take_along_axis(pg, g_sel[:, None], axis=1)[:, 0]
    el_g = jnp.take_along_axis(el, g_sel[:, None, None], axis=1)[:, 0]
    top_v, top_i = lax.top_k(el_g, TOP_K_IN_GROUP)
    pair_w = jax.nn.softmax(top_v, axis=-1) * p_sel[:, None]
    e_id = (g_sel[:, None] * EXPERTS_PER_GROUP + top_i).astype(jnp.int32)
    flat_e = e_id.reshape(-1)
    flat_w = pair_w.reshape(-1)
    m = flat_e.shape[0]
    order = jnp.argsort(flat_e).astype(jnp.int32)
    experts = jnp.arange(N_EXPERTS, dtype=jnp.int32)
    counts = jnp.sum((flat_e[:, None] == experts[None, :]).astype(jnp.int32), axis=0)
    starts = jnp.cumsum(counts) - counts
    padded = (counts + MOE_ROWS - 1) // MOE_ROWS * MOE_ROWS
    pends = jnp.cumsum(padded)
    pstarts = pends - padded
    n_blocks = (m + N_EXPERTS * (MOE_ROWS - 1) + MOE_ROWS - 1) // MOE_ROWS
    block_start = jnp.arange(n_blocks, dtype=jnp.int32) * MOE_ROWS
    block_e = jnp.minimum(jnp.sum((block_start[:, None] >= pends[None, :]).astype(jnp.int32),
                                  axis=1), N_EXPERTS - 1)
    blk = jnp.arange(n_blocks, dtype=jnp.int32)[:, None]
    rin = jnp.arange(MOE_ROWS, dtype=jnp.int32)[None, :]
    off = blk * MOE_ROWS + rin - pstarts[block_e][:, None]
    valid = off < counts[block_e][:, None]
    src = jnp.clip(starts[block_e][:, None] + off, 0, m - 1)
    assign = order[src]
    tok = assign // TOP_K_IN_GROUP
    gidx = jnp.where(valid, tok, 0)
    spare = TOP_K_IN_GROUP * t + rin
    sidx = jnp.where(valid, (assign % TOP_K_IN_GROUP) * t + tok, spare)
    sidx = jnp.concatenate([sidx, spare], axis=0)
    row_w = jnp.where(valid, flat_w[assign], 0.0)
    n_used = (pends[-1] // MOE_ROWS).astype(jnp.int32).reshape(1)
    later = (experts[None, :] > experts[:, None]) & (counts[None, :] > 0)
    next_expert = jnp.min(jnp.where(later, experts[None, :], N_EXPERTS), axis=1)
    next_expert = jnp.where(next_expert == N_EXPERTS, experts, next_expert)
    return (gidx.reshape(n_blocks, 1, MOE_ROWS), sidx.reshape(n_blocks + 1, 1, MOE_ROWS), row_w,
            block_e, next_expert[block_e], n_used)


def _pad_cols(w, n):
    return jnp.pad(w, ((0, 0), (0, n - w.shape[1])))


def _pad_rows(w, n):
    return jnp.pad(w, ((0, n - w.shape[0]), (0, 0)))


def _layer(x, mod, norm1_gain, w_in, shift_mu, w0, w_decay_up, a0, w_iclr_up, w_gate_up,
           k_k, k_a, r_k, ln_x_gain, ln_x_bias, sb_norm_gain, w_out, norm2_gain,
           w_router_group, b_router_group, w_router_expert, b_router_expert,
           w_exp_gate, w_exp_up, w_exp_down, *, rwkv_ts, tm_in, tm_out):
    bsz, seq, d = x.shape
    t = bsz * seq
    rw = w0.shape[0]
    sbw = sb_norm_gain.shape[0]
    sh1, sc1, g1, sh2, sc2, g2 = [m.reshape(bsz, 1, d) for m in jnp.split(mod, 6, axis=-1)]

    o = 3 * rw
    seg = lambda a, b: w_in[:, a:b]
    w_rwkv = jnp.concatenate([
        seg(0, o),
        _pad_cols(seg(o, o + DECAY_LORA), LANE),
        _pad_cols(seg(o + DECAY_LORA, o + DECAY_LORA + ICLR_LORA), LANE),
        _pad_cols(seg(o + DECAY_LORA + ICLR_LORA, o + DECAY_LORA + ICLR_LORA + GATE_LORA),
                  2 * LANE)], axis=1).astype(BF16)
    rcols = o + DECAY_LORA + ICLR_LORA + GATE_LORA
    w_sb = w_in[:, rcols:].astype(BF16)
    mseg = lambda a, b: shift_mu[a:b][None, :]
    mu = jnp.concatenate([
        mseg(0, o),
        _pad_cols(mseg(o, o + DECAY_LORA), LANE),
        _pad_cols(mseg(o + DECAY_LORA, o + DECAY_LORA + ICLR_LORA), LANE),
        _pad_cols(mseg(o + DECAY_LORA + ICLR_LORA, rcols), 2 * LANE)], axis=1)

    x2 = x.reshape(t, d)
    p_rwkv = _normmod_matmul(x2, norm1_gain, sc1, sh1, w_rwkv, F32, seq, tm_in, 512)
    qkv = _normmod_matmul(x2, norm1_gain, sc1, sh1, w_sb, BF16, seq, tm_in, 512)

    vecs = jnp.stack([w0, a0, k_k, k_a, r_k.reshape(-1), ln_x_gain, ln_x_bias,
                      jnp.zeros_like(w0)])
    y_a = _rwkv_time_mix(p_rwkv.reshape(bsz, seq, -1), mu, vecs,
                         _pad_rows(w_decay_up, LANE), _pad_rows(w_iclr_up, LANE),
                         _pad_rows(w_gate_up, 2 * LANE), ts=rwkv_ts)
    y_b = _sb_attention(qkv.reshape(bsz, seq, -1), sb_norm_gain, sbw)

    w_r = _pad_cols(jnp.concatenate([w_router_group, w_router_expert], axis=1), LANE)
    b_r = _pad_cols(jnp.concatenate([b_router_group, b_router_expert])[None, :], LANE)
    w_o = w_out.astype(BF16)
    x1, h2, logits = _outproj_norm_router(
        x2, y_a.reshape(t, rw), y_b.reshape(t, sbw), w_o[:rw], w_o[rw:], g1, norm2_gain,
        sc2, sh2, w_r, b_r, seq, tm_out)

    gidx, sidx, row_w, block_e, next_e, n_used = _route(logits)
    y2 = _moe_experts(h2, gidx, sidx, row_w, block_e, next_e, n_used,
                      w_exp_gate, w_exp_up, w_exp_down)
    return x1, y2, g2


def kernel(x, c, w_ada, b_ada, norm1_gain, w_in, shift_mu, w0, w_decay_up, a0, w_iclr_up, w_gate_up, k_k, k_a, r_k, ln_x_gain, ln_x_bias, sb_norm_gain, w_out, norm2_gain, w_router_group, b_router_group, w_router_expert, b_router_expert, w_exp_gate, w_exp_up, w_exp_down, final_norm_gain):
    bsz, seq, d = x.shape
    assert w_ada.shape[0] == 1, "the final norm is fused into the single layer's last kernel"
    l = 0
    tiles = dict(rwkv_ts=min(256, seq), tm_in=min(1024, seq), tm_out=min(256, seq))
    mod = _ada_mod(c, w_ada[l], b_ada[l])
    x1, y2, g2 = _layer(
        x, mod, norm1_gain[l], w_in[l], shift_mu[l], w0[l], w_decay_up[l], a0[l],
        w_iclr_up[l], w_gate_up[l], k_k[l], k_a[l], r_k[l], ln_x_gain[l], ln_x_bias[l],
        sb_norm_gain[l], w_out[l], norm2_gain[l], w_router_group[l], b_router_group[l],
        w_router_expert[l], b_router_expert[l], w_exp_gate[l], w_exp_up[l],
        w_exp_down[l], **tiles)
    out = _final_norm(x1, y2, g2, final_norm_gain, seq, min(512, seq))
    return out.reshape(bsz, seq, d)
```

```python
import functools

import jax
import jax.numpy as jnp
import numpy as np
from jax import lax
from jax.experimental import pallas as pl
from jax.experimental.pallas import tpu as pltpu

F32 = jnp.float32
BF16 = jnp.bfloat16

RMS_EPS = 1e-6
GN_EPS = 64e-5
L2_EPS = 1e-12

HEAD_DIM = 64
PAIR = 2 * HEAD_DIM
RWKV_CHUNK = 64
SB_BLOCK = 128
N_GROUPS = 8
EXPERTS_PER_GROUP = 8
N_EXPERTS = N_GROUPS * EXPERTS_PER_GROUP
TOP_K_IN_GROUP = 2
MOE_BLOCK = 128
DECAY_LORA = 64
ICLR_LORA = 64
GATE_LORA = 160
LANE = 128
MXU_TILE = 256
VMEM_LIMIT = 48 * 1024 * 1024
SB_UNDERFLOW_LOG = -104.0


def _dot(a, b):
    return lax.dot_general(a, b, (((1,), (0,)), ((), ())), preferred_element_type=F32)


def _dot_nt(a, b):
    return lax.dot_general(a, b, (((1,), (1,)), ((), ())), preferred_element_type=F32)


def _split2(x):
    hi = x.astype(BF16)
    lo = (x - hi.astype(F32)).astype(BF16)
    return hi, lo


def _mm(a, b, passes, nt=False):
    d = _dot_nt if nt else _dot
    if passes == 1:
        return d(a.astype(BF16), b.astype(BF16))
    ah, al = _split2(a)
    bh, bl = _split2(b)
    return d(ah, bh) + d(ah, bl) + d(al, bh)


def _mm_exact_rhs(a, b_exact):
    hi, lo = _split2(a)
    return _dot(hi, b_exact) + _dot(lo, b_exact)


def _mm_exact_lhs(a_exact, b):
    hi, lo = _split2(b)
    return _dot(a_exact, hi) + _dot(a_exact, lo)


def _group_sum(x, e_blk):
    w = e_blk.shape[0]
    return jnp.concatenate([_mm_exact_rhs(x[:, g * w:(g + 1) * w], e_blk)
                            for g in range(x.shape[1] // w)], axis=1)


def _sigmoid(x):
    return 1.0 / (1.0 + jnp.exp(-x))


def _softplus(x):
    return jnp.maximum(x, 0.0) + jnp.log(1.0 + jnp.exp(-jnp.abs(x)))


def _params(sem, vmem=VMEM_LIMIT):
    return pltpu.CompilerParams(dimension_semantics=sem, vmem_limit_bytes=vmem)


def _ada_kernel(c_ref, w_ref, b_ref, o_ref):
    c = c_ref[...]
    s = c * _sigmoid(c)
    o_ref[...] = _mm(s, w_ref[...], 3) + b_ref[...]


def _ada_mod(c, w, b):
    bsz, d = c.shape
    n = w.shape[1]
    rows = 8
    cp = jnp.zeros((rows, d), F32).at[:bsz].set(c)
    tn = 1024
    out = pl.pallas_call(
        _ada_kernel,
        out_shape=jax.ShapeDtypeStruct((rows, n), F32),
        grid=(n // tn,),
        in_specs=[pl.BlockSpec((rows, d), lambda j: (0, 0)),
                  pl.BlockSpec((d, tn), lambda j: (0, j)),
                  pl.BlockSpec((1, tn), lambda j: (0, j))],
        out_specs=pl.BlockSpec((rows, tn), lambda j: (0, j)),
        compiler_params=_params(("arbitrary",)),
        name="ada_mod",
    )(cp, w, b.reshape(1, n))
    return out[:bsz]


_NORM_SLAB = 128


def _normmod_mm_kernel(x_ref, gain_ref, sc_ref, sh_ref, w_ref, o_ref, h_scr):
    @pl.when(pl.program_id(1) == 0)
    def _():
        scale = gain_ref[...] * (1.0 + sc_ref[0])
        shift = sh_ref[0]
        slab = min(_NORM_SLAB, x_ref.shape[0])

        def norm_rows(k, c):
            rows = pl.ds(pl.multiple_of(k * slab, slab), slab)
            xf = x_ref[rows, :]
            ms = jnp.mean(xf * xf, axis=-1, keepdims=True)
            h_scr[rows, :] = (xf * lax.rsqrt(ms + RMS_EPS) * scale + shift).astype(BF16)
            return c
        lax.fori_loop(0, x_ref.shape[0] // slab, norm_rows, 0, unroll=2)

    o_ref[...] = _dot(h_scr[...], w_ref[pl.program_id(1)]).astype(o_ref.dtype)


def _normmod_matmul(x2, gain, sc, sh, w, out_dtype, seq, tm, tn):
    t, d = x2.shape
    n = w.shape[1]
    per_b = seq // tm
    w_tiles = w.reshape(d, n // tn, tn).transpose(1, 0, 2)
    return pl.pallas_call(
        _normmod_mm_kernel,
        out_shape=jax.ShapeDtypeStruct((t, n), out_dtype),
        grid=(t // tm, n // tn),
        in_specs=[pl.BlockSpec((tm, d), lambda i, j: (i, 0)),
                  pl.BlockSpec((1, d), lambda i, j: (0, 0)),
                  pl.BlockSpec((1, 1, d), lambda i, j: (i // per_b, 0, 0)),
                  pl.BlockSpec((1, 1, d), lambda i, j: (i // per_b, 0, 0)),
                  pl.BlockSpec(w_tiles.shape, lambda i, j: (0, 0, 0),
                               pipeline_mode=pl.Buffered(1))],
        out_specs=pl.BlockSpec((tm, tn), lambda i, j: (i, j)),
        scratch_shapes=[pltpu.VMEM((tm, d), BF16)],
        compiler_params=_params(("parallel", "arbitrary")),
        name="normmod_proj",
    )(x2, gain.reshape(1, d), sc, sh, w_tiles)


_V_W0, _V_A0, _V_KK, _V_KA, _V_RK, _V_LNG, _V_LNB = range(7)
_M_STRICT, _M_INCL, _M_BD8, _M_OFF8, _M_OFF16, _M_OFF32, _M_EYE = range(7)

RWKV_PASSES = 1


def _rwkv_masks():
    i = np.arange(PAIR)[:, None]
    j = np.arange(PAIR)[None, :]
    strict = (j < i)
    incl = (j <= i)
    bd8 = (i // 8 == j // 8)
    def off(b):
        return (i // (2 * b) == j // (2 * b)) & (i // b > j // b)
    eye = (i == j)
    m = np.stack([strict, incl, bd8 & strict, off(8), off(16), off(32), eye]).astype(np.float32)
    return jnp.asarray(m)


def _tri_inverse(lmats, masks_ref, passes):
    eye = masks_ref[_M_EYE]
    bd8 = masks_ref[_M_BD8]
    mm = lambda a, b: _mm(a, b, passes)
    dblk = [l * bd8 for l in lmats]
    x = [eye + d for d in dblk]
    p = [mm(d, d) for d in dblk]
    x = [xi + mm(pi, xi) for xi, pi in zip(x, p)]
    p = [mm(pi, pi) for pi in p]
    x = [xi + mm(pi, xi) for xi, pi in zip(x, p)]
    for plane in (_M_OFF8, _M_OFF16, _M_OFF32):
        mask = masks_ref[plane]
        t = [mm(xi, l * mask) for xi, l in zip(x, lmats)]
        x = [xi + mm(ti, xi) for xi, ti in zip(x, t)]
    return x


def _rwkv_kernel(p_ref, mu_ref, vec_ref, wd_ref, wa_ref, wg_ref, e_ref, tri_ref,
                 masks_ref, o_ref,
                 rt_s, at_s, bt_s, kt_s, v_s, bd_s, kd_s, gc_s, y_s, g_s, bon_s,
                 state_s, carry_s, *, ts, n_pairs, passes):
    s_idx = pl.program_id(1)

    @pl.when(s_idx == 0)
    def _():
        state_s[...] = jnp.zeros_like(state_s)
        carry_s[...] = jnp.zeros_like(carry_s)

    width = n_pairs * PAIR
    pt = p_ref[0]
    row = lax.broadcasted_iota(jnp.int32, pt.shape, 0)
    prev = jnp.where(row == 0, carry_s[...], pltpu.roll(pt, 1, 0))
    carry_s[...] = pt[ts - 1:ts, :]
    pm = pt + (prev - pt) * mu_ref[...]

    r = pm[:, 0:width]
    k = pm[:, width:2 * width]
    v = pm[:, 2 * width:3 * width]
    o3 = 3 * width
    xw = pm[:, o3:o3 + LANE]
    xa = pm[:, o3 + LANE:o3 + 2 * LANE]
    xg = pm[:, o3 + 2 * LANE:o3 + 4 * LANE]

    vec = lambda i: vec_ref[i:i + 1, :]
    e_mat = e_ref[...]
    wlog = -_softplus(-(vec(_V_W0) + _mm(jnp.tanh(xw), wd_ref[...], 3))) - 0.5
    logw = -jnp.exp(wlog)
    a = _sigmoid(vec(_V_A0) + _mm(xa, wa_ref[...], 1))
    g_s[...] = _mm(_sigmoid(xg), wg_ref[...], 1)
    kk = k * vec(_V_KK)
    ss = _group_sum(kk * kk, e_mat)
    kk = kk / jnp.maximum(jnp.sqrt(ss), L2_EPS)
    kp = k * (1.0 + (a - 1.0) * vec(_V_KA))
    ib = kk * a
    bon_s[...] = _group_sum(r * kp * vec(_V_RK), e_mat) * v
    cum = _mm_exact_lhs(tri_ref[...], logw)
    tot = jnp.concatenate(
        [jnp.broadcast_to(cum[c * RWKV_CHUNK + RWKV_CHUNK - 1:(c + 1) * RWKV_CHUNK, :],
                          (RWKV_CHUNK, width)) for c in range(ts // RWKV_CHUNK)], axis=0)
    rt_s[...] = r * jnp.exp(cum)
    at_s[...] = -kk * jnp.exp(cum - logw)
    dec_in = jnp.exp(-cum)
    bt_s[...] = ib * dec_in
    kt_s[...] = kp * dec_in
    v_s[...] = v
    dec_out = jnp.exp(tot - cum)
    bd_s[...] = ib * dec_out
    kd_s[...] = kp * dec_out
    gc_s[...] = jnp.exp(tot)

    lane = lax.broadcasted_iota(jnp.int32, (RWKV_CHUNK, PAIR), 1)
    m0 = (lane < HEAD_DIM).astype(F32)
    m1 = 1.0 - m0
    strict = masks_ref[_M_STRICT]
    incl = masks_ref[_M_INCL]

    def chunk_body(c, carry):
        row0 = pl.multiple_of(c * RWKV_CHUNK, RWKV_CHUNK)
        rows = pl.ds(row0, RWKV_CHUNK)
        pairs = range(n_pairs)
        lanes = [slice(j * PAIR, (j + 1) * PAIR) for j in pairs]
        mm = lambda x, w: _mm(x, w, passes)
        mm_nt = lambda x, w: _mm(x, w, passes, nt=True)

        def stacked(ref):
            blks = [ref[rows, lanes[j]] for j in pairs]
            return [jnp.concatenate([b * m0, b * m1], axis=0) for b in blks]

        rh, ah, bh, kh = stacked(rt_s), stacked(at_s), stacked(bt_s), stacked(kt_s)
        vh, bdh, kdh = stacked(v_s), stacked(bd_s), stacked(kd_s)
        st = [state_s[j] for j in pairs]
        bk = [jnp.concatenate([bh[j], kh[j]], axis=0) for j in pairs]
        ga = [mm_nt(ah[j], bk[j]) for j in pairs]
        gr = [mm_nt(rh[j], bk[j]) for j in pairs]
        tinv = _tri_inverse([ga[j][:, :PAIR] * strict for j in pairs], masks_ref, passes)
        x0 = [mm_nt(ah[j], st[j]) + mm(ga[j][:, PAIR:] * strict, vh[j]) for j in pairs]
        u = [mm(tinv[j], x0[j]) for j in pairs]
        y = [mm_nt(rh[j], st[j]) + mm(gr[j][:, :PAIR] * incl, u[j])
             + mm(gr[j][:, PAIR:] * incl, vh[j]) for j in pairs]
        for j in pairs:
            zt = jnp.concatenate([u[j], vh[j]], axis=0).T
            bkd = jnp.concatenate([bdh[j], kdh[j]], axis=0)
            gc = gc_s[pl.ds(row0, 1), lanes[j]]
            state_s[j] = st[j] * gc + mm(zt, bkd)
            y_s[rows, lanes[j]] = y[j][:RWKV_CHUNK] + y[j][RWKV_CHUNK:]
        return carry

    lax.fori_loop(0, ts // RWKV_CHUNK, chunk_body, 0)

    y = y_s[...]
    inv_n = 1.0 / HEAD_DIM
    mean = _group_sum(y, e_mat) * inv_n
    dlt = y - mean
    var = _group_sum(dlt * dlt, e_mat) * inv_n
    yn = dlt * lax.rsqrt(var + GN_EPS) * vec(_V_LNG) + vec(_V_LNB)
    o_ref[0] = ((yn + bon_s[...]) * g_s[...]).astype(o_ref.dtype)


def _rwkv_time_mix(p, mu, vecs, wd, wa, wg, *, ts, passes=RWKV_PASSES):
    bsz, seq, cols = p.shape
    width = vecs.shape[1]
    n_pairs = width // PAIR
    heads = np.arange(MXU_TILE) // HEAD_DIM
    e_mat = jnp.asarray((heads[:, None] == heads[None, :]).astype(np.float32), dtype=BF16)
    tok = np.arange(ts)
    same = (tok[:, None] // RWKV_CHUNK) == (tok[None, :] // RWKV_CHUNK)
    tri = jnp.asarray((same & (tok[None, :] <= tok[:, None])).astype(np.float32), dtype=BF16)
    masks = _rwkv_masks()
    full = lambda shape: pl.BlockSpec(shape, lambda b, s: (0,) * len(shape))
    big = lambda: pltpu.VMEM((ts, width), F32)
    kern = functools.partial(_rwkv_kernel, ts=ts, n_pairs=n_pairs, passes=passes)
    return pl.pallas_call(
        kern,
        out_shape=jax.ShapeDtypeStruct((bsz, seq, width), BF16),
        grid=(bsz, seq // ts),
        in_specs=[pl.BlockSpec((1, ts, cols), lambda b, s: (b, s, 0)),
                  full((1, cols)), full(vecs.shape), full(wd.shape), full(wa.shape),
                  full(wg.shape), full(e_mat.shape), full(tri.shape), full(masks.shape)],
        out_specs=pl.BlockSpec((1, ts, width), lambda b, s: (b, s, 0)),
        scratch_shapes=[big() for _ in range(11)]
        + [pltpu.VMEM((n_pairs, PAIR, PAIR), F32), pltpu.VMEM((1, cols), F32)],
        compiler_params=_params(("parallel", "arbitrary")),
        name="rwkv7_scan",
    )(p, mu, vecs, wd, wa, wg, e_mat, tri, masks)


SB_WINDOW = 6
SB_TAIL = 2
SB_Q_PER_STEP = 4


def _sb_kernel(q_ref, k_ref, v_ref, gain_ref, cum_ref, e2_ref, o_ref, *, q_per_step):
    lane = lax.broadcasted_iota(jnp.int32, (SB_BLOCK, PAIR), 1)
    h0 = lane < HEAD_DIM
    zero = jnp.zeros((), BF16)
    qi = lax.broadcasted_iota(jnp.int32, (SB_BLOCK, 2 * SB_BLOCK), 0)
    ki = lax.broadcasted_iota(jnp.int32, (SB_BLOCK, 2 * SB_BLOCK), 1) % SB_BLOCK
    causal = ki < qi
    cum_mat = cum_ref[...]
    scale = jnp.asarray(HEAD_DIM ** -0.5, BF16)
    width2 = 2 * SB_BLOCK

    heads = [slice(h * SB_BLOCK, (h + 1) * SB_BLOCK) for h in range(2)]

    def load_blocks(kb_top, n):
        khats, vhats, valid = [], [], []
        for j in range(n):
            kb = kb_top - j
            valid.append(kb >= 0)
            rows = pl.ds(pl.multiple_of(jnp.maximum(kb, 0) * SB_BLOCK, SB_BLOCK), SB_BLOCK)
            kblk = k_ref[0, rows, :]
            vblk = v_ref[0, rows, :]
            khats.append(jnp.concatenate([jnp.where(h0, kblk, zero), jnp.where(h0, zero, kblk)],
                                         axis=0))
            vhats.append(jnp.concatenate([jnp.where(h0, vblk, zero), jnp.where(h0, zero, vblk)],
                                         axis=0))
        return khats, vhats, valid

    def score_stage(zs, keeps):
        log_beta = [jnp.minimum(z, 0.0) - jnp.log(1.0 + jnp.exp(-jnp.abs(z))) for z in zs]
        log_1m = [lb - z for lb, z in zip(log_beta, zs)]
        log_1m = [l if m is None else jnp.where(m, l, 0.0) for l, m in zip(log_1m, keeps)]
        his = [l.astype(BF16) for l in log_1m]
        parts = [[_dot(hi[:, sl], cum_mat) for sl in heads] for hi in his]
        return log_beta, parts

    def weight_stage(log_beta, parts, keep, rest):
        between = jnp.concatenate([parts[0][:, :SB_BLOCK], parts[1][:, :SB_BLOCK]], axis=1)
        total = jnp.concatenate([parts[0][:, SB_BLOCK:], parts[1][:, SB_BLOCK:]], axis=1)
        w = jnp.exp(log_beta + between + rest)
        if keep is not None:
            w = jnp.where(keep, w, 0.0)
        return w.astype(BF16), rest + total

    def valid_mask(flag):
        return jnp.broadcast_to(flag, causal.shape)

    def sweep(q, kb_top, n, acc, rest):
        khats, vhats, valid = load_blocks(kb_top, n)
        z_all = _dot_nt(q, jnp.concatenate(khats, axis=0))
        keeps = [None] + [valid_mask(valid[j]) for j in range(1, n)]
        zs = [z_all[:, j * width2:(j + 1) * width2] for j in range(n)]
        log_beta, parts = score_stage(zs, keeps)
        ws = []
        for j in range(n):
            w, rest = weight_stage(log_beta[j], parts[j], keeps[j], rest)
            ws.append(w)
        acc = acc + _dot(jnp.concatenate(ws, axis=1), jnp.concatenate(vhats, axis=0))
        return acc, rest

    def window_pair(q2, qb_a, all_valid):
        n = SB_WINDOW
        khats, vhats, valid = load_blocks(qb_a + 1, n + 1)
        z_all = _dot_nt(q2, jnp.concatenate(khats, axis=0))
        off_diag = (lambda flag: None) if all_valid else valid_mask
        items = []
        for j in range(n):
            col = lambda jj: slice(jj * width2, (jj + 1) * width2)
            za = z_all[:SB_BLOCK, col(j + 1)]
            zb = z_all[SB_BLOCK:, col(j)]
            items.append((0, za, causal if j == 0 else off_diag(valid[j + 1])))
            items.append((1, zb, causal if j == 0 else off_diag(valid[j])))
        log_beta, parts = score_stage([it[1] for it in items], [it[2] for it in items])
        rest = [jnp.zeros((SB_BLOCK, width2), F32) for _ in range(2)]
        ws = [[], []]
        for idx, (s, _, keep) in enumerate(items):
            w, rest[s] = weight_stage(log_beta[idx], parts[idx], keep, rest[s])
            ws[s].append(w)
        acc_a = _dot(jnp.concatenate(ws[0], axis=1), jnp.concatenate(vhats[1:], axis=0))
        acc_b = _dot(jnp.concatenate(ws[1], axis=1), jnp.concatenate(vhats[:n], axis=0))
        return (acc_a, rest[0]), (acc_b, rest[1])

    def finish(q, qb, acc, rest, qrows):
        def cond(c):
            i, _, _, live = c
            return jnp.logical_and(i <= qb, live > SB_UNDERFLOW_LOG)

        def body(c):
            i, acc, rest, _ = c
            acc, rest = sweep(q, qb - i, SB_TAIL, acc, rest)
            return i + SB_TAIL, acc, rest, jnp.max(rest)

        _, acc, _, _ = lax.while_loop(cond, body,
                                      (jnp.int32(SB_WINDOW), acc, rest, jnp.max(rest)))
        ms = _mm_exact_rhs(acc * acc, e2_ref[...]) * (1.0 / HEAD_DIM)
        o_ref[0, qrows, :] = (acc * lax.rsqrt(ms + RMS_EPS) * gain_ref[...]).astype(o_ref.dtype)

    def q_pair(it, all_valid):
        qb_a = pl.program_id(2) * q_per_step + 2 * it
        row0 = pl.multiple_of(2 * it * SB_BLOCK, 2 * SB_BLOCK)
        q2 = q_ref[0, pl.ds(row0, 2 * SB_BLOCK), :] * scale
        (acc_a, rest_a), (acc_b, rest_b) = window_pair(q2, qb_a, all_valid)
        finish(q2[:SB_BLOCK], qb_a, acc_a, rest_a, pl.ds(row0, SB_BLOCK))
        finish(q2[SB_BLOCK:], qb_a + 1, acc_b, rest_b, pl.ds(row0 + SB_BLOCK, SB_BLOCK))

    def run(all_valid):
        def body(it, carry):
            q_pair(it, all_valid)
            return carry
        lax.fori_loop(0, q_per_step // 2, body, 0)

    first_full = -(-(SB_WINDOW - 1) // q_per_step)
    pl.when(pl.program_id(2) < first_full)(lambda: run(False))
    pl.when(pl.program_id(2) >= first_full)(lambda: run(True))


def _sb_attention(qkv, gain, width):
    bsz, seq, _ = qkv.shape
    n_pairs = width // PAIR
    nq = seq // SB_BLOCK
    i = np.arange(SB_BLOCK)
    upper = (i[:, None] > i[None, :]).astype(np.float32)
    half = np.concatenate([upper, np.ones_like(upper)], axis=1)
    cum_mat = jnp.asarray(half, dtype=BF16)
    hd = np.arange(PAIR) // HEAD_DIM
    e2 = jnp.asarray((hd[:, None] == hd[None, :]).astype(np.float32), dtype=BF16)
    qps = min(SB_Q_PER_STEP, nq)
    assert qps % 2 == 0 and nq % qps == 0, "query blocks are processed in adjacent pairs"
    qrows = qps * SB_BLOCK
    return pl.pallas_call(
        functools.partial(_sb_kernel, q_per_step=qps),
        out_shape=jax.ShapeDtypeStruct((bsz, seq, width), BF16),
        grid=(bsz, n_pairs, nq // qps),
        in_specs=[pl.BlockSpec((1, qrows, PAIR), lambda b, j, t: (b, t, j)),
                  pl.BlockSpec((1, seq, PAIR), lambda b, j, t: (b, 0, n_pairs + j)),
                  pl.BlockSpec((1, seq, PAIR), lambda b, j, t: (b, 0, 2 * n_pairs + j)),
                  pl.BlockSpec((1, PAIR), lambda b, j, t: (0, j)),
                  pl.BlockSpec(cum_mat.shape, lambda b, j, t: (0, 0)),
                  pl.BlockSpec(e2.shape, lambda b, j, t: (0, 0))],
        out_specs=pl.BlockSpec((1, qrows, PAIR), lambda b, j, t: (b, t, j)),
        compiler_params=_params(("parallel", "parallel", "arbitrary")),
        name="stickbreak_attn",
    )(qkv, qkv, qkv, gain.reshape(1, width), cum_mat, e2)


def _route_rows(lg):
    lane = lax.broadcasted_iota(jnp.int32, lg.shape, 1).astype(F32)
    neg = -jnp.inf
    first = lambda hit: jnp.min(jnp.where(hit, lane, float(LANE)), axis=-1, keepdims=True)
    gl = jnp.where(lane < N_GROUPS, lg, neg)
    gmax = jnp.max(gl, axis=-1, keepdims=True)
    g_sel = first(gl == gmax)
    p_sel = 1.0 / jnp.sum(jnp.exp(gl - gmax), axis=-1, keepdims=True)
    lo = N_GROUPS + g_sel * EXPERTS_PER_GROUP
    el = jnp.where(jnp.logical_and(lane >= lo, lane < lo + EXPERTS_PER_GROUP), lg, neg)
    v1 = jnp.max(el, axis=-1, keepdims=True)
    i1 = first(el == v1)
    el2 = jnp.where(lane == i1, neg, el)
    v2 = jnp.max(el2, axis=-1, keepdims=True)
    i2 = first(el2 == v2)
    t2 = jnp.exp(v2 - v1)
    w1 = p_sel / (1.0 + t2)
    return i1 - N_GROUPS, i2 - N_GROUPS, w1, w1 * t2


def _outproj_kernel(x_ref, ya_ref, yb_ref, wa_ref, wb_ref, g1_ref, gain_ref, sc_ref, sh_ref,
                    wr_ref, br_ref, x1_ref, h_ref, rt_ref):
    mix = _dot(ya_ref[...], wa_ref[...]) + _dot(yb_ref[...], wb_ref[...])
    x1 = x_ref[...] + g1_ref[0] * mix
    x1_ref[...] = x1
    ms = jnp.mean(x1 * x1, axis=-1, keepdims=True)
    h = x1 * lax.rsqrt(ms + RMS_EPS) * gain_ref[...] * (1.0 + sc_ref[0]) + sh_ref[0]
    h_ref[...] = h.astype(h_ref.dtype)
    e1, e2, w1, w2 = _route_rows(_mm(h, wr_ref[...], 3) + br_ref[...])
    lane = lax.broadcasted_iota(jnp.int32, rt_ref.shape, 1)
    rt_ref[...] = jnp.where(lane == 0, e1, jnp.where(lane == 1, e2,
                                                      jnp.where(lane == 2, w1, w2)))


def _outproj_norm_router(x2, ya, yb, w_a, w_b, g1, gain, sc, sh, w_r, b_r, seq, tm):
    t, d = x2.shape
    half = ya.shape[1]
    nr = w_r.shape[1]
    per_b = seq // tm
    row = lambda w: pl.BlockSpec((tm, w), lambda i: (i, 0))
    full = lambda shape: pl.BlockSpec(shape, lambda i: (0,) * len(shape))
    per_batch = pl.BlockSpec((1, 1, d), lambda i: (i // per_b, 0, 0))
    return pl.pallas_call(
        _outproj_kernel,
        out_shape=(jax.ShapeDtypeStruct((t, d), F32), jax.ShapeDtypeStruct((t, d), F32),
                   jax.ShapeDtypeStruct((t, nr), F32)),
        grid=(t // tm,),
        in_specs=[row(d), row(half), row(half), full((half, d)), full((half, d)), per_batch,
                  full((1, d)), per_batch, per_batch, full((d, nr)), full((1, nr))],
        out_specs=(row(d), row(d), row(nr)),
        compiler_params=_params(("parallel",)),
        name="outproj_norm_router",
    )(x2, ya, yb, w_a, w_b, g1, gain.reshape(1, d), sc, sh, w_r, b_r)


MOE_ROWS = 256
MOE_VMEM_LIMIT = 58 * 1024 * 1024
_DMA_UNROLL = 8
_CAST_STEPS = 8


def _moe_kernel(be_ref, nxt_ref, nused_ref, gcur_ref, gnext_ref, scur_ref, sprev_ref, wt_ref,
                h_hbm, wg_hbm, wu_hbm, wd_hbm, y_hbm,
                xbuf, ybuf, wg_f, wu_f, wd_f, wg_b, wu_b, wd_b, gsem, ssem, wsem):
    i = pl.program_id(0)
    n_used = nused_ref[0]
    slot = i % 2
    de = wg_b.shape[1]
    d = wd_b.shape[1]

    def weight_copies(e):
        pairs = ((wg_hbm, wg_f), (wu_hbm, wu_f), (wd_hbm, wd_f))
        return [pltpu.make_async_copy(src.at[e], dst, wsem.at[k])
                for k, (src, dst) in enumerate(pairs)]

    def gather(idx_ref, r, s):
        return pltpu.make_async_copy(h_hbm.at[pl.ds(idx_ref[0, 0, r], 1), :],
                                     xbuf.at[s, pl.ds(r, 1), :], gsem.at[s])

    def scatter(idx_ref, r, s):
        return pltpu.make_async_copy(ybuf.at[s, pl.ds(r, 1), :],
                                     y_hbm.at[pl.ds(idx_ref[0, 0, r], 1), :], ssem.at[s])

    def for_rows(fn):
        def body(r, c):
            fn(r)
            return c
        lax.fori_loop(0, MOE_ROWS, body, 0, unroll=_DMA_UNROLL)

    def gather_wait(s):
        pltpu.make_async_copy(h_hbm.at[pl.ds(0, MOE_ROWS), :], xbuf.at[s], gsem.at[s]).wait()

    def scatter_wait(s):
        pltpu.make_async_copy(ybuf.at[s], y_hbm.at[pl.ds(0, MOE_ROWS), :], ssem.at[s]).wait()

    @pl.when(jnp.logical_and(i == 0, n_used > 0))
    def _():
        for_rows(lambda r: gather(gcur_ref, r, 0).start())
        for c in weight_copies(be_ref[0]):
            c.start()
        ybuf[1] = jnp.zeros(ybuf.shape[1:], ybuf.dtype)

    @pl.when(i < n_used)
    def _():
        e = be_ref[i]

        @pl.when(jnp.logical_or(i == 0, e != be_ref[jnp.maximum(i - 1, 0)]))
        def _():
            for c in weight_copies(e):
                c.wait()

            def cast_rows(k, c):
                for src, dst in ((wg_f, wg_b), (wu_f, wu_b), (wd_f, wd_b)):
                    nrow = src.shape[0] // _CAST_STEPS
                    rows = pl.ds(pl.multiple_of(k * nrow, nrow), nrow)
                    dst[rows, :] = src[rows, :].astype(BF16)
                return c
            lax.fori_loop(0, _CAST_STEPS, cast_rows, 0)

            @pl.when(nxt_ref[i] != e)
            def _():
                for c in weight_copies(nxt_ref[i]):
                    c.start(priority=1)

        gather_wait(slot)

        def sliced_issue(n_slices, make_copy, priorities):
            per = MOE_ROWS // n_slices
            state = [0]

            def issue():
                for r in range(state[0], state[0] + per):
                    make_copy(r).start(priority=r % priorities)
                state[0] += per
            return issue

        n_up = de // MXU_TILE
        n_down = d // MXU_TILE
        issue_gather = sliced_issue(2 * n_up, lambda r: gather(gnext_ref, r, 1 - slot), 1)
        issue_scatter = sliced_issue(n_down, lambda r: scatter(sprev_ref, r, 1 - slot), 2)

        xb = xbuf[slot].astype(BF16)
        hg, hu = [], []
        for n in range(n_up):
            cols = slice(n * MXU_TILE, (n + 1) * MXU_TILE)
            hg.append(_dot(xb, wg_b[:, cols]))
            issue_gather()
            hu.append(_dot(xb, wu_b[:, cols]))
            issue_gather()
        hg = jnp.concatenate(hg, axis=1)
        hid = (hg * _sigmoid(hg) * jnp.concatenate(hu, axis=1)).astype(BF16)

        @pl.when(i >= 1)
        def _():
            scatter_wait(slot)

        wt = wt_ref[...]
        for n in range(n_down):
            cols = slice(n * MXU_TILE, (n + 1) * MXU_TILE)
            ybuf[slot, :, cols] = _dot(hid, wd_b[:, cols]) * wt
            issue_scatter()

        @pl.when(i + 1 >= n_used)
        def _():
            gather_wait(1 - slot)
            scatter_wait(1 - slot)
            for_rows(lambda r: scatter(scur_ref, r, slot).start())
            scatter_wait(slot)


def _moe_experts(h2, gidx, sidx, row_w, block_e, next_e, n_used, w_gate, w_up, w_down):
    t, d = h2.shape
    de = w_gate.shape[2]
    n_blocks = gidx.shape[0]
    assert MOE_ROWS % (2 * (de // MXU_TILE)) == 0 == MOE_ROWS % (d // MXU_TILE)
    assert d % _CAST_STEPS == 0 == de % _CAST_STEPS and sidx.shape[0] == n_blocks + 1
    idx_spec = lambda f: pl.BlockSpec((1, 1, MOE_ROWS), f, memory_space=pltpu.SMEM)
    hbm = pl.BlockSpec(memory_space=pl.ANY)
    grid_spec = pltpu.PrefetchScalarGridSpec(
        num_scalar_prefetch=3,
        grid=(n_blocks,),
        in_specs=[idx_spec(lambda i, *_: (i, 0, 0)),
                  idx_spec(lambda i, *_: (jnp.minimum(i + 1, n_blocks - 1), 0, 0)),
                  idx_spec(lambda i, *_: (i, 0, 0)),
                  idx_spec(lambda i, *_: (jnp.where(i == 0, n_blocks, i - 1), 0, 0)),
                  pl.BlockSpec((MOE_ROWS, 1), lambda i, *_: (i, 0)),
                  hbm, hbm, hbm, hbm],
        out_specs=hbm,
        scratch_shapes=[pltpu.VMEM((2, MOE_ROWS, d), F32), pltpu.VMEM((2, MOE_ROWS, d), F32),
                        pltpu.VMEM((d, de), F32), pltpu.VMEM((d, de), F32),
                        pltpu.VMEM((de, d), F32),
                        pltpu.VMEM((d, de), BF16), pltpu.VMEM((d, de), BF16),
                        pltpu.VMEM((de, d), BF16),
                        pltpu.SemaphoreType.DMA((2,)), pltpu.SemaphoreType.DMA((2,)),
                        pltpu.SemaphoreType.DMA((3,))],
    )
    return pl.pallas_call(
        _moe_kernel,
        out_shape=jax.ShapeDtypeStruct((TOP_K_IN_GROUP * t + MOE_ROWS, d), F32),
        grid_spec=grid_spec,
        compiler_params=_params(("arbitrary",), MOE_VMEM_LIMIT),
        name="moe_experts",
    )(block_e, next_e, n_used, gidx, gidx, sidx, sidx, row_w.reshape(-1, 1),
      h2, w_gate, w_up, w_down)


def _final_kernel(x1_ref, ya_ref, yb_ref, g2_ref, gain_ref, o_ref):
    x2 = x1_ref[...] + g2_ref[0] * (ya_ref[...] + yb_ref[...])
    ms = jnp.mean(x2 * x2, axis=-1, keepdims=True)
    o_ref[...] = x2 * lax.rsqrt(ms + RMS_EPS) * gain_ref[...]


def _final_norm(x1, y2, g2, gain, seq, tm):
    t, d = x1.shape
    per_b = seq // tm
    nt = t // tm
    row = pl.BlockSpec((tm, d), lambda i: (i, 0))
    return pl.pallas_call(
        _final_kernel,
        out_shape=jax.ShapeDtypeStruct((t, d), F32),
        grid=(nt,),
        in_specs=[row, row, pl.BlockSpec((tm, d), lambda i: (i + nt, 0)),
                  pl.BlockSpec((1, 1, d), lambda i: (i // per_b, 0, 0)),
                  pl.BlockSpec((1, d), lambda i: (0, 0))],
        out_specs=row,
        compiler_params=_params(("parallel",)),
        name="final_norm",
    )(x1, y2, y2, g2, gain.reshape(1, d))


def _route(routed):
    t = routed.shape[0]
    pair_w = routed[:, TOP_K_IN_GROUP:2 * TOP_K_IN_GROUP]
    flat_e = routed[:, :TOP_K_IN_GROUP].astype(jnp.int32).reshape(-1)
    flat_w = pair_w.reshape(-1)
    m = flat_e.shape[0]
    order = jnp.argsort(flat_e).astype(jnp.int32)
    experts = jnp.arange(N_EXPERTS, dtype=jnp.int32)
    counts = jnp.sum((flat_e[:, None] == experts[None, :]).astype(jnp.int32), axis=0)
    starts = jnp.cumsum(counts) - counts
    padded = (counts + MOE_ROWS - 1) // MOE_ROWS * MOE_ROWS
    pends = jnp.cumsum(padded)
    pstarts = pends - padded
    n_blocks = (m + N_EXPERTS * (MOE_ROWS - 1) + MOE_ROWS - 1) // MOE_ROWS
    block_start = jnp.arange(n_blocks, dtype=jnp.int32) * MOE_ROWS
    block_e = jnp.minimum(jnp.sum((block_start[:, None] >= pends[None, :]).astype(jnp.int32),
                                  axis=1), N_EXPERTS - 1)
    blk = jnp.arange(n_blocks, dtype=jnp.int32)[:, None]
    rin = jnp.arange(MOE_ROWS, dtype=jnp.int32)[None, :]
    off = blk * MOE_ROWS + rin - pstarts[block_e][:, None]
    valid = off < counts[block_e][:, None]
    src = jnp.clip(starts[block_e][:, None] + off, 0, m - 1)
    assign = order[src]
    tok = assign // TOP_K_IN_GROUP
    gidx = jnp.where(valid, tok, 0)
    spare = TOP_K_IN_GROUP * t + rin
    sidx = jnp.where(valid, (assign % TOP_K_IN_GROUP) * t + tok, spare)
    sidx = jnp.concatenate([sidx, spare], axis=0)
    row_w = jnp.where(valid, flat_w[assign], 0.0)
    n_used = (pends[-1] // MOE_ROWS).astype(jnp.int32).reshape(1)
    later = (experts[None, :] > experts[:, None]) & (counts[None, :] > 0)
    next_expert = jnp.min(jnp.where(later, experts[None, :], N_EXPERTS), axis=1)
    next_expert = jnp.where(next_expert == N_EXPERTS, experts, next_expert)
    return (gidx.reshape(n_blocks, 1, MOE_ROWS), sidx.reshape(n_blocks + 1, 1, MOE_ROWS), row_w,
            block_e, next_expert[block_e], n_used)


def _pad_cols(w, n):
    return jnp.pad(w, ((0, 0), (0, n - w.shape[1])))


def _pad_rows(w, n):
    return jnp.pad(w, ((0, n - w.shape[0]), (0, 0)))


def _layer(x, mod, norm1_gain, w_in, shift_mu, w0, w_decay_up, a0, w_iclr_up, w_gate_up,
           k_k, k_a, r_k, ln_x_gain, ln_x_bias, sb_norm_gain, w_out, norm2_gain,
           w_router_group, b_router_group, w_router_expert, b_router_expert,
           w_exp_gate, w_exp_up, w_exp_down, *, rwkv_ts, tm_in, tm_out):
    bsz, seq, d = x.shape
    t = bsz * seq
    rw = w0.shape[0]
    sbw = sb_norm_gain.shape[0]
    sh1, sc1, g1, sh2, sc2, g2 = [m.reshape(bsz, 1, d) for m in jnp.split(mod, 6, axis=-1)]

    o = 3 * rw
    seg = lambda a, b: w_in[:, a:b]
    w_rwkv = jnp.concatenate([
        seg(0, o),
        _pad_cols(seg(o, o + DECAY_LORA), LANE),
        _pad_cols(seg(o + DECAY_LORA, o + DECAY_LORA + ICLR_LORA), LANE),
        _pad_cols(seg(o + DECAY_LORA + ICLR_LORA, o + DECAY_LORA + ICLR_LORA + GATE_LORA),
                  2 * LANE)], axis=1).astype(BF16)
    rcols = o + DECAY_LORA + ICLR_LORA + GATE_LORA
    w_sb = w_in[:, rcols:].astype(BF16)
    mseg = lambda a, b: shift_mu[a:b][None, :]
    mu = jnp.concatenate([
        mseg(0, o),
        _pad_cols(mseg(o, o + DECAY_LORA), LANE),
        _pad_cols(mseg(o + DECAY_LORA, o + DECAY_LORA + ICLR_LORA), LANE),
        _pad_cols(mseg(o + DECAY_LORA + ICLR_LORA, rcols), 2 * LANE)], axis=1)

    x2 = x.reshape(t, d)
    p_rwkv = _normmod_matmul(x2, norm1_gain, sc1, sh1, w_rwkv, F32, seq, tm_in, 512)
    qkv = _normmod_matmul(x2, norm1_gain, sc1, sh1, w_sb, BF16, seq, tm_in, 512)

    vecs = jnp.stack([w0, a0, k_k, k_a, r_k.reshape(-1), ln_x_gain, ln_x_bias,
                      jnp.zeros_like(w0)])
    y_a = _rwkv_time_mix(p_rwkv.reshape(bsz, seq, -1), mu, vecs,
                         _pad_rows(w_decay_up, LANE), _pad_rows(w_iclr_up, LANE),
                         _pad_rows(w_gate_up, 2 * LANE), ts=rwkv_ts)
    y_b = _sb_attention(qkv.reshape(bsz, seq, -1), sb_norm_gain, sbw)

    w_r = _pad_cols(jnp.concatenate([w_router_group, w_router_expert], axis=1), LANE)
    b_r = _pad_cols(jnp.concatenate([b_router_group, b_router_expert])[None, :], LANE)
    w_o = w_out.astype(BF16)
    x1, h2, routed = _outproj_norm_router(
        x2, y_a.reshape(t, rw), y_b.reshape(t, sbw), w_o[:rw], w_o[rw:], g1, norm2_gain,
        sc2, sh2, w_r, b_r, seq, tm_out)

    gidx, sidx, row_w, block_e, next_e, n_used = _route(routed)
    y2 = _moe_experts(h2, gidx, sidx, row_w, block_e, next_e, n_used,
                      w_exp_gate, w_exp_up, w_exp_down)
    return x1, y2, g2


def kernel(x, c, w_ada, b_ada, norm1_gain, w_in, shift_mu, w0, w_decay_up, a0, w_iclr_up, w_gate_up, k_k, k_a, r_k, ln_x_gain, ln_x_bias, sb_norm_gain, w_out, norm2_gain, w_router_group, b_router_group, w_router_expert, b_router_expert, w_exp_gate, w_exp_up, w_exp_down, final_norm_gain):
    bsz, seq, d = x.shape
    assert w_ada.shape[0] == 1, "the final norm is fused into the single layer's last kernel"
    l = 0
    tiles = dict(rwkv_ts=min(256, seq), tm_in=min(1024, seq), tm_out=min(256, seq))
    mod = _ada_mod(c, w_ada[l], b_ada[l])
    x1, y2, g2 = _layer(
        x, mod, norm1_gain[l], w_in[l], shift_mu[l], w0[l], w_decay_up[l], a0[l],
        w_iclr_up[l], w_gate_up[l], k_k[l], k_a[l], r_k[l], ln_x_gain[l], ln_x_bias[l],
        sb_norm_gain[l], w_out[l], norm2_gain[l], w_router_group[l], b_router_group[l],
        w_router_expert[l], b_router_expert[l], w_exp_gate[l], w_exp_up[l],
        w_exp_down[l], **tiles)
    out = _final_norm(x1, y2, g2, final_norm_gain, seq, min(512, seq))
    return out.reshape(bsz, seq, d)
```

```python
import functools

import jax
import jax.numpy as jnp
import numpy as np
from jax import lax
from jax.experimental import pallas as pl
from jax.experimental.pallas import tpu as pltpu

F32 = jnp.float32
BF16 = jnp.bfloat16

RMS_EPS = 1e-6
GN_EPS = 64e-5
L2_EPS = 1e-12

HEAD_DIM = 64
PAIR = 2 * HEAD_DIM
RWKV_CHUNK = 64
SB_BLOCK = 128
N_GROUPS = 8
EXPERTS_PER_GROUP = 8
N_EXPERTS = N_GROUPS * EXPERTS_PER_GROUP
TOP_K_IN_GROUP = 2
MOE_BLOCK = 128
DECAY_LORA = 64
ICLR_LORA = 64
GATE_LORA = 160
LANE = 128
MXU_TILE = 256
VMEM_LIMIT = 48 * 1024 * 1024
SB_UNDERFLOW_LOG = -104.0


def _dot(a, b):
    return lax.dot_general(a, b, (((1,), (0,)), ((), ())), preferred_element_type=F32)


def _dot_nt(a, b):
    return lax.dot_general(a, b, (((1,), (1,)), ((), ())), preferred_element_type=F32)


def _split2(x):
    hi = x.astype(BF16)
    lo = (x - hi.astype(F32)).astype(BF16)
    return hi, lo


def _mm(a, b, passes, nt=False):
    d = _dot_nt if nt else _dot
    if passes == 1:
        return d(a.astype(BF16), b.astype(BF16))
    ah, al = _split2(a)
    bh, bl = _split2(b)
    return d(ah, bh) + d(ah, bl) + d(al, bh)


def _mm_exact_rhs(a, b_exact):
    hi, lo = _split2(a)
    return _dot(hi, b_exact) + _dot(lo, b_exact)


def _mm_exact_lhs(a_exact, b):
    hi, lo = _split2(b)
    return _dot(a_exact, hi) + _dot(a_exact, lo)


def _group_sum(x, e_blk):
    w = e_blk.shape[0]
    return jnp.concatenate([_mm_exact_rhs(x[:, g * w:(g + 1) * w], e_blk)
                            for g in range(x.shape[1] // w)], axis=1)


def _sigmoid(x):
    return 1.0 / (1.0 + jnp.exp(-x))


def _softplus(x):
    return jnp.maximum(x, 0.0) + jnp.log(1.0 + jnp.exp(-jnp.abs(x)))


def _params(sem, vmem=VMEM_LIMIT):
    return pltpu.CompilerParams(dimension_semantics=sem, vmem_limit_bytes=vmem)


def _ada_kernel(c_ref, w_ref, b_ref, o_ref):
    c = c_ref[...]
    s = c * _sigmoid(c)
    o_ref[...] = _mm(s, w_ref[...], 3) + b_ref[...]


def _ada_mod(c, w, b):
    bsz, d = c.shape
    n = w.shape[1]
    rows = 8
    cp = jnp.zeros((rows, d), F32).at[:bsz].set(c)
    tn = 1024
    out = pl.pallas_call(
        _ada_kernel,
        out_shape=jax.ShapeDtypeStruct((rows, n), F32),
        grid=(n // tn,),
        in_specs=[pl.BlockSpec((rows, d), lambda j: (0, 0)),
                  pl.BlockSpec((d, tn), lambda j: (0, j)),
                  pl.BlockSpec((1, tn), lambda j: (0, j))],
        out_specs=pl.BlockSpec((rows, tn), lambda j: (0, j)),
        compiler_params=_params(("arbitrary",)),
        name="ada_mod",
    )(cp, w, b.reshape(1, n))
    return out[:bsz]


_NORM_SLAB = 128


def _normmod_mm_kernel(x_ref, gain_ref, sc_ref, sh_ref, w_ref, o_ref, h_scr):
    @pl.when(pl.program_id(1) == 0)
    def _():
        scale = gain_ref[...] * (1.0 + sc_ref[0])
        shift = sh_ref[0]
        slab = min(_NORM_SLAB, x_ref.shape[0])

        def norm_rows(k, c):
            rows = pl.ds(pl.multiple_of(k * slab, slab), slab)
            xf = x_ref[rows, :]
            ms = jnp.mean(xf * xf, axis=-1, keepdims=True)
            h_scr[rows, :] = (xf * lax.rsqrt(ms + RMS_EPS) * scale + shift).astype(BF16)
            return c
        lax.fori_loop(0, x_ref.shape[0] // slab, norm_rows, 0, unroll=2)

    o_ref[...] = _dot(h_scr[...], w_ref[pl.program_id(1)]).astype(o_ref.dtype)


def _normmod_matmul(x2, gain, sc, sh, w, out_dtype, seq, tm, tn):
    t, d = x2.shape
    n = w.shape[1]
    per_b = seq // tm
    w_tiles = w.reshape(d, n // tn, tn).transpose(1, 0, 2)
    return pl.pallas_call(
        _normmod_mm_kernel,
        out_shape=jax.ShapeDtypeStruct((t, n), out_dtype),
        grid=(t // tm, n // tn),
        in_specs=[pl.BlockSpec((tm, d), lambda i, j: (i, 0)),
                  pl.BlockSpec((1, d), lambda i, j: (0, 0)),
                  pl.BlockSpec((1, 1, d), lambda i, j: (i // per_b, 0, 0)),
                  pl.BlockSpec((1, 1, d), lambda i, j: (i // per_b, 0, 0)),
                  pl.BlockSpec(w_tiles.shape, lambda i, j: (0, 0, 0),
                               pipeline_mode=pl.Buffered(1))],
        out_specs=pl.BlockSpec((tm, tn), lambda i, j: (i, j)),
        scratch_shapes=[pltpu.VMEM((tm, d), BF16)],
        compiler_params=_params(("parallel", "arbitrary")),
        name="normmod_proj",
    )(x2, gain.reshape(1, d), sc, sh, w_tiles)


_V_W0, _V_A0, _V_KK, _V_KA, _V_RK, _V_LNG, _V_LNB = range(7)
_M_STRICT, _M_INCL, _M_BD8, _M_OFF8, _M_OFF16, _M_OFF32, _M_EYE = range(7)

RWKV_PASSES = 1
RWKV_GROUP = 2


def _rwkv_masks():
    i = np.arange(PAIR)[:, None]
    j = np.arange(PAIR)[None, :]
    strict = (j < i)
    incl = (j <= i)
    bd8 = (i // 8 == j // 8)
    def off(b):
        return (i // (2 * b) == j // (2 * b)) & (i // b > j // b)
    eye = (i == j)
    m = np.stack([strict, incl, bd8 & strict, off(8), off(16), off(32), eye]).astype(np.float32)
    return jnp.asarray(m)


def _tri_inverse(lmats, masks_ref, passes):
    eye = masks_ref[_M_EYE]
    bd8 = masks_ref[_M_BD8]
    mm = lambda a, b: _mm(a, b, passes)
    dblk = [l * bd8 for l in lmats]
    x = [eye + d for d in dblk]
    p = [mm(d, d) for d in dblk]
    x = [xi + mm(pi, xi) for xi, pi in zip(x, p)]
    p = [mm(pi, pi) for pi in p]
    x = [xi + mm(pi, xi) for xi, pi in zip(x, p)]
    for plane in (_M_OFF8, _M_OFF16, _M_OFF32):
        mask = masks_ref[plane]
        t = [mm(xi, l * mask) for xi, l in zip(x, lmats)]
        x = [xi + mm(ti, xi) for xi, ti in zip(x, t)]
    return x


def _rwkv_kernel(p_ref, mu_ref, vec_ref, wd_ref, wa_ref, wg_ref, e_ref, tri_ref,
                 masks_ref, o_ref,
                 rt_s, at_s, bt_s, kt_s, v_s, bd_s, kd_s, gc_s, y_s, g_s, bon_s,
                 state_s, carry_s, *, ts, n_pairs, passes):
    s_idx = pl.program_id(1)

    @pl.when(s_idx == 0)
    def _():
        state_s[...] = jnp.zeros_like(state_s)
        carry_s[...] = jnp.zeros_like(carry_s)

    width = n_pairs * PAIR
    pt = p_ref[0]
    row = lax.broadcasted_iota(jnp.int32, pt.shape, 0)
    prev = jnp.where(row == 0, carry_s[...], pltpu.roll(pt, 1, 0))
    carry_s[...] = pt[ts - 1:ts, :]
    pm = pt + (prev - pt) * mu_ref[...]

    r = pm[:, 0:width]
    k = pm[:, width:2 * width]
    v = pm[:, 2 * width:3 * width]
    o3 = 3 * width
    xw = pm[:, o3:o3 + LANE]
    xa = pm[:, o3 + LANE:o3 + 2 * LANE]
    xg = pm[:, o3 + 2 * LANE:o3 + 4 * LANE]

    vec = lambda i: vec_ref[i:i + 1, :]
    e_mat = e_ref[...]
    logw = -float(np.exp(-0.5)) * _sigmoid(vec(_V_W0) + _mm(jnp.tanh(xw), wd_ref[...], 3))
    a = _sigmoid(vec(_V_A0) + _mm(xa, wa_ref[...], 1))
    g_s[...] = _mm(_sigmoid(xg), wg_ref[...], 1)
    kk = k * vec(_V_KK)
    ss = _group_sum(kk * kk, e_mat)
    kk = kk / jnp.maximum(jnp.sqrt(ss), L2_EPS)
    kp = k * (1.0 + (a - 1.0) * vec(_V_KA))
    ib = kk * a
    bon_s[...] = _group_sum(r * kp * vec(_V_RK), e_mat) * v
    cum = _mm_exact_lhs(tri_ref[...], logw)
    tot = jnp.concatenate(
        [jnp.broadcast_to(cum[c * RWKV_CHUNK + RWKV_CHUNK - 1:(c + 1) * RWKV_CHUNK, :],
                          (RWKV_CHUNK, width)) for c in range(ts // RWKV_CHUNK)], axis=0)
    op_dtype = rt_s.dtype
    rt_s[...] = (r * jnp.exp(cum)).astype(op_dtype)
    at_s[...] = (-kk * jnp.exp(cum - logw)).astype(op_dtype)
    dec_in = jnp.exp(-cum)
    bt_s[...] = (ib * dec_in).astype(op_dtype)
    kt_s[...] = (kp * dec_in).astype(op_dtype)
    v_s[...] = v
    dec_out = jnp.exp(tot - cum)
    bd_s[...] = (ib * dec_out).astype(op_dtype)
    kd_s[...] = (kp * dec_out).astype(op_dtype)
    gc_s[...] = jnp.exp(tot)

    lane = lax.broadcasted_iota(jnp.int32, (RWKV_CHUNK, PAIR), 1)
    in_h0 = lane < HEAD_DIM
    strict = masks_ref[_M_STRICT]
    incl = masks_ref[_M_INCL]
    operand = lambda x: x.astype(op_dtype)

    pairs = range(n_pairs)
    lanes = [slice(j * PAIR, (j + 1) * PAIR) for j in pairs]
    mm = lambda x, w: _mm(x, w, passes)
    mm_nt = lambda x, w: _mm(x, w, passes, nt=True)

    def chunk_group(g, carry):
        row0 = [pl.multiple_of((g * RWKV_GROUP + c) * RWKV_CHUNK, RWKV_CHUNK)
                for c in range(RWKV_GROUP)]
        items = [(c, j) for c in range(RWKV_GROUP) for j in pairs]

        def stacked(ref):
            blks = [ref[pl.ds(row0[c], RWKV_CHUNK), lanes[j]] for c, j in items]
            zero = jnp.zeros((), ref.dtype)
            return [jnp.concatenate([jnp.where(in_h0, b, zero), jnp.where(in_h0, zero, b)],
                                    axis=0) for b in blks]

        rh, ah, bh, kh = stacked(rt_s), stacked(at_s), stacked(bt_s), stacked(kt_s)
        vh, bdh, kdh = stacked(v_s), stacked(bd_s), stacked(kd_s)
        vo = [operand(x) for x in vh]
        bk = [jnp.concatenate([b, k], axis=0) for b, k in zip(bh, kh)]
        ga = [mm_nt(a, b) for a, b in zip(ah, bk)]
        gr = [mm_nt(r_, b) for r_, b in zip(rh, bk)]
        tinv = _tri_inverse([g_[:, :PAIR] * strict for g_ in ga], masks_ref, passes)
        for c in range(RWKV_GROUP):
            idx = [c * n_pairs + j for j in pairs]
            st = [state_s[j] for j in pairs]
            so = [operand(x) for x in st]
            x0 = [mm_nt(ah[i], so[j]) + mm(ga[i][:, PAIR:] * strict, vo[i])
                  for i, j in zip(idx, pairs)]
            u = [mm(tinv[i], x0[j]) for i, j in zip(idx, pairs)]
            y = [mm_nt(rh[i], so[j]) + mm(gr[i][:, :PAIR] * incl, u[j])
                 + mm(gr[i][:, PAIR:] * incl, vo[i]) for i, j in zip(idx, pairs)]
            for i, j in zip(idx, pairs):
                zt = jnp.concatenate([u[j], vh[i]], axis=0).T
                bkd = jnp.concatenate([bdh[i], kdh[i]], axis=0)
                gc = gc_s[pl.ds(row0[c], 1), lanes[j]]
                state_s[j] = st[j] * gc + mm(zt, bkd)
                y_s[pl.ds(row0[c], RWKV_CHUNK), lanes[j]] = y[j][:RWKV_CHUNK] + y[j][RWKV_CHUNK:]
        return carry

    lax.fori_loop(0, ts // (RWKV_CHUNK * RWKV_GROUP), chunk_group, 0)

    y = y_s[...]
    inv_n = 1.0 / HEAD_DIM
    mean = _group_sum(y, e_mat) * inv_n
    dlt = y - mean
    var = _group_sum(dlt * dlt, e_mat) * inv_n
    yn = dlt * lax.rsqrt(var + GN_EPS) * vec(_V_LNG) + vec(_V_LNB)
    o_ref[0] = ((yn + bon_s[...]) * g_s[...]).astype(o_ref.dtype)


def _rwkv_time_mix(p, mu, vecs, wd, wa, wg, *, ts, passes=RWKV_PASSES):
    bsz, seq, cols = p.shape
    width = vecs.shape[1]
    n_pairs = width // PAIR
    heads = np.arange(MXU_TILE) // HEAD_DIM
    e_mat = jnp.asarray((heads[:, None] == heads[None, :]).astype(np.float32), dtype=BF16)
    tok = np.arange(ts)
    same = (tok[:, None] // RWKV_CHUNK) == (tok[None, :] // RWKV_CHUNK)
    tri = jnp.asarray((same & (tok[None, :] <= tok[:, None])).astype(np.float32), dtype=BF16)
    masks = _rwkv_masks()
    full = lambda shape: pl.BlockSpec(shape, lambda b, s: (0,) * len(shape))
    big = lambda: pltpu.VMEM((ts, width), F32)
    op = lambda: pltpu.VMEM((ts, width), BF16 if passes == 1 else F32)
    kern = functools.partial(_rwkv_kernel, ts=ts, n_pairs=n_pairs, passes=passes)
    return pl.pallas_call(
        kern,
        out_shape=jax.ShapeDtypeStruct((bsz, seq, width), BF16),
        grid=(bsz, seq // ts),
        in_specs=[pl.BlockSpec((1, ts, cols), lambda b, s: (b, s, 0)),
                  full((1, cols)), full(vecs.shape), full(wd.shape), full(wa.shape),
                  full(wg.shape), full(e_mat.shape), full(tri.shape), full(masks.shape)],
        out_specs=pl.BlockSpec((1, ts, width), lambda b, s: (b, s, 0)),
        scratch_shapes=[op(), op(), op(), op(), big(), op(), op(), big(), big(), big(), big(),
                        pltpu.VMEM((n_pairs, PAIR, PAIR), F32), pltpu.VMEM((1, cols), F32)],
        compiler_params=_params(("parallel", "arbitrary")),
        name="rwkv7_scan",
    )(p, mu, vecs, wd, wa, wg, e_mat, tri, masks)


SB_WINDOW = 6
SB_TAIL = 2
SB_Q_PER_STEP = 4


def _sb_kernel(q_ref, k_ref, v_ref, gain_ref, cum_ref, e2_ref, o_ref, *, q_per_step):
    lane = lax.broadcasted_iota(jnp.int32, (SB_BLOCK, PAIR), 1)
    h0 = lane < HEAD_DIM
    zero = jnp.zeros((), BF16)
    qi = lax.broadcasted_iota(jnp.int32, (SB_BLOCK, 2 * SB_BLOCK), 0)
    ki = lax.broadcasted_iota(jnp.int32, (SB_BLOCK, 2 * SB_BLOCK), 1) % SB_BLOCK
    causal = ki < qi
    cum_mat = cum_ref[...]
    scale = jnp.asarray(HEAD_DIM ** -0.5, BF16)
    width2 = 2 * SB_BLOCK

    heads = [slice(h * SB_BLOCK, (h + 1) * SB_BLOCK) for h in range(2)]

    def load_blocks(kb_top, n):
        khats, vhats, valid = [], [], []
        for j in range(n):
            kb = kb_top - j
            valid.append(kb >= 0)
            rows = pl.ds(pl.multiple_of(jnp.maximum(kb, 0) * SB_BLOCK, SB_BLOCK), SB_BLOCK)
            kblk = k_ref[0, rows, :]
            vblk = v_ref[0, rows, :]
            khats.append(jnp.concatenate([jnp.where(h0, kblk, zero), jnp.where(h0, zero, kblk)],
                                         axis=0))
            vhats.append(jnp.concatenate([jnp.where(h0, vblk, zero), jnp.where(h0, zero, vblk)],
                                         axis=0))
        return khats, vhats, valid

    def score_stage(zs, keeps):
        log_beta = [jnp.minimum(z, 0.0) - jnp.log(1.0 + jnp.exp(-jnp.abs(z))) for z in zs]
        log_1m = [lb - z for lb, z in zip(log_beta, zs)]
        log_1m = [l if m is None else jnp.where(m, l, 0.0) for l, m in zip(log_1m, keeps)]
        his = [l.astype(BF16) for l in log_1m]
        parts = [[_dot(hi[:, sl], cum_mat) for sl in heads] for hi in his]
        return log_beta, parts

    def weight_stage(log_beta, parts, keep, rest):
        between = jnp.concatenate([parts[0][:, :SB_BLOCK], parts[1][:, :SB_BLOCK]], axis=1)
        total = jnp.concatenate([parts[0][:, SB_BLOCK:], parts[1][:, SB_BLOCK:]], axis=1)
        w = jnp.exp(log_beta + between + rest)
        if keep is not None:
            w = jnp.where(keep, w, 0.0)
        return w.astype(BF16), rest + total

    def valid_mask(flag):
        return jnp.broadcast_to(flag, causal.shape)

    def sweep(q, kb_top, n, acc, rest):
        khats, vhats, valid = load_blocks(kb_top, n)
        z_all = _dot_nt(q, jnp.concatenate(khats, axis=0))
        keeps = [None] + [valid_mask(valid[j]) for j in range(1, n)]
        zs = [z_all[:, j * width2:(j + 1) * width2] for j in range(n)]
        log_beta, parts = score_stage(zs, keeps)
        ws = []
        for j in range(n):
            w, rest = weight_stage(log_beta[j], parts[j], keeps[j], rest)
            ws.append(w)
        acc = acc + _dot(jnp.concatenate(ws, axis=1), jnp.concatenate(vhats, axis=0))
        return acc, rest

    def window_pair(q2, qb_a, all_valid):
        n = SB_WINDOW
        khats, vhats, valid = load_blocks(qb_a + 1, n + 1)
        z_all = _dot_nt(q2, jnp.concatenate(khats, axis=0))
        off_diag = (lambda flag: None) if all_valid else valid_mask
        items = []
        for j in range(n):
            col = lambda jj: slice(jj * width2, (jj + 1) * width2)
            za = z_all[:SB_BLOCK, col(j + 1)]
            zb = z_all[SB_BLOCK:, col(j)]
            items.append((0, za, causal if j == 0 else off_diag(valid[j + 1])))
            items.append((1, zb, causal if j == 0 else off_diag(valid[j])))
        log_beta, parts = score_stage([it[1] for it in items], [it[2] for it in items])
        rest = [jnp.zeros((SB_BLOCK, width2), F32) for _ in range(2)]
        ws = [[], []]
        for idx, (s, _, keep) in enumerate(items):
            w, rest[s] = weight_stage(log_beta[idx], parts[idx], keep, rest[s])
            ws[s].append(w)
        acc_a = _dot(jnp.concatenate(ws[0], axis=1), jnp.concatenate(vhats[1:], axis=0))
        acc_b = _dot(jnp.concatenate(ws[1], axis=1), jnp.concatenate(vhats[:n], axis=0))
        return (acc_a, rest[0]), (acc_b, rest[1])

    def finish(q, qb, acc, rest, qrows):
        def cond(c):
            i, _, _, live = c
            return jnp.logical_and(i <= qb, live > SB_UNDERFLOW_LOG)

        def body(c):
            i, acc, rest, _ = c
            acc, rest = sweep(q, qb - i, SB_TAIL, acc, rest)
            return i + SB_TAIL, acc, rest, jnp.max(rest)

        _, acc, _, _ = lax.while_loop(cond, body,
                                      (jnp.int32(SB_WINDOW), acc, rest, jnp.max(rest)))
        ms = _mm_exact_rhs(acc * acc, e2_ref[...]) * (1.0 / HEAD_DIM)
        o_ref[0, qrows, :] = (acc * lax.rsqrt(ms + RMS_EPS) * gain_ref[...]).astype(o_ref.dtype)

    def q_pair(it, all_valid):
        qb_a = pl.program_id(2) * q_per_step + 2 * it
        row0 = pl.multiple_of(2 * it * SB_BLOCK, 2 * SB_BLOCK)
        q2 = q_ref[0, pl.ds(row0, 2 * SB_BLOCK), :] * scale
        (acc_a, rest_a), (acc_b, rest_b) = window_pair(q2, qb_a, all_valid)
        finish(q2[:SB_BLOCK], qb_a, acc_a, rest_a, pl.ds(row0, SB_BLOCK))
        finish(q2[SB_BLOCK:], qb_a + 1, acc_b, rest_b, pl.ds(row0 + SB_BLOCK, SB_BLOCK))

    def run(all_valid):
        def body(it, carry):
            q_pair(it, all_valid)
            return carry
        lax.fori_loop(0, q_per_step // 2, body, 0)

    first_full = -(-(SB_WINDOW - 1) // q_per_step)
    pl.when(pl.program_id(2) < first_full)(lambda: run(False))
    pl.when(pl.program_id(2) >= first_full)(lambda: run(True))


def _sb_attention(qkv, gain, width):
    bsz, seq, _ = qkv.shape
    n_pairs = width // PAIR
    nq = seq // SB_BLOCK
    i = np.arange(SB_BLOCK)
    upper = (i[:, None] > i[None, :]).astype(np.float32)
    half = np.concatenate([upper, np.ones_like(upper)], axis=1)
    cum_mat = jnp.asarray(half, dtype=BF16)
    hd = np.arange(PAIR) // HEAD_DIM
    e2 = jnp.asarray((hd[:, None] == hd[None, :]).astype(np.float32), dtype=BF16)
    qps = min(SB_Q_PER_STEP, nq)
    assert qps % 2 == 0 and nq % qps == 0, "query blocks are processed in adjacent pairs"
    qrows = qps * SB_BLOCK
    return pl.pallas_call(
        functools.partial(_sb_kernel, q_per_step=qps),
        out_shape=jax.ShapeDtypeStruct((bsz, seq, width), BF16),
        grid=(bsz, n_pairs, nq // qps),
        in_specs=[pl.BlockSpec((1, qrows, PAIR), lambda b, j, t: (b, t, j)),
                  pl.BlockSpec((1, seq, PAIR), lambda b, j, t: (b, 0, n_pairs + j)),
                  pl.BlockSpec((1, seq, PAIR), lambda b, j, t: (b, 0, 2 * n_pairs + j)),
                  pl.BlockSpec((1, PAIR), lambda b, j, t: (0, j)),
                  pl.BlockSpec(cum_mat.shape, lambda b, j, t: (0, 0)),
                  pl.BlockSpec(e2.shape, lambda b, j, t: (0, 0))],
        out_specs=pl.BlockSpec((1, qrows, PAIR), lambda b, j, t: (b, t, j)),
        compiler_params=_params(("parallel", "parallel", "arbitrary")),
        name="stickbreak_attn",
    )(qkv, qkv, qkv, gain.reshape(1, width), cum_mat, e2)


def _route_rows(lg):
    lane = lax.broadcasted_iota(jnp.int32, lg.shape, 1).astype(F32)
    neg = -jnp.inf
    first = lambda hit: jnp.min(jnp.where(hit, lane, float(LANE)), axis=-1, keepdims=True)
    gl = jnp.where(lane < N_GROUPS, lg, neg)
    gmax = jnp.max(gl, axis=-1, keepdims=True)
    g_sel = first(gl == gmax)
    p_sel = 1.0 / jnp.sum(jnp.exp(gl - gmax), axis=-1, keepdims=True)
    lo = N_GROUPS + g_sel * EXPERTS_PER_GROUP
    el = jnp.where(jnp.logical_and(lane >= lo, lane < lo + EXPERTS_PER_GROUP), lg, neg)
    v1 = jnp.max(el, axis=-1, keepdims=True)
    i1 = first(el == v1)
    el2 = jnp.where(lane == i1, neg, el)
    v2 = jnp.max(el2, axis=-1, keepdims=True)
    i2 = first(el2 == v2)
    t2 = jnp.exp(v2 - v1)
    w1 = p_sel / (1.0 + t2)
    return i1 - N_GROUPS, i2 - N_GROUPS, w1, w1 * t2


def _outproj_kernel(x_ref, ya_ref, yb_ref, wa_ref, wb_ref, g1_ref, gain_ref, sc_ref, sh_ref,
                    wr_ref, br_ref, x1_ref, h_ref, rt_ref):
    mix = _dot(ya_ref[...], wa_ref[...]) + _dot(yb_ref[...], wb_ref[...])
    x1 = x_ref[...] + g1_ref[0] * mix
    x1_ref[...] = x1
    ms = jnp.mean(x1 * x1, axis=-1, keepdims=True)
    h = x1 * lax.rsqrt(ms + RMS_EPS) * gain_ref[...] * (1.0 + sc_ref[0]) + sh_ref[0]
    h_ref[...] = h.astype(h_ref.dtype)
    e1, e2, w1, w2 = _route_rows(_mm(h, wr_ref[...], 3) + br_ref[...])
    lane = lax.broadcasted_iota(jnp.int32, rt_ref.shape, 1)
    rt_ref[...] = jnp.where(lane == 0, e1, jnp.where(lane == 1, e2,
                                                      jnp.where(lane == 2, w1, w2)))


def _outproj_norm_router(x2, ya, yb, w_a, w_b, g1, gain, sc, sh, w_r, b_r, seq, tm):
    t, d = x2.shape
    half = ya.shape[1]
    nr = w_r.shape[1]
    per_b = seq // tm
    row = lambda w: pl.BlockSpec((tm, w), lambda i: (i, 0))
    full = lambda shape: pl.BlockSpec(shape, lambda i: (0,) * len(shape))
    per_batch = pl.BlockSpec((1, 1, d), lambda i: (i // per_b, 0, 0))
    return pl.pallas_call(
        _outproj_kernel,
        out_shape=(jax.ShapeDtypeStruct((t, d), F32), jax.ShapeDtypeStruct((t, d), F32),
                   jax.ShapeDtypeStruct((t, nr), F32)),
        grid=(t // tm,),
        in_specs=[row(d), row(half), row(half), full((half, d)), full((half, d)), per_batch,
                  full((1, d)), per_batch, per_batch, full((d, nr)), full((1, nr))],
        out_specs=(row(d), row(d), row(nr)),
        compiler_params=_params(("parallel",)),
        name="outproj_norm_router",
    )(x2, ya, yb, w_a, w_b, g1, gain.reshape(1, d), sc, sh, w_r, b_r)


MOE_ROWS = 256
MOE_VMEM_LIMIT = 58 * 1024 * 1024
_DMA_UNROLL = 8
_CAST_STEPS = 8


def _moe_kernel(be_ref, nxt_ref, nused_ref, gcur_ref, gnext_ref, scur_ref, sprev_ref, wt_ref,
                h_hbm, wg_hbm, wu_hbm, wd_hbm, y_hbm,
                xbuf, ybuf, wg_f, wu_f, wd_f, wg_b, wu_b, wd_b, gsem, ssem, wsem):
    i = pl.program_id(0)
    n_used = nused_ref[0]
    slot = i % 2
    de = wg_b.shape[1]
    d = wd_b.shape[1]

    def weight_copies(e):
        pairs = ((wg_hbm, wg_f), (wu_hbm, wu_f), (wd_hbm, wd_f))
        return [pltpu.make_async_copy(src.at[e], dst, wsem.at[k])
                for k, (src, dst) in enumerate(pairs)]

    def gather(idx_ref, r, s):
        return pltpu.make_async_copy(h_hbm.at[pl.ds(idx_ref[0, 0, r], 1), :],
                                     xbuf.at[s, pl.ds(r, 1), :], gsem.at[s])

    def scatter(idx_ref, r, s):
        return pltpu.make_async_copy(ybuf.at[s, pl.ds(r, 1), :],
                                     y_hbm.at[pl.ds(idx_ref[0, 0, r], 1), :], ssem.at[s])

    def for_rows(fn):
        def body(r, c):
            fn(r)
            return c
        lax.fori_loop(0, MOE_ROWS, body, 0, unroll=_DMA_UNROLL)

    def gather_wait(s):
        pltpu.make_async_copy(h_hbm.at[pl.ds(0, MOE_ROWS), :], xbuf.at[s], gsem.at[s]).wait()

    def scatter_wait(s):
        pltpu.make_async_copy(ybuf.at[s], y_hbm.at[pl.ds(0, MOE_ROWS), :], ssem.at[s]).wait()

    @pl.when(jnp.logical_and(i == 0, n_used > 0))
    def _():
        for_rows(lambda r: gather(gcur_ref, r, 0).start())
        for c in weight_copies(be_ref[0]):
            c.start()
        ybuf[1] = jnp.zeros(ybuf.shape[1:], ybuf.dtype)

    @pl.when(i < n_used)
    def _():
        e = be_ref[i]

        @pl.when(jnp.logical_or(i == 0, e != be_ref[jnp.maximum(i - 1, 0)]))
        def _():
            for c in weight_copies(e):
                c.wait()

            def cast_rows(k, c):
                for src, dst in ((wg_f, wg_b), (wu_f, wu_b), (wd_f, wd_b)):
                    nrow = src.shape[0] // _CAST_STEPS
                    rows = pl.ds(pl.multiple_of(k * nrow, nrow), nrow)
                    dst[rows, :] = src[rows, :].astype(BF16)
                return c
            lax.fori_loop(0, _CAST_STEPS, cast_rows, 0)

            @pl.when(nxt_ref[i] != e)
            def _():
                for c in weight_copies(nxt_ref[i]):
                    c.start(priority=1)

        gather_wait(slot)

        def sliced_issue(n_slices, make_copy, priorities):
            per = MOE_ROWS // n_slices
            state = [0]

            def issue():
                for r in range(state[0], state[0] + per):
                    make_copy(r).start(priority=r % priorities)
                state[0] += per
            return issue

        n_up = de // MXU_TILE
        n_down = d // MXU_TILE
        issue_gather = sliced_issue(2 * n_up, lambda r: gather(gnext_ref, r, 1 - slot), 1)
        issue_scatter = sliced_issue(n_down, lambda r: scatter(sprev_ref, r, 1 - slot), 2)

        xb = xbuf[slot].astype(BF16)
        hg, hu = [], []
        for n in range(n_up):
            cols = slice(n * MXU_TILE, (n + 1) * MXU_TILE)
            hg.append(_dot(xb, wg_b[:, cols]))
            issue_gather()
            hu.append(_dot(xb, wu_b[:, cols]))
            issue_gather()
        hg = jnp.concatenate(hg, axis=1)
        hid = (hg * _sigmoid(hg) * jnp.concatenate(hu, axis=1)).astype(BF16)

        @pl.when(i >= 1)
        def _():
            scatter_wait(slot)

        wt = wt_ref[...]
        for n in range(n_down):
            cols = slice(n * MXU_TILE, (n + 1) * MXU_TILE)
            ybuf[slot, :, cols] = _dot(hid, wd_b[:, cols]) * wt
            issue_scatter()

        @pl.when(i + 1 >= n_used)
        def _():
            gather_wait(1 - slot)
            scatter_wait(1 - slot)
            for_rows(lambda r: scatter(scur_ref, r, slot).start())
            scatter_wait(slot)


def _moe_experts(h2, gidx, sidx, row_w, block_e, next_e, n_used, w_gate, w_up, w_down):
    t, d = h2.shape
    de = w_gate.shape[2]
    n_blocks = gidx.shape[0]
    assert MOE_ROWS % (2 * (de // MXU_TILE)) == 0 == MOE_ROWS % (d // MXU_TILE)
    assert d % _CAST_STEPS == 0 == de % _CAST_STEPS and sidx.shape[0] == n_blocks + 1
    idx_spec = lambda f: pl.BlockSpec((1, 1, MOE_ROWS), f, memory_space=pltpu.SMEM)
    hbm = pl.BlockSpec(memory_space=pl.ANY)
    grid_spec = pltpu.PrefetchScalarGridSpec(
        num_scalar_prefetch=3,
        grid=(n_blocks,),
        in_specs=[idx_spec(lambda i, *_: (i, 0, 0)),
                  idx_spec(lambda i, *_: (jnp.minimum(i + 1, n_blocks - 1), 0, 0)),
                  idx_spec(lambda i, *_: (i, 0, 0)),
                  idx_spec(lambda i, *_: (jnp.where(i == 0, n_blocks, i - 1), 0, 0)),
                  pl.BlockSpec((MOE_ROWS, 1), lambda i, *_: (i, 0)),
                  hbm, hbm, hbm, hbm],
        out_specs=hbm,
        scratch_shapes=[pltpu.VMEM((2, MOE_ROWS, d), F32), pltpu.VMEM((2, MOE_ROWS, d), F32),
                        pltpu.VMEM((d, de), F32), pltpu.VMEM((d, de), F32),
                        pltpu.VMEM((de, d), F32),
                        pltpu.VMEM((d, de), BF16), pltpu.VMEM((d, de), BF16),
                        pltpu.VMEM((de, d), BF16),
                        pltpu.SemaphoreType.DMA((2,)), pltpu.SemaphoreType.DMA((2,)),
                        pltpu.SemaphoreType.DMA((3,))],
    )
    return pl.pallas_call(
        _moe_kernel,
        out_shape=jax.ShapeDtypeStruct((TOP_K_IN_GROUP * t + MOE_ROWS, d), F32),
        grid_spec=grid_spec,
        compiler_params=_params(("arbitrary",), MOE_VMEM_LIMIT),
        name="moe_experts",
    )(block_e, next_e, n_used, gidx, gidx, sidx, sidx, row_w.reshape(-1, 1),
      h2, w_gate, w_up, w_down)


def _final_kernel(x1_ref, ya_ref, yb_ref, g2_ref, gain_ref, o_ref):
    x2 = x1_ref[...] + g2_ref[0] * (ya_ref[...] + yb_ref[...])
    ms = jnp.mean(x2 * x2, axis=-1, keepdims=True)
    o_ref[...] = x2 * lax.rsqrt(ms + RMS_EPS) * gain_ref[...]


def _final_norm(x1, y2, g2, gain, seq, tm):
    t, d = x1.shape
    per_b = seq // tm
    nt = t // tm
    row = pl.BlockSpec((tm, d), lambda i: (i, 0))
    return pl.pallas_call(
        _final_kernel,
        out_shape=jax.ShapeDtypeStruct((t, d), F32),
        grid=(nt,),
        in_specs=[row, row, pl.BlockSpec((tm, d), lambda i: (i + nt, 0)),
                  pl.BlockSpec((1, 1, d), lambda i: (i // per_b, 0, 0)),
                  pl.BlockSpec((1, d), lambda i: (0, 0))],
        out_specs=row,
        compiler_params=_params(("parallel",)),
        name="final_norm",
    )(x1, y2, y2, g2, gain.reshape(1, d))


def _route(routed):
    t = routed.shape[0]
    pair_w = routed[:, TOP_K_IN_GROUP:2 * TOP_K_IN_GROUP]
    flat_e = routed[:, :TOP_K_IN_GROUP].astype(jnp.int32).reshape(-1)
    flat_w = pair_w.reshape(-1)
    m = flat_e.shape[0]
    order = jnp.argsort(flat_e).astype(jnp.int32)
    experts = jnp.arange(N_EXPERTS, dtype=jnp.int32)
    counts = jnp.sum((flat_e[:, None] == experts[None, :]).astype(jnp.int32), axis=0)
    starts = jnp.cumsum(counts) - counts
    padded = (counts + MOE_ROWS - 1) // MOE_ROWS * MOE_ROWS
    pends = jnp.cumsum(padded)
    pstarts = pends - padded
    n_blocks = (m + N_EXPERTS * (MOE_ROWS - 1) + MOE_ROWS - 1) // MOE_ROWS
    block_start = jnp.arange(n_blocks, dtype=jnp.int32) * MOE_ROWS
    block_e = jnp.minimum(jnp.sum((block_start[:, None] >= pends[None, :]).astype(jnp.int32),
                                  axis=1), N_EXPERTS - 1)
    blk = jnp.arange(n_blocks, dtype=jnp.int32)[:, None]
    rin = jnp.arange(MOE_ROWS, dtype=jnp.int32)[None, :]
    pstart_b, count_b, start_b = lax.optimization_barrier(
        (pstarts[block_e], counts[block_e], starts[block_e]))
    off = blk * MOE_ROWS + rin - pstart_b[:, None]
    valid = off < count_b[:, None]
    src = jnp.clip(start_b[:, None] + off, 0, m - 1)
    assign = order[src]
    tok = assign // TOP_K_IN_GROUP
    gidx = jnp.where(valid, tok, 0)
    spare = TOP_K_IN_GROUP * t + rin
    sidx = jnp.where(valid, (assign % TOP_K_IN_GROUP) * t + tok, spare)
    sidx = jnp.concatenate([sidx, spare], axis=0)
    row_w = jnp.where(valid, flat_w[assign], 0.0)
    n_used = (pends[-1] // MOE_ROWS).astype(jnp.int32).reshape(1)
    later = (experts[None, :] > experts[:, None]) & (counts[None, :] > 0)
    next_expert = jnp.min(jnp.where(later, experts[None, :], N_EXPERTS), axis=1)
    next_expert = jnp.where(next_expert == N_EXPERTS, experts, next_expert)
    return (gidx.reshape(n_blocks, 1, MOE_ROWS), sidx.reshape(n_blocks + 1, 1, MOE_ROWS), row_w,
            block_e, next_expert[block_e], n_used)


def _pad_cols(w, n):
    return jnp.pad(w, ((0, 0), (0, n - w.shape[1])))


def _pad_rows(w, n):
    return jnp.pad(w, ((0, n - w.shape[0]), (0, 0)))


def _layer(x, mod, norm1_gain, w_in, shift_mu, w0, w_decay_up, a0, w_iclr_up, w_gate_up,
           k_k, k_a, r_k, ln_x_gain, ln_x_bias, sb_norm_gain, w_out, norm2_gain,
           w_router_group, b_router_group, w_router_expert, b_router_expert,
           w_exp_gate, w_exp_up, w_exp_down, *, rwkv_ts, tm_in, tm_out):
    bsz, seq, d = x.shape
    t = bsz * seq
    rw = w0.shape[0]
    sbw = sb_norm_gain.shape[0]
    sh1, sc1, g1, sh2, sc2, g2 = [m.reshape(bsz, 1, d) for m in jnp.split(mod, 6, axis=-1)]

    o = 3 * rw
    seg = lambda a, b: w_in[:, a:b]
    w_rwkv = jnp.concatenate([
        seg(0, o),
        _pad_cols(seg(o, o + DECAY_LORA), LANE),
        _pad_cols(seg(o + DECAY_LORA, o + DECAY_LORA + ICLR_LORA), LANE),
        _pad_cols(seg(o + DECAY_LORA + ICLR_LORA, o + DECAY_LORA + ICLR_LORA + GATE_LORA),
                  2 * LANE)], axis=1).astype(BF16)
    rcols = o + DECAY_LORA + ICLR_LORA + GATE_LORA
    w_sb = w_in[:, rcols:].astype(BF16)
    mseg = lambda a, b: shift_mu[a:b][None, :]
    mu = jnp.concatenate([
        mseg(0, o),
        _pad_cols(mseg(o, o + DECAY_LORA), LANE),
        _pad_cols(mseg(o + DECAY_LORA, o + DECAY_LORA + ICLR_LORA), LANE),
        _pad_cols(mseg(o + DECAY_LORA + ICLR_LORA, rcols), 2 * LANE)], axis=1)

    x2 = x.reshape(t, d)
    p_rwkv = _normmod_matmul(x2, norm1_gain, sc1, sh1, w_rwkv, F32, seq, tm_in, 512)
    qkv = _normmod_matmul(x2, norm1_gain, sc1, sh1, w_sb, BF16, seq, tm_in, 512)

    vecs = jnp.stack([w0, a0, k_k, k_a, r_k.reshape(-1), ln_x_gain, ln_x_bias,
                      jnp.zeros_like(w0)])
    y_a = _rwkv_time_mix(p_rwkv.reshape(bsz, seq, -1), mu, vecs,
                         _pad_rows(w_decay_up, LANE), _pad_rows(w_iclr_up, LANE),
                         _pad_rows(w_gate_up, 2 * LANE), ts=rwkv_ts)
    y_b = _sb_attention(qkv.reshape(bsz, seq, -1), sb_norm_gain, sbw)

    w_r = _pad_cols(jnp.concatenate([w_router_group, w_router_expert], axis=1), LANE)
    b_r = _pad_cols(jnp.concatenate([b_router_group, b_router_expert])[None, :], LANE)
    w_o = w_out.astype(BF16)
    x1, h2, routed = _outproj_norm_router(
        x2, y_a.reshape(t, rw), y_b.reshape(t, sbw), w_o[:rw], w_o[rw:], g1, norm2_gain,
        sc2, sh2, w_r, b_r, seq, tm_out)

    gidx, sidx, row_w, block_e, next_e, n_used = _route(routed)
    y2 = _moe_experts(h2, gidx, sidx, row_w, block_e, next_e, n_used,
                      w_exp_gate, w_exp_up, w_exp_down)
    return x1, y2, g2


def kernel(x, c, w_ada, b_ada, norm1_gain, w_in, shift_mu, w0, w_decay_up, a0, w_iclr_up, w_gate_up, k_k, k_a, r_k, ln_x_gain, ln_x_bias, sb_norm_gain, w_out, norm2_gain, w_router_group, b_router_group, w_router_expert, b_router_expert, w_exp_gate, w_exp_up, w_exp_down, final_norm_gain):
    bsz, seq, d = x.shape
    assert w_ada.shape[0] == 1, "the final norm is fused into the single layer's last kernel"
    l = 0
    tiles = dict(rwkv_ts=min(256, seq), tm_in=min(1024, seq), tm_out=min(256, seq))
    mod = _ada_mod(c, w_ada[l], b_ada[l])
    x1, y2, g2 = _layer(
        x, mod, norm1_gain[l], w_in[l], shift_mu[l], w0[l], w_decay_up[l], a0[l],
        w_iclr_up[l], w_gate_up[l], k_k[l], k_a[l], r_k[l], ln_x_gain[l], ln_x_bias[l],
        sb_norm_gain[l], w_out[l], norm2_gain[l], w_router_group[l], b_router_group[l],
        w_router_expert[l], b_router_expert[l], w_exp_gate[l], w_exp_up[l],
        w_exp_down[l], **tiles)
    out = _final_norm(x1, y2, g2, final_norm_gain, seq, min(512, seq))
    return out.reshape(bsz, seq, d)
```

```python
import functools

import jax
import jax.numpy as jnp
import numpy as np
from jax import lax
from jax.experimental import pallas as pl
from jax.experimental.pallas import tpu as pltpu

F32 = jnp.float32
BF16 = jnp.bfloat16

RMS_EPS = 1e-6
GN_EPS = 64e-5
L2_EPS = 1e-12

HEAD_DIM = 64
PAIR = 2 * HEAD_DIM
RWKV_CHUNK = 64
SB_BLOCK = 128
N_GROUPS = 8
EXPERTS_PER_GROUP = 8
N_EXPERTS = N_GROUPS * EXPERTS_PER_GROUP
TOP_K_IN_GROUP = 2
MOE_BLOCK = 128
DECAY_LORA = 64
ICLR_LORA = 64
GATE_LORA = 160
LANE = 128
MXU_TILE = 256
VMEM_LIMIT = 48 * 1024 * 1024
SB_UNDERFLOW_LOG = -104.0


def _dot(a, b):
    return lax.dot_general(a, b, (((1,), (0,)), ((), ())), preferred_element_type=F32)


def _dot_nt(a, b):
    return lax.dot_general(a, b, (((1,), (1,)), ((), ())), preferred_element_type=F32)


def _split2(x):
    hi = x.astype(BF16)
    lo = (x - hi.astype(F32)).astype(BF16)
    return hi, lo


def _mm(a, b, passes, nt=False):
    d = _dot_nt if nt else _dot
    if passes == 1:
        return d(a.astype(BF16), b.astype(BF16))
    ah, al = _split2(a)
    bh, bl = _split2(b)
    return d(ah, bh) + d(ah, bl) + d(al, bh)


def _mm_exact_rhs(a, b_exact):
    hi, lo = _split2(a)
    return _dot(hi, b_exact) + _dot(lo, b_exact)


def _mm_exact_lhs(a_exact, b):
    hi, lo = _split2(b)
    return _dot(a_exact, hi) + _dot(a_exact, lo)


def _group_sum(x, e_blk):
    w = e_blk.shape[0]
    return jnp.concatenate([_mm_exact_rhs(x[:, g * w:(g + 1) * w], e_blk)
                            for g in range(x.shape[1] // w)], axis=1)


def _sigmoid(x):
    return 1.0 / (1.0 + jnp.exp(-x))


def _softplus(x):
    return jnp.maximum(x, 0.0) + jnp.log(1.0 + jnp.exp(-jnp.abs(x)))


def _params(sem, vmem=VMEM_LIMIT):
    return pltpu.CompilerParams(dimension_semantics=sem, vmem_limit_bytes=vmem)


def _ada_kernel(c_ref, w_ref, b_ref, o_ref):
    c = c_ref[...]
    s = c * _sigmoid(c)
    o_ref[...] = _mm(s, w_ref[...], 3) + b_ref[...]


def _ada_mod(c, w, b):
    bsz, d = c.shape
    n = w.shape[1]
    rows = 8
    cp = jnp.zeros((rows, d), F32).at[:bsz].set(c)
    tn = 1024
    out = pl.pallas_call(
        _ada_kernel,
        out_shape=jax.ShapeDtypeStruct((rows, n), F32),
        grid=(n // tn,),
        in_specs=[pl.BlockSpec((rows, d), lambda j: (0, 0)),
                  pl.BlockSpec((d, tn), lambda j: (0, j)),
                  pl.BlockSpec((1, tn), lambda j: (0, j))],
        out_specs=pl.BlockSpec((rows, tn), lambda j: (0, j)),
        compiler_params=_params(("arbitrary",)),
        name="ada_mod",
    )(cp, w, b.reshape(1, n))
    return out[:bsz]


_NORM_SLAB = 128


def _normmod_mm_kernel(x_ref, gain_ref, sc_ref, sh_ref, w_ref, o_ref, h_scr):
    @pl.when(pl.program_id(1) == 0)
    def _():
        scale = gain_ref[...] * (1.0 + sc_ref[0])
        shift = sh_ref[0]
        slab = min(_NORM_SLAB, x_ref.shape[0])

        def norm_rows(k, c):
            rows = pl.ds(pl.multiple_of(k * slab, slab), slab)
            xf = x_ref[rows, :]
            ms = jnp.mean(xf * xf, axis=-1, keepdims=True)
            h_scr[rows, :] = (xf * lax.rsqrt(ms + RMS_EPS) * scale + shift).astype(BF16)
            return c
        lax.fori_loop(0, x_ref.shape[0] // slab, norm_rows, 0, unroll=2)

    o_ref[...] = _dot(h_scr[...], w_ref[pl.program_id(1)]).astype(o_ref.dtype)


def _normmod_matmul(x2, gain, sc, sh, w, out_dtype, seq, tm, tn):
    t, d = x2.shape
    n = w.shape[1]
    per_b = seq // tm
    w_tiles = w.reshape(d, n // tn, tn).transpose(1, 0, 2)
    return pl.pallas_call(
        _normmod_mm_kernel,
        out_shape=jax.ShapeDtypeStruct((t, n), out_dtype),
        grid=(t // tm, n // tn),
        in_specs=[pl.BlockSpec((tm, d), lambda i, j: (i, 0)),
                  pl.BlockSpec((1, d), lambda i, j: (0, 0)),
                  pl.BlockSpec((1, 1, d), lambda i, j: (i // per_b, 0, 0)),
                  pl.BlockSpec((1, 1, d), lambda i, j: (i // per_b, 0, 0)),
                  pl.BlockSpec(w_tiles.shape, lambda i, j: (0, 0, 0),
                               pipeline_mode=pl.Buffered(1))],
        out_specs=pl.BlockSpec((tm, tn), lambda i, j: (i, j)),
        scratch_shapes=[pltpu.VMEM((tm, d), BF16)],
        compiler_params=_params(("parallel", "arbitrary")),
        name="normmod_proj",
    )(x2, gain.reshape(1, d), sc, sh, w_tiles)


_V_W0, _V_A0, _V_KK, _V_KA, _V_RK, _V_LNG, _V_LNB = range(7)
_M_STRICT, _M_INCL, _M_BD8, _M_OFF8, _M_OFF16, _M_OFF32, _M_EYE = range(7)

RWKV_PASSES = 1
RWKV_GROUP = 2


def _rwkv_masks():
    i = np.arange(PAIR)[:, None]
    j = np.arange(PAIR)[None, :]
    strict = (j < i)
    incl = (j <= i)
    bd8 = (i // 8 == j // 8)
    def off(b):
        return (i // (2 * b) == j // (2 * b)) & (i // b > j // b)
    eye = (i == j)
    m = np.stack([strict, incl, bd8 & strict, off(8), off(16), off(32), eye]).astype(np.float32)
    return jnp.asarray(m)


def _tri_inverse(lmats, masks_ref, passes):
    eye = masks_ref[_M_EYE]
    bd8 = masks_ref[_M_BD8]
    mm = lambda a, b: _mm(a, b, passes)
    dblk = [l * bd8 for l in lmats]
    x = [eye + d for d in dblk]
    p = [mm(d, d) for d in dblk]
    x = [xi + mm(pi, xi) for xi, pi in zip(x, p)]
    p = [mm(pi, pi) for pi in p]
    x = [xi + mm(pi, xi) for xi, pi in zip(x, p)]
    for plane in (_M_OFF8, _M_OFF16, _M_OFF32):
        mask = masks_ref[plane]
        t = [mm(xi, l * mask) for xi, l in zip(x, lmats)]
        x = [xi + mm(ti, xi) for xi, ti in zip(x, t)]
    return x


def _rwkv_kernel(p_ref, mu_ref, vec_ref, wd_ref, wa_ref, wg_ref, e_ref, tri_ref,
                 masks_ref, o_ref,
                 rt_s, at_s, bt_s, kt_s, v_s, bd_s, kd_s, gc_s, y_s, g_s, bon_s,
                 state_s, carry_s, *, ts, n_pairs, passes):
    s_idx = pl.program_id(1)

    @pl.when(s_idx == 0)
    def _():
        state_s[...] = jnp.zeros_like(state_s)
        carry_s[...] = jnp.zeros_like(carry_s)

    width = n_pairs * PAIR
    pt = p_ref[0]
    row = lax.broadcasted_iota(jnp.int32, pt.shape, 0)
    prev = jnp.where(row == 0, carry_s[...], pltpu.roll(pt, 1, 0))
    carry_s[...] = pt[ts - 1:ts, :]
    pm = pt + (prev - pt) * mu_ref[...]

    r = pm[:, 0:width]
    k = pm[:, width:2 * width]
    v = pm[:, 2 * width:3 * width]
    o3 = 3 * width
    xw = pm[:, o3:o3 + LANE]
    xa = pm[:, o3 + LANE:o3 + 2 * LANE]
    xg = pm[:, o3 + 2 * LANE:o3 + 4 * LANE]

    vec = lambda i: vec_ref[i:i + 1, :]
    e_mat = e_ref[...]
    logw = -float(np.exp(-0.5)) * _sigmoid(vec(_V_W0) + _mm(jnp.tanh(xw), wd_ref[...], 3))
    a = _sigmoid(vec(_V_A0) + _mm(xa, wa_ref[...], 1))
    g_s[...] = _mm(_sigmoid(xg), wg_ref[...], 1)
    kk = k * vec(_V_KK)
    ss = _group_sum(kk * kk, e_mat)
    kk = kk / jnp.maximum(jnp.sqrt(ss), L2_EPS)
    kp = k * (1.0 + (a - 1.0) * vec(_V_KA))
    ib = kk * a
    bon_s[...] = _group_sum(r * kp * vec(_V_RK), e_mat) * v
    cum = _mm_exact_lhs(tri_ref[...], logw)
    tot = jnp.concatenate(
        [jnp.broadcast_to(cum[c * RWKV_CHUNK + RWKV_CHUNK - 1:(c + 1) * RWKV_CHUNK, :],
                          (RWKV_CHUNK, width)) for c in range(ts // RWKV_CHUNK)], axis=0)
    op_dtype = rt_s.dtype
    rt_s[...] = (r * jnp.exp(cum)).astype(op_dtype)
    at_s[...] = (-kk * jnp.exp(cum - logw)).astype(op_dtype)
    dec_in = jnp.exp(-cum)
    bt_s[...] = (ib * dec_in).astype(op_dtype)
    kt_s[...] = (kp * dec_in).astype(op_dtype)
    v_s[...] = v
    dec_out = jnp.exp(tot - cum)
    bd_s[...] = (ib * dec_out).astype(op_dtype)
    kd_s[...] = (kp * dec_out).astype(op_dtype)
    gc_s[...] = jnp.exp(tot)

    lane = lax.broadcasted_iota(jnp.int32, (RWKV_CHUNK, PAIR), 1)
    in_h0 = lane < HEAD_DIM
    strict = masks_ref[_M_STRICT]
    incl = masks_ref[_M_INCL]
    operand = lambda x: x.astype(op_dtype)

    pairs = range(n_pairs)
    lanes = [slice(j * PAIR, (j + 1) * PAIR) for j in pairs]
    mm = lambda x, w: _mm(x, w, passes)
    mm_nt = lambda x, w: _mm(x, w, passes, nt=True)

    def chunk_group(g, carry):
        row0 = [pl.multiple_of((g * RWKV_GROUP + c) * RWKV_CHUNK, RWKV_CHUNK)
                for c in range(RWKV_GROUP)]
        items = [(c, j) for c in range(RWKV_GROUP) for j in pairs]

        def stacked(ref):
            blks = [ref[pl.ds(row0[c], RWKV_CHUNK), lanes[j]] for c, j in items]
            zero = jnp.zeros((), ref.dtype)
            return [jnp.concatenate([jnp.where(in_h0, b, zero), jnp.where(in_h0, zero, b)],
                                    axis=0) for b in blks]

        rh, ah, bh, kh = stacked(rt_s), stacked(at_s), stacked(bt_s), stacked(kt_s)
        vh, bdh, kdh = stacked(v_s), stacked(bd_s), stacked(kd_s)
        vo = [operand(x) for x in vh]
        bk = [jnp.concatenate([b, k], axis=0) for b, k in zip(bh, kh)]
        ga = [mm_nt(a, b) for a, b in zip(ah, bk)]
        gr = [mm_nt(r_, b) for r_, b in zip(rh, bk)]
        tinv = _tri_inverse([g_[:, :PAIR] * strict for g_ in ga], masks_ref, passes)
        for c in range(RWKV_GROUP):
            idx = [c * n_pairs + j for j in pairs]
            st = [state_s[j] for j in pairs]
            so = [operand(x) for x in st]
            x0 = [mm_nt(ah[i], so[j]) + mm(ga[i][:, PAIR:] * strict, vo[i])
                  for i, j in zip(idx, pairs)]
            u = [mm(tinv[i], x0[j]) for i, j in zip(idx, pairs)]
            y = [mm_nt(rh[i], so[j]) + mm(gr[i][:, :PAIR] * incl, u[j])
                 + mm(gr[i][:, PAIR:] * incl, vo[i]) for i, j in zip(idx, pairs)]
            for i, j in zip(idx, pairs):
                zt = jnp.concatenate([u[j], vh[i]], axis=0).T
                bkd = jnp.concatenate([bdh[i], kdh[i]], axis=0)
                gc = gc_s[pl.ds(row0[c], 1), lanes[j]]
                state_s[j] = st[j] * gc + mm(zt, bkd)
                y_s[pl.ds(row0[c], RWKV_CHUNK), lanes[j]] = y[j][:RWKV_CHUNK] + y[j][RWKV_CHUNK:]
        return carry

    lax.fori_loop(0, ts // (RWKV_CHUNK * RWKV_GROUP), chunk_group, 0)

    y = y_s[...]
    inv_n = 1.0 / HEAD_DIM
    mean = _group_sum(y, e_mat) * inv_n
    dlt = y - mean
    var = _group_sum(dlt * dlt, e_mat) * inv_n
    yn = dlt * lax.rsqrt(var + GN_EPS) * vec(_V_LNG) + vec(_V_LNB)
    o_ref[0] = ((yn + bon_s[...]) * g_s[...]).astype(o_ref.dtype)


def _rwkv_time_mix(p, mu, vecs, wd, wa, wg, *, ts, passes=RWKV_PASSES):
    bsz, seq, cols = p.shape
    width = vecs.shape[1]
    n_pairs = width // PAIR
    heads = np.arange(MXU_TILE) // HEAD_DIM
    e_mat = jnp.asarray((heads[:, None] == heads[None, :]).astype(np.float32), dtype=BF16)
    tok = np.arange(ts)
    same = (tok[:, None] // RWKV_CHUNK) == (tok[None, :] // RWKV_CHUNK)
    tri = jnp.asarray((same & (tok[None, :] <= tok[:, None])).astype(np.float32), dtype=BF16)
    masks = _rwkv_masks()
    full = lambda shape: pl.BlockSpec(shape, lambda b, s: (0,) * len(shape))
    big = lambda: pltpu.VMEM((ts, width), F32)
    op = lambda: pltpu.VMEM((ts, width), BF16 if passes == 1 else F32)
    kern = functools.partial(_rwkv_kernel, ts=ts, n_pairs=n_pairs, passes=passes)
    return pl.pallas_call(
        kern,
        out_shape=jax.ShapeDtypeStruct((bsz, seq, width), BF16),
        grid=(bsz, seq // ts),
        in_specs=[pl.BlockSpec((1, ts, cols), lambda b, s: (b, s, 0)),
                  full((1, cols)), full(vecs.shape), full(wd.shape), full(wa.shape),
                  full(wg.shape), full(e_mat.shape), full(tri.shape), full(masks.shape)],
        out_specs=pl.BlockSpec((1, ts, width), lambda b, s: (b, s, 0)),
        scratch_shapes=[op(), op(), op(), op(), big(), op(), op(), big(), big(), big(), big(),
                        pltpu.VMEM((n_pairs, PAIR, PAIR), F32), pltpu.VMEM((1, cols), F32)],
        compiler_params=_params(("parallel", "arbitrary")),
        name="rwkv7_scan",
    )(p, mu, vecs, wd, wa, wg, e_mat, tri, masks)


SB_WINDOW = 6
SB_TAIL = 2
SB_Q_PER_STEP = 4


def _sb_kernel(q_ref, k_ref, v_ref, gain_ref, cum_ref, e2_ref, o_ref, *, q_per_step):
    lane = lax.broadcasted_iota(jnp.int32, (SB_BLOCK, PAIR), 1)
    h0 = lane < HEAD_DIM
    zero = jnp.zeros((), BF16)
    qi = lax.broadcasted_iota(jnp.int32, (SB_BLOCK, 2 * SB_BLOCK), 0)
    ki = lax.broadcasted_iota(jnp.int32, (SB_BLOCK, 2 * SB_BLOCK), 1) % SB_BLOCK
    causal = ki < qi
    cum_mat = cum_ref[...]
    scale = jnp.asarray(HEAD_DIM ** -0.5, BF16)
    width2 = 2 * SB_BLOCK

    heads = [slice(h * SB_BLOCK, (h + 1) * SB_BLOCK) for h in range(2)]

    def load_blocks(kb_top, n):
        khats, vhats, valid = [], [], []
        for j in range(n):
            kb = kb_top - j
            valid.append(kb >= 0)
            rows = pl.ds(pl.multiple_of(jnp.maximum(kb, 0) * SB_BLOCK, SB_BLOCK), SB_BLOCK)
            kblk = k_ref[0, rows, :]
            vblk = v_ref[0, rows, :]
            khats.append(jnp.concatenate([jnp.where(h0, kblk, zero), jnp.where(h0, zero, kblk)],
                                         axis=0))
            vhats.append(jnp.concatenate([jnp.where(h0, vblk, zero), jnp.where(h0, zero, vblk)],
                                         axis=0))
        return khats, vhats, valid

    def score_stage(zs, keeps):
        log_beta = [jnp.minimum(z, 0.0) - jnp.log(1.0 + jnp.exp(-jnp.abs(z))) for z in zs]
        log_1m = [lb - z for lb, z in zip(log_beta, zs)]
        log_1m = [l if m is None else jnp.where(m, l, 0.0) for l, m in zip(log_1m, keeps)]
        his = [l.astype(BF16) for l in log_1m]
        parts = [[_dot(hi[:, sl], cum_mat) for sl in heads] for hi in his]
        return log_beta, parts

    def weight_stage(log_beta, parts, keep, rest):
        between = jnp.concatenate([parts[0][:, :SB_BLOCK], parts[1][:, :SB_BLOCK]], axis=1)
        total = jnp.concatenate([parts[0][:, SB_BLOCK:], parts[1][:, SB_BLOCK:]], axis=1)
        w = jnp.exp(log_beta + between + rest)
        if keep is not None:
            w = jnp.where(keep, w, 0.0)
        return w.astype(BF16), rest + total

    def valid_mask(flag):
        return jnp.broadcast_to(flag, causal.shape)

    def sweep(q, kb_top, n, acc, rest):
        khats, vhats, valid = load_blocks(kb_top, n)
        z_all = _dot_nt(q, jnp.concatenate(khats, axis=0))
        keeps = [None] + [valid_mask(valid[j]) for j in range(1, n)]
        zs = [z_all[:, j * width2:(j + 1) * width2] for j in range(n)]
        log_beta, parts = score_stage(zs, keeps)
        ws = []
        for j in range(n):
            w, rest = weight_stage(log_beta[j], parts[j], keeps[j], rest)
            ws.append(w)
        acc = acc + _dot(jnp.concatenate(ws, axis=1), jnp.concatenate(vhats, axis=0))
        return acc, rest

    def window_pair(q2, qb_a, all_valid):
        n = SB_WINDOW
        khats, vhats, valid = load_blocks(qb_a + 1, n + 1)
        z_all = _dot_nt(q2, jnp.concatenate(khats, axis=0))
        off_diag = (lambda flag: None) if all_valid else valid_mask
        items = []
        for j in range(n):
            col = lambda jj: slice(jj * width2, (jj + 1) * width2)
            za = z_all[:SB_BLOCK, col(j + 1)]
            zb = z_all[SB_BLOCK:, col(j)]
            items.append((0, za, causal if j == 0 else off_diag(valid[j + 1])))
            items.append((1, zb, causal if j == 0 else off_diag(valid[j])))
        log_beta, parts = score_stage([it[1] for it in items], [it[2] for it in items])
        rest = [jnp.zeros((SB_BLOCK, width2), F32) for _ in range(2)]
        ws = [[], []]
        for idx, (s, _, keep) in enumerate(items):
            w, rest[s] = weight_stage(log_beta[idx], parts[idx], keep, rest[s])
            ws[s].append(w)
        acc_a = _dot(jnp.concatenate(ws[0], axis=1), jnp.concatenate(vhats[1:], axis=0))
        acc_b = _dot(jnp.concatenate(ws[1], axis=1), jnp.concatenate(vhats[:n], axis=0))
        return (acc_a, rest[0]), (acc_b, rest[1])

    def finish(q, qb, acc, rest, qrows):
        def cond(c):
            i, _, _, live = c
            return jnp.logical_and(i <= qb, live > SB_UNDERFLOW_LOG)

        def body(c):
            i, acc, rest, _ = c
            acc, rest = sweep(q, qb - i, SB_TAIL, acc, rest)
            return i + SB_TAIL, acc, rest, jnp.max(rest)

        _, acc, _, _ = lax.while_loop(cond, body,
                                      (jnp.int32(SB_WINDOW), acc, rest, jnp.max(rest)))
        ms = _mm_exact_rhs(acc * acc, e2_ref[...]) * (1.0 / HEAD_DIM)
        o_ref[0, qrows, :] = (acc * lax.rsqrt(ms + RMS_EPS) * gain_ref[...]).astype(o_ref.dtype)

    def q_pair(it, all_valid):
        qb_a = pl.program_id(2) * q_per_step + 2 * it
        row0 = pl.multiple_of(2 * it * SB_BLOCK, 2 * SB_BLOCK)
        q2 = q_ref[0, pl.ds(row0, 2 * SB_BLOCK), :] * scale
        (acc_a, rest_a), (acc_b, rest_b) = window_pair(q2, qb_a, all_valid)
        finish(q2[:SB_BLOCK], qb_a, acc_a, rest_a, pl.ds(row0, SB_BLOCK))
        finish(q2[SB_BLOCK:], qb_a + 1, acc_b, rest_b, pl.ds(row0 + SB_BLOCK, SB_BLOCK))

    def run(all_valid):
        def body(it, carry):
            q_pair(it, all_valid)
            return carry
        lax.fori_loop(0, q_per_step // 2, body, 0)

    first_full = -(-(SB_WINDOW - 1) // q_per_step)
    pl.when(pl.program_id(2) < first_full)(lambda: run(False))
    pl.when(pl.program_id(2) >= first_full)(lambda: run(True))


def _sb_attention(qkv, gain, width):
    bsz, seq, _ = qkv.shape
    n_pairs = width // PAIR
    nq = seq // SB_BLOCK
    i = np.arange(SB_BLOCK)
    upper = (i[:, None] > i[None, :]).astype(np.float32)
    half = np.concatenate([upper, np.ones_like(upper)], axis=1)
    cum_mat = jnp.asarray(half, dtype=BF16)
    hd = np.arange(PAIR) // HEAD_DIM
    e2 = jnp.asarray((hd[:, None] == hd[None, :]).astype(np.float32), dtype=BF16)
    qps = min(SB_Q_PER_STEP, nq)
    assert qps % 2 == 0 and nq % qps == 0, "query blocks are processed in adjacent pairs"
    qrows = qps * SB_BLOCK
    return pl.pallas_call(
        functools.partial(_sb_kernel, q_per_step=qps),
        out_shape=jax.ShapeDtypeStruct((bsz, seq, width), BF16),
        grid=(bsz, n_pairs, nq // qps),
        in_specs=[pl.BlockSpec((1, qrows, PAIR), lambda b, j, t: (b, t, j)),
                  pl.BlockSpec((1, seq, PAIR), lambda b, j, t: (b, 0, n_pairs + j)),
                  pl.BlockSpec((1, seq, PAIR), lambda b, j, t: (b, 0, 2 * n_pairs + j)),
                  pl.BlockSpec((1, PAIR), lambda b, j, t: (0, j)),
                  pl.BlockSpec(cum_mat.shape, lambda b, j, t: (0, 0)),
                  pl.BlockSpec(e2.shape, lambda b, j, t: (0, 0))],
        out_specs=pl.BlockSpec((1, qrows, PAIR), lambda b, j, t: (b, t, j)),
        compiler_params=_params(("parallel", "parallel", "arbitrary")),
        name="stickbreak_attn",
    )(qkv, qkv, qkv, gain.reshape(1, width), cum_mat, e2)


def _route_rows(lg):
    lane = lax.broadcasted_iota(jnp.int32, lg.shape, 1).astype(F32)
    neg = -jnp.inf
    first = lambda hit: jnp.min(jnp.where(hit, lane, float(LANE)), axis=-1, keepdims=True)
    gl = jnp.where(lane < N_GROUPS, lg, neg)
    gmax = jnp.max(gl, axis=-1, keepdims=True)
    g_sel = first(gl == gmax)
    p_sel = 1.0 / jnp.sum(jnp.exp(gl - gmax), axis=-1, keepdims=True)
    lo = N_GROUPS + g_sel * EXPERTS_PER_GROUP
    el = jnp.where(jnp.logical_and(lane >= lo, lane < lo + EXPERTS_PER_GROUP), lg, neg)
    v1 = jnp.max(el, axis=-1, keepdims=True)
    i1 = first(el == v1)
    el2 = jnp.where(lane == i1, neg, el)
    v2 = jnp.max(el2, axis=-1, keepdims=True)
    i2 = first(el2 == v2)
    t2 = jnp.exp(v2 - v1)
    w1 = p_sel / (1.0 + t2)
    return i1 - N_GROUPS, i2 - N_GROUPS, w1, w1 * t2


def _outproj_kernel(x_ref, ya_ref, yb_ref, wa_ref, wb_ref, g1_ref, gain_ref, sc_ref, sh_ref,
                    wr_ref, br_ref, x1_ref, h_ref, rt_ref):
    mix = _dot(ya_ref[...], wa_ref[...]) + _dot(yb_ref[...], wb_ref[...])
    x1 = x_ref[...] + g1_ref[0] * mix
    x1_ref[...] = x1
    ms = jnp.mean(x1 * x1, axis=-1, keepdims=True)
    h = x1 * lax.rsqrt(ms + RMS_EPS) * gain_ref[...] * (1.0 + sc_ref[0]) + sh_ref[0]
    h_ref[...] = h.astype(h_ref.dtype)
    e1, e2, w1, w2 = _route_rows(_mm(h, wr_ref[...], 3) + br_ref[...])
    lane = lax.broadcasted_iota(jnp.int32, rt_ref.shape, 1)
    rt_ref[...] = jnp.where(lane == 0, e1, jnp.where(lane == 1, e2,
                                                      jnp.where(lane == 2, w1, w2)))


def _outproj_norm_router(x2, ya, yb, w_a, w_b, g1, gain, sc, sh, w_r, b_r, seq, tm):
    t, d = x2.shape
    half = ya.shape[1]
    nr = w_r.shape[1]
    per_b = seq // tm
    row = lambda w: pl.BlockSpec((tm, w), lambda i: (i, 0))
    full = lambda shape: pl.BlockSpec(shape, lambda i: (0,) * len(shape))
    per_batch = pl.BlockSpec((1, 1, d), lambda i: (i // per_b, 0, 0))
    return pl.pallas_call(
        _outproj_kernel,
        out_shape=(jax.ShapeDtypeStruct((t, d), F32), jax.ShapeDtypeStruct((t, d), F32),
                   jax.ShapeDtypeStruct((t, nr), F32)),
        grid=(t // tm,),
        in_specs=[row(d), row(half), row(half), full((half, d)), full((half, d)), per_batch,
                  full((1, d)), per_batch, per_batch, full((d, nr)), full((1, nr))],
        out_specs=(row(d), row(d), row(nr)),
        compiler_params=_params(("parallel",)),
        name="outproj_norm_router",
    )(x2, ya, yb, w_a, w_b, g1, gain.reshape(1, d), sc, sh, w_r, b_r)


MOE_ROWS = 256
MOE_VMEM_LIMIT = 58 * 1024 * 1024
_DMA_UNROLL = 8
_CAST_STEPS = 8
_GATHER_SLOTS = 3


def _moe_kernel(be_ref, nxt_ref, nused_ref, gcur_ref, gnext_ref, gahead_ref, scur_ref, sprev_ref,
                wt_ref, h_hbm, wg_hbm, wu_hbm, wd_hbm, y_hbm,
                xbuf, ybuf, wg_f, wu_f, wd_f, wg_b, wu_b, wd_b, gsem, ssem, wsem):
    i = pl.program_id(0)
    n_used = nused_ref[0]
    slot = i % 2
    gslot = i % _GATHER_SLOTS
    ahead = (i + 2) % _GATHER_SLOTS
    de = wg_b.shape[1]
    d = wd_b.shape[1]

    def weight_copies(e):
        pairs = ((wg_hbm, wg_f), (wu_hbm, wu_f), (wd_hbm, wd_f))
        return [pltpu.make_async_copy(src.at[e], dst, wsem.at[k])
                for k, (src, dst) in enumerate(pairs)]

    def gather(idx_ref, r, s):
        return pltpu.make_async_copy(h_hbm.at[pl.ds(idx_ref[0, 0, r], 1), :],
                                     xbuf.at[s, pl.ds(r, 1), :], gsem.at[s])

    def scatter(idx_ref, r, s):
        return pltpu.make_async_copy(ybuf.at[s, pl.ds(r, 1), :],
                                     y_hbm.at[pl.ds(idx_ref[0, 0, r], 1), :], ssem.at[s])

    def for_rows(fn):
        def body(r, c):
            fn(r)
            return c
        lax.fori_loop(0, MOE_ROWS, body, 0, unroll=_DMA_UNROLL)

    def gather_wait(s):
        pltpu.make_async_copy(h_hbm.at[pl.ds(0, MOE_ROWS), :], xbuf.at[s], gsem.at[s]).wait()

    def scatter_wait(s):
        pltpu.make_async_copy(ybuf.at[s], y_hbm.at[pl.ds(0, MOE_ROWS), :], ssem.at[s]).wait()

    @pl.when(jnp.logical_and(i == 0, n_used > 0))
    def _():
        for_rows(lambda r: gather(gcur_ref, r, 0).start())
        for_rows(lambda r: gather(gnext_ref, r, 1).start())
        for c in weight_copies(be_ref[0]):
            c.start()
        ybuf[1] = jnp.zeros(ybuf.shape[1:], ybuf.dtype)

    @pl.when(i < n_used)
    def _():
        e = be_ref[i]

        @pl.when(jnp.logical_or(i == 0, e != be_ref[jnp.maximum(i - 1, 0)]))
        def _():
            for c in weight_copies(e):
                c.wait()

            def cast_rows(k, c):
                for src, dst in ((wg_f, wg_b), (wu_f, wu_b), (wd_f, wd_b)):
                    nrow = src.shape[0] // _CAST_STEPS
                    rows = pl.ds(pl.multiple_of(k * nrow, nrow), nrow)
                    dst[rows, :] = src[rows, :].astype(BF16)
                return c
            lax.fori_loop(0, _CAST_STEPS, cast_rows, 0)

            @pl.when(nxt_ref[i] != e)
            def _():
                for c in weight_copies(nxt_ref[i]):
                    c.start(priority=1)

        gather_wait(gslot)

        def sliced_issue(n_slices, make_copy, priorities):
            per = MOE_ROWS // n_slices
            state = [0]

            def issue():
                for r in range(state[0], state[0] + per):
                    make_copy(r).start(priority=r % priorities)
                state[0] += per
            return issue

        n_up = de // MXU_TILE
        n_down = d // MXU_TILE
        issue_gather = sliced_issue(2 * n_up, lambda r: gather(gahead_ref, r, ahead), 1)
        issue_scatter = sliced_issue(n_down, lambda r: scatter(sprev_ref, r, 1 - slot), 2)

        xb = xbuf[gslot].astype(BF16)
        hg, hu = [], []
        for n in range(n_up):
            cols = slice(n * MXU_TILE, (n + 1) * MXU_TILE)
            hg.append(_dot(xb, wg_b[:, cols]))
            issue_gather()
            hu.append(_dot(xb, wu_b[:, cols]))
            issue_gather()
        hg = jnp.concatenate(hg, axis=1)
        hid = (hg * _sigmoid(hg) * jnp.concatenate(hu, axis=1)).astype(BF16)

        @pl.when(i >= 1)
        def _():
            scatter_wait(slot)

        wt = wt_ref[...]
        for n in range(n_down):
            cols = slice(n * MXU_TILE, (n + 1) * MXU_TILE)
            ybuf[slot, :, cols] = _dot(hid, wd_b[:, cols]) * wt
            issue_scatter()

        @pl.when(i + 1 >= n_used)
        def _():
            gather_wait((i + 1) % _GATHER_SLOTS)
            gather_wait(ahead)
            scatter_wait(1 - slot)
            for_rows(lambda r: scatter(scur_ref, r, slot).start())
            scatter_wait(slot)


def _moe_experts(h2, gidx, sidx, row_w, block_e, next_e, n_used, w_gate, w_up, w_down):
    t, d = h2.shape
    de = w_gate.shape[2]
    n_blocks = gidx.shape[0]
    assert MOE_ROWS % (2 * (de // MXU_TILE)) == 0 == MOE_ROWS % (d // MXU_TILE)
    assert d % _CAST_STEPS == 0 == de % _CAST_STEPS and sidx.shape[0] == n_blocks + 1
    idx_spec = lambda f: pl.BlockSpec((1, 1, MOE_ROWS), f, memory_space=pltpu.SMEM)
    hbm = pl.BlockSpec(memory_space=pl.ANY)
    grid_spec = pltpu.PrefetchScalarGridSpec(
        num_scalar_prefetch=3,
        grid=(n_blocks,),
        in_specs=[idx_spec(lambda i, *_: (i, 0, 0)),
                  idx_spec(lambda i, *_: (jnp.minimum(i + 1, n_blocks - 1), 0, 0)),
                  idx_spec(lambda i, *_: (jnp.minimum(i + 2, n_blocks - 1), 0, 0)),
                  idx_spec(lambda i, *_: (i, 0, 0)),
                  idx_spec(lambda i, *_: (jnp.where(i == 0, n_blocks, i - 1), 0, 0)),
                  pl.BlockSpec((MOE_ROWS, 1), lambda i, *_: (i, 0)),
                  hbm, hbm, hbm, hbm],
        out_specs=hbm,
        scratch_shapes=[pltpu.VMEM((_GATHER_SLOTS, MOE_ROWS, d), F32),
                        pltpu.VMEM((2, MOE_ROWS, d), F32),
                        pltpu.VMEM((d, de), F32), pltpu.VMEM((d, de), F32),
                        pltpu.VMEM((de, d), F32),
                        pltpu.VMEM((d, de), BF16), pltpu.VMEM((d, de), BF16),
                        pltpu.VMEM((de, d), BF16),
                        pltpu.SemaphoreType.DMA((_GATHER_SLOTS,)), pltpu.SemaphoreType.DMA((2,)),
                        pltpu.SemaphoreType.DMA((3,))],
    )
    return pl.pallas_call(
        _moe_kernel,
        out_shape=jax.ShapeDtypeStruct((TOP_K_IN_GROUP * t + MOE_ROWS, d), F32),
        grid_spec=grid_spec,
        compiler_params=_params(("arbitrary",), MOE_VMEM_LIMIT),
        name="moe_experts",
    )(block_e, next_e, n_used, gidx, gidx, gidx, sidx, sidx, row_w.reshape(-1, 1),
      h2, w_gate, w_up, w_down)


def _final_kernel(x1_ref, ya_ref, yb_ref, g2_ref, gain_ref, o_ref):
    x2 = x1_ref[...] + g2_ref[0] * (ya_ref[...] + yb_ref[...])
    ms = jnp.mean(x2 * x2, axis=-1, keepdims=True)
    o_ref[...] = x2 * lax.rsqrt(ms + RMS_EPS) * gain_ref[...]


def _final_norm(x1, y2, g2, gain, seq, tm):
    t, d = x1.shape
    per_b = seq // tm
    nt = t // tm
    row = pl.BlockSpec((tm, d), lambda i: (i, 0))
    return pl.pallas_call(
        _final_kernel,
        out_shape=jax.ShapeDtypeStruct((t, d), F32),
        grid=(nt,),
        in_specs=[row, row, pl.BlockSpec((tm, d), lambda i: (i + nt, 0)),
                  pl.BlockSpec((1, 1, d), lambda i: (i // per_b, 0, 0)),
                  pl.BlockSpec((1, d), lambda i: (0, 0))],
        out_specs=row,
        compiler_params=_params(("parallel",)),
        name="final_norm",
    )(x1, y2, y2, g2, gain.reshape(1, d))


def _route(routed):
    t = routed.shape[0]
    pair_w = routed[:, TOP_K_IN_GROUP:2 * TOP_K_IN_GROUP]
    flat_e = routed[:, :TOP_K_IN_GROUP].astype(jnp.int32).reshape(-1)
    flat_w = pair_w.reshape(-1)
    m = flat_e.shape[0]
    order = jnp.argsort(flat_e).astype(jnp.int32)
    experts = jnp.arange(N_EXPERTS, dtype=jnp.int32)
    counts = jnp.sum((flat_e[:, None] == experts[None, :]).astype(jnp.int32), axis=0)
    starts = jnp.cumsum(counts) - counts
    padded = (counts + MOE_ROWS - 1) // MOE_ROWS * MOE_ROWS
    pends = jnp.cumsum(padded)
    pstarts = pends - padded
    n_blocks = (m + N_EXPERTS * (MOE_ROWS - 1) + MOE_ROWS - 1) // MOE_ROWS
    block_start = jnp.arange(n_blocks, dtype=jnp.int32) * MOE_ROWS
    block_e = jnp.minimum(jnp.sum((block_start[:, None] >= pends[None, :]).astype(jnp.int32),
                                  axis=1), N_EXPERTS - 1)
    blk = jnp.arange(n_blocks, dtype=jnp.int32)[:, None]
    rin = jnp.arange(MOE_ROWS, dtype=jnp.int32)[None, :]
    pstart_b, count_b, start_b = lax.optimization_barrier(
        (pstarts[block_e], counts[block_e], starts[block_e]))
    off = blk * MOE_ROWS + rin - pstart_b[:, None]
    valid = off < count_b[:, None]
    src = jnp.clip(start_b[:, None] + off, 0, m - 1)
    assign = order[src]
    tok = assign // TOP_K_IN_GROUP
    gidx = jnp.where(valid, tok, 0)
    spare = TOP_K_IN_GROUP * t + rin
    sidx = jnp.where(valid, (assign % TOP_K_IN_GROUP) * t + tok, spare)
    sidx = jnp.concatenate([sidx, spare], axis=0)
    row_w = jnp.where(valid, flat_w[assign], 0.0)
    n_used = (pends[-1] // MOE_ROWS).astype(jnp.int32).reshape(1)
    later = (experts[None, :] > experts[:, None]) & (counts[None, :] > 0)
    next_expert = jnp.min(jnp.where(later, experts[None, :], N_EXPERTS), axis=1)
    next_expert = jnp.where(next_expert == N_EXPERTS, experts, next_expert)
    return (gidx.reshape(n_blocks, 1, MOE_ROWS), sidx.reshape(n_blocks + 1, 1, MOE_ROWS), row_w,
            block_e, next_expert[block_e], n_used)


def _pad_cols(w, n):
    return jnp.pad(w, ((0, 0), (0, n - w.shape[1])))


def _pad_rows(w, n):
    return jnp.pad(w, ((0, n - w.shape[0]), (0, 0)))


def _layer(x, mod, norm1_gain, w_in, shift_mu, w0, w_decay_up, a0, w_iclr_up, w_gate_up,
           k_k, k_a, r_k, ln_x_gain, ln_x_bias, sb_norm_gain, w_out, norm2_gain,
           w_router_group, b_router_group, w_router_expert, b_router_expert,
           w_exp_gate, w_exp_up, w_exp_down, *, rwkv_ts, tm_in, tm_out):
    bsz, seq, d = x.shape
    t = bsz * seq
    rw = w0.shape[0]
    sbw = sb_norm_gain.shape[0]
    sh1, sc1, g1, sh2, sc2, g2 = [m.reshape(bsz, 1, d) for m in jnp.split(mod, 6, axis=-1)]

    o = 3 * rw
    seg = lambda a, b: w_in[:, a:b]
    w_rwkv = jnp.concatenate([
        seg(0, o),
        _pad_cols(seg(o, o + DECAY_LORA), LANE),
        _pad_cols(seg(o + DECAY_LORA, o + DECAY_LORA + ICLR_LORA), LANE),
        _pad_cols(seg(o + DECAY_LORA + ICLR_LORA, o + DECAY_LORA + ICLR_LORA + GATE_LORA),
                  2 * LANE)], axis=1).astype(BF16)
    rcols = o + DECAY_LORA + ICLR_LORA + GATE_LORA
    w_sb = w_in[:, rcols:].astype(BF16)
    mseg = lambda a, b: shift_mu[a:b][None, :]
    mu = jnp.concatenate([
        mseg(0, o),
        _pad_cols(mseg(o, o + DECAY_LORA), LANE),
        _pad_cols(mseg(o + DECAY_LORA, o + DECAY_LORA + ICLR_LORA), LANE),
        _pad_cols(mseg(o + DECAY_LORA + ICLR_LORA, rcols), 2 * LANE)], axis=1)

    x2 = x.reshape(t, d)
    p_rwkv = _normmod_matmul(x2, norm1_gain, sc1, sh1, w_rwkv, F32, seq, tm_in, 512)
    qkv = _normmod_matmul(x2, norm1_gain, sc1, sh1, w_sb, BF16, seq, tm_in, 512)

    vecs = jnp.stack([w0, a0, k_k, k_a, r_k.reshape(-1), ln_x_gain, ln_x_bias,
                      jnp.zeros_like(w0)])
    y_a = _rwkv_time_mix(p_rwkv.reshape(bsz, seq, -1), mu, vecs,
                         _pad_rows(w_decay_up, LANE), _pad_rows(w_iclr_up, LANE),
                         _pad_rows(w_gate_up, 2 * LANE), ts=rwkv_ts)
    y_b = _sb_attention(qkv.reshape(bsz, seq, -1), sb_norm_gain, sbw)

    w_r = _pad_cols(jnp.concatenate([w_router_group, w_router_expert], axis=1), LANE)
    b_r = _pad_cols(jnp.concatenate([b_router_group, b_router_expert])[None, :], LANE)
    w_o = w_out.astype(BF16)
    x1, h2, routed = _outproj_norm_router(
        x2, y_a.reshape(t, rw), y_b.reshape(t, sbw), w_o[:rw], w_o[rw:], g1, norm2_gain,
        sc2, sh2, w_r, b_r, seq, tm_out)

    gidx, sidx, row_w, block_e, next_e, n_used = _route(routed)
    y2 = _moe_experts(h2, gidx, sidx, row_w, block_e, next_e, n_used,
                      w_exp_gate, w_exp_up, w_exp_down)
    return x1, y2, g2


def kernel(x, c, w_ada, b_ada, norm1_gain, w_in, shift_mu, w0, w_decay_up, a0, w_iclr_up, w_gate_up, k_k, k_a, r_k, ln_x_gain, ln_x_bias, sb_norm_gain, w_out, norm2_gain, w_router_group, b_router_group, w_router_expert, b_router_expert, w_exp_gate, w_exp_up, w_exp_down, final_norm_gain):
    bsz, seq, d = x.shape
    assert w_ada.shape[0] == 1, "the final norm is fused into the single layer's last kernel"
    l = 0
    tiles = dict(rwkv_ts=min(256, seq), tm_in=min(1024, seq), tm_out=min(256, seq))
    mod = _ada_mod(c, w_ada[l], b_ada[l])
    x1, y2, g2 = _layer(
        x, mod, norm1_gain[l], w_in[l], shift_mu[l], w0[l], w_decay_up[l], a0[l],
        w_iclr_up[l], w_gate_up[l], k_k[l], k_a[l], r_k[l], ln_x_gain[l], ln_x_bias[l],
        sb_norm_gain[l], w_out[l], norm2_gain[l], w_router_group[l], b_router_group[l],
        w_router_expert[l], b_router_expert[l], w_exp_gate[l], w_exp_up[l],
        w_exp_down[l], **tiles)
    out = _final_norm(x1, y2, g2, final_norm_gain, seq, min(512, seq))
    return out.reshape(bsz, seq, d)
```

```python
import functools

import jax
import jax.numpy as jnp
import numpy as np
from jax import lax
from jax.experimental import pallas as pl
from jax.experimental.pallas import tpu as pltpu

F32 = jnp.float32
BF16 = jnp.bfloat16

RMS_EPS = 1e-6
GN_EPS = 64e-5
L2_EPS = 1e-12

HEAD_DIM = 64
PAIR = 2 * HEAD_DIM
RWKV_CHUNK = 64
SB_BLOCK = 128
N_GROUPS = 8
EXPERTS_PER_GROUP = 8
N_EXPERTS = N_GROUPS * EXPERTS_PER_GROUP
TOP_K_IN_GROUP = 2
MOE_BLOCK = 128
DECAY_LORA = 64
ICLR_LORA = 64
GATE_LORA = 160
LANE = 128
MXU_TILE = 256
VMEM_LIMIT = 48 * 1024 * 1024
SB_UNDERFLOW_LOG = -104.0


def _dot(a, b):
    return lax.dot_general(a, b, (((1,), (0,)), ((), ())), preferred_element_type=F32)


def _dot_nt(a, b):
    return lax.dot_general(a, b, (((1,), (1,)), ((), ())), preferred_element_type=F32)


def _split2(x):
    hi = x.astype(BF16)
    lo = (x - hi.astype(F32)).astype(BF16)
    return hi, lo


def _mm(a, b, passes, nt=False):
    d = _dot_nt if nt else _dot
    if passes == 1:
        return d(a.astype(BF16), b.astype(BF16))
    ah, al = _split2(a)
    bh, bl = _split2(b)
    return d(ah, bh) + d(ah, bl) + d(al, bh)


def _mm_exact_rhs(a, b_exact):
    hi, lo = _split2(a)
    return _dot(hi, b_exact) + _dot(lo, b_exact)


def _mm_exact_lhs(a_exact, b):
    hi, lo = _split2(b)
    return _dot(a_exact, hi) + _dot(a_exact, lo)


def _group_sum(x, e_blk, split=True):
    w = e_blk.shape[0]
    one_pass = lambda a, b: _dot(a.astype(BF16), b)
    mm = _mm_exact_rhs if split else one_pass
    return jnp.concatenate([mm(x[:, g * w:(g + 1) * w], e_blk)
                            for g in range(x.shape[1] // w)], axis=1)


def _sigmoid(x):
    return 1.0 / (1.0 + jnp.exp(-x))


def _params(sem, vmem=VMEM_LIMIT):
    return pltpu.CompilerParams(dimension_semantics=sem, vmem_limit_bytes=vmem)


def _ada_kernel(c_ref, w_ref, b_ref, o_ref):
    c = c_ref[...]
    s = c * _sigmoid(c)
    o_ref[...] = _mm(s, w_ref[...], 3) + b_ref[...]


def _ada_mod(c, w, b):
    bsz, d = c.shape
    n = w.shape[1]
    rows = 8
    cp = jnp.zeros((rows, d), F32).at[:bsz].set(c)
    tn = 1024
    out = pl.pallas_call(
        _ada_kernel,
        out_shape=jax.ShapeDtypeStruct((rows, n), F32),
        grid=(n // tn,),
        in_specs=[pl.BlockSpec((rows, d), lambda j: (0, 0)),
                  pl.BlockSpec((d, tn), lambda j: (0, j)),
                  pl.BlockSpec((1, tn), lambda j: (0, j))],
        out_specs=pl.BlockSpec((rows, tn), lambda j: (0, j)),
        compiler_params=_params(("arbitrary",)),
        name="ada_mod",
    )(cp, w, b.reshape(1, n))
    return out[:bsz]


_NORM_SLAB = 128


def _normmod_mm_kernel(x_ref, gain_ref, sc_ref, sh_ref, w_ref, o_ref, h_scr):
    @pl.when(pl.program_id(1) == 0)
    def _():
        scale = gain_ref[...] * (1.0 + sc_ref[0])
        shift = sh_ref[0]
        slab = min(_NORM_SLAB, x_ref.shape[0])

        def norm_rows(k, c):
            rows = pl.ds(pl.multiple_of(k * slab, slab), slab)
            xf = x_ref[rows, :]
            ms = jnp.mean(xf * xf, axis=-1, keepdims=True)
            h_scr[rows, :] = (xf * lax.rsqrt(ms + RMS_EPS) * scale + shift).astype(BF16)
            return c
        lax.fori_loop(0, x_ref.shape[0] // slab, norm_rows, 0, unroll=2)

    o_ref[...] = _dot(h_scr[...], w_ref[pl.program_id(1)]).astype(o_ref.dtype)


def _normmod_matmul(x2, gain, sc, sh, w, out_dtype, seq, tm, tn):
    t, d = x2.shape
    n = w.shape[1]
    per_b = seq // tm
    w_tiles = w.reshape(d, n // tn, tn).transpose(1, 0, 2)
    return pl.pallas_call(
        _normmod_mm_kernel,
        out_shape=jax.ShapeDtypeStruct((t, n), out_dtype),
        grid=(t // tm, n // tn),
        in_specs=[pl.BlockSpec((tm, d), lambda i, j: (i, 0)),
                  pl.BlockSpec((1, d), lambda i, j: (0, 0)),
                  pl.BlockSpec((1, 1, d), lambda i, j: (i // per_b, 0, 0)),
                  pl.BlockSpec((1, 1, d), lambda i, j: (i // per_b, 0, 0)),
                  pl.BlockSpec(w_tiles.shape, lambda i, j: (0, 0, 0),
                               pipeline_mode=pl.Buffered(1))],
        out_specs=pl.BlockSpec((tm, tn), lambda i, j: (i, j)),
        scratch_shapes=[pltpu.VMEM((tm, d), BF16)],
        compiler_params=_params(("parallel", "arbitrary")),
        name="normmod_proj",
    )(x2, gain.reshape(1, d), sc, sh, w_tiles)


_V_W0, _V_A0, _V_KK, _V_KA, _V_RK, _V_LNG, _V_LNB = range(7)
_M_STRICT, _M_INCL, _M_BD8, _M_OFF8, _M_OFF16, _M_OFF32, _M_EYE = range(7)

RWKV_PASSES = 1
RWKV_GROUP = 2


def _rwkv_masks():
    i = np.arange(PAIR)[:, None]
    j = np.arange(PAIR)[None, :]
    strict = (j < i)
    incl = (j <= i)
    bd8 = (i // 8 == j // 8)
    def off(b):
        return (i // (2 * b) == j // (2 * b)) & (i // b > j // b)
    eye = (i == j)
    m = np.stack([strict, incl, bd8 & strict, off(8), off(16), off(32), eye]).astype(np.float32)
    return jnp.asarray(m)


def _tri_inverse(lmats, masks_ref, passes):
    eye = masks_ref[_M_EYE]
    bd8 = masks_ref[_M_BD8]
    mm = lambda a, b: _mm(a, b, passes)
    dblk = [l * bd8 for l in lmats]
    x = [eye + d for d in dblk]
    p = [mm(d, d) for d in dblk]
    x = [xi + mm(pi, xi) for xi, pi in zip(x, p)]
    p = [mm(pi, pi) for pi in p]
    x = [xi + mm(pi, xi) for xi, pi in zip(x, p)]
    for plane in (_M_OFF8, _M_OFF16, _M_OFF32):
        mask = masks_ref[plane]
        t = [mm(xi, l * mask) for xi, l in zip(x, lmats)]
        x = [xi + mm(ti, xi) for xi, ti in zip(x, t)]
    return x


def _rwkv_kernel(p_ref, mu_ref, vec_ref, wd_ref, wa_ref, wg_ref, e_ref, tri_ref,
                 masks_ref, o_ref,
                 rt_s, at_s, bt_s, kt_s, v_s, bd_s, kd_s, gc_s, y_s, g_s, bon_s,
                 state_s, carry_s, *, ts, n_pairs, passes):
    s_idx = pl.program_id(1)

    @pl.when(s_idx == 0)
    def _():
        state_s[...] = jnp.zeros_like(state_s)
        carry_s[...] = jnp.zeros_like(carry_s)

    width = n_pairs * PAIR
    pt = p_ref[0]
    row = lax.broadcasted_iota(jnp.int32, pt.shape, 0)
    prev = jnp.where(row == 0, carry_s[...], pltpu.roll(pt, 1, 0))
    carry_s[...] = pt[ts - 1:ts, :]
    pm = pt + (prev - pt) * mu_ref[...]

    r = pm[:, 0:width]
    k = pm[:, width:2 * width]
    v = pm[:, 2 * width:3 * width]
    o3 = 3 * width
    xw = pm[:, o3:o3 + LANE]
    xa = pm[:, o3 + LANE:o3 + 2 * LANE]
    xg = pm[:, o3 + 2 * LANE:o3 + 4 * LANE]

    vec = lambda i: vec_ref[i:i + 1, :]
    e_mat = e_ref[...]
    logw = -float(np.exp(-0.5)) * _sigmoid(vec(_V_W0) + _mm(jnp.tanh(xw), wd_ref[...], 3))
    a = _sigmoid(vec(_V_A0) + _mm(xa, wa_ref[...], 1))
    g_s[...] = _mm(_sigmoid(xg), wg_ref[...], 1)
    kk = k * vec(_V_KK)
    ss = _group_sum(kk * kk, e_mat)
    kk = kk * lax.rsqrt(jnp.maximum(ss, L2_EPS * L2_EPS))
    kp = k * (1.0 + (a - 1.0) * vec(_V_KA))
    ib = kk * a
    bon_s[...] = _group_sum(r * kp * vec(_V_RK), e_mat, split=False) * v
    cum = _mm_exact_lhs(tri_ref[...], logw)
    tot = jnp.concatenate(
        [jnp.broadcast_to(cum[c * RWKV_CHUNK + RWKV_CHUNK - 1:(c + 1) * RWKV_CHUNK, :],
                          (RWKV_CHUNK, width)) for c in range(ts // RWKV_CHUNK)], axis=0)
    op_dtype = rt_s.dtype
    rt_s[...] = (r * jnp.exp(cum)).astype(op_dtype)
    at_s[...] = (-kk * jnp.exp(cum - logw)).astype(op_dtype)
    dec_in = jnp.exp(-cum)
    bt_s[...] = (ib * dec_in).astype(op_dtype)
    kt_s[...] = (kp * dec_in).astype(op_dtype)
    v_s[...] = v
    dec_out = jnp.exp(tot - cum)
    bd_s[...] = (ib * dec_out).astype(op_dtype)
    kd_s[...] = (kp * dec_out).astype(op_dtype)
    gc_s[...] = jnp.exp(tot)

    lane = lax.broadcasted_iota(jnp.int32, (RWKV_CHUNK, PAIR), 1)
    in_h0 = lane < HEAD_DIM
    strict = masks_ref[_M_STRICT]
    incl = masks_ref[_M_INCL]
    operand = lambda x: x.astype(op_dtype)

    pairs = range(n_pairs)
    lanes = [slice(j * PAIR, (j + 1) * PAIR) for j in pairs]
    mm = lambda x, w: _mm(x, w, passes)
    mm_nt = lambda x, w: _mm(x, w, passes, nt=True)

    def chunk_group(g, carry):
        row0 = [pl.multiple_of((g * RWKV_GROUP + c) * RWKV_CHUNK, RWKV_CHUNK)
                for c in range(RWKV_GROUP)]
        items = [(c, j) for c in range(RWKV_GROUP) for j in pairs]

        def stacked(ref):
            blks = [ref[pl.ds(row0[c], RWKV_CHUNK), lanes[j]] for c, j in items]
            zero = jnp.zeros((), ref.dtype)
            return [jnp.concatenate([jnp.where(in_h0, b, zero), jnp.where(in_h0, zero, b)],
                                    axis=0) for b in blks]

        rh, ah, bh, kh = stacked(rt_s), stacked(at_s), stacked(bt_s), stacked(kt_s)
        vh, bdh, kdh = stacked(v_s), stacked(bd_s), stacked(kd_s)
        vo = [operand(x) for x in vh]
        bk = [jnp.concatenate([b, k], axis=0) for b, k in zip(bh, kh)]
        ga = [mm_nt(a, b) for a, b in zip(ah, bk)]
        gr = [mm_nt(r_, b) for r_, b in zip(rh, bk)]
        tinv = _tri_inverse([g_[:, :PAIR] * strict for g_ in ga], masks_ref, passes)
        for c in range(RWKV_GROUP):
            idx = [c * n_pairs + j for j in pairs]
            st = [state_s[j] for j in pairs]
            so = [operand(x) for x in st]
            x0 = [mm_nt(ah[i], so[j]) + mm(ga[i][:, PAIR:] * strict, vo[i])
                  for i, j in zip(idx, pairs)]
            u = [mm(tinv[i], x0[j]) for i, j in zip(idx, pairs)]
            y = [mm_nt(rh[i], so[j]) + mm(gr[i][:, :PAIR] * incl, u[j])
                 + mm(gr[i][:, PAIR:] * incl, vo[i]) for i, j in zip(idx, pairs)]
            for i, j in zip(idx, pairs):
                zt = jnp.concatenate([u[j], vh[i]], axis=0).T
                bkd = jnp.concatenate([bdh[i], kdh[i]], axis=0)
                gc = gc_s[pl.ds(row0[c], 1), lanes[j]]
                state_s[j] = st[j] * gc + mm(zt, bkd)
                y_s[pl.ds(row0[c], RWKV_CHUNK), lanes[j]] = y[j][:RWKV_CHUNK] + y[j][RWKV_CHUNK:]
        return carry

    lax.fori_loop(0, ts // (RWKV_CHUNK * RWKV_GROUP), chunk_group, 0)

    y = y_s[...]
    inv_n = 1.0 / HEAD_DIM
    mean = _group_sum(y, e_mat, split=False) * inv_n
    dlt = y - mean
    var = _group_sum(dlt * dlt, e_mat, split=False) * inv_n
    yn = dlt * lax.rsqrt(var + GN_EPS) * vec(_V_LNG) + vec(_V_LNB)
    o_ref[0] = ((yn + bon_s[...]) * g_s[...]).astype(o_ref.dtype)


def _rwkv_time_mix(p, mu, vecs, wd, wa, wg, *, ts, passes=RWKV_PASSES):
    bsz, seq, cols = p.shape
    width = vecs.shape[1]
    n_pairs = width // PAIR
    heads = np.arange(MXU_TILE) // HEAD_DIM
    e_mat = jnp.asarray((heads[:, None] == heads[None, :]).astype(np.float32), dtype=BF16)
    tok = np.arange(ts)
    same = (tok[:, None] // RWKV_CHUNK) == (tok[None, :] // RWKV_CHUNK)
    tri = jnp.asarray((same & (tok[None, :] <= tok[:, None])).astype(np.float32), dtype=BF16)
    masks = _rwkv_masks()
    full = lambda shape: pl.BlockSpec(shape, lambda b, s: (0,) * len(shape))
    big = lambda: pltpu.VMEM((ts, width), F32)
    op = lambda: pltpu.VMEM((ts, width), BF16 if passes == 1 else F32)
    kern = functools.partial(_rwkv_kernel, ts=ts, n_pairs=n_pairs, passes=passes)
    return pl.pallas_call(
        kern,
        out_shape=jax.ShapeDtypeStruct((bsz, seq, width), BF16),
        grid=(bsz, seq // ts),
        in_specs=[pl.BlockSpec((1, ts, cols), lambda b, s: (b, s, 0)),
                  full((1, cols)), full(vecs.shape), full(wd.shape), full(wa.shape),
                  full(wg.shape), full(e_mat.shape), full(tri.shape), full(masks.shape)],
        out_specs=pl.BlockSpec((1, ts, width), lambda b, s: (b, s, 0)),
        scratch_shapes=[op(), op(), op(), op(), big(), op(), op(), big(), big(), big(), big(),
                        pltpu.VMEM((n_pairs, PAIR, PAIR), F32), pltpu.VMEM((1, cols), F32)],
        compiler_params=_params(("parallel", "arbitrary")),
        name="rwkv7_scan",
    )(p, mu, vecs, wd, wa, wg, e_mat, tri, masks)


SB_WINDOW = 6
SB_TAIL = 2
SB_Q_PER_STEP = 4
SB_PAIRS_TOGETHER = 1


def _sb_kernel(q_ref, k_ref, v_ref, gain_ref, cum_ref, e2_ref, o_ref, *, q_per_step):
    lane = lax.broadcasted_iota(jnp.int32, (SB_BLOCK, PAIR), 1)
    h0 = lane < HEAD_DIM
    zero = jnp.zeros((), BF16)
    qi = lax.broadcasted_iota(jnp.int32, (SB_BLOCK, 2 * SB_BLOCK), 0)
    ki = lax.broadcasted_iota(jnp.int32, (SB_BLOCK, 2 * SB_BLOCK), 1) % SB_BLOCK
    causal = ki < qi
    cum_mat = cum_ref[...]
    scale = jnp.asarray(HEAD_DIM ** -0.5, BF16)
    width2 = 2 * SB_BLOCK

    heads = [slice(h * SB_BLOCK, (h + 1) * SB_BLOCK) for h in range(2)]

    def load_blocks(kb_top, n):
        khats, vhats, valid = [], [], []
        for j in range(n):
            kb = kb_top - j
            valid.append(kb >= 0)
            rows = pl.ds(pl.multiple_of(jnp.maximum(kb, 0) * SB_BLOCK, SB_BLOCK), SB_BLOCK)
            kblk = k_ref[0, rows, :]
            vblk = v_ref[0, rows, :]
            khats.append(jnp.concatenate([jnp.where(h0, kblk, zero), jnp.where(h0, zero, kblk)],
                                         axis=0))
            vhats.append(jnp.concatenate([jnp.where(h0, vblk, zero), jnp.where(h0, zero, vblk)],
                                         axis=0))
        return khats, vhats, valid

    def score_stage(zs, keeps):
        log_beta = [jnp.minimum(z, 0.0) - jnp.log(1.0 + jnp.exp(-jnp.abs(z))) for z in zs]
        log_1m = [lb - z for lb, z in zip(log_beta, zs)]
        log_1m = [l if m is None else jnp.where(m, l, 0.0) for l, m in zip(log_1m, keeps)]
        his = [l.astype(BF16) for l in log_1m]
        parts = [[_dot(hi[:, sl], cum_mat) for sl in heads] for hi in his]
        return log_beta, parts

    def weight_stage(log_beta, parts, keep, rest):
        between = jnp.concatenate([parts[0][:, :SB_BLOCK], parts[1][:, :SB_BLOCK]], axis=1)
        total = jnp.concatenate([parts[0][:, SB_BLOCK:], parts[1][:, SB_BLOCK:]], axis=1)
        w = jnp.exp(log_beta + between + rest)
        if keep is not None:
            w = jnp.where(keep, w, 0.0)
        return w.astype(BF16), rest + total

    def valid_mask(flag):
        return jnp.broadcast_to(flag, causal.shape)

    def sweep(q, kb_top, n, acc, rest):
        khats, vhats, valid = load_blocks(kb_top, n)
        z_all = _dot_nt(q, jnp.concatenate(khats, axis=0))
        keeps = [None] + [valid_mask(valid[j]) for j in range(1, n)]
        zs = [z_all[:, j * width2:(j + 1) * width2] for j in range(n)]
        log_beta, parts = score_stage(zs, keeps)
        ws = []
        for j in range(n):
            w, rest = weight_stage(log_beta[j], parts[j], keeps[j], rest)
            ws.append(w)
        acc = acc + _dot(jnp.concatenate(ws, axis=1), jnp.concatenate(vhats, axis=0))
        return acc, rest

    def window_pairs(q2s, qb_as, all_valid):
        n = SB_WINDOW
        off_diag = (lambda flag: None) if all_valid else valid_mask
        col = lambda jj: slice(jj * width2, (jj + 1) * width2)
        loaded = [load_blocks(qb_a + 1, n + 1) for qb_a in qb_as]
        z_alls = [_dot_nt(q2, jnp.concatenate(ld[0], axis=0))
                  for q2, ld in zip(q2s, loaded)]
        items = []
        for j in range(n):
            for p, (z_all, (_, _, valid)) in enumerate(zip(z_alls, loaded)):
                za = z_all[:SB_BLOCK, col(j + 1)]
                zb = z_all[SB_BLOCK:, col(j)]
                items.append((2 * p, za, causal if j == 0 else off_diag(valid[j + 1])))
                items.append((2 * p + 1, zb, causal if j == 0 else off_diag(valid[j])))
        log_beta, parts = score_stage([it[1] for it in items], [it[2] for it in items])
        n_streams = 2 * len(q2s)
        rest = [jnp.zeros((SB_BLOCK, width2), F32) for _ in range(n_streams)]
        ws = [[] for _ in range(n_streams)]
        for idx, (s, _, keep) in enumerate(items):
            w, rest[s] = weight_stage(log_beta[idx], parts[idx], keep, rest[s])
            ws[s].append(w)
        out = []
        for p, (_, vhats, _) in enumerate(loaded):
            acc_a = _dot(jnp.concatenate(ws[2 * p], axis=1), jnp.concatenate(vhats[1:], axis=0))
            acc_b = _dot(jnp.concatenate(ws[2 * p + 1], axis=1),
                         jnp.concatenate(vhats[:n], axis=0))
            out += [(acc_a, rest[2 * p]), (acc_b, rest[2 * p + 1])]
        return out

    def finish(q, qb, acc, rest, qrows):
        def cond(c):
            i, _, _, live = c
            return jnp.logical_and(i <= qb, live > SB_UNDERFLOW_LOG)

        def body(c):
            i, acc, rest, _ = c
            acc, rest = sweep(q, qb - i, SB_TAIL, acc, rest)
            return i + SB_TAIL, acc, rest, jnp.max(rest)

        _, acc, _, _ = lax.while_loop(cond, body,
                                      (jnp.int32(SB_WINDOW), acc, rest, jnp.max(rest)))
        ms = _mm_exact_rhs(acc * acc, e2_ref[...]) * (1.0 / HEAD_DIM)
        o_ref[0, qrows, :] = (acc * lax.rsqrt(ms + RMS_EPS) * gain_ref[...]).astype(o_ref.dtype)

    n_together = min(SB_PAIRS_TOGETHER, q_per_step // 2)

    def q_group(it, all_valid):
        blocks_per_group = 2 * n_together
        qb0 = pl.program_id(2) * q_per_step + blocks_per_group * it
        row0 = pl.multiple_of(blocks_per_group * it * SB_BLOCK, blocks_per_group * SB_BLOCK)
        q2s = [q_ref[0, pl.ds(row0 + 2 * p * SB_BLOCK, 2 * SB_BLOCK), :] * scale
               for p in range(n_together)]
        results = window_pairs(q2s, [qb0 + 2 * p for p in range(n_together)], all_valid)
        for b, (acc, rest) in enumerate(results):
            q = q2s[b // 2][(b % 2) * SB_BLOCK:(b % 2 + 1) * SB_BLOCK]
            finish(q, qb0 + b, acc, rest, pl.ds(row0 + b * SB_BLOCK, SB_BLOCK))

    def run(all_valid):
        def body(it, carry):
            q_group(it, all_valid)
            return carry
        lax.fori_loop(0, q_per_step // (2 * n_together), body, 0)

    first_full = -(-(SB_WINDOW - 1) // q_per_step)
    pl.when(pl.program_id(2) < first_full)(lambda: run(False))
    pl.when(pl.program_id(2) >= first_full)(lambda: run(True))


def _sb_attention(qkv, gain, width):
    bsz, seq, _ = qkv.shape
    n_pairs = width // PAIR
    nq = seq // SB_BLOCK
    i = np.arange(SB_BLOCK)
    upper = (i[:, None] > i[None, :]).astype(np.float32)
    half = np.concatenate([upper, np.ones_like(upper)], axis=1)
    cum_mat = jnp.asarray(half, dtype=BF16)
    hd = np.arange(PAIR) // HEAD_DIM
    e2 = jnp.asarray((hd[:, None] == hd[None, :]).astype(np.float32), dtype=BF16)
    qps = min(SB_Q_PER_STEP, nq)
    assert qps % 2 == 0 and nq % qps == 0, "query blocks are processed in adjacent pairs"
    qrows = qps * SB_BLOCK
    return pl.pallas_call(
        functools.partial(_sb_kernel, q_per_step=qps),
        out_shape=jax.ShapeDtypeStruct((bsz, seq, width), BF16),
        grid=(bsz, n_pairs, nq // qps),
        in_specs=[pl.BlockSpec((1, qrows, PAIR), lambda b, j, t: (b, t, j)),
                  pl.BlockSpec((1, seq, PAIR), lambda b, j, t: (b, 0, n_pairs + j)),
                  pl.BlockSpec((1, seq, PAIR), lambda b, j, t: (b, 0, 2 * n_pairs + j)),
                  pl.BlockSpec((1, PAIR), lambda b, j, t: (0, j)),
                  pl.BlockSpec(cum_mat.shape, lambda b, j, t: (0, 0)),
                  pl.BlockSpec(e2.shape, lambda b, j, t: (0, 0))],
        out_specs=pl.BlockSpec((1, qrows, PAIR), lambda b, j, t: (b, t, j)),
        compiler_params=_params(("parallel", "parallel", "arbitrary")),
        name="stickbreak_attn",
    )(qkv, qkv, qkv, gain.reshape(1, width), cum_mat, e2)


def _route_rows(lg):
    lane = lax.broadcasted_iota(jnp.int32, lg.shape, 1).astype(F32)
    neg = -jnp.inf
    first = lambda hit: jnp.min(jnp.where(hit, lane, float(LANE)), axis=-1, keepdims=True)
    gl = jnp.where(lane < N_GROUPS, lg, neg)
    gmax = jnp.max(gl, axis=-1, keepdims=True)
    g_sel = first(gl == gmax)
    p_sel = 1.0 / jnp.sum(jnp.exp(gl - gmax), axis=-1, keepdims=True)
    lo = N_GROUPS + g_sel * EXPERTS_PER_GROUP
    el = jnp.where(jnp.logical_and(lane >= lo, lane < lo + EXPERTS_PER_GROUP), lg, neg)
    v1 = jnp.max(el, axis=-1, keepdims=True)
    i1 = first(el == v1)
    el2 = jnp.where(lane == i1, neg, el)
    v2 = jnp.max(el2, axis=-1, keepdims=True)
    i2 = first(el2 == v2)
    t2 = jnp.exp(v2 - v1)
    w1 = p_sel / (1.0 + t2)
    return i1 - N_GROUPS, i2 - N_GROUPS, w1, w1 * t2


def _outproj_kernel(x_ref, ya_ref, yb_ref, wa_ref, wb_ref, g1_ref, gain_ref, sc_ref, sh_ref,
                    wr_ref, br_ref, x1_ref, h_ref, rt_ref):
    mix = _dot(ya_ref[...], wa_ref[...]) + _dot(yb_ref[...], wb_ref[...])
    x1 = x_ref[...] + g1_ref[0] * mix
    x1_ref[...] = x1
    ms = jnp.mean(x1 * x1, axis=-1, keepdims=True)
    h = x1 * lax.rsqrt(ms + RMS_EPS) * gain_ref[...] * (1.0 + sc_ref[0]) + sh_ref[0]
    h_ref[...] = h.astype(h_ref.dtype)
    h_hi, h_lo = _split2(h)
    w_hl = wr_ref[...]
    nr = w_hl.shape[1] // 2
    both = _dot(h_hi, w_hl)
    logits = both[:, :nr] + both[:, nr:] + _dot(h_lo, w_hl[:, :nr]) + br_ref[...]
    e1, e2, w1, w2 = _route_rows(logits)
    lane = lax.broadcasted_iota(jnp.int32, rt_ref.shape, 1)
    rt_ref[...] = jnp.where(lane == 0, e1, jnp.where(lane == 1, e2,
                                                      jnp.where(lane == 2, w1, w2)))


def _outproj_norm_router(x2, ya, yb, w_a, w_b, g1, gain, sc, sh, w_r, b_r, seq, tm):
    t, d = x2.shape
    half = ya.shape[1]
    nr = b_r.shape[1]
    w_hi, w_lo = _split2(w_r)
    w_r = jnp.concatenate([w_hi, w_lo], axis=1)
    per_b = seq // tm
    row = lambda w: pl.BlockSpec((tm, w), lambda i: (i, 0))
    full = lambda shape: pl.BlockSpec(shape, lambda i: (0,) * len(shape))
    per_batch = pl.BlockSpec((1, 1, d), lambda i: (i // per_b, 0, 0))
    return pl.pallas_call(
        _outproj_kernel,
        out_shape=(jax.ShapeDtypeStruct((t, d), F32), jax.ShapeDtypeStruct((t, d), F32),
                   jax.ShapeDtypeStruct((t, nr), F32)),
        grid=(t // tm,),
        in_specs=[row(d), row(half), row(half), full((half, d)), full((half, d)), per_batch,
                  full((1, d)), per_batch, per_batch, full((d, 2 * nr)), full((1, nr))],
        out_specs=(row(d), row(d), row(nr)),
        compiler_params=_params(("parallel",)),
        name="outproj_norm_router",
    )(x2, ya, yb, w_a, w_b, g1, gain.reshape(1, d), sc, sh, w_r, b_r)


MOE_ROWS = 256
MOE_VMEM_LIMIT = 58 * 1024 * 1024
_DMA_UNROLL = 8
_CAST_STEPS = 8
_GATHER_SLOTS = 3


def _moe_kernel(be_ref, nxt_ref, nused_ref, gcur_ref, gnext_ref, gahead_ref, scur_ref, sprev_ref,
                wt_ref, h_hbm, wg_hbm, wu_hbm, wd_hbm, y_hbm,
                xbuf, ybuf, wg_f, wu_f, wd_f, wg_b, wu_b, wd_b, gsem, ssem, wsem):
    i = pl.program_id(0)
    n_used = nused_ref[0]
    slot = i % 2
    gslot = i % _GATHER_SLOTS
    ahead = (i + 2) % _GATHER_SLOTS
    de = wg_b.shape[1]
    d = wd_b.shape[1]

    def weight_copies(e):
        pairs = ((wg_hbm, wg_f), (wu_hbm, wu_f), (wd_hbm, wd_f))
        return [pltpu.make_async_copy(src.at[e], dst, wsem.at[k])
                for k, (src, dst) in enumerate(pairs)]

    def gather(idx_ref, r, s):
        return pltpu.make_async_copy(h_hbm.at[pl.ds(idx_ref[0, 0, r], 1), :],
                                     xbuf.at[s, pl.ds(r, 1), :], gsem.at[s])

    def scatter(idx_ref, r, s):
        return pltpu.make_async_copy(ybuf.at[s, pl.ds(r, 1), :],
                                     y_hbm.at[pl.ds(idx_ref[0, 0, r], 1), :], ssem.at[s])

    def for_rows(fn):
        def body(r, c):
            fn(r)
            return c
        lax.fori_loop(0, MOE_ROWS, body, 0, unroll=_DMA_UNROLL)

    def gather_wait(s):
        pltpu.make_async_copy(h_hbm.at[pl.ds(0, MOE_ROWS), :], xbuf.at[s], gsem.at[s]).wait()

    def scatter_wait(s):
        pltpu.make_async_copy(ybuf.at[s], y_hbm.at[pl.ds(0, MOE_ROWS), :], ssem.at[s]).wait()

    @pl.when(jnp.logical_and(i == 0, n_used > 0))
    def _():
        for_rows(lambda r: gather(gcur_ref, r, 0).start())
        for_rows(lambda r: gather(gnext_ref, r, 1).start())
        for c in weight_copies(be_ref[0]):
            c.start()
        ybuf[1] = jnp.zeros(ybuf.shape[1:], ybuf.dtype)

    @pl.when(i < n_used)
    def _():
        e = be_ref[i]

        @pl.when(jnp.logical_or(i == 0, e != be_ref[jnp.maximum(i - 1, 0)]))
        def _():
            for c in weight_copies(e):
                c.wait()

            def cast_rows(k, c):
                for src, dst in ((wg_f, wg_b), (wu_f, wu_b), (wd_f, wd_b)):
                    nrow = src.shape[0] // _CAST_STEPS
                    rows = pl.ds(pl.multiple_of(k * nrow, nrow), nrow)
                    dst[rows, :] = src[rows, :].astype(BF16)
                return c
            lax.fori_loop(0, _CAST_STEPS, cast_rows, 0)

            @pl.when(nxt_ref[i] != e)
            def _():
                for c in weight_copies(nxt_ref[i]):
                    c.start(priority=1)

        gather_wait(gslot)

        def sliced_issue(n_slices, make_copy, priorities):
            per = MOE_ROWS // n_slices
            state = [0]

            def issue():
                for r in range(state[0], state[0] + per):
                    make_copy(r).start(priority=r % priorities)
                state[0] += per
            return issue

        n_up = de // MXU_TILE
        n_down = d // MXU_TILE
        issue_gather = sliced_issue(2 * n_up, lambda r: gather(gahead_ref, r, ahead), 1)
        issue_scatter = sliced_issue(n_down, lambda r: scatter(sprev_ref, r, 1 - slot), 2)

        xb = xbuf[gslot].astype(BF16)
        hg, hu = [], []
        for n in range(n_up):
            cols = slice(n * MXU_TILE, (n + 1) * MXU_TILE)
            hg.append(_dot(xb, wg_b[:, cols]))
            issue_gather()
            hu.append(_dot(xb, wu_b[:, cols]))
            issue_gather()
        hg = jnp.concatenate(hg, axis=1)
        hid = (hg * _sigmoid(hg) * jnp.concatenate(hu, axis=1)).astype(BF16)

        @pl.when(i >= 1)
        def _():
            scatter_wait(slot)

        wt = wt_ref[...]
        for n in range(n_down):
            cols = slice(n * MXU_TILE, (n + 1) * MXU_TILE)
            ybuf[slot, :, cols] = _dot(hid, wd_b[:, cols]) * wt
            issue_scatter()

        @pl.when(i + 1 >= n_used)
        def _():
            gather_wait((i + 1) % _GATHER_SLOTS)
            gather_wait(ahead)
            scatter_wait(1 - slot)
            for_rows(lambda r: scatter(scur_ref, r, slot).start())
            scatter_wait(slot)


def _moe_experts(h2, gidx, sidx, row_w, block_e, next_e, n_used, w_gate, w_up, w_down):
    t, d = h2.shape
    de = w_gate.shape[2]
    n_blocks = gidx.shape[0]
    assert MOE_ROWS % (2 * (de // MXU_TILE)) == 0 == MOE_ROWS % (d // MXU_TILE)
    assert d % _CAST_STEPS == 0 == de % _CAST_STEPS and sidx.shape[0] == n_blocks + 1
    idx_spec = lambda f: pl.BlockSpec((1, 1, MOE_ROWS), f, memory_space=pltpu.SMEM)
    hbm = pl.BlockSpec(memory_space=pl.ANY)
    grid_spec = pltpu.PrefetchScalarGridSpec(
        num_scalar_prefetch=3,
        grid=(n_blocks,),
        in_specs=[idx_spec(lambda i, *_: (i, 0, 0)),
                  idx_spec(lambda i, *_: (jnp.minimum(i + 1, n_blocks - 1), 0, 0)),
                  idx_spec(lambda i, *_: (jnp.minimum(i + 2, n_blocks - 1), 0, 0)),
                  idx_spec(lambda i, *_: (i, 0, 0)),
                  idx_spec(lambda i, *_: (jnp.where(i == 0, n_blocks, i - 1), 0, 0)),
                  pl.BlockSpec((MOE_ROWS, 1), lambda i, *_: (i, 0)),
                  hbm, hbm, hbm, hbm],
        out_specs=hbm,
        scratch_shapes=[pltpu.VMEM((_GATHER_SLOTS, MOE_ROWS, d), F32),
                        pltpu.VMEM((2, MOE_ROWS, d), F32),
                        pltpu.VMEM((d, de), F32), pltpu.VMEM((d, de), F32),
                        pltpu.VMEM((de, d), F32),
                        pltpu.VMEM((d, de), BF16), pltpu.VMEM((d, de), BF16),
                        pltpu.VMEM((de, d), BF16),
                        pltpu.SemaphoreType.DMA((_GATHER_SLOTS,)), pltpu.SemaphoreType.DMA((2,)),
                        pltpu.SemaphoreType.DMA((3,))],
    )
    return pl.pallas_call(
        _moe_kernel,
        out_shape=jax.ShapeDtypeStruct((TOP_K_IN_GROUP * t + MOE_ROWS, d), F32),
        grid_spec=grid_spec,
        compiler_params=_params(("arbitrary",), MOE_VMEM_LIMIT),
        name="moe_experts",
    )(block_e, next_e, n_used, gidx, gidx, gidx, sidx, sidx, row_w.reshape(-1, 1),
      h2, w_gate, w_up, w_down)


def _final_kernel(x1_ref, ya_ref, yb_ref, g2_ref, gain_ref, o_ref):
    x2 = x1_ref[...] + g2_ref[0] * (ya_ref[...] + yb_ref[...])
    ms = jnp.mean(x2 * x2, axis=-1, keepdims=True)
    o_ref[...] = x2 * lax.rsqrt(ms + RMS_EPS) * gain_ref[...]


def _final_norm(x1, y2, g2, gain, seq, tm):
    t, d = x1.shape
    per_b = seq // tm
    nt = t // tm
    row = pl.BlockSpec((tm, d), lambda i: (i, 0))
    return pl.pallas_call(
        _final_kernel,
        out_shape=jax.ShapeDtypeStruct((t, d), F32),
        grid=(nt,),
        in_specs=[row, row, pl.BlockSpec((tm, d), lambda i: (i + nt, 0)),
                  pl.BlockSpec((1, 1, d), lambda i: (i // per_b, 0, 0)),
                  pl.BlockSpec((1, d), lambda i: (0, 0))],
        out_specs=row,
        compiler_params=_params(("parallel",)),
        name="final_norm",
    )(x1, y2, y2, g2, gain.reshape(1, d))


def _route(routed):
    t = routed.shape[0]
    pair_w = routed[:, TOP_K_IN_GROUP:2 * TOP_K_IN_GROUP]
    flat_e = routed[:, :TOP_K_IN_GROUP].astype(jnp.int32).reshape(-1)
    flat_w = pair_w.reshape(-1)
    m = flat_e.shape[0]
    order = jnp.argsort(flat_e).astype(jnp.int32)
    experts = jnp.arange(N_EXPERTS, dtype=jnp.int32)
    counts = jnp.sum((flat_e[:, None] == experts[None, :]).astype(jnp.int32), axis=0)
    starts = jnp.cumsum(counts) - counts
    padded = (counts + MOE_ROWS - 1) // MOE_ROWS * MOE_ROWS
    pends = jnp.cumsum(padded)
    pstarts = pends - padded
    n_blocks = (m + N_EXPERTS * (MOE_ROWS - 1) + MOE_ROWS - 1) // MOE_ROWS
    block_start = jnp.arange(n_blocks, dtype=jnp.int32) * MOE_ROWS
    block_e = jnp.minimum(jnp.sum((block_start[:, None] >= pends[None, :]).astype(jnp.int32),
                                  axis=1), N_EXPERTS - 1)
    blk = jnp.arange(n_blocks, dtype=jnp.int32)[:, None]
    rin = jnp.arange(MOE_ROWS, dtype=jnp.int32)[None, :]
    pstart_b, count_b, start_b = lax.optimization_barrier(
        (pstarts[block_e], counts[block_e], starts[block_e]))
    off = blk * MOE_ROWS + rin - pstart_b[:, None]
    valid = off < count_b[:, None]
    src = jnp.clip(start_b[:, None] + off, 0, m - 1)
    assign = order[src]
    tok = assign // TOP_K_IN_GROUP
    gidx = jnp.where(valid, tok, 0)
    spare = TOP_K_IN_GROUP * t + rin
    sidx = jnp.where(valid, (assign % TOP_K_IN_GROUP) * t + tok, spare)
    sidx = jnp.concatenate([sidx, spare], axis=0)
    row_w = jnp.where(valid, flat_w[assign], 0.0)
    n_used = (pends[-1] // MOE_ROWS).astype(jnp.int32).reshape(1)
    later = (experts[None, :] > experts[:, None]) & (counts[None, :] > 0)
    next_expert = jnp.min(jnp.where(later, experts[None, :], N_EXPERTS), axis=1)
    next_expert = jnp.where(next_expert == N_EXPERTS, experts, next_expert)
    return (gidx.reshape(n_blocks, 1, MOE_ROWS), sidx.reshape(n_blocks + 1, 1, MOE_ROWS), row_w,
            block_e, next_expert[block_e], n_used)


def _pad_cols(w, n):
    return jnp.pad(w, ((0, 0), (0, n - w.shape[1])))


def _pad_rows(w, n):
    return jnp.pad(w, ((0, n - w.shape[0]), (0, 0)))


def _layer(x, mod, norm1_gain, w_in, shift_mu, w0, w_decay_up, a0, w_iclr_up, w_gate_up,
           k_k, k_a, r_k, ln_x_gain, ln_x_bias, sb_norm_gain, w_out, norm2_gain,
           w_router_group, b_router_group, w_router_expert, b_router_expert,
           w_exp_gate, w_exp_up, w_exp_down, *, rwkv_ts, tm_in, tm_out):
    bsz, seq, d = x.shape
    t = bsz * seq
    rw = w0.shape[0]
    sbw = sb_norm_gain.shape[0]
    sh1, sc1, g1, sh2, sc2, g2 = [m.reshape(bsz, 1, d) for m in jnp.split(mod, 6, axis=-1)]

    o = 3 * rw
    seg = lambda a, b: w_in[:, a:b]
    w_rwkv = jnp.concatenate([
        seg(0, o),
        _pad_cols(seg(o, o + DECAY_LORA), LANE),
        _pad_cols(seg(o + DECAY_LORA, o + DECAY_LORA + ICLR_LORA), LANE),
        _pad_cols(seg(o + DECAY_LORA + ICLR_LORA, o + DECAY_LORA + ICLR_LORA + GATE_LORA),
                  2 * LANE)], axis=1).astype(BF16)
    rcols = o + DECAY_LORA + ICLR_LORA + GATE_LORA
    w_sb = w_in[:, rcols:].astype(BF16)
    mseg = lambda a, b: shift_mu[a:b][None, :]
    mu = jnp.concatenate([
        mseg(0, o),
        _pad_cols(mseg(o, o + DECAY_LORA), LANE),
        _pad_cols(mseg(o + DECAY_LORA, o + DECAY_LORA + ICLR_LORA), LANE),
        _pad_cols(mseg(o + DECAY_LORA + ICLR_LORA, rcols), 2 * LANE)], axis=1)

    x2 = x.reshape(t, d)
    p_rwkv = _normmod_matmul(x2, norm1_gain, sc1, sh1, w_rwkv, F32, seq, tm_in, 512)
    qkv = _normmod_matmul(x2, norm1_gain, sc1, sh1, w_sb, BF16, seq, tm_in, 512)

    vecs = jnp.stack([w0, a0, k_k, k_a, r_k.reshape(-1), ln_x_gain, ln_x_bias,
                      jnp.zeros_like(w0)])
    y_a = _rwkv_time_mix(p_rwkv.reshape(bsz, seq, -1), mu, vecs,
                         _pad_rows(w_decay_up, LANE), _pad_rows(w_iclr_up, LANE),
                         _pad_rows(w_gate_up, 2 * LANE), ts=rwkv_ts)
    y_b = _sb_attention(qkv.reshape(bsz, seq, -1), sb_norm_gain, sbw)

    w_r = _pad_cols(jnp.concatenate([w_router_group, w_router_expert], axis=1), LANE)
    b_r = _pad_cols(jnp.concatenate([b_router_group, b_router_expert])[None, :], LANE)
    w_o = w_out.astype(BF16)
    x1, h2, routed = _outproj_norm_router(
        x2, y_a.reshape(t, rw), y_b.reshape(t, sbw), w_o[:rw], w_o[rw:], g1, norm2_gain,
        sc2, sh2, w_r, b_r, seq, tm_out)

    gidx, sidx, row_w, block_e, next_e, n_used = _route(routed)
    y2 = _moe_experts(h2, gidx, sidx, row_w, block_e, next_e, n_used,
                      w_exp_gate, w_exp_up, w_exp_down)
    return x1, y2, g2


def kernel(x, c, w_ada, b_ada, norm1_gain, w_in, shift_mu, w0, w_decay_up, a0, w_iclr_up, w_gate_up, k_k, k_a, r_k, ln_x_gain, ln_x_bias, sb_norm_gain, w_out, norm2_gain, w_router_group, b_router_group, w_router_expert, b_router_expert, w_exp_gate, w_exp_up, w_exp_down, final_norm_gain):
    bsz, seq, d = x.shape
    assert w_ada.shape[0] == 1, "the final norm is fused into the single layer's last kernel"
    l = 0
    tiles = dict(rwkv_ts=min(256, seq), tm_in=min(1024, seq), tm_out=min(256, seq))
    mod = _ada_mod(c, w_ada[l], b_ada[l])
    x1, y2, g2 = _layer(
        x, mod, norm1_gain[l], w_in[l], shift_mu[l], w0[l], w_decay_up[l], a0[l],
        w_iclr_up[l], w_gate_up[l], k_k[l], k_a[l], r_k[l], ln_x_gain[l], ln_x_bias[l],
        sb_norm_gain[l], w_out[l], norm2_gain[l], w_router_group[l], b_router_group[l],
        w_router_expert[l], b_router_expert[l], w_exp_gate[l], w_exp_up[l],
        w_exp_down[l], **tiles)
    out = _final_norm(x1, y2, g2, final_norm_gain, seq, min(512, seq))
    return out.reshape(bsz, seq, d)
```

```python
import functools

import jax
import jax.numpy as jnp
import numpy as np
from jax import lax
from jax.experimental import pallas as pl
from jax.experimental.pallas import tpu as pltpu

F32 = jnp.float32
BF16 = jnp.bfloat16

RMS_EPS = 1e-6
GN_EPS = 64e-5
L2_EPS = 1e-12

HEAD_DIM = 64
PAIR = 2 * HEAD_DIM
RWKV_CHUNK = 64
SB_BLOCK = 128
N_GROUPS = 8
EXPERTS_PER_GROUP = 8
N_EXPERTS = N_GROUPS * EXPERTS_PER_GROUP
TOP_K_IN_GROUP = 2
MOE_BLOCK = 128
DECAY_LORA = 64
ICLR_LORA = 64
GATE_LORA = 160
LANE = 128
MXU_TILE = 256
VMEM_LIMIT = 48 * 1024 * 1024
SB_UNDERFLOW_LOG = -104.0


def _dot(a, b):
    return lax.dot_general(a, b, (((1,), (0,)), ((), ())), preferred_element_type=F32)


def _dot_nt(a, b):
    return lax.dot_general(a, b, (((1,), (1,)), ((), ())), preferred_element_type=F32)


def _split2(x):
    hi = x.astype(BF16)
    lo = (x - hi.astype(F32)).astype(BF16)
    return hi, lo


def _mm(a, b, passes, nt=False):
    d = _dot_nt if nt else _dot
    if passes == 1:
        return d(a.astype(BF16), b.astype(BF16))
    ah, al = _split2(a)
    bh, bl = _split2(b)
    return d(ah, bh) + d(ah, bl) + d(al, bh)


def _mm_exact_rhs(a, b_exact):
    hi, lo = _split2(a)
    return _dot(hi, b_exact) + _dot(lo, b_exact)


def _mm_exact_lhs(a_exact, b):
    hi, lo = _split2(b)
    return _dot(a_exact, hi) + _dot(a_exact, lo)


def _group_sum(x, e_blk, split=True):
    w = e_blk.shape[0]
    one_pass = lambda a, b: _dot(a.astype(BF16), b)
    mm = _mm_exact_rhs if split else one_pass
    return jnp.concatenate([mm(x[:, g * w:(g + 1) * w], e_blk)
                            for g in range(x.shape[1] // w)], axis=1)


def _sigmoid(x):
    return 1.0 / (1.0 + jnp.exp(-x))


def _params(sem, vmem=VMEM_LIMIT):
    return pltpu.CompilerParams(dimension_semantics=sem, vmem_limit_bytes=vmem)


def _ada_kernel(c_ref, w_ref, b_ref, o_ref):
    c = c_ref[...]
    s = c * _sigmoid(c)
    o_ref[...] = _mm(s, w_ref[...], 3) + b_ref[...]


def _ada_mod(c, w, b):
    bsz, d = c.shape
    n = w.shape[1]
    rows = 8
    cp = jnp.zeros((rows, d), F32).at[:bsz].set(c)
    tn = 1024
    out = pl.pallas_call(
        _ada_kernel,
        out_shape=jax.ShapeDtypeStruct((rows, n), F32),
        grid=(n // tn,),
        in_specs=[pl.BlockSpec((rows, d), lambda j: (0, 0)),
                  pl.BlockSpec((d, tn), lambda j: (0, j)),
                  pl.BlockSpec((1, tn), lambda j: (0, j))],
        out_specs=pl.BlockSpec((rows, tn), lambda j: (0, j)),
        compiler_params=_params(("arbitrary",)),
        name="ada_mod",
    )(cp, w, b.reshape(1, n))
    return out[:bsz]


_NORM_SLAB = 128


def _normmod_mm_kernel(x_ref, gain_ref, sc_ref, sh_ref, w_ref, o_ref, h_scr):
    @pl.when(pl.program_id(1) == 0)
    def _():
        scale = gain_ref[...] * (1.0 + sc_ref[0])
        shift = sh_ref[0]
        slab = min(_NORM_SLAB, x_ref.shape[0])

        def norm_rows(k, c):
            rows = pl.ds(pl.multiple_of(k * slab, slab), slab)
            xf = x_ref[rows, :]
            ms = jnp.mean(xf * xf, axis=-1, keepdims=True)
            h_scr[rows, :] = (xf * lax.rsqrt(ms + RMS_EPS) * scale + shift).astype(BF16)
            return c
        lax.fori_loop(0, x_ref.shape[0] // slab, norm_rows, 0, unroll=2)

    o_ref[...] = _dot(h_scr[...], w_ref[pl.program_id(1)]).astype(o_ref.dtype)


def _normmod_matmul(x2, gain, sc, sh, w, out_dtype, seq, tm, tn):
    t, d = x2.shape
    n = w.shape[1]
    per_b = seq // tm
    w_tiles = w.reshape(d, n // tn, tn).transpose(1, 0, 2)
    return pl.pallas_call(
        _normmod_mm_kernel,
        out_shape=jax.ShapeDtypeStruct((t, n), out_dtype),
        grid=(t // tm, n // tn),
        in_specs=[pl.BlockSpec((tm, d), lambda i, j: (i, 0)),
                  pl.BlockSpec((1, d), lambda i, j: (0, 0)),
                  pl.BlockSpec((1, 1, d), lambda i, j: (i // per_b, 0, 0)),
                  pl.BlockSpec((1, 1, d), lambda i, j: (i // per_b, 0, 0)),
                  pl.BlockSpec(w_tiles.shape, lambda i, j: (0, 0, 0),
                               pipeline_mode=pl.Buffered(1))],
        out_specs=pl.BlockSpec((tm, tn), lambda i, j: (i, j)),
        scratch_shapes=[pltpu.VMEM((tm, d), BF16)],
        compiler_params=_params(("parallel", "arbitrary")),
        name="normmod_proj",
    )(x2, gain.reshape(1, d), sc, sh, w_tiles)


_V_W0, _V_A0, _V_KK, _V_KA, _V_RK, _V_LNG, _V_LNB = range(7)
_M_STRICT, _M_INCL, _M_BD8, _M_OFF8, _M_OFF16, _M_OFF32, _M_EYE = range(7)

RWKV_PASSES = 1
RWKV_GROUP = 2


def _rwkv_masks():
    i = np.arange(PAIR)[:, None]
    j = np.arange(PAIR)[None, :]
    strict = (j < i)
    incl = (j <= i)
    bd8 = (i // 8 == j // 8)
    def off(b):
        return (i // (2 * b) == j // (2 * b)) & (i // b > j // b)
    eye = (i == j)
    m = np.stack([strict, incl, bd8 & strict, off(8), off(16), off(32), eye]).astype(np.float32)
    return jnp.asarray(m)


def _tri_inverse(lmats, masks_ref, passes):
    eye = masks_ref[_M_EYE]
    bd8 = masks_ref[_M_BD8]
    mm = lambda a, b: _mm(a, b, passes)
    dblk = [l * bd8 for l in lmats]
    x = [eye + d for d in dblk]
    p = [mm(d, d) for d in dblk]
    x = [xi + mm(pi, xi) for xi, pi in zip(x, p)]
    p = [mm(pi, pi) for pi in p]
    x = [xi + mm(pi, xi) for xi, pi in zip(x, p)]
    for plane in (_M_OFF8, _M_OFF16, _M_OFF32):
        mask = masks_ref[plane]
        t = [mm(xi, l * mask) for xi, l in zip(x, lmats)]
        x = [xi + mm(ti, xi) for xi, ti in zip(x, t)]
    return x


def _rwkv_kernel(p_ref, mu_ref, vec_ref, wd_ref, wa_ref, wg_ref, e_ref, tri_ref,
                 masks_ref, o_ref,
                 rt_s, at_s, bt_s, kt_s, v_s, bd_s, kd_s, gc_s, y_s, g_s, bon_s,
                 state_s, carry_s, *, ts, n_pairs, passes):
    s_idx = pl.program_id(1)

    @pl.when(s_idx == 0)
    def _():
        state_s[...] = jnp.zeros_like(state_s)
        carry_s[...] = jnp.zeros_like(carry_s)

    width = n_pairs * PAIR
    pt = p_ref[0]
    row = lax.broadcasted_iota(jnp.int32, pt.shape, 0)
    prev = jnp.where(row == 0, carry_s[...], pltpu.roll(pt, 1, 0))
    carry_s[...] = pt[ts - 1:ts, :]
    pm = pt + (prev - pt) * mu_ref[...]

    r = pm[:, 0:width]
    k = pm[:, width:2 * width]
    v = pm[:, 2 * width:3 * width]
    o3 = 3 * width
    xw = pm[:, o3:o3 + LANE]
    xa = pm[:, o3 + LANE:o3 + 2 * LANE]
    xg = pm[:, o3 + 2 * LANE:o3 + 4 * LANE]

    vec = lambda i: vec_ref[i:i + 1, :]
    e_mat = e_ref[...]
    logw = -float(np.exp(-0.5)) * _sigmoid(vec(_V_W0) + _mm(jnp.tanh(xw), wd_ref[...], 3))
    a = _sigmoid(vec(_V_A0) + _mm(xa, wa_ref[...], 1))
    g_s[...] = _mm(_sigmoid(xg), wg_ref[...], 1)
    kk = k * vec(_V_KK)
    ss = _group_sum(kk * kk, e_mat)
    kk = kk * lax.rsqrt(jnp.maximum(ss, L2_EPS * L2_EPS))
    kp = k * (1.0 + (a - 1.0) * vec(_V_KA))
    ib = kk * a
    bon_s[...] = _group_sum(r * kp * vec(_V_RK), e_mat, split=False) * v
    cum = _mm_exact_lhs(tri_ref[...], logw)
    tot = jnp.concatenate(
        [jnp.broadcast_to(cum[c * RWKV_CHUNK + RWKV_CHUNK - 1:(c + 1) * RWKV_CHUNK, :],
                          (RWKV_CHUNK, width)) for c in range(ts // RWKV_CHUNK)], axis=0)
    op_dtype = rt_s.dtype
    rt_s[...] = (r * jnp.exp(cum)).astype(op_dtype)
    at_s[...] = (-kk * jnp.exp(cum - logw)).astype(op_dtype)
    dec_in = jnp.exp(-cum)
    bt_s[...] = (ib * dec_in).astype(op_dtype)
    kt_s[...] = (kp * dec_in).astype(op_dtype)
    v_s[...] = v
    dec_out = jnp.exp(tot - cum)
    bd_s[...] = (ib * dec_out).astype(op_dtype)
    kd_s[...] = (kp * dec_out).astype(op_dtype)
    gc_s[...] = jnp.exp(tot)

    lane = lax.broadcasted_iota(jnp.int32, (RWKV_CHUNK, PAIR), 1)
    in_h0 = lane < HEAD_DIM
    strict = masks_ref[_M_STRICT]
    incl = masks_ref[_M_INCL]
    operand = lambda x: x.astype(op_dtype)

    pairs = range(n_pairs)
    lanes = [slice(j * PAIR, (j + 1) * PAIR) for j in pairs]
    mm = lambda x, w: _mm(x, w, passes)
    mm_nt = lambda x, w: _mm(x, w, passes, nt=True)

    def chunk_group(g, carry):
        row0 = [pl.multiple_of((g * RWKV_GROUP + c) * RWKV_CHUNK, RWKV_CHUNK)
                for c in range(RWKV_GROUP)]
        items = [(c, j) for c in range(RWKV_GROUP) for j in pairs]

        def stacked(ref):
            blks = [ref[pl.ds(row0[c], RWKV_CHUNK), lanes[j]] for c, j in items]
            zero = jnp.zeros((), ref.dtype)
            return [jnp.concatenate([jnp.where(in_h0, b, zero), jnp.where(in_h0, zero, b)],
                                    axis=0) for b in blks]

        rh, ah, bh, kh = stacked(rt_s), stacked(at_s), stacked(bt_s), stacked(kt_s)
        vh, bdh, kdh = stacked(v_s), stacked(bd_s), stacked(kd_s)
        vo = [operand(x) for x in vh]
        bk = [jnp.concatenate([b, k], axis=0) for b, k in zip(bh, kh)]
        ga = [mm_nt(a, b) for a, b in zip(ah, bk)]
        gr = [mm_nt(r_, b) for r_, b in zip(rh, bk)]
        tinv = _tri_inverse([g_[:, :PAIR] * strict for g_ in ga], masks_ref, passes)
        for c in range(RWKV_GROUP):
            idx = [c * n_pairs + j for j in pairs]
            st = [state_s[j] for j in pairs]
            so = [operand(x) for x in st]
            x0 = [mm_nt(ah[i], so[j]) + mm(ga[i][:, PAIR:] * strict, vo[i])
                  for i, j in zip(idx, pairs)]
            u = [mm(tinv[i], x0[j]) for i, j in zip(idx, pairs)]
            y = [mm_nt(rh[i], so[j]) + mm(gr[i][:, :PAIR] * incl, u[j])
                 + mm(gr[i][:, PAIR:] * incl, vo[i]) for i, j in zip(idx, pairs)]
            for i, j in zip(idx, pairs):
                zt = jnp.concatenate([u[j], vh[i]], axis=0).T
                bkd = jnp.concatenate([bdh[i], kdh[i]], axis=0)
                gc = gc_s[pl.ds(row0[c], 1), lanes[j]]
                state_s[j] = st[j] * gc + mm(zt, bkd)
                y_s[pl.ds(row0[c], RWKV_CHUNK), lanes[j]] = y[j][:RWKV_CHUNK] + y[j][RWKV_CHUNK:]
        return carry

    lax.fori_loop(0, ts // (RWKV_CHUNK * RWKV_GROUP), chunk_group, 0)

    y = y_s[...]
    inv_n = 1.0 / HEAD_DIM
    mean = _group_sum(y, e_mat, split=False) * inv_n
    dlt = y - mean
    var = _group_sum(dlt * dlt, e_mat, split=False) * inv_n
    yn = dlt * lax.rsqrt(var + GN_EPS) * vec(_V_LNG) + vec(_V_LNB)
    o_ref[0] = ((yn + bon_s[...]) * g_s[...]).astype(o_ref.dtype)


def _rwkv_time_mix(p, mu, vecs, wd, wa, wg, *, ts, passes=RWKV_PASSES):
    bsz, seq, cols = p.shape
    width = vecs.shape[1]
    n_pairs = width // PAIR
    heads = np.arange(MXU_TILE) // HEAD_DIM
    e_mat = jnp.asarray((heads[:, None] == heads[None, :]).astype(np.float32), dtype=BF16)
    tok = np.arange(ts)
    same = (tok[:, None] // RWKV_CHUNK) == (tok[None, :] // RWKV_CHUNK)
    tri = jnp.asarray((same & (tok[None, :] <= tok[:, None])).astype(np.float32), dtype=BF16)
    masks = _rwkv_masks()
    full = lambda shape: pl.BlockSpec(shape, lambda b, s: (0,) * len(shape))
    big = lambda: pltpu.VMEM((ts, width), F32)
    op = lambda: pltpu.VMEM((ts, width), BF16 if passes == 1 else F32)
    kern = functools.partial(_rwkv_kernel, ts=ts, n_pairs=n_pairs, passes=passes)
    return pl.pallas_call(
        kern,
        out_shape=jax.ShapeDtypeStruct((bsz, seq, width), BF16),
        grid=(bsz, seq // ts),
        in_specs=[pl.BlockSpec((1, ts, cols), lambda b, s: (b, s, 0)),
                  full((1, cols)), full(vecs.shape), full(wd.shape), full(wa.shape),
                  full(wg.shape), full(e_mat.shape), full(tri.shape), full(masks.shape)],
        out_specs=pl.BlockSpec((1, ts, width), lambda b, s: (b, s, 0)),
        scratch_shapes=[op(), op(), op(), op(), big(), op(), op(), big(), big(), big(), big(),
                        pltpu.VMEM((n_pairs, PAIR, PAIR), F32), pltpu.VMEM((1, cols), F32)],
        compiler_params=_params(("parallel", "arbitrary")),
        name="rwkv7_scan",
    )(p, mu, vecs, wd, wa, wg, e_mat, tri, masks)


SB_WINDOW = 6
SB_TAIL = 2
SB_Q_PER_STEP = 4
SB_PAIRS_TOGETHER = 1


def _sb_kernel(q_ref, k_ref, v_ref, gain_ref, cum_ref, e2_ref, o_ref, *, q_per_step):
    lane = lax.broadcasted_iota(jnp.int32, (SB_BLOCK, PAIR), 1)
    h0 = lane < HEAD_DIM
    zero = jnp.zeros((), BF16)
    qi = lax.broadcasted_iota(jnp.int32, (SB_BLOCK, 2 * SB_BLOCK), 0)
    ki = lax.broadcasted_iota(jnp.int32, (SB_BLOCK, 2 * SB_BLOCK), 1) % SB_BLOCK
    causal = ki < qi
    cum_mat = cum_ref[...]
    scale = jnp.asarray(HEAD_DIM ** -0.5, BF16)
    width2 = 2 * SB_BLOCK

    heads = [slice(h * SB_BLOCK, (h + 1) * SB_BLOCK) for h in range(2)]

    def load_blocks(kb_top, n):
        khats, vhats, valid = [], [], []
        for j in range(n):
            kb = kb_top - j
            valid.append(kb >= 0)
            rows = pl.ds(pl.multiple_of(jnp.maximum(kb, 0) * SB_BLOCK, SB_BLOCK), SB_BLOCK)
            kblk = k_ref[0, rows, :]
            vblk = v_ref[0, rows, :]
            khats.append(jnp.concatenate([jnp.where(h0, kblk, zero), jnp.where(h0, zero, kblk)],
                                         axis=0))
            vhats.append(jnp.concatenate([jnp.where(h0, vblk, zero), jnp.where(h0, zero, vblk)],
                                         axis=0))
        return khats, vhats, valid

    def score_stage(zs, keeps):
        log_beta = [jnp.minimum(z, 0.0) - jnp.log(1.0 + jnp.exp(-jnp.abs(z))) for z in zs]
        log_1m = [lb - z for lb, z in zip(log_beta, zs)]
        log_1m = [l if m is None else jnp.where(m, l, 0.0) for l, m in zip(log_1m, keeps)]
        his = [l.astype(BF16) for l in log_1m]
        parts = [[_dot(hi[:, sl], cum_mat) for sl in heads] for hi in his]
        return log_beta, parts

    def weight_stage(log_beta, parts, keep, rest):
        between = jnp.concatenate([parts[0][:, :SB_BLOCK], parts[1][:, :SB_BLOCK]], axis=1)
        total = jnp.concatenate([parts[0][:, SB_BLOCK:], parts[1][:, SB_BLOCK:]], axis=1)
        w = jnp.exp(log_beta + between + rest)
        if keep is not None:
            w = jnp.where(keep, w, 0.0)
        return w.astype(BF16), rest + total

    def valid_mask(flag):
        return jnp.broadcast_to(flag, causal.shape)

    def sweep(q, kb_top, n, acc, rest):
        khats, vhats, valid = load_blocks(kb_top, n)
        z_all = _dot_nt(q, jnp.concatenate(khats, axis=0))
        keeps = [None] + [valid_mask(valid[j]) for j in range(1, n)]
        zs = [z_all[:, j * width2:(j + 1) * width2] for j in range(n)]
        log_beta, parts = score_stage(zs, keeps)
        ws = []
        for j in range(n):
            w, rest = weight_stage(log_beta[j], parts[j], keeps[j], rest)
            ws.append(w)
        acc = acc + _dot(jnp.concatenate(ws, axis=1), jnp.concatenate(vhats, axis=0))
        return acc, rest

    def window_pairs(q2s, qb_as, all_valid):
        n = SB_WINDOW
        off_diag = (lambda flag: None) if all_valid else valid_mask
        col = lambda jj: slice(jj * width2, (jj + 1) * width2)
        loaded = [load_blocks(qb_a + 1, n + 1) for qb_a in qb_as]
        z_alls = [_dot_nt(q2, jnp.concatenate(ld[0], axis=0))
                  for q2, ld in zip(q2s, loaded)]
        items = []
        for j in range(n):
            for p, (z_all, (_, _, valid)) in enumerate(zip(z_alls, loaded)):
                za = z_all[:SB_BLOCK, col(j + 1)]
                zb = z_all[SB_BLOCK:, col(j)]
                items.append((2 * p, za, causal if j == 0 else off_diag(valid[j + 1])))
                items.append((2 * p + 1, zb, causal if j == 0 else off_diag(valid[j])))
        log_beta, parts = score_stage([it[1] for it in items], [it[2] for it in items])
        n_streams = 2 * len(q2s)
        rest = [jnp.zeros((SB_BLOCK, width2), F32) for _ in range(n_streams)]
        ws = [[] for _ in range(n_streams)]
        for idx, (s, _, keep) in enumerate(items):
            w, rest[s] = weight_stage(log_beta[idx], parts[idx], keep, rest[s])
            ws[s].append(w)
        out = []
        for p, (_, vhats, _) in enumerate(loaded):
            acc_a = _dot(jnp.concatenate(ws[2 * p], axis=1), jnp.concatenate(vhats[1:], axis=0))
            acc_b = _dot(jnp.concatenate(ws[2 * p + 1], axis=1),
                         jnp.concatenate(vhats[:n], axis=0))
            out += [(acc_a, rest[2 * p]), (acc_b, rest[2 * p + 1])]
        return out

    def finish(q, qb, acc, rest, qrows):
        def cond(c):
            i, _, _, live = c
            return jnp.logical_and(i <= qb, live > SB_UNDERFLOW_LOG)

        def body(c):
            i, acc, rest, _ = c
            acc, rest = sweep(q, qb - i, SB_TAIL, acc, rest)
            return i + SB_TAIL, acc, rest, jnp.max(rest)

        _, acc, _, _ = lax.while_loop(cond, body,
                                      (jnp.int32(SB_WINDOW), acc, rest, jnp.max(rest)))
        ms = _mm_exact_rhs(acc * acc, e2_ref[...]) * (1.0 / HEAD_DIM)
        o_ref[0, qrows, :] = (acc * lax.rsqrt(ms + RMS_EPS) * gain_ref[...]).astype(o_ref.dtype)

    n_together = min(SB_PAIRS_TOGETHER, q_per_step // 2)

    def q_group(it, all_valid):
        blocks_per_group = 2 * n_together
        qb0 = pl.program_id(2) * q_per_step + blocks_per_group * it
        row0 = pl.multiple_of(blocks_per_group * it * SB_BLOCK, blocks_per_group * SB_BLOCK)
        q2s = [q_ref[0, pl.ds(row0 + 2 * p * SB_BLOCK, 2 * SB_BLOCK), :] * scale
               for p in range(n_together)]
        results = window_pairs(q2s, [qb0 + 2 * p for p in range(n_together)], all_valid)
        for b, (acc, rest) in enumerate(results):
            q = q2s[b // 2][(b % 2) * SB_BLOCK:(b % 2 + 1) * SB_BLOCK]
            finish(q, qb0 + b, acc, rest, pl.ds(row0 + b * SB_BLOCK, SB_BLOCK))

    def run(all_valid):
        def body(it, carry):
            q_group(it, all_valid)
            return carry
        lax.fori_loop(0, q_per_step // (2 * n_together), body, 0)

    first_full = -(-(SB_WINDOW - 1) // q_per_step)
    pl.when(pl.program_id(2) < first_full)(lambda: run(False))
    pl.when(pl.program_id(2) >= first_full)(lambda: run(True))


def _sb_attention(qkv, gain, width):
    bsz, seq, _ = qkv.shape
    n_pairs = width // PAIR
    nq = seq // SB_BLOCK
    i = np.arange(SB_BLOCK)
    upper = (i[:, None] > i[None, :]).astype(np.float32)
    half = np.concatenate([upper, np.ones_like(upper)], axis=1)
    cum_mat = jnp.asarray(half, dtype=BF16)
    hd = np.arange(PAIR) // HEAD_DIM
    e2 = jnp.asarray((hd[:, None] == hd[None, :]).astype(np.float32), dtype=BF16)
    qps = min(SB_Q_PER_STEP, nq)
    assert qps % 2 == 0 and nq % qps == 0, "query blocks are processed in adjacent pairs"
    qrows = qps * SB_BLOCK
    return pl.pallas_call(
        functools.partial(_sb_kernel, q_per_step=qps),
        out_shape=jax.ShapeDtypeStruct((bsz, seq, width), BF16),
        grid=(bsz, n_pairs, nq // qps),
        in_specs=[pl.BlockSpec((1, qrows, PAIR), lambda b, j, t: (b, t, j)),
                  pl.BlockSpec((1, seq, PAIR), lambda b, j, t: (b, 0, n_pairs + j)),
                  pl.BlockSpec((1, seq, PAIR), lambda b, j, t: (b, 0, 2 * n_pairs + j)),
                  pl.BlockSpec((1, PAIR), lambda b, j, t: (0, j)),
                  pl.BlockSpec(cum_mat.shape, lambda b, j, t: (0, 0)),
                  pl.BlockSpec(e2.shape, lambda b, j, t: (0, 0))],
        out_specs=pl.BlockSpec((1, qrows, PAIR), lambda b, j, t: (b, t, j)),
        compiler_params=_params(("parallel", "parallel", "arbitrary")),
        name="stickbreak_attn",
    )(qkv, qkv, qkv, gain.reshape(1, width), cum_mat, e2)


def _route_rows(lg):
    lane = lax.broadcasted_iota(jnp.int32, lg.shape, 1).astype(F32)
    neg = -jnp.inf
    first = lambda hit: jnp.min(jnp.where(hit, lane, float(LANE)), axis=-1, keepdims=True)
    gl = jnp.where(lane < N_GROUPS, lg, neg)
    gmax = jnp.max(gl, axis=-1, keepdims=True)
    g_sel = first(gl == gmax)
    p_sel = 1.0 / jnp.sum(jnp.exp(gl - gmax), axis=-1, keepdims=True)
    lo = N_GROUPS + g_sel * EXPERTS_PER_GROUP
    el = jnp.where(jnp.logical_and(lane >= lo, lane < lo + EXPERTS_PER_GROUP), lg, neg)
    v1 = jnp.max(el, axis=-1, keepdims=True)
    i1 = first(el == v1)
    el2 = jnp.where(lane == i1, neg, el)
    v2 = jnp.max(el2, axis=-1, keepdims=True)
    i2 = first(el2 == v2)
    t2 = jnp.exp(v2 - v1)
    w1 = p_sel / (1.0 + t2)
    return i1 - N_GROUPS, i2 - N_GROUPS, w1, w1 * t2


def _outproj_kernel(x_ref, ya_ref, yb_ref, wa_ref, wb_ref, g1_ref, gain_ref, sc_ref, sh_ref,
                    wr_ref, br_ref, x1_ref, h_ref, rt_ref):
    mix = _dot(ya_ref[...], wa_ref[...]) + _dot(yb_ref[...], wb_ref[...])
    x1 = x_ref[...] + g1_ref[0] * mix
    x1_ref[...] = x1
    ms = jnp.mean(x1 * x1, axis=-1, keepdims=True)
    h = x1 * lax.rsqrt(ms + RMS_EPS) * gain_ref[...] * (1.0 + sc_ref[0]) + sh_ref[0]
    h_ref[...] = h.astype(h_ref.dtype)
    h_hi, h_lo = _split2(h)
    w_hl = wr_ref[...]
    nr = w_hl.shape[1] // 2
    both = _dot(h_hi, w_hl)
    logits = both[:, :nr] + both[:, nr:] + _dot(h_lo, w_hl[:, :nr]) + br_ref[...]
    e1, e2, w1, w2 = _route_rows(logits)
    lane = lax.broadcasted_iota(jnp.int32, rt_ref.shape, 1)
    rt_ref[...] = jnp.where(lane == 0, e1, jnp.where(lane == 1, e2,
                                                      jnp.where(lane == 2, w1, w2)))


def _outproj_norm_router(x2, ya, yb, w_a, w_b, g1, gain, sc, sh, w_r, b_r, seq, tm):
    t, d = x2.shape
    half = ya.shape[1]
    nr = b_r.shape[1]
    w_hi, w_lo = _split2(w_r)
    w_r = jnp.concatenate([w_hi, w_lo], axis=1)
    per_b = seq // tm
    row = lambda w: pl.BlockSpec((tm, w), lambda i: (i, 0))
    full = lambda shape: pl.BlockSpec(shape, lambda i: (0,) * len(shape))
    per_batch = pl.BlockSpec((1, 1, d), lambda i: (i // per_b, 0, 0))
    return pl.pallas_call(
        _outproj_kernel,
        out_shape=(jax.ShapeDtypeStruct((t, d), F32), jax.ShapeDtypeStruct((t, d), F32),
                   jax.ShapeDtypeStruct((t, nr), F32)),
        grid=(t // tm,),
        in_specs=[row(d), row(half), row(half), full((half, d)), full((half, d)), per_batch,
                  full((1, d)), per_batch, per_batch, full((d, 2 * nr)), full((1, nr))],
        out_specs=(row(d), row(d), row(nr)),
        compiler_params=_params(("parallel",)),
        name="outproj_norm_router",
    )(x2, ya, yb, w_a, w_b, g1, gain.reshape(1, d), sc, sh, w_r, b_r)


MOE_ROWS = 256
MOE_VMEM_LIMIT = 58 * 1024 * 1024
_DMA_UNROLL = 8
_CAST_STEPS = 8
_GATHER_SLOTS = 3


def _moe_kernel(be_ref, nxt_ref, nused_ref, gcur_ref, gnext_ref, gahead_ref, scur_ref, sprev_ref,
                wt_ref, h_hbm, wg_hbm, wu_hbm, wd_hbm, y_hbm,
                xbuf, ybuf, wg_f, wu_f, wd_f, wg_b, wu_b, wd_b, gsem, ssem, wsem):
    i = pl.program_id(0)
    n_used = nused_ref[0]
    slot = i % 2
    gslot = i % _GATHER_SLOTS
    ahead = (i + 2) % _GATHER_SLOTS
    de = wg_b.shape[1]
    d = wd_b.shape[1]

    def weight_copies(e):
        pairs = ((wg_hbm, wg_f), (wu_hbm, wu_f), (wd_hbm, wd_f))
        return [pltpu.make_async_copy(src.at[e], dst, wsem.at[k])
                for k, (src, dst) in enumerate(pairs)]

    def gather(idx_ref, r, s):
        return pltpu.make_async_copy(h_hbm.at[pl.ds(idx_ref[0, 0, r], 1), :],
                                     xbuf.at[s, pl.ds(r, 1), :], gsem.at[s])

    def scatter(idx_ref, r, s):
        return pltpu.make_async_copy(ybuf.at[s, pl.ds(r, 1), :],
                                     y_hbm.at[pl.ds(idx_ref[0, 0, r], 1), :], ssem.at[s])

    def for_rows(fn):
        def body(r, c):
            fn(r)
            return c
        lax.fori_loop(0, MOE_ROWS, body, 0, unroll=_DMA_UNROLL)

    def gather_wait(s):
        pltpu.make_async_copy(h_hbm.at[pl.ds(0, MOE_ROWS), :], xbuf.at[s], gsem.at[s]).wait()

    def scatter_wait(s):
        pltpu.make_async_copy(ybuf.at[s], y_hbm.at[pl.ds(0, MOE_ROWS), :], ssem.at[s]).wait()

    @pl.when(jnp.logical_and(i == 0, n_used > 0))
    def _():
        for_rows(lambda r: gather(gcur_ref, r, 0).start())
        for_rows(lambda r: gather(gnext_ref, r, 1).start())
        for c in weight_copies(be_ref[0]):
            c.start()
        ybuf[1] = jnp.zeros(ybuf.shape[1:], ybuf.dtype)

    @pl.when(i < n_used)
    def _():
        e = be_ref[i]

        @pl.when(jnp.logical_or(i == 0, e != be_ref[jnp.maximum(i - 1, 0)]))
        def _():
            for c in weight_copies(e):
                c.wait()

            def cast_rows(k, c):
                for src, dst in ((wg_f, wg_b), (wu_f, wu_b), (wd_f, wd_b)):
                    nrow = src.shape[0] // _CAST_STEPS
                    rows = pl.ds(pl.multiple_of(k * nrow, nrow), nrow)
                    dst[rows, :] = src[rows, :].astype(BF16)
                return c
            lax.fori_loop(0, _CAST_STEPS, cast_rows, 0)

            @pl.when(nxt_ref[i] != e)
            def _():
                for c in weight_copies(nxt_ref[i]):
                    c.start(priority=1)

        gather_wait(gslot)

        def sliced_issue(n_slices, make_copy, priorities):
            per = MOE_ROWS // n_slices
            state = [0]

            def issue():
                for r in range(state[0], state[0] + per):
                    make_copy(r).start(priority=r % priorities)
                state[0] += per
            return issue

        n_up = de // MXU_TILE
        n_down = d // MXU_TILE
        issue_gather = sliced_issue(2 * n_up, lambda r: gather(gahead_ref, r, ahead), 2)
        issue_scatter = sliced_issue(n_down, lambda r: scatter(sprev_ref, r, 1 - slot), 2)

        xb = xbuf[gslot].astype(BF16)
        hg, hu = [], []
        for n in range(n_up):
            cols = slice(n * MXU_TILE, (n + 1) * MXU_TILE)
            hg.append(_dot(xb, wg_b[:, cols]))
            issue_gather()
            hu.append(_dot(xb, wu_b[:, cols]))
            issue_gather()
        hg = jnp.concatenate(hg, axis=1)
        hid = (hg * _sigmoid(hg) * jnp.concatenate(hu, axis=1)).astype(BF16)

        @pl.when(i >= 1)
        def _():
            scatter_wait(slot)

        wt = wt_ref[...]
        for n in range(n_down):
            cols = slice(n * MXU_TILE, (n + 1) * MXU_TILE)
            ybuf[slot, :, cols] = _dot(hid, wd_b[:, cols]) * wt
            issue_scatter()

        @pl.when(i + 1 >= n_used)
        def _():
            gather_wait((i + 1) % _GATHER_SLOTS)
            gather_wait(ahead)
            scatter_wait(1 - slot)
            for_rows(lambda r: scatter(scur_ref, r, slot).start())
            scatter_wait(slot)


def _moe_experts(h2, gidx, sidx, row_w, block_e, next_e, n_used, w_gate, w_up, w_down):
    t, d = h2.shape
    de = w_gate.shape[2]
    n_blocks = gidx.shape[0]
    assert MOE_ROWS % (2 * (de // MXU_TILE)) == 0 == MOE_ROWS % (d // MXU_TILE)
    assert d % _CAST_STEPS == 0 == de % _CAST_STEPS and sidx.shape[0] == n_blocks + 1
    idx_spec = lambda f: pl.BlockSpec((1, 1, MOE_ROWS), f, memory_space=pltpu.SMEM)
    hbm = pl.BlockSpec(memory_space=pl.ANY)
    grid_spec = pltpu.PrefetchScalarGridSpec(
        num_scalar_prefetch=3,
        grid=(n_blocks,),
        in_specs=[idx_spec(lambda i, *_: (i, 0, 0)),
                  idx_spec(lambda i, *_: (jnp.minimum(i + 1, n_blocks - 1), 0, 0)),
                  idx_spec(lambda i, *_: (jnp.minimum(i + 2, n_blocks - 1), 0, 0)),
                  idx_spec(lambda i, *_: (i, 0, 0)),
                  idx_spec(lambda i, *_: (jnp.where(i == 0, n_blocks, i - 1), 0, 0)),
                  pl.BlockSpec((MOE_ROWS, 1), lambda i, *_: (i, 0)),
                  hbm, hbm, hbm, hbm],
        out_specs=hbm,
        scratch_shapes=[pltpu.VMEM((_GATHER_SLOTS, MOE_ROWS, d), F32),
                        pltpu.VMEM((2, MOE_ROWS, d), F32),
                        pltpu.VMEM((d, de), F32), pltpu.VMEM((d, de), F32),
                        pltpu.VMEM((de, d), F32),
                        pltpu.VMEM((d, de), BF16), pltpu.VMEM((d, de), BF16),
                        pltpu.VMEM((de, d), BF16),
                        pltpu.SemaphoreType.DMA((_GATHER_SLOTS,)), pltpu.SemaphoreType.DMA((2,)),
                        pltpu.SemaphoreType.DMA((3,))],
    )
    return pl.pallas_call(
        _moe_kernel,
        out_shape=jax.ShapeDtypeStruct((TOP_K_IN_GROUP * t + MOE_ROWS, d), F32),
        grid_spec=grid_spec,
        compiler_params=_params(("arbitrary",), MOE_VMEM_LIMIT),
        name="moe_experts",
    )(block_e, next_e, n_used, gidx, gidx, gidx, sidx, sidx, row_w.reshape(-1, 1),
      h2, w_gate, w_up, w_down)


def _final_kernel(x1_ref, ya_ref, yb_ref, g2_ref, gain_ref, o_ref):
    x2 = x1_ref[...] + g2_ref[0] * (ya_ref[...] + yb_ref[...])
    ms = jnp.mean(x2 * x2, axis=-1, keepdims=True)
    o_ref[...] = x2 * lax.rsqrt(ms + RMS_EPS) * gain_ref[...]


def _final_norm(x1, y2, g2, gain, seq, tm):
    t, d = x1.shape
    per_b = seq // tm
    nt = t // tm
    row = pl.BlockSpec((tm, d), lambda i: (i, 0))
    return pl.pallas_call(
        _final_kernel,
        out_shape=jax.ShapeDtypeStruct((t, d), F32),
        grid=(nt,),
        in_specs=[row, row, pl.BlockSpec((tm, d), lambda i: (i + nt, 0)),
                  pl.BlockSpec((1, 1, d), lambda i: (i // per_b, 0, 0)),
                  pl.BlockSpec((1, d), lambda i: (0, 0))],
        out_specs=row,
        compiler_params=_params(("parallel",)),
        name="final_norm",
    )(x1, y2, y2, g2, gain.reshape(1, d))


def _route(routed):
    t = routed.shape[0]
    pair_w = routed[:, TOP_K_IN_GROUP:2 * TOP_K_IN_GROUP]
    flat_e = routed[:, :TOP_K_IN_GROUP].astype(jnp.int32).reshape(-1)
    flat_w = pair_w.reshape(-1)
    m = flat_e.shape[0]
    order = jnp.argsort(flat_e).astype(jnp.int32)
    experts = jnp.arange(N_EXPERTS, dtype=jnp.int32)
    counts = jnp.sum((flat_e[:, None] == experts[None, :]).astype(jnp.int32), axis=0)
    starts = jnp.cumsum(counts) - counts
    padded = (counts + MOE_ROWS - 1) // MOE_ROWS * MOE_ROWS
    pends = jnp.cumsum(padded)
    pstarts = pends - padded
    n_blocks = (m + N_EXPERTS * (MOE_ROWS - 1) + MOE_ROWS - 1) // MOE_ROWS
    block_start = jnp.arange(n_blocks, dtype=jnp.int32) * MOE_ROWS
    block_e = jnp.minimum(jnp.sum((block_start[:, None] >= pends[None, :]).astype(jnp.int32),
                                  axis=1), N_EXPERTS - 1)
    blk = jnp.arange(n_blocks, dtype=jnp.int32)[:, None]
    rin = jnp.arange(MOE_ROWS, dtype=jnp.int32)[None, :]
    pstart_b, count_b, start_b = lax.optimization_barrier(
        (pstarts[block_e], counts[block_e], starts[block_e]))
    off = blk * MOE_ROWS + rin - pstart_b[:, None]
    valid = off < count_b[:, None]
    src = jnp.clip(start_b[:, None] + off, 0, m - 1)
    assign = order[src]
    tok = assign // TOP_K_IN_GROUP
    gidx = jnp.where(valid, tok, 0)
    spare = TOP_K_IN_GROUP * t + rin
    sidx = jnp.where(valid, (assign % TOP_K_IN_GROUP) * t + tok, spare)
    sidx = jnp.concatenate([sidx, spare], axis=0)
    row_w = jnp.where(valid, flat_w[assign], 0.0)
    n_used = (pends[-1] // MOE_ROWS).astype(jnp.int32).reshape(1)
    later = (experts[None, :] > experts[:, None]) & (counts[None, :] > 0)
    next_expert = jnp.min(jnp.where(later, experts[None, :], N_EXPERTS), axis=1)
    next_expert = jnp.where(next_expert == N_EXPERTS, experts, next_expert)
    return (gidx.reshape(n_blocks, 1, MOE_ROWS), sidx.reshape(n_blocks + 1, 1, MOE_ROWS), row_w,
            block_e, next_expert[block_e], n_used)


def _pad_cols(w, n):
    return jnp.pad(w, ((0, 0), (0, n - w.shape[1])))


def _pad_rows(w, n):
    return jnp.pad(w, ((0, n - w.shape[0]), (0, 0)))


def _layer(x, mod, norm1_gain, w_in, shift_mu, w0, w_decay_up, a0, w_iclr_up, w_gate_up,
           k_k, k_a, r_k, ln_x_gain, ln_x_bias, sb_norm_gain, w_out, norm2_gain,
           w_router_group, b_router_group, w_router_expert, b_router_expert,
           w_exp_gate, w_exp_up, w_exp_down, *, rwkv_ts, tm_in, tm_out):
    bsz, seq, d = x.shape
    t = bsz * seq
    rw = w0.shape[0]
    sbw = sb_norm_gain.shape[0]
    sh1, sc1, g1, sh2, sc2, g2 = [m.reshape(bsz, 1, d) for m in jnp.split(mod, 6, axis=-1)]

    o = 3 * rw
    seg = lambda a, b: w_in[:, a:b]
    w_rwkv = jnp.concatenate([
        seg(0, o),
        _pad_cols(seg(o, o + DECAY_LORA), LANE),
        _pad_cols(seg(o + DECAY_LORA, o + DECAY_LORA + ICLR_LORA), LANE),
        _pad_cols(seg(o + DECAY_LORA + ICLR_LORA, o + DECAY_LORA + ICLR_LORA + GATE_LORA),
                  2 * LANE)], axis=1).astype(BF16)
    rcols = o + DECAY_LORA + ICLR_LORA + GATE_LORA
    w_sb = w_in[:, rcols:].astype(BF16)
    mseg = lambda a, b: shift_mu[a:b][None, :]
    mu = jnp.concatenate([
        mseg(0, o),
        _pad_cols(mseg(o, o + DECAY_LORA), LANE),
        _pad_cols(mseg(o + DECAY_LORA, o + DECAY_LORA + ICLR_LORA), LANE),
        _pad_cols(mseg(o + DECAY_LORA + ICLR_LORA, rcols), 2 * LANE)], axis=1)

    x2 = x.reshape(t, d)
    p_rwkv = _normmod_matmul(x2, norm1_gain, sc1, sh1, w_rwkv, F32, seq, tm_in, 512)
    qkv = _normmod_matmul(x2, norm1_gain, sc1, sh1, w_sb, BF16, seq, tm_in, 512)

    vecs = jnp.stack([w0, a0, k_k, k_a, r_k.reshape(-1), ln_x_gain, ln_x_bias,
                      jnp.zeros_like(w0)])
    y_a = _rwkv_time_mix(p_rwkv.reshape(bsz, seq, -1), mu, vecs,
                         _pad_rows(w_decay_up, LANE), _pad_rows(w_iclr_up, LANE),
                         _pad_rows(w_gate_up, 2 * LANE), ts=rwkv_ts)
    y_b = _sb_attention(qkv.reshape(bsz, seq, -1), sb_norm_gain, sbw)

    w_r = _pad_cols(jnp.concatenate([w_router_group, w_router_expert], axis=1), LANE)
    b_r = _pad_cols(jnp.concatenate([b_router_group, b_router_expert])[None, :], LANE)
    w_o = w_out.astype(BF16)
    x1, h2, routed = _outproj_norm_router(
        x2, y_a.reshape(t, rw), y_b.reshape(t, sbw), w_o[:rw], w_o[rw:], g1, norm2_gain,
        sc2, sh2, w_r, b_r, seq, tm_out)

    gidx, sidx, row_w, block_e, next_e, n_used = _route(routed)
    y2 = _moe_experts(h2, gidx, sidx, row_w, block_e, next_e, n_used,
                      w_exp_gate, w_exp_up, w_exp_down)
    return x1, y2, g2


def kernel(x, c, w_ada, b_ada, norm1_gain, w_in, shift_mu, w0, w_decay_up, a0, w_iclr_up, w_gate_up, k_k, k_a, r_k, ln_x_gain, ln_x_bias, sb_norm_gain, w_out, norm2_gain, w_router_group, b_router_group, w_router_expert, b_router_expert, w_exp_gate, w_exp_up, w_exp_down, final_norm_gain):
    bsz, seq, d = x.shape
    assert w_ada.shape[0] == 1, "the final norm is fused into the single layer's last kernel"
    l = 0
    tiles = dict(rwkv_ts=min(256, seq), tm_in=min(1024, seq), tm_out=min(256, seq))
    mod = _ada_mod(c, w_ada[l], b_ada[l])
    x1, y2, g2 = _layer(
        x, mod, norm1_gain[l], w_in[l], shift_mu[l], w0[l], w_decay_up[l], a0[l],
        w_iclr_up[l], w_gate_up[l], k_k[l], k_a[l], r_k[l], ln_x_gain[l], ln_x_bias[l],
        sb_norm_gain[l], w_out[l], norm2_gain[l], w_router_group[l], b_router_group[l],
        w_router_expert[l], b_router_expert[l], w_exp_gate[l], w_exp_up[l],
        w_exp_down[l], **tiles)
    out = _final_norm(x1, y2, g2, final_norm_gain, seq, min(512, seq))
    return out.reshape(bsz, seq, d)
```

```python
import functools

import jax
import jax.numpy as jnp
import numpy as np
from jax import lax
from jax.experimental import pallas as pl
from jax.experimental.pallas import tpu as pltpu

F32 = jnp.float32
BF16 = jnp.bfloat16

RMS_EPS = 1e-6
GN_EPS = 64e-5
L2_EPS = 1e-12

HEAD_DIM = 64
PAIR = 2 * HEAD_DIM
RWKV_CHUNK = 64
SB_BLOCK = 128
N_GROUPS = 8
EXPERTS_PER_GROUP = 8
N_EXPERTS = N_GROUPS * EXPERTS_PER_GROUP
TOP_K_IN_GROUP = 2
MOE_BLOCK = 128
DECAY_LORA = 64
ICLR_LORA = 64
GATE_LORA = 160
LANE = 128
MXU_TILE = 256
VMEM_LIMIT = 48 * 1024 * 1024
SB_UNDERFLOW_LOG = -104.0


def _dot(a, b):
    return lax.dot_general(a, b, (((1,), (0,)), ((), ())), preferred_element_type=F32)


def _dot_nt(a, b):
    return lax.dot_general(a, b, (((1,), (1,)), ((), ())), preferred_element_type=F32)


def _split2(x):
    hi = x.astype(BF16)
    lo = (x - hi.astype(F32)).astype(BF16)
    return hi, lo


def _mm(a, b, passes, nt=False):
    d = _dot_nt if nt else _dot
    if passes == 1:
        return d(a.astype(BF16), b.astype(BF16))
    ah, al = _split2(a)
    bh, bl = _split2(b)
    return d(ah, bh) + d(ah, bl) + d(al, bh)


def _mm_exact_rhs(a, b_exact):
    hi, lo = _split2(a)
    return _dot(hi, b_exact) + _dot(lo, b_exact)


def _mm_exact_lhs(a_exact, b):
    hi, lo = _split2(b)
    return _dot(a_exact, hi) + _dot(a_exact, lo)


def _group_sum(x, e_blk, split=True):
    w = e_blk.shape[0]
    one_pass = lambda a, b: _dot(a.astype(BF16), b)
    mm = _mm_exact_rhs if split else one_pass
    return jnp.concatenate([mm(x[:, g * w:(g + 1) * w], e_blk)
                            for g in range(x.shape[1] // w)], axis=1)


def _sigmoid(x):
    return 1.0 / (1.0 + jnp.exp(-x))


def _params(sem, vmem=VMEM_LIMIT):
    return pltpu.CompilerParams(dimension_semantics=sem, vmem_limit_bytes=vmem)


def _ada_kernel(c_ref, w_ref, b_ref, o_ref):
    c = c_ref[...]
    s = c * _sigmoid(c)
    o_ref[...] = _mm(s, w_ref[...], 3) + b_ref[...]


def _ada_mod(c, w, b):
    bsz, d = c.shape
    n = w.shape[1]
    rows = 8
    cp = jnp.zeros((rows, d), F32).at[:bsz].set(c)
    tn = 1024
    out = pl.pallas_call(
        _ada_kernel,
        out_shape=jax.ShapeDtypeStruct((rows, n), F32),
        grid=(n // tn,),
        in_specs=[pl.BlockSpec((rows, d), lambda j: (0, 0)),
                  pl.BlockSpec((d, tn), lambda j: (0, j)),
                  pl.BlockSpec((1, tn), lambda j: (0, j))],
        out_specs=pl.BlockSpec((rows, tn), lambda j: (0, j)),
        compiler_params=_params(("arbitrary",)),
        name="ada_mod",
    )(cp, w, b.reshape(1, n))
    return out[:bsz]


_NORM_SLAB = 128


def _normmod_mm_kernel(x_ref, gain_ref, sc_ref, sh_ref, w_ref, o_ref, h_scr):
    @pl.when(pl.program_id(1) == 0)
    def _():
        scale = gain_ref[...] * (1.0 + sc_ref[0])
        shift = sh_ref[0]
        slab = min(_NORM_SLAB, x_ref.shape[0])

        def norm_rows(k, c):
            rows = pl.ds(pl.multiple_of(k * slab, slab), slab)
            xf = x_ref[rows, :]
            ms = jnp.mean(xf * xf, axis=-1, keepdims=True)
            h_scr[rows, :] = (xf * lax.rsqrt(ms + RMS_EPS) * scale + shift).astype(BF16)
            return c
        lax.fori_loop(0, x_ref.shape[0] // slab, norm_rows, 0, unroll=2)

    o_ref[...] = _dot(h_scr[...], w_ref[pl.program_id(1)]).astype(o_ref.dtype)


def _normmod_matmul(x2, gain, sc, sh, w, out_dtype, seq, tm, tn):
    t, d = x2.shape
    n = w.shape[1]
    per_b = seq // tm
    w_tiles = w.reshape(d, n // tn, tn).transpose(1, 0, 2)
    return pl.pallas_call(
        _normmod_mm_kernel,
        out_shape=jax.ShapeDtypeStruct((t, n), out_dtype),
        grid=(t // tm, n // tn),
        in_specs=[pl.BlockSpec((tm, d), lambda i, j: (i, 0)),
                  pl.BlockSpec((1, d), lambda i, j: (0, 0)),
                  pl.BlockSpec((1, 1, d), lambda i, j: (i // per_b, 0, 0)),
                  pl.BlockSpec((1, 1, d), lambda i, j: (i // per_b, 0, 0)),
                  pl.BlockSpec(w_tiles.shape, lambda i, j: (0, 0, 0),
                               pipeline_mode=pl.Buffered(1))],
        out_specs=pl.BlockSpec((tm, tn), lambda i, j: (i, j)),
        scratch_shapes=[pltpu.VMEM((tm, d), BF16)],
        compiler_params=_params(("parallel", "arbitrary")),
        name="normmod_proj",
    )(x2, gain.reshape(1, d), sc, sh, w_tiles)


_V_W0, _V_A0, _V_KK, _V_KA, _V_RK, _V_LNG, _V_LNB = range(7)
_M_STRICT, _M_INCL, _M_BD8, _M_OFF8, _M_OFF16, _M_OFF32, _M_EYE = range(7)

RWKV_PASSES = 1
RWKV_GROUP = 2


def _rwkv_masks():
    i = np.arange(PAIR)[:, None]
    j = np.arange(PAIR)[None, :]
    strict = (j < i)
    incl = (j <= i)
    bd8 = (i // 8 == j // 8)
    def off(b):
        return (i // (2 * b) == j // (2 * b)) & (i // b > j // b)
    eye = (i == j)
    m = np.stack([strict, incl, bd8 & strict, off(8), off(16), off(32), eye]).astype(np.float32)
    return jnp.asarray(m)


def _tri_inverse(lmats, masks_ref, passes):
    eye = masks_ref[_M_EYE]
    bd8 = masks_ref[_M_BD8]
    mm = lambda a, b: _mm(a, b, passes)
    dblk = [l * bd8 for l in lmats]
    x = [eye + d for d in dblk]
    p = [mm(d, d) for d in dblk]
    x = [xi + mm(pi, xi) for xi, pi in zip(x, p)]
    p = [mm(pi, pi) for pi in p]
    x = [xi + mm(pi, xi) for xi, pi in zip(x, p)]
    for plane in (_M_OFF8, _M_OFF16, _M_OFF32):
        mask = masks_ref[plane]
        t = [mm(xi, l * mask) for xi, l in zip(x, lmats)]
        x = [xi + mm(ti, xi) for xi, ti in zip(x, t)]
    return x


def _rwkv_kernel(p_ref, mu_ref, vec_ref, wd_ref, wa_ref, wg_ref, e_ref, tri_ref,
                 masks_ref, o_ref,
                 rt_s, at_s, bt_s, kt_s, v_s, bd_s, kd_s, gc_s, y_s, g_s, bon_s,
                 state_s, carry_s, *, ts, n_pairs, passes):
    s_idx = pl.program_id(1)

    @pl.when(s_idx == 0)
    def _():
        state_s[...] = jnp.zeros_like(state_s)
        carry_s[...] = jnp.zeros_like(carry_s)

    width = n_pairs * PAIR
    pt = p_ref[0]
    row = lax.broadcasted_iota(jnp.int32, pt.shape, 0)
    prev = jnp.where(row == 0, carry_s[...], pltpu.roll(pt, 1, 0))
    carry_s[...] = pt[ts - 1:ts, :]
    pm = pt + (prev - pt) * mu_ref[...]

    r = pm[:, 0:width]
    k = pm[:, width:2 * width]
    v = pm[:, 2 * width:3 * width]
    o3 = 3 * width
    xw = pm[:, o3:o3 + LANE]
    xa = pm[:, o3 + LANE:o3 + 2 * LANE]
    xg = pm[:, o3 + 2 * LANE:o3 + 4 * LANE]

    vec = lambda i: vec_ref[i:i + 1, :]
    e_mat = e_ref[...]
    logw = -float(np.exp(-0.5)) * _sigmoid(vec(_V_W0) + _mm(jnp.tanh(xw), wd_ref[...], 3))
    a = _sigmoid(vec(_V_A0) + _mm(xa, wa_ref[...], 1))
    g_s[...] = _mm(_sigmoid(xg), wg_ref[...], 1)
    kk = k * vec(_V_KK)
    ss = _group_sum(kk * kk, e_mat)
    kk = kk * lax.rsqrt(jnp.maximum(ss, L2_EPS * L2_EPS))
    kp = k * (1.0 + (a - 1.0) * vec(_V_KA))
    ib = kk * a
    bon_s[...] = _group_sum(r * kp * vec(_V_RK), e_mat, split=False) * v
    cum = _mm_exact_lhs(tri_ref[...], logw)
    tot = jnp.concatenate(
        [jnp.broadcast_to(cum[c * RWKV_CHUNK + RWKV_CHUNK - 1:(c + 1) * RWKV_CHUNK, :],
                          (RWKV_CHUNK, width)) for c in range(ts // RWKV_CHUNK)], axis=0)
    op_dtype = rt_s.dtype
    rt_s[...] = (r * jnp.exp(cum)).astype(op_dtype)
    at_s[...] = (-kk * jnp.exp(cum - logw)).astype(op_dtype)
    dec_in = jnp.exp(-cum)
    bt_s[...] = (ib * dec_in).astype(op_dtype)
    kt_s[...] = (kp * dec_in).astype(op_dtype)
    v_s[...] = v
    dec_out = jnp.exp(tot - cum)
    bd_s[...] = (ib * dec_out).astype(op_dtype)
    kd_s[...] = (kp * dec_out).astype(op_dtype)
    gc_s[...] = jnp.exp(tot)

    lane = lax.broadcasted_iota(jnp.int32, (RWKV_CHUNK, PAIR), 1)
    in_h0 = lane < HEAD_DIM
    strict = masks_ref[_M_STRICT]
    incl = masks_ref[_M_INCL]
    operand = lambda x: x.astype(op_dtype)

    pairs = range(n_pairs)
    lanes = [slice(j * PAIR, (j + 1) * PAIR) for j in pairs]
    mm = lambda x, w: _mm(x, w, passes)
    mm_nt = lambda x, w: _mm(x, w, passes, nt=True)

    def chunk_group(g, carry):
        row0 = [pl.multiple_of((g * RWKV_GROUP + c) * RWKV_CHUNK, RWKV_CHUNK)
                for c in range(RWKV_GROUP)]
        items = [(c, j) for c in range(RWKV_GROUP) for j in pairs]

        def stacked(ref):
            blks = [ref[pl.ds(row0[c], RWKV_CHUNK), lanes[j]] for c, j in items]
            zero = jnp.zeros((), ref.dtype)
            return [jnp.concatenate([jnp.where(in_h0, b, zero), jnp.where(in_h0, zero, b)],
                                    axis=0) for b in blks]

        rh, ah, bh, kh = stacked(rt_s), stacked(at_s), stacked(bt_s), stacked(kt_s)
        vh, bdh, kdh = stacked(v_s), stacked(bd_s), stacked(kd_s)
        vo = [operand(x) for x in vh]
        bk = [jnp.concatenate([b, k], axis=0) for b, k in zip(bh, kh)]
        ga = [mm_nt(a, b) for a, b in zip(ah, bk)]
        gr = [mm_nt(r_, b) for r_, b in zip(rh, bk)]
        tinv = _tri_inverse([g_[:, :PAIR] * strict for g_ in ga], masks_ref, passes)
        for c in range(RWKV_GROUP):
            idx = [c * n_pairs + j for j in pairs]
            st = [state_s[j] for j in pairs]
            so = [operand(x) for x in st]
            x0 = [mm_nt(ah[i], so[j]) + mm(ga[i][:, PAIR:] * strict, vo[i])
                  for i, j in zip(idx, pairs)]
            u = [mm(tinv[i], x0[j]) for i, j in zip(idx, pairs)]
            y = [mm_nt(rh[i], so[j]) + mm(gr[i][:, :PAIR] * incl, u[j])
                 + mm(gr[i][:, PAIR:] * incl, vo[i]) for i, j in zip(idx, pairs)]
            for i, j in zip(idx, pairs):
                zt = jnp.concatenate([u[j], vh[i]], axis=0).T
                bkd = jnp.concatenate([bdh[i], kdh[i]], axis=0)
                gc = gc_s[pl.ds(row0[c], 1), lanes[j]]
                state_s[j] = st[j] * gc + mm(zt, bkd)
                y_s[pl.ds(row0[c], RWKV_CHUNK), lanes[j]] = y[j][:RWKV_CHUNK] + y[j][RWKV_CHUNK:]
        return carry

    lax.fori_loop(0, ts // (RWKV_CHUNK * RWKV_GROUP), chunk_group, 0)

    y = y_s[...]
    inv_n = 1.0 / HEAD_DIM
    mean = _group_sum(y, e_mat, split=False) * inv_n
    dlt = y - mean
    var = _group_sum(dlt * dlt, e_mat, split=False) * inv_n
    yn = dlt * lax.rsqrt(var + GN_EPS) * vec(_V_LNG) + vec(_V_LNB)
    o_ref[0] = ((yn + bon_s[...]) * g_s[...]).astype(o_ref.dtype)


def _rwkv_time_mix(p, mu, vecs, wd, wa, wg, *, ts, passes=RWKV_PASSES):
    bsz, seq, cols = p.shape
    width = vecs.shape[1]
    n_pairs = width // PAIR
    heads = np.arange(MXU_TILE) // HEAD_DIM
    e_mat = jnp.asarray((heads[:, None] == heads[None, :]).astype(np.float32), dtype=BF16)
    tok = np.arange(ts)
    same = (tok[:, None] // RWKV_CHUNK) == (tok[None, :] // RWKV_CHUNK)
    tri = jnp.asarray((same & (tok[None, :] <= tok[:, None])).astype(np.float32), dtype=BF16)
    masks = _rwkv_masks()
    full = lambda shape: pl.BlockSpec(shape, lambda b, s: (0,) * len(shape))
    big = lambda: pltpu.VMEM((ts, width), F32)
    op = lambda: pltpu.VMEM((ts, width), BF16 if passes == 1 else F32)
    kern = functools.partial(_rwkv_kernel, ts=ts, n_pairs=n_pairs, passes=passes)
    return pl.pallas_call(
        kern,
        out_shape=jax.ShapeDtypeStruct((bsz, seq, width), BF16),
        grid=(bsz, seq // ts),
        in_specs=[pl.BlockSpec((1, ts, cols), lambda b, s: (b, s, 0)),
                  full((1, cols)), full(vecs.shape), full(wd.shape), full(wa.shape),
                  full(wg.shape), full(e_mat.shape), full(tri.shape), full(masks.shape)],
        out_specs=pl.BlockSpec((1, ts, width), lambda b, s: (b, s, 0)),
        scratch_shapes=[op(), op(), op(), op(), big(), op(), op(), big(), big(), big(), big(),
                        pltpu.VMEM((n_pairs, PAIR, PAIR), F32), pltpu.VMEM((1, cols), F32)],
        compiler_params=_params(("parallel", "arbitrary")),
        name="rwkv7_scan",
    )(p, mu, vecs, wd, wa, wg, e_mat, tri, masks)


SB_WINDOW = 6
SB_TAIL = 2
SB_Q_PER_STEP = 4
SB_PAIRS_TOGETHER = 2


def _sb_kernel(q_ref, k_ref, v_ref, gain_ref, cum_ref, e2_ref, o_ref, *, q_per_step):
    lane = lax.broadcasted_iota(jnp.int32, (SB_BLOCK, PAIR), 1)
    h0 = lane < HEAD_DIM
    zero = jnp.zeros((), BF16)
    qi = lax.broadcasted_iota(jnp.int32, (SB_BLOCK, 2 * SB_BLOCK), 0)
    ki = lax.broadcasted_iota(jnp.int32, (SB_BLOCK, 2 * SB_BLOCK), 1) % SB_BLOCK
    causal = ki < qi
    cum_mat = cum_ref[...]
    scale = jnp.asarray(HEAD_DIM ** -0.5, BF16)
    width2 = 2 * SB_BLOCK

    heads = [slice(h * SB_BLOCK, (h + 1) * SB_BLOCK) for h in range(2)]

    def load_blocks(kb_top, n):
        khats, vhats, valid = [], [], []
        for j in range(n):
            kb = kb_top - j
            valid.append(kb >= 0)
            rows = pl.ds(pl.multiple_of(jnp.maximum(kb, 0) * SB_BLOCK, SB_BLOCK), SB_BLOCK)
            kblk = k_ref[0, rows, :]
            vblk = v_ref[0, rows, :]
            khats.append(jnp.concatenate([jnp.where(h0, kblk, zero), jnp.where(h0, zero, kblk)],
                                         axis=0))
            vhats.append(jnp.concatenate([jnp.where(h0, vblk, zero), jnp.where(h0, zero, vblk)],
                                         axis=0))
        return khats, vhats, valid

    def score_stage(zs, keeps):
        log_beta = [jnp.minimum(z, 0.0) - jnp.log(1.0 + jnp.exp(-jnp.abs(z))) for z in zs]
        log_1m = [lb - z for lb, z in zip(log_beta, zs)]
        log_1m = [l if m is None else jnp.where(m, l, 0.0) for l, m in zip(log_1m, keeps)]
        his = [l.astype(BF16) for l in log_1m]
        parts = [[_dot(hi[:, sl], cum_mat) for sl in heads] for hi in his]
        return log_beta, parts

    def weight_stage(log_beta, parts, keep, rest):
        between = jnp.concatenate([parts[0][:, :SB_BLOCK], parts[1][:, :SB_BLOCK]], axis=1)
        total = jnp.concatenate([parts[0][:, SB_BLOCK:], parts[1][:, SB_BLOCK:]], axis=1)
        w = jnp.exp(log_beta + between + rest)
        if keep is not None:
            w = jnp.where(keep, w, 0.0)
        return w.astype(BF16), rest + total

    def valid_mask(flag):
        return jnp.broadcast_to(flag, causal.shape)

    def sweep(q, kb_top, n, acc, rest):
        khats, vhats, valid = load_blocks(kb_top, n)
        z_all = _dot_nt(q, jnp.concatenate(khats, axis=0))
        keeps = [None] + [valid_mask(valid[j]) for j in range(1, n)]
        zs = [z_all[:, j * width2:(j + 1) * width2] for j in range(n)]
        log_beta, parts = score_stage(zs, keeps)
        ws = []
        for j in range(n):
            w, rest = weight_stage(log_beta[j], parts[j], keeps[j], rest)
            ws.append(w)
        acc = acc + _dot(jnp.concatenate(ws, axis=1), jnp.concatenate(vhats, axis=0))
        return acc, rest

    def window_pairs(q2s, qb_as, all_valid):
        n = SB_WINDOW
        off_diag = (lambda flag: None) if all_valid else valid_mask
        col = lambda jj: slice(jj * width2, (jj + 1) * width2)
        loaded = [load_blocks(qb_a + 1, n + 1) for qb_a in qb_as]
        z_alls = [_dot_nt(q2, jnp.concatenate(ld[0], axis=0))
                  for q2, ld in zip(q2s, loaded)]
        items = []
        for j in range(n):
            for p, (z_all, (_, _, valid)) in enumerate(zip(z_alls, loaded)):
                za = z_all[:SB_BLOCK, col(j + 1)]
                zb = z_all[SB_BLOCK:, col(j)]
                items.append((2 * p, za, causal if j == 0 else off_diag(valid[j + 1])))
                items.append((2 * p + 1, zb, causal if j == 0 else off_diag(valid[j])))
        log_beta, parts = score_stage([it[1] for it in items], [it[2] for it in items])
        n_streams = 2 * len(q2s)
        rest = [jnp.zeros((SB_BLOCK, width2), F32) for _ in range(n_streams)]
        ws = [[] for _ in range(n_streams)]
        for idx, (s, _, keep) in enumerate(items):
            w, rest[s] = weight_stage(log_beta[idx], parts[idx], keep, rest[s])
            ws[s].append(w)
        out = []
        for p, (_, vhats, _) in enumerate(loaded):
            acc_a = _dot(jnp.concatenate(ws[2 * p], axis=1), jnp.concatenate(vhats[1:], axis=0))
            acc_b = _dot(jnp.concatenate(ws[2 * p + 1], axis=1),
                         jnp.concatenate(vhats[:n], axis=0))
            out += [(acc_a, rest[2 * p]), (acc_b, rest[2 * p + 1])]
        return out

    def finish(q, qb, acc, rest, qrows):
        def cond(c):
            i, _, _, live = c
            return jnp.logical_and(i <= qb, live > SB_UNDERFLOW_LOG)

        def body(c):
            i, acc, rest, _ = c
            acc, rest = sweep(q, qb - i, SB_TAIL, acc, rest)
            return i + SB_TAIL, acc, rest, jnp.max(rest)

        _, acc, _, _ = lax.while_loop(cond, body,
                                      (jnp.int32(SB_WINDOW), acc, rest, jnp.max(rest)))
        ms = _mm_exact_rhs(acc * acc, e2_ref[...]) * (1.0 / HEAD_DIM)
        o_ref[0, qrows, :] = (acc * lax.rsqrt(ms + RMS_EPS) * gain_ref[...]).astype(o_ref.dtype)

    n_together = min(SB_PAIRS_TOGETHER, q_per_step // 2)

    def q_group(it, all_valid):
        blocks_per_group = 2 * n_together
        qb0 = pl.program_id(2) * q_per_step + blocks_per_group * it
        row0 = pl.multiple_of(blocks_per_group * it * SB_BLOCK, blocks_per_group * SB_BLOCK)
        q2s = [q_ref[0, pl.ds(row0 + 2 * p * SB_BLOCK, 2 * SB_BLOCK), :] * scale
               for p in range(n_together)]
        results = window_pairs(q2s, [qb0 + 2 * p for p in range(n_together)], all_valid)
        for b, (acc, rest) in enumerate(results):
            q = q2s[b // 2][(b % 2) * SB_BLOCK:(b % 2 + 1) * SB_BLOCK]
            finish(q, qb0 + b, acc, rest, pl.ds(row0 + b * SB_BLOCK, SB_BLOCK))

    def run(all_valid):
        def body(it, carry):
            q_group(it, all_valid)
            return carry
        lax.fori_loop(0, q_per_step // (2 * n_together), body, 0)

    first_full = -(-(SB_WINDOW - 1) // q_per_step)
    pl.when(pl.program_id(2) < first_full)(lambda: run(False))
    pl.when(pl.program_id(2) >= first_full)(lambda: run(True))


def _sb_attention(qkv, gain, width):
    bsz, seq, _ = qkv.shape
    n_pairs = width // PAIR
    nq = seq // SB_BLOCK
    i = np.arange(SB_BLOCK)
    upper = (i[:, None] > i[None, :]).astype(np.float32)
    half = np.concatenate([upper, np.ones_like(upper)], axis=1)
    cum_mat = jnp.asarray(half, dtype=BF16)
    hd = np.arange(PAIR) // HEAD_DIM
    e2 = jnp.asarray((hd[:, None] == hd[None, :]).astype(np.float32), dtype=BF16)
    qps = min(SB_Q_PER_STEP, nq)
    assert qps % 2 == 0 and nq % qps == 0, "query blocks are processed in adjacent pairs"
    qrows = qps * SB_BLOCK
    return pl.pallas_call(
        functools.partial(_sb_kernel, q_per_step=qps),
        out_shape=jax.ShapeDtypeStruct((bsz, seq, width), BF16),
        grid=(bsz, n_pairs, nq // qps),
        in_specs=[pl.BlockSpec((1, qrows, PAIR), lambda b, j, t: (b, t, j)),
                  pl.BlockSpec((1, seq, PAIR), lambda b, j, t: (b, 0, n_pairs + j)),
                  pl.BlockSpec((1, seq, PAIR), lambda b, j, t: (b, 0, 2 * n_pairs + j)),
                  pl.BlockSpec((1, PAIR), lambda b, j, t: (0, j)),
                  pl.BlockSpec(cum_mat.shape, lambda b, j, t: (0, 0)),
                  pl.BlockSpec(e2.shape, lambda b, j, t: (0, 0))],
        out_specs=pl.BlockSpec((1, qrows, PAIR), lambda b, j, t: (b, t, j)),
        compiler_params=_params(("parallel", "parallel", "arbitrary")),
        name="stickbreak_attn",
    )(qkv, qkv, qkv, gain.reshape(1, width), cum_mat, e2)


def _route_rows(lg):
    lane = lax.broadcasted_iota(jnp.int32, lg.shape, 1).astype(F32)
    neg = -jnp.inf
    first = lambda hit: jnp.min(jnp.where(hit, lane, float(LANE)), axis=-1, keepdims=True)
    gl = jnp.where(lane < N_GROUPS, lg, neg)
    gmax = jnp.max(gl, axis=-1, keepdims=True)
    g_sel = first(gl == gmax)
    p_sel = 1.0 / jnp.sum(jnp.exp(gl - gmax), axis=-1, keepdims=True)
    lo = N_GROUPS + g_sel * EXPERTS_PER_GROUP
    el = jnp.where(jnp.logical_and(lane >= lo, lane < lo + EXPERTS_PER_GROUP), lg, neg)
    v1 = jnp.max(el, axis=-1, keepdims=True)
    i1 = first(el == v1)
    el2 = jnp.where(lane == i1, neg, el)
    v2 = jnp.max(el2, axis=-1, keepdims=True)
    i2 = first(el2 == v2)
    t2 = jnp.exp(v2 - v1)
    w1 = p_sel / (1.0 + t2)
    return i1 - N_GROUPS, i2 - N_GROUPS, w1, w1 * t2


def _outproj_kernel(x_ref, ya_ref, yb_ref, wa_ref, wb_ref, g1_ref, gain_ref, sc_ref, sh_ref,
                    wr_ref, br_ref, x1_ref, h_ref, rt_ref):
    mix = _dot(ya_ref[...], wa_ref[...]) + _dot(yb_ref[...], wb_ref[...])
    x1 = x_ref[...] + g1_ref[0] * mix
    x1_ref[...] = x1
    ms = jnp.mean(x1 * x1, axis=-1, keepdims=True)
    h = x1 * lax.rsqrt(ms + RMS_EPS) * gain_ref[...] * (1.0 + sc_ref[0]) + sh_ref[0]
    h_ref[...] = h.astype(h_ref.dtype)
    h_hi, h_lo = _split2(h)
    w_hl = wr_ref[...]
    nr = w_hl.shape[1] // 2
    both = _dot(h_hi, w_hl)
    logits = both[:, :nr] + both[:, nr:] + _dot(h_lo, w_hl[:, :nr]) + br_ref[...]
    e1, e2, w1, w2 = _route_rows(logits)
    lane = lax.broadcasted_iota(jnp.int32, rt_ref.shape, 1)
    rt_ref[...] = jnp.where(lane == 0, e1, jnp.where(lane == 1, e2,
                                                      jnp.where(lane == 2, w1, w2)))


def _outproj_norm_router(x2, ya, yb, w_a, w_b, g1, gain, sc, sh, w_r, b_r, seq, tm):
    t, d = x2.shape
    half = ya.shape[1]
    nr = b_r.shape[1]
    w_hi, w_lo = _split2(w_r)
    w_r = jnp.concatenate([w_hi, w_lo], axis=1)
    per_b = seq // tm
    row = lambda w: pl.BlockSpec((tm, w), lambda i: (i, 0))
    full = lambda shape: pl.BlockSpec(shape, lambda i: (0,) * len(shape))
    per_batch = pl.BlockSpec((1, 1, d), lambda i: (i // per_b, 0, 0))
    return pl.pallas_call(
        _outproj_kernel,
        out_shape=(jax.ShapeDtypeStruct((t, d), F32), jax.ShapeDtypeStruct((t, d), F32),
                   jax.ShapeDtypeStruct((t, nr), F32)),
        grid=(t // tm,),
        in_specs=[row(d), row(half), row(half), full((half, d)), full((half, d)), per_batch,
                  full((1, d)), per_batch, per_batch, full((d, 2 * nr)), full((1, nr))],
        out_specs=(row(d), row(d), row(nr)),
        compiler_params=_params(("parallel",)),
        name="outproj_norm_router",
    )(x2, ya, yb, w_a, w_b, g1, gain.reshape(1, d), sc, sh, w_r, b_r)


MOE_ROWS = 256
MOE_VMEM_LIMIT = 58 * 1024 * 1024
_DMA_UNROLL = 8
_CAST_STEPS = 8
_GATHER_SLOTS = 3


def _moe_kernel(be_ref, nxt_ref, nused_ref, gcur_ref, gnext_ref, gahead_ref, scur_ref, sprev_ref,
                wt_ref, h_hbm, wg_hbm, wu_hbm, wd_hbm, y_hbm,
                xbuf, ybuf, wg_f, wu_f, wd_f, wg_b, wu_b, wd_b, gsem, ssem, wsem):
    i = pl.program_id(0)
    n_used = nused_ref[0]
    slot = i % 2
    gslot = i % _GATHER_SLOTS
    ahead = (i + 2) % _GATHER_SLOTS
    de = wg_b.shape[1]
    d = wd_b.shape[1]

    def weight_copies(e):
        pairs = ((wg_hbm, wg_f), (wu_hbm, wu_f), (wd_hbm, wd_f))
        return [pltpu.make_async_copy(src.at[e], dst, wsem.at[k])
                for k, (src, dst) in enumerate(pairs)]

    def gather(idx_ref, r, s):
        return pltpu.make_async_copy(h_hbm.at[pl.ds(idx_ref[0, 0, r], 1), :],
                                     xbuf.at[s, pl.ds(r, 1), :], gsem.at[s])

    def scatter(idx_ref, r, s):
        return pltpu.make_async_copy(ybuf.at[s, pl.ds(r, 1), :],
                                     y_hbm.at[pl.ds(idx_ref[0, 0, r], 1), :], ssem.at[s])

    def for_rows(fn):
        def body(r, c):
            fn(r)
            return c
        lax.fori_loop(0, MOE_ROWS, body, 0, unroll=_DMA_UNROLL)

    def gather_wait(s):
        pltpu.make_async_copy(h_hbm.at[pl.ds(0, MOE_ROWS), :], xbuf.at[s], gsem.at[s]).wait()

    def scatter_wait(s):
        pltpu.make_async_copy(ybuf.at[s], y_hbm.at[pl.ds(0, MOE_ROWS), :], ssem.at[s]).wait()

    @pl.when(jnp.logical_and(i == 0, n_used > 0))
    def _():
        for_rows(lambda r: gather(gcur_ref, r, 0).start())
        for_rows(lambda r: gather(gnext_ref, r, 1).start())
        for c in weight_copies(be_ref[0]):
            c.start()
        ybuf[1] = jnp.zeros(ybuf.shape[1:], ybuf.dtype)

    @pl.when(i < n_used)
    def _():
        e = be_ref[i]

        @pl.when(jnp.logical_or(i == 0, e != be_ref[jnp.maximum(i - 1, 0)]))
        def _():
            for c in weight_copies(e):
                c.wait()

            def cast_rows(k, c):
                for src, dst in ((wg_f, wg_b), (wu_f, wu_b), (wd_f, wd_b)):
                    nrow = src.shape[0] // _CAST_STEPS
                    rows = pl.ds(pl.multiple_of(k * nrow, nrow), nrow)
                    dst[rows, :] = src[rows, :].astype(BF16)
                return c
            lax.fori_loop(0, _CAST_STEPS, cast_rows, 0)

            @pl.when(nxt_ref[i] != e)
            def _():
                for c in weight_copies(nxt_ref[i]):
                    c.start(priority=1)

        gather_wait(gslot)

        def sliced_issue(n_slices, make_copy, priorities):
            per = MOE_ROWS // n_slices
            state = [0]

            def issue():
                for r in range(state[0], state[0] + per):
                    make_copy(r).start(priority=r % priorities)
                state[0] += per
            return issue

        n_up = de // MXU_TILE
        n_down = d // MXU_TILE
        issue_gather = sliced_issue(2 * n_up, lambda r: gather(gahead_ref, r, ahead), 1)
        issue_scatter = sliced_issue(n_down, lambda r: scatter(sprev_ref, r, 1 - slot), 2)

        xb = xbuf[gslot].astype(BF16)
        hg, hu = [], []
        for n in range(n_up):
            cols = slice(n * MXU_TILE, (n + 1) * MXU_TILE)
            hg.append(_dot(xb, wg_b[:, cols]))
            issue_gather()
            hu.append(_dot(xb, wu_b[:, cols]))
            issue_gather()
        hg = jnp.concatenate(hg, axis=1)
        hid = (hg * _sigmoid(hg) * jnp.concatenate(hu, axis=1)).astype(BF16)

        @pl.when(i >= 1)
        def _():
            scatter_wait(slot)

        wt = wt_ref[...]
        for n in range(n_down):
            cols = slice(n * MXU_TILE, (n + 1) * MXU_TILE)
            ybuf[slot, :, cols] = _dot(hid, wd_b[:, cols]) * wt
            issue_scatter()

        @pl.when(i + 1 >= n_used)
        def _():
            gather_wait((i + 1) % _GATHER_SLOTS)
            gather_wait(ahead)
            scatter_wait(1 - slot)
            for_rows(lambda r: scatter(scur_ref, r, slot).start())
            scatter_wait(slot)


def _moe_experts(h2, gidx, sidx, row_w, block_e, next_e, n_used, w_gate, w_up, w_down):
    t, d = h2.shape
    de = w_gate.shape[2]
    n_blocks = gidx.shape[0]
    assert MOE_ROWS % (2 * (de // MXU_TILE)) == 0 == MOE_ROWS % (d // MXU_TILE)
    assert d % _CAST_STEPS == 0 == de % _CAST_STEPS and sidx.shape[0] == n_blocks + 1
    idx_spec = lambda f: pl.BlockSpec((1, 1, MOE_ROWS), f, memory_space=pltpu.SMEM)
    hbm = pl.BlockSpec(memory_space=pl.ANY)
    grid_spec = pltpu.PrefetchScalarGridSpec(
        num_scalar_prefetch=3,
        grid=(n_blocks,),
        in_specs=[idx_spec(lambda i, *_: (i, 0, 0)),
                  idx_spec(lambda i, *_: (jnp.minimum(i + 1, n_blocks - 1), 0, 0)),
                  idx_spec(lambda i, *_: (jnp.minimum(i + 2, n_blocks - 1), 0, 0)),
                  idx_spec(lambda i, *_: (i, 0, 0)),
                  idx_spec(lambda i, *_: (jnp.where(i == 0, n_blocks, i - 1), 0, 0)),
                  pl.BlockSpec((MOE_ROWS, 1), lambda i, *_: (i, 0)),
                  hbm, hbm, hbm, hbm],
        out_specs=hbm,
        scratch_shapes=[pltpu.VMEM((_GATHER_SLOTS, MOE_ROWS, d), F32),
                        pltpu.VMEM((2, MOE_ROWS, d), F32),
                        pltpu.VMEM((d, de), F32), pltpu.VMEM((d, de), F32),
                        pltpu.VMEM((de, d), F32),
                        pltpu.VMEM((d, de), BF16), pltpu.VMEM((d, de), BF16),
                        pltpu.VMEM((de, d), BF16),
                        pltpu.SemaphoreType.DMA((_GATHER_SLOTS,)), pltpu.SemaphoreType.DMA((2,)),
                        pltpu.SemaphoreType.DMA((3,))],
    )
    return pl.pallas_call(
        _moe_kernel,
        out_shape=jax.ShapeDtypeStruct((TOP_K_IN_GROUP * t + MOE_ROWS, d), F32),
        grid_spec=grid_spec,
        compiler_params=_params(("arbitrary",), MOE_VMEM_LIMIT),
        name="moe_experts",
    )(block_e, next_e, n_used, gidx, gidx, gidx, sidx, sidx, row_w.reshape(-1, 1),
      h2, w_gate, w_up, w_down)


def _final_kernel(x1_ref, ya_ref, yb_ref, g2_ref, gain_ref, o_ref):
    x2 = x1_ref[...] + g2_ref[0] * (ya_ref[...] + yb_ref[...])
    ms = jnp.mean(x2 * x2, axis=-1, keepdims=True)
    o_ref[...] = x2 * lax.rsqrt(ms + RMS_EPS) * gain_ref[...]


def _final_norm(x1, y2, g2, gain, seq, tm):
    t, d = x1.shape
    per_b = seq // tm
    nt = t // tm
    row = pl.BlockSpec((tm, d), lambda i: (i, 0))
    return pl.pallas_call(
        _final_kernel,
        out_shape=jax.ShapeDtypeStruct((t, d), F32),
        grid=(nt,),
        in_specs=[row, row, pl.BlockSpec((tm, d), lambda i: (i + nt, 0)),
                  pl.BlockSpec((1, 1, d), lambda i: (i // per_b, 0, 0)),
                  pl.BlockSpec((1, d), lambda i: (0, 0))],
        out_specs=row,
        compiler_params=_params(("parallel",)),
        name="final_norm",
    )(x1, y2, y2, g2, gain.reshape(1, d))


def _route(routed):
    t = routed.shape[0]
    pair_w = routed[:, TOP_K_IN_GROUP:2 * TOP_K_IN_GROUP]
    flat_e = routed[:, :TOP_K_IN_GROUP].astype(jnp.int32).reshape(-1)
    flat_w = pair_w.reshape(-1)
    m = flat_e.shape[0]
    order = jnp.argsort(flat_e).astype(jnp.int32)
    experts = jnp.arange(N_EXPERTS, dtype=jnp.int32)
    counts = jnp.sum((flat_e[:, None] == experts[None, :]).astype(jnp.int32), axis=0)
    starts = jnp.cumsum(counts) - counts
    padded = (counts + MOE_ROWS - 1) // MOE_ROWS * MOE_ROWS
    pends = jnp.cumsum(padded)
    pstarts = pends - padded
    n_blocks = (m + N_EXPERTS * (MOE_ROWS - 1) + MOE_ROWS - 1) // MOE_ROWS
    block_start = jnp.arange(n_blocks, dtype=jnp.int32) * MOE_ROWS
    block_e = jnp.minimum(jnp.sum((block_start[:, None] >= pends[None, :]).astype(jnp.int32),
                                  axis=1), N_EXPERTS - 1)
    blk = jnp.arange(n_blocks, dtype=jnp.int32)[:, None]
    rin = jnp.arange(MOE_ROWS, dtype=jnp.int32)[None, :]
    pstart_b, count_b, start_b = lax.optimization_barrier(
        (pstarts[block_e], counts[block_e], starts[block_e]))
    off = blk * MOE_ROWS + rin - pstart_b[:, None]
    valid = off < count_b[:, None]
    src = jnp.clip(start_b[:, None] + off, 0, m - 1)
    assign = order[src]
    tok = assign // TOP_K_IN_GROUP
    gidx = jnp.where(valid, tok, 0)
    spare = TOP_K_IN_GROUP * t + rin
    sidx = jnp.where(valid, (assign % TOP_K_IN_GROUP) * t + tok, spare)
    sidx = jnp.concatenate([sidx, spare], axis=0)
    row_w = jnp.where(valid, flat_w[assign], 0.0)
    n_used = (pends[-1] // MOE_ROWS).astype(jnp.int32).reshape(1)
    later = (experts[None, :] > experts[:, None]) & (counts[None, :] > 0)
    next_expert = jnp.min(jnp.where(later, experts[None, :], N_EXPERTS), axis=1)
    next_expert = jnp.where(next_expert == N_EXPERTS, experts, next_expert)
    return (gidx.reshape(n_blocks, 1, MOE_ROWS), sidx.reshape(n_blocks + 1, 1, MOE_ROWS), row_w,
            block_e, next_expert[block_e], n_used)


def _pad_cols(w, n):
    return jnp.pad(w, ((0, 0), (0, n - w.shape[1])))


def _pad_rows(w, n):
    return jnp.pad(w, ((0, n - w.shape[0]), (0, 0)))


def _layer(x, mod, norm1_gain, w_in, shift_mu, w0, w_decay_up, a0, w_iclr_up, w_gate_up,
           k_k, k_a, r_k, ln_x_gain, ln_x_bias, sb_norm_gain, w_out, norm2_gain,
           w_router_group, b_router_group, w_router_expert, b_router_expert,
           w_exp_gate, w_exp_up, w_exp_down, *, rwkv_ts, tm_in, tm_out):
    bsz, seq, d = x.shape
    t = bsz * seq
    rw = w0.shape[0]
    sbw = sb_norm_gain.shape[0]
    sh1, sc1, g1, sh2, sc2, g2 = [m.reshape(bsz, 1, d) for m in jnp.split(mod, 6, axis=-1)]

    o = 3 * rw
    seg = lambda a, b: w_in[:, a:b]
    w_rwkv = jnp.concatenate([
        seg(0, o),
        _pad_cols(seg(o, o + DECAY_LORA), LANE),
        _pad_cols(seg(o + DECAY_LORA, o + DECAY_LORA + ICLR_LORA), LANE),
        _pad_cols(seg(o + DECAY_LORA + ICLR_LORA, o + DECAY_LORA + ICLR_LORA + GATE_LORA),
                  2 * LANE)], axis=1).astype(BF16)
    rcols = o + DECAY_LORA + ICLR_LORA + GATE_LORA
    w_sb = w_in[:, rcols:].astype(BF16)
    mseg = lambda a, b: shift_mu[a:b][None, :]
    mu = jnp.concatenate([
        mseg(0, o),
        _pad_cols(mseg(o, o + DECAY_LORA), LANE),
        _pad_cols(mseg(o + DECAY_LORA, o + DECAY_LORA + ICLR_LORA), LANE),
        _pad_cols(mseg(o + DECAY_LORA + ICLR_LORA, rcols), 2 * LANE)], axis=1)

    x2 = x.reshape(t, d)
    p_rwkv = _normmod_matmul(x2, norm1_gain, sc1, sh1, w_rwkv, F32, seq, tm_in, 512)
    qkv = _normmod_matmul(x2, norm1_gain, sc1, sh1, w_sb, BF16, seq, tm_in, 512)

    vecs = jnp.stack([w0, a0, k_k, k_a, r_k.reshape(-1), ln_x_gain, ln_x_bias,
                      jnp.zeros_like(w0)])
    y_a = _rwkv_time_mix(p_rwkv.reshape(bsz, seq, -1), mu, vecs,
                         _pad_rows(w_decay_up, LANE), _pad_rows(w_iclr_up, LANE),
                         _pad_rows(w_gate_up, 2 * LANE), ts=rwkv_ts)
    y_b = _sb_attention(qkv.reshape(bsz, seq, -1), sb_norm_gain, sbw)

    w_r = _pad_cols(jnp.concatenate([w_router_group, w_router_expert], axis=1), LANE)
    b_r = _pad_cols(jnp.concatenate([b_router_group, b_router_expert])[None, :], LANE)
    w_o = w_out.astype(BF16)
    x1, h2, routed = _outproj_norm_router(
        x2, y_a.reshape(t, rw), y_b.reshape(t, sbw), w_o[:rw], w_o[rw:], g1, norm2_gain,
        sc2, sh2, w_r, b_r, seq, tm_out)

    gidx, sidx, row_w, block_e, next_e, n_used = _route(routed)
    y2 = _moe_experts(h2, gidx, sidx, row_w, block_e, next_e, n_used,
                      w_exp_gate, w_exp_up, w_exp_down)
    return x1, y2, g2


def kernel(x, c, w_ada, b_ada, norm1_gain, w_in, shift_mu, w0, w_decay_up, a0, w_iclr_up, w_gate_up, k_k, k_a, r_k, ln_x_gain, ln_x_bias, sb_norm_gain, w_out, norm2_gain, w_router_group, b_router_group, w_router_expert, b_router_expert, w_exp_gate, w_exp_up, w_exp_down, final_norm_gain):
    bsz, seq, d = x.shape
    assert w_ada.shape[0] == 1, "the final norm is fused into the single layer's last kernel"
    l = 0
    tiles = dict(rwkv_ts=min(256, seq), tm_in=min(1024, seq), tm_out=min(256, seq))
    mod = _ada_mod(c, w_ada[l], b_ada[l])
    x1, y2, g2 = _layer(
        x, mod, norm1_gain[l], w_in[l], shift_mu[l], w0[l], w_decay_up[l], a0[l],
        w_iclr_up[l], w_gate_up[l], k_k[l], k_a[l], r_k[l], ln_x_gain[l], ln_x_bias[l],
        sb_norm_gain[l], w_out[l], norm2_gain[l], w_router_group[l], b_router_group[l],
        w_router_expert[l], b_router_expert[l], w_exp_gate[l], w_exp_up[l],
        w_exp_down[l], **tiles)
    out = _final_norm(x1, y2, g2, final_norm_gain, seq, min(512, seq))
    return out.reshape(bsz, seq, d)
```

```python
import functools

import jax
import jax.numpy as jnp
import numpy as np
from jax import lax
from jax.experimental import pallas as pl
from jax.experimental.pallas import tpu as pltpu

F32 = jnp.float32
BF16 = jnp.bfloat16

RMS_EPS = 1e-6
GN_EPS = 64e-5
L2_EPS = 1e-12

HEAD_DIM = 64
PAIR = 2 * HEAD_DIM
RWKV_CHUNK = 64
SB_BLOCK = 128
N_GROUPS = 8
EXPERTS_PER_GROUP = 8
N_EXPERTS = N_GROUPS * EXPERTS_PER_GROUP
TOP_K_IN_GROUP = 2
MOE_BLOCK = 128
DECAY_LORA = 64
ICLR_LORA = 64
GATE_LORA = 160
LANE = 128
MXU_TILE = 256
VMEM_LIMIT = 48 * 1024 * 1024
SB_UNDERFLOW_LOG = -104.0


def _dot(a, b):
    return lax.dot_general(a, b, (((1,), (0,)), ((), ())), preferred_element_type=F32)


def _dot_nt(a, b):
    return lax.dot_general(a, b, (((1,), (1,)), ((), ())), preferred_element_type=F32)


def _split2(x):
    hi = x.astype(BF16)
    lo = (x - hi.astype(F32)).astype(BF16)
    return hi, lo


def _mm(a, b, passes, nt=False):
    d = _dot_nt if nt else _dot
    if passes == 1:
        return d(a.astype(BF16), b.astype(BF16))
    ah, al = _split2(a)
    bh, bl = _split2(b)
    return d(ah, bh) + d(ah, bl) + d(al, bh)


def _mm_exact_rhs(a, b_exact):
    hi, lo = _split2(a)
    return _dot(hi, b_exact) + _dot(lo, b_exact)


def _mm_exact_lhs(a_exact, b):
    hi, lo = _split2(b)
    return _dot(a_exact, hi) + _dot(a_exact, lo)


def _group_sum(x, e_blk, split=True):
    w = e_blk.shape[0]
    one_pass = lambda a, b: _dot(a.astype(BF16), b)
    mm = _mm_exact_rhs if split else one_pass
    return jnp.concatenate([mm(x[:, g * w:(g + 1) * w], e_blk)
                            for g in range(x.shape[1] // w)], axis=1)


def _sigmoid(x):
    return 1.0 / (1.0 + jnp.exp(-x))


def _params(sem, vmem=VMEM_LIMIT):
    return pltpu.CompilerParams(dimension_semantics=sem, vmem_limit_bytes=vmem)


def _ada_kernel(c_ref, w_ref, b_ref, o_ref):
    c = c_ref[...]
    s = c * _sigmoid(c)
    o_ref[...] = _mm(s, w_ref[...], 3) + b_ref[...]


def _ada_mod(c, w, b):
    bsz, d = c.shape
    n = w.shape[1]
    rows = 8
    cp = jnp.zeros((rows, d), F32).at[:bsz].set(c)
    tn = 1024
    out = pl.pallas_call(
        _ada_kernel,
        out_shape=jax.ShapeDtypeStruct((rows, n), F32),
        grid=(n // tn,),
        in_specs=[pl.BlockSpec((rows, d), lambda j: (0, 0)),
                  pl.BlockSpec((d, tn), lambda j: (0, j)),
                  pl.BlockSpec((1, tn), lambda j: (0, j))],
        out_specs=pl.BlockSpec((rows, tn), lambda j: (0, j)),
        compiler_params=_params(("arbitrary",)),
        name="ada_mod",
    )(cp, w, b.reshape(1, n))
    return out[:bsz]


_NORM_SLAB = 128


def _normmod_mm_kernel(x_ref, gain_ref, sc_ref, sh_ref, w_ref, o_ref, h_scr):
    @pl.when(pl.program_id(1) == 0)
    def _():
        scale = gain_ref[...] * (1.0 + sc_ref[0])
        shift = sh_ref[0]
        slab = min(_NORM_SLAB, x_ref.shape[0])

        def norm_rows(k, c):
            rows = pl.ds(pl.multiple_of(k * slab, slab), slab)
            xf = x_ref[rows, :]
            ms = jnp.mean(xf * xf, axis=-1, keepdims=True)
            h_scr[rows, :] = (xf * lax.rsqrt(ms + RMS_EPS) * scale + shift).astype(BF16)
            return c
        lax.fori_loop(0, x_ref.shape[0] // slab, norm_rows, 0, unroll=2)

    o_ref[...] = _dot(h_scr[...], w_ref[pl.program_id(1)]).astype(o_ref.dtype)


def _normmod_matmul(x2, gain, sc, sh, w, out_dtype, seq, tm, tn):
    t, d = x2.shape
    n = w.shape[1]
    per_b = seq // tm
    w_tiles = w.reshape(d, n // tn, tn).transpose(1, 0, 2)
    return pl.pallas_call(
        _normmod_mm_kernel,
        out_shape=jax.ShapeDtypeStruct((t, n), out_dtype),
        grid=(t // tm, n // tn),
        in_specs=[pl.BlockSpec((tm, d), lambda i, j: (i, 0)),
                  pl.BlockSpec((1, d), lambda i, j: (0, 0)),
                  pl.BlockSpec((1, 1, d), lambda i, j: (i // per_b, 0, 0)),
                  pl.BlockSpec((1, 1, d), lambda i, j: (i // per_b, 0, 0)),
                  pl.BlockSpec(w_tiles.shape, lambda i, j: (0, 0, 0),
                               pipeline_mode=pl.Buffered(1))],
        out_specs=pl.BlockSpec((tm, tn), lambda i, j: (i, j)),
        scratch_shapes=[pltpu.VMEM((tm, d), BF16)],
        compiler_params=_params(("parallel", "arbitrary")),
        name="normmod_proj",
    )(x2, gain.reshape(1, d), sc, sh, w_tiles)


_V_W0, _V_A0, _V_KK, _V_KA, _V_RK, _V_LNG, _V_LNB = range(7)
_M_STRICT, _M_INCL, _M_BD8, _M_OFF8, _M_OFF16, _M_OFF32, _M_EYE = range(7)

RWKV_PASSES = 1
RWKV_GROUP = 2


def _rwkv_masks():
    i = np.arange(PAIR)[:, None]
    j = np.arange(PAIR)[None, :]
    strict = (j < i)
    incl = (j <= i)
    bd8 = (i // 8 == j // 8)
    def off(b):
        return (i // (2 * b) == j // (2 * b)) & (i // b > j // b)
    eye = (i == j)
    m = np.stack([strict, incl, bd8 & strict, off(8), off(16), off(32), eye]).astype(np.float32)
    return jnp.asarray(m)


def _tri_inverse(lmats, masks_ref, passes):
    eye = masks_ref[_M_EYE]
    bd8 = masks_ref[_M_BD8]
    mm = lambda a, b: _mm(a, b, passes)
    dblk = [l * bd8 for l in lmats]
    x = [eye + d for d in dblk]
    p = [mm(d, d) for d in dblk]
    x = [xi + mm(pi, xi) for xi, pi in zip(x, p)]
    p = [mm(pi, pi) for pi in p]
    x = [xi + mm(pi, xi) for xi, pi in zip(x, p)]
    for plane in (_M_OFF8, _M_OFF16, _M_OFF32):
        mask = masks_ref[plane]
        t = [mm(xi, l * mask) for xi, l in zip(x, lmats)]
        x = [xi + mm(ti, xi) for xi, ti in zip(x, t)]
    return x


def _rwkv_kernel(p_ref, mu_ref, vec_ref, wd_ref, wa_ref, wg_ref, e_ref, tri_ref,
                 masks_ref, o_ref,
                 rt_s, at_s, bt_s, kt_s, v_s, bd_s, kd_s, gc_s, y_s, g_s, bon_s,
                 state_s, carry_s, *, ts, n_pairs, passes):
    s_idx = pl.program_id(1)

    @pl.when(s_idx == 0)
    def _():
        state_s[...] = jnp.zeros_like(state_s)
        carry_s[...] = jnp.zeros_like(carry_s)

    width = n_pairs * PAIR
    pt = p_ref[0]
    row = lax.broadcasted_iota(jnp.int32, pt.shape, 0)
    prev = jnp.where(row == 0, carry_s[...], pltpu.roll(pt, 1, 0))
    carry_s[...] = pt[ts - 1:ts, :]
    pm = pt + (prev - pt) * mu_ref[...]

    r = pm[:, 0:width]
    k = pm[:, width:2 * width]
    v = pm[:, 2 * width:3 * width]
    o3 = 3 * width
    xw = pm[:, o3:o3 + LANE]
    xa = pm[:, o3 + LANE:o3 + 2 * LANE]
    xg = pm[:, o3 + 2 * LANE:o3 + 4 * LANE]

    vec = lambda i: vec_ref[i:i + 1, :]
    e_mat = e_ref[...]
    logw = -float(np.exp(-0.5)) * _sigmoid(vec(_V_W0) + _mm(jnp.tanh(xw), wd_ref[...], 3))
    a = _sigmoid(vec(_V_A0) + _mm(xa, wa_ref[...], 1))
    g_s[...] = _mm(_sigmoid(xg), wg_ref[...], 1)
    kk = k * vec(_V_KK)
    ss = _group_sum(kk * kk, e_mat)
    kk = kk * lax.rsqrt(jnp.maximum(ss, L2_EPS * L2_EPS))
    kp = k * (1.0 + (a - 1.0) * vec(_V_KA))
    ib = kk * a
    bon_s[...] = _group_sum(r * kp * vec(_V_RK), e_mat, split=False) * v
    cum = _mm_exact_lhs(tri_ref[...], logw)
    tot = jnp.concatenate(
        [jnp.broadcast_to(cum[c * RWKV_CHUNK + RWKV_CHUNK - 1:(c + 1) * RWKV_CHUNK, :],
                          (RWKV_CHUNK, width)) for c in range(ts // RWKV_CHUNK)], axis=0)
    op_dtype = rt_s.dtype
    rt_s[...] = (r * jnp.exp(cum)).astype(op_dtype)
    at_s[...] = (-kk * jnp.exp(cum - logw)).astype(op_dtype)
    dec_in = jnp.exp(-cum)
    bt_s[...] = (ib * dec_in).astype(op_dtype)
    kt_s[...] = (kp * dec_in).astype(op_dtype)
    v_s[...] = v
    dec_out = jnp.exp(tot - cum)
    bd_s[...] = (ib * dec_out).astype(op_dtype)
    kd_s[...] = (kp * dec_out).astype(op_dtype)
    gc_s[...] = jnp.exp(tot)

    lane = lax.broadcasted_iota(jnp.int32, (RWKV_CHUNK, PAIR), 1)
    in_h0 = lane < HEAD_DIM
    strict = masks_ref[_M_STRICT]
    incl = masks_ref[_M_INCL]
    operand = lambda x: x.astype(op_dtype)

    pairs = range(n_pairs)
    lanes = [slice(j * PAIR, (j + 1) * PAIR) for j in pairs]
    mm = lambda x, w: _mm(x, w, passes)
    mm_nt = lambda x, w: _mm(x, w, passes, nt=True)

    def chunk_group(g, carry):
        row0 = [pl.multiple_of((g * RWKV_GROUP + c) * RWKV_CHUNK, RWKV_CHUNK)
                for c in range(RWKV_GROUP)]
        items = [(c, j) for c in range(RWKV_GROUP) for j in pairs]

        def stacked(ref):
            blks = [ref[pl.ds(row0[c], RWKV_CHUNK), lanes[j]] for c, j in items]
            zero = jnp.zeros((), ref.dtype)
            return [jnp.concatenate([jnp.where(in_h0, b, zero), jnp.where(in_h0, zero, b)],
                                    axis=0) for b in blks]

        rh, ah, bh, kh = stacked(rt_s), stacked(at_s), stacked(bt_s), stacked(kt_s)
        vh, bdh, kdh = stacked(v_s), stacked(bd_s), stacked(kd_s)
        vo = [operand(x) for x in vh]
        bk = [jnp.concatenate([b, k], axis=0) for b, k in zip(bh, kh)]
        ga = [mm_nt(a, b) for a, b in zip(ah, bk)]
        gr = [mm_nt(r_, b) for r_, b in zip(rh, bk)]
        tinv = _tri_inverse([g_[:, :PAIR] * strict for g_ in ga], masks_ref, passes)
        for c in range(RWKV_GROUP):
            idx = [c * n_pairs + j for j in pairs]
            st = [state_s[j] for j in pairs]
            so = [operand(x) for x in st]
            x0 = [mm_nt(ah[i], so[j]) + mm(ga[i][:, PAIR:] * strict, vo[i])
                  for i, j in zip(idx, pairs)]
            u = [mm(tinv[i], x0[j]) for i, j in zip(idx, pairs)]
            y = [mm_nt(rh[i], so[j]) + mm(gr[i][:, :PAIR] * incl, u[j])
                 + mm(gr[i][:, PAIR:] * incl, vo[i]) for i, j in zip(idx, pairs)]
            for i, j in zip(idx, pairs):
                zt = jnp.concatenate([u[j], vh[i]], axis=0).T
                bkd = jnp.concatenate([bdh[i], kdh[i]], axis=0)
                gc = gc_s[pl.ds(row0[c], 1), lanes[j]]
                state_s[j] = st[j] * gc + mm(zt, bkd)
                y_s[pl.ds(row0[c], RWKV_CHUNK), lanes[j]] = y[j][:RWKV_CHUNK] + y[j][RWKV_CHUNK:]
        return carry

    lax.fori_loop(0, ts // (RWKV_CHUNK * RWKV_GROUP), chunk_group, 0)

    y = y_s[...]
    inv_n = 1.0 / HEAD_DIM
    mean = _group_sum(y, e_mat, split=False) * inv_n
    dlt = y - mean
    var = _group_sum(dlt * dlt, e_mat, split=False) * inv_n
    yn = dlt * lax.rsqrt(var + GN_EPS) * vec(_V_LNG) + vec(_V_LNB)
    o_ref[0] = ((yn + bon_s[...]) * g_s[...]).astype(o_ref.dtype)


def _rwkv_time_mix(p, mu, vecs, wd, wa, wg, *, ts, passes=RWKV_PASSES):
    bsz, seq, cols = p.shape
    width = vecs.shape[1]
    n_pairs = width // PAIR
    heads = np.arange(MXU_TILE) // HEAD_DIM
    e_mat = jnp.asarray((heads[:, None] == heads[None, :]).astype(np.float32), dtype=BF16)
    tok = np.arange(ts)
    same = (tok[:, None] // RWKV_CHUNK) == (tok[None, :] // RWKV_CHUNK)
    tri = jnp.asarray((same & (tok[None, :] <= tok[:, None])).astype(np.float32), dtype=BF16)
    masks = _rwkv_masks()
    full = lambda shape: pl.BlockSpec(shape, lambda b, s: (0,) * len(shape))
    big = lambda: pltpu.VMEM((ts, width), F32)
    op = lambda: pltpu.VMEM((ts, width), BF16 if passes == 1 else F32)
    kern = functools.partial(_rwkv_kernel, ts=ts, n_pairs=n_pairs, passes=passes)
    return pl.pallas_call(
        kern,
        out_shape=jax.ShapeDtypeStruct((bsz, seq, width), BF16),
        grid=(bsz, seq // ts),
        in_specs=[pl.BlockSpec((1, ts, cols), lambda b, s: (b, s, 0)),
                  full((1, cols)), full(vecs.shape), full(wd.shape), full(wa.shape),
                  full(wg.shape), full(e_mat.shape), full(tri.shape), full(masks.shape)],
        out_specs=pl.BlockSpec((1, ts, width), lambda b, s: (b, s, 0)),
        scratch_shapes=[op(), op(), op(), op(), big(), op(), op(), big(), big(), big(), big(),
                        pltpu.VMEM((n_pairs, PAIR, PAIR), F32), pltpu.VMEM((1, cols), F32)],
        compiler_params=_params(("parallel", "arbitrary")),
        name="rwkv7_scan",
    )(p, mu, vecs, wd, wa, wg, e_mat, tri, masks)


SB_WINDOW = 6
SB_TAIL = 2
SB_Q_PER_STEP = 8
SB_PAIRS_TOGETHER = 2


def _sb_kernel(q_ref, k_ref, v_ref, gain_ref, cum_ref, e2_ref, o_ref, *, q_per_step):
    lane = lax.broadcasted_iota(jnp.int32, (SB_BLOCK, PAIR), 1)
    h0 = lane < HEAD_DIM
    zero = jnp.zeros((), BF16)
    qi = lax.broadcasted_iota(jnp.int32, (SB_BLOCK, 2 * SB_BLOCK), 0)
    ki = lax.broadcasted_iota(jnp.int32, (SB_BLOCK, 2 * SB_BLOCK), 1) % SB_BLOCK
    causal = ki < qi
    cum_mat = cum_ref[...]
    scale = jnp.asarray(HEAD_DIM ** -0.5, BF16)
    width2 = 2 * SB_BLOCK

    heads = [slice(h * SB_BLOCK, (h + 1) * SB_BLOCK) for h in range(2)]

    def load_blocks(kb_top, n):
        khats, vhats, valid = [], [], []
        for j in range(n):
            kb = kb_top - j
            valid.append(kb >= 0)
            rows = pl.ds(pl.multiple_of(jnp.maximum(kb, 0) * SB_BLOCK, SB_BLOCK), SB_BLOCK)
            kblk = k_ref[0, rows, :]
            vblk = v_ref[0, rows, :]
            khats.append(jnp.concatenate([jnp.where(h0, kblk, zero), jnp.where(h0, zero, kblk)],
                                         axis=0))
            vhats.append(jnp.concatenate([jnp.where(h0, vblk, zero), jnp.where(h0, zero, vblk)],
                                         axis=0))
        return khats, vhats, valid

    def score_stage(zs, keeps):
        log_beta = [jnp.minimum(z, 0.0) - jnp.log(1.0 + jnp.exp(-jnp.abs(z))) for z in zs]
        log_1m = [lb - z for lb, z in zip(log_beta, zs)]
        log_1m = [l if m is None else jnp.where(m, l, 0.0) for l, m in zip(log_1m, keeps)]
        his = [l.astype(BF16) for l in log_1m]
        parts = [[_dot(hi[:, sl], cum_mat) for sl in heads] for hi in his]
        return log_beta, parts

    def weight_stage(log_beta, parts, keep, rest):
        between = jnp.concatenate([parts[0][:, :SB_BLOCK], parts[1][:, :SB_BLOCK]], axis=1)
        total = jnp.concatenate([parts[0][:, SB_BLOCK:], parts[1][:, SB_BLOCK:]], axis=1)
        w = jnp.exp(log_beta + between + rest)
        if keep is not None:
            w = jnp.where(keep, w, 0.0)
        return w.astype(BF16), rest + total

    def valid_mask(flag):
        return jnp.broadcast_to(flag, causal.shape)

    def sweep(q, kb_top, n, acc, rest):
        khats, vhats, valid = load_blocks(kb_top, n)
        z_all = _dot_nt(q, jnp.concatenate(khats, axis=0))
        keeps = [None] + [valid_mask(valid[j]) for j in range(1, n)]
        zs = [z_all[:, j * width2:(j + 1) * width2] for j in range(n)]
        log_beta, parts = score_stage(zs, keeps)
        ws = []
        for j in range(n):
            w, rest = weight_stage(log_beta[j], parts[j], keeps[j], rest)
            ws.append(w)
        acc = acc + _dot(jnp.concatenate(ws, axis=1), jnp.concatenate(vhats, axis=0))
        return acc, rest

    def window_pairs(q2s, qb_as, all_valid):
        n = SB_WINDOW
        off_diag = (lambda flag: None) if all_valid else valid_mask
        col = lambda jj: slice(jj * width2, (jj + 1) * width2)
        loaded = [load_blocks(qb_a + 1, n + 1) for qb_a in qb_as]
        z_alls = [_dot_nt(q2, jnp.concatenate(ld[0], axis=0))
                  for q2, ld in zip(q2s, loaded)]
        items = []
        for j in range(n):
            for p, (z_all, (_, _, valid)) in enumerate(zip(z_alls, loaded)):
                za = z_all[:SB_BLOCK, col(j + 1)]
                zb = z_all[SB_BLOCK:, col(j)]
                items.append((2 * p, za, causal if j == 0 else off_diag(valid[j + 1])))
                items.append((2 * p + 1, zb, causal if j == 0 else off_diag(valid[j])))
        log_beta, parts = score_stage([it[1] for it in items], [it[2] for it in items])
        n_streams = 2 * len(q2s)
        rest = [jnp.zeros((SB_BLOCK, width2), F32) for _ in range(n_streams)]
        ws = [[] for _ in range(n_streams)]
        for idx, (s, _, keep) in enumerate(items):
            w, rest[s] = weight_stage(log_beta[idx], parts[idx], keep, rest[s])
            ws[s].append(w)
        out = []
        for p, (_, vhats, _) in enumerate(loaded):
            acc_a = _dot(jnp.concatenate(ws[2 * p], axis=1), jnp.concatenate(vhats[1:], axis=0))
            acc_b = _dot(jnp.concatenate(ws[2 * p + 1], axis=1),
                         jnp.concatenate(vhats[:n], axis=0))
            out += [(acc_a, rest[2 * p]), (acc_b, rest[2 * p + 1])]
        return out

    def finish(q, qb, acc, rest, qrows):
        def cond(c):
            i, _, _, live = c
            return jnp.logical_and(i <= qb, live > SB_UNDERFLOW_LOG)

        def body(c):
            i, acc, rest, _ = c
            acc, rest = sweep(q, qb - i, SB_TAIL, acc, rest)
            return i + SB_TAIL, acc, rest, jnp.max(rest)

        _, acc, _, _ = lax.while_loop(cond, body,
                                      (jnp.int32(SB_WINDOW), acc, rest, jnp.max(rest)))
        ms = _mm_exact_rhs(acc * acc, e2_ref[...]) * (1.0 / HEAD_DIM)
        o_ref[0, qrows, :] = (acc * lax.rsqrt(ms + RMS_EPS) * gain_ref[...]).astype(o_ref.dtype)

    n_together = min(SB_PAIRS_TOGETHER, q_per_step // 2)

    def q_group(it, all_valid):
        blocks_per_group = 2 * n_together
        qb0 = pl.program_id(2) * q_per_step + blocks_per_group * it
        row0 = pl.multiple_of(blocks_per_group * it * SB_BLOCK, blocks_per_group * SB_BLOCK)
        q2s = [q_ref[0, pl.ds(row0 + 2 * p * SB_BLOCK, 2 * SB_BLOCK), :] * scale
               for p in range(n_together)]
        results = window_pairs(q2s, [qb0 + 2 * p for p in range(n_together)], all_valid)
        for b, (acc, rest) in enumerate(results):
            q = q2s[b // 2][(b % 2) * SB_BLOCK:(b % 2 + 1) * SB_BLOCK]
            finish(q, qb0 + b, acc, rest, pl.ds(row0 + b * SB_BLOCK, SB_BLOCK))

    def run(all_valid):
        def body(it, carry):
            q_group(it, all_valid)
            return carry
        lax.fori_loop(0, q_per_step // (2 * n_together), body, 0)

    first_full = -(-(SB_WINDOW - 1) // q_per_step)
    pl.when(pl.program_id(2) < first_full)(lambda: run(False))
    pl.when(pl.program_id(2) >= first_full)(lambda: run(True))


def _sb_attention(qkv, gain, width):
    bsz, seq, _ = qkv.shape
    n_pairs = width // PAIR
    nq = seq // SB_BLOCK
    i = np.arange(SB_BLOCK)
    upper = (i[:, None] > i[None, :]).astype(np.float32)
    half = np.concatenate([upper, np.ones_like(upper)], axis=1)
    cum_mat = jnp.asarray(half, dtype=BF16)
    hd = np.arange(PAIR) // HEAD_DIM
    e2 = jnp.asarray((hd[:, None] == hd[None, :]).astype(np.float32), dtype=BF16)
    qps = min(SB_Q_PER_STEP, nq)
    assert qps % 2 == 0 and nq % qps == 0, "query blocks are processed in adjacent pairs"
    qrows = qps * SB_BLOCK
    return pl.pallas_call(
        functools.partial(_sb_kernel, q_per_step=qps),
        out_shape=jax.ShapeDtypeStruct((bsz, seq, width), BF16),
        grid=(bsz, n_pairs, nq // qps),
        in_specs=[pl.BlockSpec((1, qrows, PAIR), lambda b, j, t: (b, t, j)),
                  pl.BlockSpec((1, seq, PAIR), lambda b, j, t: (b, 0, n_pairs + j)),
                  pl.BlockSpec((1, seq, PAIR), lambda b, j, t: (b, 0, 2 * n_pairs + j)),
                  pl.BlockSpec((1, PAIR), lambda b, j, t: (0, j)),
                  pl.BlockSpec(cum_mat.shape, lambda b, j, t: (0, 0)),
                  pl.BlockSpec(e2.shape, lambda b, j, t: (0, 0))],
        out_specs=pl.BlockSpec((1, qrows, PAIR), lambda b, j, t: (b, t, j)),
        compiler_params=_params(("parallel", "parallel", "arbitrary")),
        name="stickbreak_attn",
    )(qkv, qkv, qkv, gain.reshape(1, width), cum_mat, e2)


def _route_rows(lg):
    lane = lax.broadcasted_iota(jnp.int32, lg.shape, 1).astype(F32)
    neg = -jnp.inf
    first = lambda hit: jnp.min(jnp.where(hit, lane, float(LANE)), axis=-1, keepdims=True)
    gl = jnp.where(lane < N_GROUPS, lg, neg)
    gmax = jnp.max(gl, axis=-1, keepdims=True)
    g_sel = first(gl == gmax)
    p_sel = 1.0 / jnp.sum(jnp.exp(gl - gmax), axis=-1, keepdims=True)
    lo = N_GROUPS + g_sel * EXPERTS_PER_GROUP
    el = jnp.where(jnp.logical_and(lane >= lo, lane < lo + EXPERTS_PER_GROUP), lg, neg)
    v1 = jnp.max(el, axis=-1, keepdims=True)
    i1 = first(el == v1)
    el2 = jnp.where(lane == i1, neg, el)
    v2 = jnp.max(el2, axis=-1, keepdims=True)
    i2 = first(el2 == v2)
    t2 = jnp.exp(v2 - v1)
    w1 = p_sel / (1.0 + t2)
    return i1 - N_GROUPS, i2 - N_GROUPS, w1, w1 * t2


def _outproj_kernel(x_ref, ya_ref, yb_ref, wa_ref, wb_ref, g1_ref, gain_ref, sc_ref, sh_ref,
                    wr_ref, br_ref, x1_ref, h_ref, rt_ref):
    mix = _dot(ya_ref[...], wa_ref[...]) + _dot(yb_ref[...], wb_ref[...])
    x1 = x_ref[...] + g1_ref[0] * mix
    x1_ref[...] = x1
    ms = jnp.mean(x1 * x1, axis=-1, keepdims=True)
    h = x1 * lax.rsqrt(ms + RMS_EPS) * gain_ref[...] * (1.0 + sc_ref[0]) + sh_ref[0]
    h_ref[...] = h.astype(h_ref.dtype)
    h_hi, h_lo = _split2(h)
    w_hl = wr_ref[...]
    nr = w_hl.shape[1] // 2
    both = _dot(h_hi, w_hl)
    logits = both[:, :nr] + both[:, nr:] + _dot(h_lo, w_hl[:, :nr]) + br_ref[...]
    e1, e2, w1, w2 = _route_rows(logits)
    lane = lax.broadcasted_iota(jnp.int32, rt_ref.shape, 1)
    rt_ref[...] = jnp.where(lane == 0, e1, jnp.where(lane == 1, e2,
                                                      jnp.where(lane == 2, w1, w2)))


def _outproj_norm_router(x2, ya, yb, w_a, w_b, g1, gain, sc, sh, w_r, b_r, seq, tm):
    t, d = x2.shape
    half = ya.shape[1]
    nr = b_r.shape[1]
    w_hi, w_lo = _split2(w_r)
    w_r = jnp.concatenate([w_hi, w_lo], axis=1)
    per_b = seq // tm
    row = lambda w: pl.BlockSpec((tm, w), lambda i: (i, 0))
    full = lambda shape: pl.BlockSpec(shape, lambda i: (0,) * len(shape))
    per_batch = pl.BlockSpec((1, 1, d), lambda i: (i // per_b, 0, 0))
    return pl.pallas_call(
        _outproj_kernel,
        out_shape=(jax.ShapeDtypeStruct((t, d), F32), jax.ShapeDtypeStruct((t, d), F32),
                   jax.ShapeDtypeStruct((t, nr), F32)),
        grid=(t // tm,),
        in_specs=[row(d), row(half), row(half), full((half, d)), full((half, d)), per_batch,
                  full((1, d)), per_batch, per_batch, full((d, 2 * nr)), full((1, nr))],
        out_specs=(row(d), row(d), row(nr)),
        compiler_params=_params(("parallel",)),
        name="outproj_norm_router",
    )(x2, ya, yb, w_a, w_b, g1, gain.reshape(1, d), sc, sh, w_r, b_r)


MOE_ROWS = 256
MOE_VMEM_LIMIT = 58 * 1024 * 1024
_DMA_UNROLL = 8
_CAST_STEPS = 8
_GATHER_SLOTS = 3


def _moe_kernel(be_ref, nxt_ref, nused_ref, gcur_ref, gnext_ref, gahead_ref, scur_ref, sprev_ref,
                wt_ref, h_hbm, wg_hbm, wu_hbm, wd_hbm, y_hbm,
                xbuf, ybuf, wg_f, wu_f, wd_f, wg_b, wu_b, wd_b, gsem, ssem, wsem):
    i = pl.program_id(0)
    n_used = nused_ref[0]
    slot = i % 2
    gslot = i % _GATHER_SLOTS
    ahead = (i + 2) % _GATHER_SLOTS
    de = wg_b.shape[1]
    d = wd_b.shape[1]

    def weight_copies(e):
        pairs = ((wg_hbm, wg_f), (wu_hbm, wu_f), (wd_hbm, wd_f))
        return [pltpu.make_async_copy(src.at[e], dst, wsem.at[k])
                for k, (src, dst) in enumerate(pairs)]

    def gather(idx_ref, r, s):
        return pltpu.make_async_copy(h_hbm.at[pl.ds(idx_ref[0, 0, r], 1), :],
                                     xbuf.at[s, pl.ds(r, 1), :], gsem.at[s])

    def scatter(idx_ref, r, s):
        return pltpu.make_async_copy(ybuf.at[s, pl.ds(r, 1), :],
                                     y_hbm.at[pl.ds(idx_ref[0, 0, r], 1), :], ssem.at[s])

    def for_rows(fn):
        def body(r, c):
            fn(r)
            return c
        lax.fori_loop(0, MOE_ROWS, body, 0, unroll=_DMA_UNROLL)

    def gather_wait(s):
        pltpu.make_async_copy(h_hbm.at[pl.ds(0, MOE_ROWS), :], xbuf.at[s], gsem.at[s]).wait()

    def scatter_wait(s):
        pltpu.make_async_copy(ybuf.at[s], y_hbm.at[pl.ds(0, MOE_ROWS), :], ssem.at[s]).wait()

    @pl.when(jnp.logical_and(i == 0, n_used > 0))
    def _():
        for_rows(lambda r: gather(gcur_ref, r, 0).start())
        for_rows(lambda r: gather(gnext_ref, r, 1).start())
        for c in weight_copies(be_ref[0]):
            c.start()
        ybuf[1] = jnp.zeros(ybuf.shape[1:], ybuf.dtype)

    @pl.when(i < n_used)
    def _():
        e = be_ref[i]

        @pl.when(jnp.logical_or(i == 0, e != be_ref[jnp.maximum(i - 1, 0)]))
        def _():
            for c in weight_copies(e):
                c.wait()

            def cast_rows(k, c):
                for src, dst in ((wg_f, wg_b), (wu_f, wu_b), (wd_f, wd_b)):
                    nrow = src.shape[0] // _CAST_STEPS
                    rows = pl.ds(pl.multiple_of(k * nrow, nrow), nrow)
                    dst[rows, :] = src[rows, :].astype(BF16)
                return c
            lax.fori_loop(0, _CAST_STEPS, cast_rows, 0)

            @pl.when(nxt_ref[i] != e)
            def _():
                for c in weight_copies(nxt_ref[i]):
                    c.start(priority=1)

        gather_wait(gslot)

        def sliced_issue(n_slices, make_copy, priorities):
            per = MOE_ROWS // n_slices
            state = [0]

            def issue():
                for r in range(state[0], state[0] + per):
                    make_copy(r).start(priority=r % priorities)
                state[0] += per
            return issue

        n_up = de // MXU_TILE
        n_down = d // MXU_TILE
        issue_gather = sliced_issue(2 * n_up, lambda r: gather(gahead_ref, r, ahead), 1)
        issue_scatter = sliced_issue(n_down, lambda r: scatter(sprev_ref, r, 1 - slot), 2)

        xb = xbuf[gslot].astype(BF16)
        hg, hu = [], []
        for n in range(n_up):
            cols = slice(n * MXU_TILE, (n + 1) * MXU_TILE)
            hg.append(_dot(xb, wg_b[:, cols]))
            issue_gather()
            hu.append(_dot(xb, wu_b[:, cols]))
            issue_gather()
        hg = jnp.concatenate(hg, axis=1)
        hid = (hg * _sigmoid(hg) * jnp.concatenate(hu, axis=1)).astype(BF16)

        @pl.when(i >= 1)
        def _():
            scatter_wait(slot)

        wt = wt_ref[...]
        for n in range(n_down):
            cols = slice(n * MXU_TILE, (n + 1) * MXU_TILE)
            ybuf[slot, :, cols] = _dot(hid, wd_b[:, cols]) * wt
            issue_scatter()

        @pl.when(i + 1 >= n_used)
        def _():
            gather_wait((i + 1) % _GATHER_SLOTS)
            gather_wait(ahead)
            scatter_wait(1 - slot)
            for_rows(lambda r: scatter(scur_ref, r, slot).start())
            scatter_wait(slot)


def _moe_experts(h2, gidx, sidx, row_w, block_e, next_e, n_used, w_gate, w_up, w_down):
    t, d = h2.shape
    de = w_gate.shape[2]
    n_blocks = gidx.shape[0]
    assert MOE_ROWS % (2 * (de // MXU_TILE)) == 0 == MOE_ROWS % (d // MXU_TILE)
    assert d % _CAST_STEPS == 0 == de % _CAST_STEPS and sidx.shape[0] == n_blocks + 1
    idx_spec = lambda f: pl.BlockSpec((1, 1, MOE_ROWS), f, memory_space=pltpu.SMEM)
    hbm = pl.BlockSpec(memory_space=pl.ANY)
    grid_spec = pltpu.PrefetchScalarGridSpec(
        num_scalar_prefetch=3,
        grid=(n_blocks,),
        in_specs=[idx_spec(lambda i, *_: (i, 0, 0)),
                  idx_spec(lambda i, *_: (jnp.minimum(i + 1, n_blocks - 1), 0, 0)),
                  idx_spec(lambda i, *_: (jnp.minimum(i + 2, n_blocks - 1), 0, 0)),
                  idx_spec(lambda i, *_: (i, 0, 0)),
                  idx_spec(lambda i, *_: (jnp.where(i == 0, n_blocks, i - 1), 0, 0)),
                  pl.BlockSpec((MOE_ROWS, 1), lambda i, *_: (i, 0)),
                  hbm, hbm, hbm, hbm],
        out_specs=hbm,
        scratch_shapes=[pltpu.VMEM((_GATHER_SLOTS, MOE_ROWS, d), F32),
                        pltpu.VMEM((2, MOE_ROWS, d), F32),
                        pltpu.VMEM((d, de), F32), pltpu.VMEM((d, de), F32),
                        pltpu.VMEM((de, d), F32),
                        pltpu.VMEM((d, de), BF16), pltpu.VMEM((d, de), BF16),
                        pltpu.VMEM((de, d), BF16),
                        pltpu.SemaphoreType.DMA((_GATHER_SLOTS,)), pltpu.SemaphoreType.DMA((2,)),
                        pltpu.SemaphoreType.DMA((3,))],
    )
    return pl.pallas_call(
        _moe_kernel,
        out_shape=jax.ShapeDtypeStruct((TOP_K_IN_GROUP * t + MOE_ROWS, d), F32),
        grid_spec=grid_spec,
        compiler_params=_params(("arbitrary",), MOE_VMEM_LIMIT),
        name="moe_experts",
    )(block_e, next_e, n_used, gidx, gidx, gidx, sidx, sidx, row_w.reshape(-1, 1),
      h2, w_gate, w_up, w_down)


def _final_kernel(x1_ref, ya_ref, yb_ref, g2_ref, gain_ref, o_ref):
    x2 = x1_ref[...] + g2_ref[0] * (ya_ref[...] + yb_ref[...])
    ms = jnp.mean(x2 * x2, axis=-1, keepdims=True)
    o_ref[...] = x2 * lax.rsqrt(ms + RMS_EPS) * gain_ref[...]


def _final_norm(x1, y2, g2, gain, seq, tm):
    t, d = x1.shape
    per_b = seq // tm
    nt = t // tm
    row = pl.BlockSpec((tm, d), lambda i: (i, 0))
    return pl.pallas_call(
        _final_kernel,
        out_shape=jax.ShapeDtypeStruct((t, d), F32),
        grid=(nt,),
        in_specs=[row, row, pl.BlockSpec((tm, d), lambda i: (i + nt, 0)),
                  pl.BlockSpec((1, 1, d), lambda i: (i // per_b, 0, 0)),
                  pl.BlockSpec((1, d), lambda i: (0, 0))],
        out_specs=row,
        compiler_params=_params(("parallel",)),
        name="final_norm",
    )(x1, y2, y2, g2, gain.reshape(1, d))


def _route(routed):
    t = routed.shape[0]
    pair_w = routed[:, TOP_K_IN_GROUP:2 * TOP_K_IN_GROUP]
    flat_e = routed[:, :TOP_K_IN_GROUP].astype(jnp.int32).reshape(-1)
    flat_w = pair_w.reshape(-1)
    m = flat_e.shape[0]
    order = jnp.argsort(flat_e).astype(jnp.int32)
    experts = jnp.arange(N_EXPERTS, dtype=jnp.int32)
    counts = jnp.sum((flat_e[:, None] == experts[None, :]).astype(jnp.int32), axis=0)
    starts = jnp.cumsum(counts) - counts
    padded = (counts + MOE_ROWS - 1) // MOE_ROWS * MOE_ROWS
    pends = jnp.cumsum(padded)
    pstarts = pends - padded
    n_blocks = (m + N_EXPERTS * (MOE_ROWS - 1) + MOE_ROWS - 1) // MOE_ROWS
    block_start = jnp.arange(n_blocks, dtype=jnp.int32) * MOE_ROWS
    block_e = jnp.minimum(jnp.sum((block_start[:, None] >= pends[None, :]).astype(jnp.int32),
                                  axis=1), N_EXPERTS - 1)
    blk = jnp.arange(n_blocks, dtype=jnp.int32)[:, None]
    rin = jnp.arange(MOE_ROWS, dtype=jnp.int32)[None, :]
    pstart_b, count_b, start_b = lax.optimization_barrier(
        (pstarts[block_e], counts[block_e], starts[block_e]))
    off = blk * MOE_ROWS + rin - pstart_b[:, None]
    valid = off < count_b[:, None]
    src = jnp.clip(start_b[:, None] + off, 0, m - 1)
    assign = order[src]
    tok = assign // TOP_K_IN_GROUP
    gidx = jnp.where(valid, tok, 0)
    spare = TOP_K_IN_GROUP * t + rin
    sidx = jnp.where(valid, (assign % TOP_K_IN_GROUP) * t + tok, spare)
    sidx = jnp.concatenate([sidx, spare], axis=0)
    row_w = jnp.where(valid, flat_w[assign], 0.0)
    n_used = (pends[-1] // MOE_ROWS).astype(jnp.int32).reshape(1)
    later = (experts[None, :] > experts[:, None]) & (counts[None, :] > 0)
    next_expert = jnp.min(jnp.where(later, experts[None, :], N_EXPERTS), axis=1)
    next_expert = jnp.where(next_expert == N_EXPERTS, experts, next_expert)
    return (gidx.reshape(n_blocks, 1, MOE_ROWS), sidx.reshape(n_blocks + 1, 1, MOE_ROWS), row_w,
            block_e, next_expert[block_e], n_used)


def _pad_cols(w, n):
    return jnp.pad(w, ((0, 0), (0, n - w.shape[1])))


def _pad_rows(w, n):
    return jnp.pad(w, ((0, n - w.shape[0]), (0, 0)))


def _layer(x, mod, norm1_gain, w_in, shift_mu, w0, w_decay_up, a0, w_iclr_up, w_gate_up,
           k_k, k_a, r_k, ln_x_gain, ln_x_bias, sb_norm_gain, w_out, norm2_gain,
           w_router_group, b_router_group, w_router_expert, b_router_expert,
           w_exp_gate, w_exp_up, w_exp_down, *, rwkv_ts, tm_in, tm_out):
    bsz, seq, d = x.shape
    t = bsz * seq
    rw = w0.shape[0]
    sbw = sb_norm_gain.shape[0]
    sh1, sc1, g1, sh2, sc2, g2 = [m.reshape(bsz, 1, d) for m in jnp.split(mod, 6, axis=-1)]

    o = 3 * rw
    seg = lambda a, b: w_in[:, a:b]
    w_rwkv = jnp.concatenate([
        seg(0, o),
        _pad_cols(seg(o, o + DECAY_LORA), LANE),
        _pad_cols(seg(o + DECAY_LORA, o + DECAY_LORA + ICLR_LORA), LANE),
        _pad_cols(seg(o + DECAY_LORA + ICLR_LORA, o + DECAY_LORA + ICLR_LORA + GATE_LORA),
                  2 * LANE)], axis=1).astype(BF16)
    rcols = o + DECAY_LORA + ICLR_LORA + GATE_LORA
    w_sb = w_in[:, rcols:].astype(BF16)
    mseg = lambda a, b: shift_mu[a:b][None, :]
    mu = jnp.concatenate([
        mseg(0, o),
        _pad_cols(mseg(o, o + DECAY_LORA), LANE),
        _pad_cols(mseg(o + DECAY_LORA, o + DECAY_LORA + ICLR_LORA), LANE),
        _pad_cols(mseg(o + DECAY_LORA + ICLR_LORA, rcols), 2 * LANE)], axis=1)

    x2 = x.reshape(t, d)
    p_rwkv = _normmod_matmul(x2, norm1_gain, sc1, sh1, w_rwkv, F32, seq, tm_in, 512)
    qkv = _normmod_matmul(x2, norm1_gain, sc1, sh1, w_sb, BF16, seq, tm_in, 512)

    vecs = jnp.stack([w0, a0, k_k, k_a, r_k.reshape(-1), ln_x_gain, ln_x_bias,
                      jnp.zeros_like(w0)])
    y_a = _rwkv_time_mix(p_rwkv.reshape(bsz, seq, -1), mu, vecs,
                         _pad_rows(w_decay_up, LANE), _pad_rows(w_iclr_up, LANE),
                         _pad_rows(w_gate_up, 2 * LANE), ts=rwkv_ts)
    y_b = _sb_attention(qkv.reshape(bsz, seq, -1), sb_norm_gain, sbw)

    w_r = _pad_cols(jnp.concatenate([w_router_group, w_router_expert], axis=1), LANE)
    b_r = _pad_cols(jnp.concatenate([b_router_group, b_router_expert])[None, :], LANE)
    w_o = w_out.astype(BF16)
    x1, h2, routed = _outproj_norm_router(
        x2, y_a.reshape(t, rw), y_b.reshape(t, sbw), w_o[:rw], w_o[rw:], g1, norm2_gain,
        sc2, sh2, w_r, b_r, seq, tm_out)

    gidx, sidx, row_w, block_e, next_e, n_used = _route(routed)
    y2 = _moe_experts(h2, gidx, sidx, row_w, block_e, next_e, n_used,
                      w_exp_gate, w_exp_up, w_exp_down)
    return x1, y2, g2


def kernel(x, c, w_ada, b_ada, norm1_gain, w_in, shift_mu, w0, w_decay_up, a0, w_iclr_up, w_gate_up, k_k, k_a, r_k, ln_x_gain, ln_x_bias, sb_norm_gain, w_out, norm2_gain, w_router_group, b_router_group, w_router_expert, b_router_expert, w_exp_gate, w_exp_up, w_exp_down, final_norm_gain):
    bsz, seq, d = x.shape
    assert w_ada.shape[0] == 1, "the final norm is fused into the single layer's last kernel"
    l = 0
    tiles = dict(rwkv_ts=min(256, seq), tm_in=min(1024, seq), tm_out=min(256, seq))
    mod = _ada_mod(c, w_ada[l], b_ada[l])
    x1, y2, g2 = _layer(
        x, mod, norm1_gain[l], w_in[l], shift_mu[l], w0[l], w_decay_up[l], a0[l],
        w_iclr_up[l], w_gate_up[l], k_k[l], k_a[l], r_k[l], ln_x_gain[l], ln_x_bias[l],
        sb_norm_gain[l], w_out[l], norm2_gain[l], w_router_group[l], b_router_group[l],
        w_router_expert[l], b_router_expert[l], w_exp_gate[l], w_exp_up[l],
        w_exp_down[l], **tiles)
    out = _final_norm(x1, y2, g2, final_norm_gain, seq, min(512, seq))
    return out.reshape(bsz, seq, d)
```

```python
import functools

import jax
import jax.numpy as jnp
import numpy as np
from jax import lax
from jax.experimental import pallas as pl
from jax.experimental.pallas import tpu as pltpu

F32 = jnp.float32
BF16 = jnp.bfloat16

RMS_EPS = 1e-6
GN_EPS = 64e-5
L2_EPS = 1e-12

HEAD_DIM = 64
PAIR = 2 * HEAD_DIM
RWKV_CHUNK = 64
SB_BLOCK = 128
N_GROUPS = 8
EXPERTS_PER_GROUP = 8
N_EXPERTS = N_GROUPS * EXPERTS_PER_GROUP
TOP_K_IN_GROUP = 2
MOE_BLOCK = 128
DECAY_LORA = 64
ICLR_LORA = 64
GATE_LORA = 160
LANE = 128
MXU_TILE = 256
VMEM_LIMIT = 48 * 1024 * 1024
SB_UNDERFLOW_LOG = -104.0


def _dot(a, b):
    return lax.dot_general(a, b, (((1,), (0,)), ((), ())), preferred_element_type=F32)


def _dot_nt(a, b):
    return lax.dot_general(a, b, (((1,), (1,)), ((), ())), preferred_element_type=F32)


def _split2(x):
    hi = x.astype(BF16)
    lo = (x - hi.astype(F32)).astype(BF16)
    return hi, lo


def _mm(a, b, passes, nt=False):
    d = _dot_nt if nt else _dot
    if passes == 1:
        return d(a.astype(BF16), b.astype(BF16))
    ah, al = _split2(a)
    bh, bl = _split2(b)
    return d(ah, bh) + d(ah, bl) + d(al, bh)


def _mm_exact_rhs(a, b_exact):
    hi, lo = _split2(a)
    return _dot(hi, b_exact) + _dot(lo, b_exact)


def _mm_exact_lhs(a_exact, b):
    hi, lo = _split2(b)
    return _dot(a_exact, hi) + _dot(a_exact, lo)


def _group_sum(x, e_blk, split=True):
    w = e_blk.shape[0]
    one_pass = lambda a, b: _dot(a.astype(BF16), b)
    mm = _mm_exact_rhs if split else one_pass
    return jnp.concatenate([mm(x[:, g * w:(g + 1) * w], e_blk)
                            for g in range(x.shape[1] // w)], axis=1)


def _sigmoid(x):
    return 1.0 / (1.0 + jnp.exp(-x))


def _params(sem, vmem=VMEM_LIMIT):
    return pltpu.CompilerParams(dimension_semantics=sem, vmem_limit_bytes=vmem)


def _ada_kernel(c_ref, w_ref, b_ref, o_ref):
    c = c_ref[...]
    s = c * _sigmoid(c)
    o_ref[...] = _mm(s, w_ref[...], 3) + b_ref[...]


def _ada_mod(c, w, b):
    bsz, d = c.shape
    n = w.shape[1]
    rows = 8
    cp = jnp.zeros((rows, d), F32).at[:bsz].set(c)
    tn = 1024
    out = pl.pallas_call(
        _ada_kernel,
        out_shape=jax.ShapeDtypeStruct((rows, n), F32),
        grid=(n // tn,),
        in_specs=[pl.BlockSpec((rows, d), lambda j: (0, 0)),
                  pl.BlockSpec((d, tn), lambda j: (0, j)),
                  pl.BlockSpec((1, tn), lambda j: (0, j))],
        out_specs=pl.BlockSpec((rows, tn), lambda j: (0, j)),
        compiler_params=_params(("arbitrary",)),
        name="ada_mod",
    )(cp, w, b.reshape(1, n))
    return out[:bsz]


_NORM_SLAB = 128


def _normmod_mm_kernel(x_ref, gain_ref, sc_ref, sh_ref, w_ref, o_ref, h_scr):
    @pl.when(pl.program_id(1) == 0)
    def _():
        scale = gain_ref[...] * (1.0 + sc_ref[0])
        shift = sh_ref[0]
        slab = min(_NORM_SLAB, x_ref.shape[0])

        def norm_rows(k, c):
            rows = pl.ds(pl.multiple_of(k * slab, slab), slab)
            xf = x_ref[rows, :]
            ms = jnp.mean(xf * xf, axis=-1, keepdims=True)
            h_scr[rows, :] = (xf * lax.rsqrt(ms + RMS_EPS) * scale + shift).astype(BF16)
            return c
        lax.fori_loop(0, x_ref.shape[0] // slab, norm_rows, 0, unroll=2)

    o_ref[...] = _dot(h_scr[...], w_ref[pl.program_id(1)]).astype(o_ref.dtype)


def _normmod_matmul(x2, gain, sc, sh, w, out_dtype, seq, tm, tn):
    t, d = x2.shape
    n = w.shape[1]
    per_b = seq // tm
    w_tiles = w.reshape(d, n // tn, tn).transpose(1, 0, 2)
    return pl.pallas_call(
        _normmod_mm_kernel,
        out_shape=jax.ShapeDtypeStruct((t, n), out_dtype),
        grid=(t // tm, n // tn),
        in_specs=[pl.BlockSpec((tm, d), lambda i, j: (i, 0)),
                  pl.BlockSpec((1, d), lambda i, j: (0, 0)),
                  pl.BlockSpec((1, 1, d), lambda i, j: (i // per_b, 0, 0)),
                  pl.BlockSpec((1, 1, d), lambda i, j: (i // per_b, 0, 0)),
                  pl.BlockSpec(w_tiles.shape, lambda i, j: (0, 0, 0),
                               pipeline_mode=pl.Buffered(1))],
        out_specs=pl.BlockSpec((tm, tn), lambda i, j: (i, j)),
        scratch_shapes=[pltpu.VMEM((tm, d), BF16)],
        compiler_params=_params(("parallel", "arbitrary")),
        name="normmod_proj",
    )(x2, gain.reshape(1, d), sc, sh, w_tiles)


_V_W0, _V_A0, _V_KK, _V_KA, _V_RK, _V_LNG, _V_LNB = range(7)
_M_STRICT, _M_INCL, _M_BD8, _M_OFF8, _M_OFF16, _M_OFF32, _M_EYE = range(7)

RWKV_PASSES = 1
RWKV_GROUP = 2


def _rwkv_masks():
    i = np.arange(PAIR)[:, None]
    j = np.arange(PAIR)[None, :]
    strict = (j < i)
    incl = (j <= i)
    bd8 = (i // 8 == j // 8)
    def off(b):
        return (i // (2 * b) == j // (2 * b)) & (i // b > j // b)
    eye = (i == j)
    m = np.stack([strict, incl, bd8 & strict, off(8), off(16), off(32), eye]).astype(np.float32)
    return jnp.asarray(m)


def _tri_inverse(lmats, masks_ref, passes):
    eye = masks_ref[_M_EYE]
    bd8 = masks_ref[_M_BD8]
    mm = lambda a, b: _mm(a, b, passes)
    dblk = [l * bd8 for l in lmats]
    x = [eye + d for d in dblk]
    p = [mm(d, d) for d in dblk]
    x = [xi + mm(pi, xi) for xi, pi in zip(x, p)]
    p = [mm(pi, pi) for pi in p]
    x = [xi + mm(pi, xi) for xi, pi in zip(x, p)]
    for plane in (_M_OFF8, _M_OFF16, _M_OFF32):
        mask = masks_ref[plane]
        t = [mm(xi, l * mask) for xi, l in zip(x, lmats)]
        x = [xi + mm(ti, xi) for xi, ti in zip(x, t)]
    return x


def _rwkv_kernel(p_ref, mu_ref, vec_ref, wd_ref, wa_ref, wg_ref, e_ref, tri_ref,
                 masks_ref, o_ref,
                 rt_s, at_s, bt_s, kt_s, v_s, bd_s, kd_s, gc_s, y_s, g_s, bon_s,
                 state_s, carry_s, *, ts, n_pairs, passes):
    s_idx = pl.program_id(1)

    @pl.when(s_idx == 0)
    def _():
        state_s[...] = jnp.zeros_like(state_s)
        carry_s[...] = jnp.zeros_like(carry_s)

    width = n_pairs * PAIR
    pt = p_ref[0]
    row = lax.broadcasted_iota(jnp.int32, pt.shape, 0)
    prev = jnp.where(row == 0, carry_s[...], pltpu.roll(pt, 1, 0))
    carry_s[...] = pt[ts - 1:ts, :]
    pm = pt + (prev - pt) * mu_ref[...]

    r = pm[:, 0:width]
    k = pm[:, width:2 * width]
    v = pm[:, 2 * width:3 * width]
    o3 = 3 * width
    xw = pm[:, o3:o3 + LANE]
    xa = pm[:, o3 + LANE:o3 + 2 * LANE]
    xg = pm[:, o3 + 2 * LANE:o3 + 4 * LANE]

    vec = lambda i: vec_ref[i:i + 1, :]
    e_mat = e_ref[...]
    logw = -float(np.exp(-0.5)) * _sigmoid(vec(_V_W0) + _mm(jnp.tanh(xw), wd_ref[...], 3))
    a = _sigmoid(vec(_V_A0) + _mm(xa, wa_ref[...], 1))
    g_s[...] = _mm(_sigmoid(xg), wg_ref[...], 1)
    kk = k * vec(_V_KK)
    ss = _group_sum(kk * kk, e_mat, split=False)
    kk = kk * lax.rsqrt(jnp.maximum(ss, L2_EPS * L2_EPS))
    kp = k * (1.0 + (a - 1.0) * vec(_V_KA))
    ib = kk * a
    bon_s[...] = _group_sum(r * kp * vec(_V_RK), e_mat, split=False) * v
    cum = _mm_exact_lhs(tri_ref[...], logw)
    tot = jnp.concatenate(
        [jnp.broadcast_to(cum[c * RWKV_CHUNK + RWKV_CHUNK - 1:(c + 1) * RWKV_CHUNK, :],
                          (RWKV_CHUNK, width)) for c in range(ts // RWKV_CHUNK)], axis=0)
    op_dtype = rt_s.dtype
    rt_s[...] = (r * jnp.exp(cum)).astype(op_dtype)
    at_s[...] = (-kk * jnp.exp(cum - logw)).astype(op_dtype)
    dec_in = jnp.exp(-cum)
    bt_s[...] = (ib * dec_in).astype(op_dtype)
    kt_s[...] = (kp * dec_in).astype(op_dtype)
    v_s[...] = v
    dec_out = jnp.exp(tot - cum)
    bd_s[...] = (ib * dec_out).astype(op_dtype)
    kd_s[...] = (kp * dec_out).astype(op_dtype)
    gc_s[...] = jnp.exp(tot)

    lane = lax.broadcasted_iota(jnp.int32, (RWKV_CHUNK, PAIR), 1)
    in_h0 = lane < HEAD_DIM
    strict = masks_ref[_M_STRICT]
    incl = masks_ref[_M_INCL]
    operand = lambda x: x.astype(op_dtype)

    pairs = range(n_pairs)
    lanes = [slice(j * PAIR, (j + 1) * PAIR) for j in pairs]
    mm = lambda x, w: _mm(x, w, passes)
    mm_nt = lambda x, w: _mm(x, w, passes, nt=True)

    def chunk_group(g, carry):
        row0 = [pl.multiple_of((g * RWKV_GROUP + c) * RWKV_CHUNK, RWKV_CHUNK)
                for c in range(RWKV_GROUP)]
        items = [(c, j) for c in range(RWKV_GROUP) for j in pairs]

        def stacked(ref):
            blks = [ref[pl.ds(row0[c], RWKV_CHUNK), lanes[j]] for c, j in items]
            zero = jnp.zeros((), ref.dtype)
            return [jnp.concatenate([jnp.where(in_h0, b, zero), jnp.where(in_h0, zero, b)],
                                    axis=0) for b in blks]

        rh, ah, bh, kh = stacked(rt_s), stacked(at_s), stacked(bt_s), stacked(kt_s)
        vh, bdh, kdh = stacked(v_s), stacked(bd_s), stacked(kd_s)
        vo = [operand(x) for x in vh]
        bk = [jnp.concatenate([b, k], axis=0) for b, k in zip(bh, kh)]
        ga = [mm_nt(a, b) for a, b in zip(ah, bk)]
        gr = [mm_nt(r_, b) for r_, b in zip(rh, bk)]
        tinv = _tri_inverse([g_[:, :PAIR] * strict for g_ in ga], masks_ref, passes)
        for c in range(RWKV_GROUP):
            idx = [c * n_pairs + j for j in pairs]
            st = [state_s[j] for j in pairs]
            so = [operand(x) for x in st]
            x0 = [mm_nt(ah[i], so[j]) + mm(ga[i][:, PAIR:] * strict, vo[i])
                  for i, j in zip(idx, pairs)]
            u = [mm(tinv[i], x0[j]) for i, j in zip(idx, pairs)]
            y = [mm_nt(rh[i], so[j]) + mm(gr[i][:, :PAIR] * incl, u[j])
                 + mm(gr[i][:, PAIR:] * incl, vo[i]) for i, j in zip(idx, pairs)]
            for i, j in zip(idx, pairs):
                zt = jnp.concatenate([u[j], vh[i]], axis=0).T
                bkd = jnp.concatenate([bdh[i], kdh[i]], axis=0)
                gc = gc_s[pl.ds(row0[c], 1), lanes[j]]
                state_s[j] = st[j] * gc + mm(zt, bkd)
                y_s[pl.ds(row0[c], RWKV_CHUNK), lanes[j]] = y[j][:RWKV_CHUNK] + y[j][RWKV_CHUNK:]
        return carry

    lax.fori_loop(0, ts // (RWKV_CHUNK * RWKV_GROUP), chunk_group, 0)

    y = y_s[...]
    inv_n = 1.0 / HEAD_DIM
    mean = _group_sum(y, e_mat, split=False) * inv_n
    dlt = y - mean
    var = _group_sum(dlt * dlt, e_mat, split=False) * inv_n
    yn = dlt * lax.rsqrt(var + GN_EPS) * vec(_V_LNG) + vec(_V_LNB)
    o_ref[0] = ((yn + bon_s[...]) * g_s[...]).astype(o_ref.dtype)


def _rwkv_time_mix(p, mu, vecs, wd, wa, wg, *, ts, passes=RWKV_PASSES):
    bsz, seq, cols = p.shape
    width = vecs.shape[1]
    n_pairs = width // PAIR
    heads = np.arange(MXU_TILE) // HEAD_DIM
    e_mat = jnp.asarray((heads[:, None] == heads[None, :]).astype(np.float32), dtype=BF16)
    tok = np.arange(ts)
    same = (tok[:, None] // RWKV_CHUNK) == (tok[None, :] // RWKV_CHUNK)
    tri = jnp.asarray((same & (tok[None, :] <= tok[:, None])).astype(np.float32), dtype=BF16)
    masks = _rwkv_masks()
    full = lambda shape: pl.BlockSpec(shape, lambda b, s: (0,) * len(shape))
    big = lambda: pltpu.VMEM((ts, width), F32)
    op = lambda: pltpu.VMEM((ts, width), BF16 if passes == 1 else F32)
    kern = functools.partial(_rwkv_kernel, ts=ts, n_pairs=n_pairs, passes=passes)
    return pl.pallas_call(
        kern,
        out_shape=jax.ShapeDtypeStruct((bsz, seq, width), BF16),
        grid=(bsz, seq // ts),
        in_specs=[pl.BlockSpec((1, ts, cols), lambda b, s: (b, s, 0)),
                  full((1, cols)), full(vecs.shape), full(wd.shape), full(wa.shape),
                  full(wg.shape), full(e_mat.shape), full(tri.shape), full(masks.shape)],
        out_specs=pl.BlockSpec((1, ts, width), lambda b, s: (b, s, 0)),
        scratch_shapes=[op(), op(), op(), op(), big(), op(), op(), big(), big(), big(), big(),
                        pltpu.VMEM((n_pairs, PAIR, PAIR), F32), pltpu.VMEM((1, cols), F32)],
        compiler_params=_params(("parallel", "arbitrary")),
        name="rwkv7_scan",
    )(p, mu, vecs, wd, wa, wg, e_mat, tri, masks)


SB_WINDOW = 6
SB_TAIL = 2
SB_Q_PER_STEP = 8
SB_PAIRS_TOGETHER = 2


def _sb_kernel(q_ref, k_ref, v_ref, gain_ref, cum_ref, e2_ref, o_ref, *, q_per_step):
    lane = lax.broadcasted_iota(jnp.int32, (SB_BLOCK, PAIR), 1)
    h0 = lane < HEAD_DIM
    zero = jnp.zeros((), BF16)
    qi = lax.broadcasted_iota(jnp.int32, (SB_BLOCK, 2 * SB_BLOCK), 0)
    ki = lax.broadcasted_iota(jnp.int32, (SB_BLOCK, 2 * SB_BLOCK), 1) % SB_BLOCK
    causal = ki < qi
    cum_mat = cum_ref[...]
    scale = jnp.asarray(HEAD_DIM ** -0.5, BF16)
    width2 = 2 * SB_BLOCK

    heads = [slice(h * SB_BLOCK, (h + 1) * SB_BLOCK) for h in range(2)]

    def load_blocks(kb_top, n):
        khats, vhats, valid = [], [], []
        for j in range(n):
            kb = kb_top - j
            valid.append(kb >= 0)
            rows = pl.ds(pl.multiple_of(jnp.maximum(kb, 0) * SB_BLOCK, SB_BLOCK), SB_BLOCK)
            kblk = k_ref[0, rows, :]
            vblk = v_ref[0, rows, :]
            khats.append(jnp.concatenate([jnp.where(h0, kblk, zero), jnp.where(h0, zero, kblk)],
                                         axis=0))
            vhats.append(jnp.concatenate([jnp.where(h0, vblk, zero), jnp.where(h0, zero, vblk)],
                                         axis=0))
        return khats, vhats, valid

    def score_stage(zs, keeps):
        log_beta = [jnp.minimum(z, 0.0) - jnp.log(1.0 + jnp.exp(-jnp.abs(z))) for z in zs]
        log_1m = [lb - z for lb, z in zip(log_beta, zs)]
        log_1m = [l if m is None else jnp.where(m, l, 0.0) for l, m in zip(log_1m, keeps)]
        his = [l.astype(BF16) for l in log_1m]
        parts = [[_dot(hi[:, sl], cum_mat) for sl in heads] for hi in his]
        return log_beta, parts

    def weight_stage(log_beta, parts, keep, rest):
        between = jnp.concatenate([parts[0][:, :SB_BLOCK], parts[1][:, :SB_BLOCK]], axis=1)
        total = jnp.concatenate([parts[0][:, SB_BLOCK:], parts[1][:, SB_BLOCK:]], axis=1)
        w = jnp.exp(log_beta + between + rest)
        if keep is not None:
            w = jnp.where(keep, w, 0.0)
        return w.astype(BF16), rest + total

    def valid_mask(flag):
        return jnp.broadcast_to(flag, causal.shape)

    def sweep(q, kb_top, n, acc, rest):
        khats, vhats, valid = load_blocks(kb_top, n)
        z_all = _dot_nt(q, jnp.concatenate(khats, axis=0))
        keeps = [None] + [valid_mask(valid[j]) for j in range(1, n)]
        zs = [z_all[:, j * width2:(j + 1) * width2] for j in range(n)]
        log_beta, parts = score_stage(zs, keeps)
        ws = []
        for j in range(n):
            w, rest = weight_stage(log_beta[j], parts[j], keeps[j], rest)
            ws.append(w)
        acc = acc + _dot(jnp.concatenate(ws, axis=1), jnp.concatenate(vhats, axis=0))
        return acc, rest

    def window_pairs(q2s, qb_as, all_valid):
        n = SB_WINDOW
        off_diag = (lambda flag: None) if all_valid else valid_mask
        col = lambda jj: slice(jj * width2, (jj + 1) * width2)
        loaded = [load_blocks(qb_a + 1, n + 1) for qb_a in qb_as]
        z_alls = [_dot_nt(q2, jnp.concatenate(ld[0], axis=0))
                  for q2, ld in zip(q2s, loaded)]
        items = []
        for j in range(n):
            for p, (z_all, (_, _, valid)) in enumerate(zip(z_alls, loaded)):
                za = z_all[:SB_BLOCK, col(j + 1)]
                zb = z_all[SB_BLOCK:, col(j)]
                items.append((2 * p, za, causal if j == 0 else off_diag(valid[j + 1])))
                items.append((2 * p + 1, zb, causal if j == 0 else off_diag(valid[j])))
        log_beta, parts = score_stage([it[1] for it in items], [it[2] for it in items])
        n_streams = 2 * len(q2s)
        rest = [jnp.zeros((SB_BLOCK, width2), F32) for _ in range(n_streams)]
        ws = [[] for _ in range(n_streams)]
        for idx, (s, _, keep) in enumerate(items):
            w, rest[s] = weight_stage(log_beta[idx], parts[idx], keep, rest[s])
            ws[s].append(w)
        out = []
        for p, (_, vhats, _) in enumerate(loaded):
            acc_a = _dot(jnp.concatenate(ws[2 * p], axis=1), jnp.concatenate(vhats[1:], axis=0))
            acc_b = _dot(jnp.concatenate(ws[2 * p + 1], axis=1),
                         jnp.concatenate(vhats[:n], axis=0))
            out += [(acc_a, rest[2 * p]), (acc_b, rest[2 * p + 1])]
        return out

    def finish(q, qb, acc, rest, qrows):
        def cond(c):
            i, _, _, live = c
            return jnp.logical_and(i <= qb, live > SB_UNDERFLOW_LOG)

        def body(c):
            i, acc, rest, _ = c
            acc, rest = sweep(q, qb - i, SB_TAIL, acc, rest)
            return i + SB_TAIL, acc, rest, jnp.max(rest)

        _, acc, _, _ = lax.while_loop(cond, body,
                                      (jnp.int32(SB_WINDOW), acc, rest, jnp.max(rest)))
        ms = _mm_exact_rhs(acc * acc, e2_ref[...]) * (1.0 / HEAD_DIM)
        o_ref[0, qrows, :] = (acc * lax.rsqrt(ms + RMS_EPS) * gain_ref[...]).astype(o_ref.dtype)

    n_together = min(SB_PAIRS_TOGETHER, q_per_step // 2)

    def q_group(it, all_valid):
        blocks_per_group = 2 * n_together
        qb0 = pl.program_id(2) * q_per_step + blocks_per_group * it
        row0 = pl.multiple_of(blocks_per_group * it * SB_BLOCK, blocks_per_group * SB_BLOCK)
        q2s = [q_ref[0, pl.ds(row0 + 2 * p * SB_BLOCK, 2 * SB_BLOCK), :] * scale
               for p in range(n_together)]
        results = window_pairs(q2s, [qb0 + 2 * p for p in range(n_together)], all_valid)
        for b, (acc, rest) in enumerate(results):
            q = q2s[b // 2][(b % 2) * SB_BLOCK:(b % 2 + 1) * SB_BLOCK]
            finish(q, qb0 + b, acc, rest, pl.ds(row0 + b * SB_BLOCK, SB_BLOCK))

    def run(all_valid):
        def body(it, carry):
            q_group(it, all_valid)
            return carry
        lax.fori_loop(0, q_per_step // (2 * n_together), body, 0)

    first_full = -(-(SB_WINDOW - 1) // q_per_step)
    pl.when(pl.program_id(2) < first_full)(lambda: run(False))
    pl.when(pl.program_id(2) >= first_full)(lambda: run(True))


def _sb_attention(qkv, gain, width):
    bsz, seq, _ = qkv.shape
    n_pairs = width // PAIR
    nq = seq // SB_BLOCK
    i = np.arange(SB_BLOCK)
    upper = (i[:, None] > i[None, :]).astype(np.float32)
    half = np.concatenate([upper, np.ones_like(upper)], axis=1)
    cum_mat = jnp.asarray(half, dtype=BF16)
    hd = np.arange(PAIR) // HEAD_DIM
    e2 = jnp.asarray((hd[:, None] == hd[None, :]).astype(np.float32), dtype=BF16)
    qps = min(SB_Q_PER_STEP, nq)
    assert qps % 2 == 0 and nq % qps == 0, "query blocks are processed in adjacent pairs"
    qrows = qps * SB_BLOCK
    return pl.pallas_call(
        functools.partial(_sb_kernel, q_per_step=qps),
        out_shape=jax.ShapeDtypeStruct((bsz, seq, width), BF16),
        grid=(bsz, n_pairs, nq // qps),
        in_specs=[pl.BlockSpec((1, qrows, PAIR), lambda b, j, t: (b, t, j)),
                  pl.BlockSpec((1, seq, PAIR), lambda b, j, t: (b, 0, n_pairs + j)),
                  pl.BlockSpec((1, seq, PAIR), lambda b, j, t: (b, 0, 2 * n_pairs + j)),
                  pl.BlockSpec((1, PAIR), lambda b, j, t: (0, j)),
                  pl.BlockSpec(cum_mat.shape, lambda b, j, t: (0, 0)),
                  pl.BlockSpec(e2.shape, lambda b, j, t: (0, 0))],
        out_specs=pl.BlockSpec((1, qrows, PAIR), lambda b, j, t: (b, t, j)),
        compiler_params=_params(("parallel", "parallel", "arbitrary")),
        name="stickbreak_attn",
    )(qkv, qkv, qkv, gain.reshape(1, width), cum_mat, e2)


def _route_rows(lg):
    lane = lax.broadcasted_iota(jnp.int32, lg.shape, 1).astype(F32)
    neg = -jnp.inf
    first = lambda hit: jnp.min(jnp.where(hit, lane, float(LANE)), axis=-1, keepdims=True)
    gl = jnp.where(lane < N_GROUPS, lg, neg)
    gmax = jnp.max(gl, axis=-1, keepdims=True)
    g_sel = first(gl == gmax)
    p_sel = 1.0 / jnp.sum(jnp.exp(gl - gmax), axis=-1, keepdims=True)
    lo = N_GROUPS + g_sel * EXPERTS_PER_GROUP
    el = jnp.where(jnp.logical_and(lane >= lo, lane < lo + EXPERTS_PER_GROUP), lg, neg)
    v1 = jnp.max(el, axis=-1, keepdims=True)
    i1 = first(el == v1)
    el2 = jnp.where(lane == i1, neg, el)
    v2 = jnp.max(el2, axis=-1, keepdims=True)
    i2 = first(el2 == v2)
    t2 = jnp.exp(v2 - v1)
    w1 = p_sel / (1.0 + t2)
    return i1 - N_GROUPS, i2 - N_GROUPS, w1, w1 * t2


def _outproj_kernel(x_ref, ya_ref, yb_ref, wa_ref, wb_ref, g1_ref, gain_ref, sc_ref, sh_ref,
                    wr_ref, br_ref, x1_ref, h_ref, rt_ref):
    mix = _dot(ya_ref[...], wa_ref[...]) + _dot(yb_ref[...], wb_ref[...])
    x1 = x_ref[...] + g1_ref[0] * mix
    x1_ref[...] = x1
    ms = jnp.mean(x1 * x1, axis=-1, keepdims=True)
    h = x1 * lax.rsqrt(ms + RMS_EPS) * gain_ref[...] * (1.0 + sc_ref[0]) + sh_ref[0]
    h_ref[...] = h.astype(h_ref.dtype)
    h_hi, h_lo = _split2(h)
    w_hl = wr_ref[...]
    nr = w_hl.shape[1] // 2
    both = _dot(h_hi, w_hl)
    logits = both[:, :nr] + both[:, nr:] + _dot(h_lo, w_hl[:, :nr]) + br_ref[...]
    e1, e2, w1, w2 = _route_rows(logits)
    lane = lax.broadcasted_iota(jnp.int32, rt_ref.shape, 1)
    rt_ref[...] = jnp.where(lane == 0, e1, jnp.where(lane == 1, e2,
                                                      jnp.where(lane == 2, w1, w2)))


def _outproj_norm_router(x2, ya, yb, w_a, w_b, g1, gain, sc, sh, w_r, b_r, seq, tm):
    t, d = x2.shape
    half = ya.shape[1]
    nr = b_r.shape[1]
    w_hi, w_lo = _split2(w_r)
    w_r = jnp.concatenate([w_hi, w_lo], axis=1)
    per_b = seq // tm
    row = lambda w: pl.BlockSpec((tm, w), lambda i: (i, 0))
    full = lambda shape: pl.BlockSpec(shape, lambda i: (0,) * len(shape))
    per_batch = pl.BlockSpec((1, 1, d), lambda i: (i // per_b, 0, 0))
    return pl.pallas_call(
        _outproj_kernel,
        out_shape=(jax.ShapeDtypeStruct((t, d), F32), jax.ShapeDtypeStruct((t, d), F32),
                   jax.ShapeDtypeStruct((t, nr), F32)),
        grid=(t // tm,),
        in_specs=[row(d), row(half), row(half), full((half, d)), full((half, d)), per_batch,
                  full((1, d)), per_batch, per_batch, full((d, 2 * nr)), full((1, nr))],
        out_specs=(row(d), row(d), row(nr)),
        compiler_params=_params(("parallel",)),
        name="outproj_norm_router",
    )(x2, ya, yb, w_a, w_b, g1, gain.reshape(1, d), sc, sh, w_r, b_r)


MOE_ROWS = 256
MOE_VMEM_LIMIT = 58 * 1024 * 1024
_DMA_UNROLL = 8
_CAST_STEPS = 8
_GATHER_SLOTS = 3


def _moe_kernel(be_ref, nxt_ref, nused_ref, gcur_ref, gnext_ref, gahead_ref, scur_ref, sprev_ref,
                wt_ref, h_hbm, wg_hbm, wu_hbm, wd_hbm, y_hbm,
                xbuf, ybuf, wg_f, wu_f, wd_f, wg_b, wu_b, wd_b, gsem, ssem, wsem):
    i = pl.program_id(0)
    n_used = nused_ref[0]
    slot = i % 2
    gslot = i % _GATHER_SLOTS
    ahead = (i + 2) % _GATHER_SLOTS
    de = wg_b.shape[1]
    d = wd_b.shape[1]

    def weight_copies(e):
        pairs = ((wg_hbm, wg_f), (wu_hbm, wu_f), (wd_hbm, wd_f))
        return [pltpu.make_async_copy(src.at[e], dst, wsem.at[k])
                for k, (src, dst) in enumerate(pairs)]

    def gather(idx_ref, r, s):
        return pltpu.make_async_copy(h_hbm.at[pl.ds(idx_ref[0, 0, r], 1), :],
                                     xbuf.at[s, pl.ds(r, 1), :], gsem.at[s])

    def scatter(idx_ref, r, s):
        return pltpu.make_async_copy(ybuf.at[s, pl.ds(r, 1), :],
                                     y_hbm.at[pl.ds(idx_ref[0, 0, r], 1), :], ssem.at[s])

    def for_rows(fn):
        def body(r, c):
            fn(r)
            return c
        lax.fori_loop(0, MOE_ROWS, body, 0, unroll=_DMA_UNROLL)

    def gather_wait(s):
        pltpu.make_async_copy(h_hbm.at[pl.ds(0, MOE_ROWS), :], xbuf.at[s], gsem.at[s]).wait()

    def scatter_wait(s):
        pltpu.make_async_copy(ybuf.at[s], y_hbm.at[pl.ds(0, MOE_ROWS), :], ssem.at[s]).wait()

    @pl.when(jnp.logical_and(i == 0, n_used > 0))
    def _():
        for_rows(lambda r: gather(gcur_ref, r, 0).start())
        for_rows(lambda r: gather(gnext_ref, r, 1).start())
        for c in weight_copies(be_ref[0]):
            c.start()
        ybuf[1] = jnp.zeros(ybuf.shape[1:], ybuf.dtype)

    @pl.when(i < n_used)
    def _():
        e = be_ref[i]

        @pl.when(jnp.logical_or(i == 0, e != be_ref[jnp.maximum(i - 1, 0)]))
        def _():
            for c in weight_copies(e):
                c.wait()

            def cast_rows(k, c):
                for src, dst in ((wg_f, wg_b), (wu_f, wu_b), (wd_f, wd_b)):
                    nrow = src.shape[0] // _CAST_STEPS
                    rows = pl.ds(pl.multiple_of(k * nrow, nrow), nrow)
                    dst[rows, :] = src[rows, :].astype(BF16)
                return c
            lax.fori_loop(0, _CAST_STEPS, cast_rows, 0)

            @pl.when(nxt_ref[i] != e)
            def _():
                for c in weight_copies(nxt_ref[i]):
                    c.start(priority=1)

        gather_wait(gslot)

        def sliced_issue(n_slices, make_copy, priorities):
            per = MOE_ROWS // n_slices
            state = [0]

            def issue():
                for r in range(state[0], state[0] + per):
                    make_copy(r).start(priority=r % priorities)
                state[0] += per
            return issue

        n_up = de // MXU_TILE
        n_down = d // MXU_TILE
        issue_gather = sliced_issue(2 * n_up, lambda r: gather(gahead_ref, r, ahead), 1)
        issue_scatter = sliced_issue(n_down, lambda r: scatter(sprev_ref, r, 1 - slot), 2)

        xb = xbuf[gslot].astype(BF16)
        hg, hu = [], []
        for n in range(n_up):
            cols = slice(n * MXU_TILE, (n + 1) * MXU_TILE)
            hg.append(_dot(xb, wg_b[:, cols]))
            issue_gather()
            hu.append(_dot(xb, wu_b[:, cols]))
            issue_gather()
        hg = jnp.concatenate(hg, axis=1)
        hid = (hg * _sigmoid(hg) * jnp.concatenate(hu, axis=1)).astype(BF16)

        @pl.when(i >= 1)
        def _():
            scatter_wait(slot)

        wt = wt_ref[...]
        for n in range(n_down):
            cols = slice(n * MXU_TILE, (n + 1) * MXU_TILE)
            ybuf[slot, :, cols] = _dot(hid, wd_b[:, cols]) * wt
            issue_scatter()

        @pl.when(i + 1 >= n_used)
        def _():
            gather_wait((i + 1) % _GATHER_SLOTS)
            gather_wait(ahead)
            scatter_wait(1 - slot)
            for_rows(lambda r: scatter(scur_ref, r, slot).start())
            scatter_wait(slot)


def _moe_experts(h2, gidx, sidx, row_w, block_e, next_e, n_used, w_gate, w_up, w_down):
    t, d = h2.shape
    de = w_gate.shape[2]
    n_blocks = gidx.shape[0]
    assert MOE_ROWS % (2 * (de // MXU_TILE)) == 0 == MOE_ROWS % (d // MXU_TILE)
    assert d % _CAST_STEPS == 0 == de % _CAST_STEPS and sidx.shape[0] == n_blocks + 1
    idx_spec = lambda f: pl.BlockSpec((1, 1, MOE_ROWS), f, memory_space=pltpu.SMEM)
    hbm = pl.BlockSpec(memory_space=pl.ANY)
    grid_spec = pltpu.PrefetchScalarGridSpec(
        num_scalar_prefetch=3,
        grid=(n_blocks,),
        in_specs=[idx_spec(lambda i, *_: (i, 0, 0)),
                  idx_spec(lambda i, *_: (jnp.minimum(i + 1, n_blocks - 1), 0, 0)),
                  idx_spec(lambda i, *_: (jnp.minimum(i + 2, n_blocks - 1), 0, 0)),
                  idx_spec(lambda i, *_: (i, 0, 0)),
                  idx_spec(lambda i, *_: (jnp.where(i == 0, n_blocks, i - 1), 0, 0)),
                  pl.BlockSpec((MOE_ROWS, 1), lambda i, *_: (i, 0)),
                  hbm, hbm, hbm, hbm],
        out_specs=hbm,
        scratch_shapes=[pltpu.VMEM((_GATHER_SLOTS, MOE_ROWS, d), F32),
                        pltpu.VMEM((2, MOE_ROWS, d), F32),
                        pltpu.VMEM((d, de), F32), pltpu.VMEM((d, de), F32),
                        pltpu.VMEM((de, d), F32),
                        pltpu.VMEM((d, de), BF16), pltpu.VMEM((d, de), BF16),
                        pltpu.VMEM((de, d), BF16),
                        pltpu.SemaphoreType.DMA((_GATHER_SLOTS,)), pltpu.SemaphoreType.DMA((2,)),
                        pltpu.SemaphoreType.DMA((3,))],
    )
    return pl.pallas_call(
        _moe_kernel,
        out_shape=jax.ShapeDtypeStruct((TOP_K_IN_GROUP * t + MOE_ROWS, d), F32),
        grid_spec=grid_spec,
        compiler_params=_params(("arbitrary",), MOE_VMEM_LIMIT),
        name="moe_experts",
    )(block_e, next_e, n_used, gidx, gidx, gidx, sidx, sidx, row_w.reshape(-1, 1),
      h2, w_gate, w_up, w_down)


def _final_kernel(x1_ref, ya_ref, yb_ref, g2_ref, gain_ref, o_ref):
    x2 = x1_ref[...] + g2_ref[0] * (ya_ref[...] + yb_ref[...])
    ms = jnp.mean(x2 * x2, axis=-1, keepdims=True)
    o_ref[...] = x2 * lax.rsqrt(ms + RMS_EPS) * gain_ref[...]


def _final_norm(x1, y2, g2, gain, seq, tm):
    t, d = x1.shape
    per_b = seq // tm
    nt = t // tm
    row = pl.BlockSpec((tm, d), lambda i: (i, 0))
    return pl.pallas_call(
        _final_kernel,
        out_shape=jax.ShapeDtypeStruct((t, d), F32),
        grid=(nt,),
        in_specs=[row, row, pl.BlockSpec((tm, d), lambda i: (i + nt, 0)),
                  pl.BlockSpec((1, 1, d), lambda i: (i // per_b, 0, 0)),
                  pl.BlockSpec((1, d), lambda i: (0, 0))],
        out_specs=row,
        compiler_params=_params(("parallel",)),
        name="final_norm",
    )(x1, y2, y2, g2, gain.reshape(1, d))


def _route(routed):
    t = routed.shape[0]
    pair_w = routed[:, TOP_K_IN_GROUP:2 * TOP_K_IN_GROUP]
    flat_e = routed[:, :TOP_K_IN_GROUP].astype(jnp.int32).reshape(-1)
    flat_w = pair_w.reshape(-1)
    m = flat_e.shape[0]
    order = jnp.argsort(flat_e).astype(jnp.int32)
    experts = jnp.arange(N_EXPERTS, dtype=jnp.int32)
    counts = jnp.sum((flat_e[:, None] == experts[None, :]).astype(jnp.int32), axis=0)
    starts = jnp.cumsum(counts) - counts
    padded = (counts + MOE_ROWS - 1) // MOE_ROWS * MOE_ROWS
    pends = jnp.cumsum(padded)
    pstarts = pends - padded
    n_blocks = (m + N_EXPERTS * (MOE_ROWS - 1) + MOE_ROWS - 1) // MOE_ROWS
    block_start = jnp.arange(n_blocks, dtype=jnp.int32) * MOE_ROWS
    block_e = jnp.minimum(jnp.sum((block_start[:, None] >= pends[None, :]).astype(jnp.int32),
                                  axis=1), N_EXPERTS - 1)
    blk = jnp.arange(n_blocks, dtype=jnp.int32)[:, None]
    rin = jnp.arange(MOE_ROWS, dtype=jnp.int32)[None, :]
    pstart_b, count_b, start_b = lax.optimization_barrier(
        (pstarts[block_e], counts[block_e], starts[block_e]))
    off = blk * MOE_ROWS + rin - pstart_b[:, None]
    valid = off < count_b[:, None]
    src = jnp.clip(start_b[:, None] + off, 0, m - 1)
    assign = order[src]
    tok = assign // TOP_K_IN_GROUP
    gidx = jnp.where(valid, tok, 0)
    spare = TOP_K_IN_GROUP * t + rin
    sidx = jnp.where(valid, (assign % TOP_K_IN_GROUP) * t + tok, spare)
    sidx = jnp.concatenate([sidx, spare], axis=0)
    row_w = jnp.where(valid, flat_w[assign], 0.0)
    n_used = (pends[-1] // MOE_ROWS).astype(jnp.int32).reshape(1)
    later = (experts[None, :] > experts[:, None]) & (counts[None, :] > 0)
    next_expert = jnp.min(jnp.where(later, experts[None, :], N_EXPERTS), axis=1)
    next_expert = jnp.where(next_expert == N_EXPERTS, experts, next_expert)
    return (gidx.reshape(n_blocks, 1, MOE_ROWS), sidx.reshape(n_blocks + 1, 1, MOE_ROWS), row_w,
            block_e, next_expert[block_e], n_used)


def _pad_cols(w, n):
    return jnp.pad(w, ((0, 0), (0, n - w.shape[1])))


def _pad_rows(w, n):
    return jnp.pad(w, ((0, n - w.shape[0]), (0, 0)))


def _layer(x, mod, norm1_gain, w_in, shift_mu, w0, w_decay_up, a0, w_iclr_up, w_gate_up,
           k_k, k_a, r_k, ln_x_gain, ln_x_bias, sb_norm_gain, w_out, norm2_gain,
           w_router_group, b_router_group, w_router_expert, b_router_expert,
           w_exp_gate, w_exp_up, w_exp_down, *, rwkv_ts, tm_in, tm_out):
    bsz, seq, d = x.shape
    t = bsz * seq
    rw = w0.shape[0]
    sbw = sb_norm_gain.shape[0]
    sh1, sc1, g1, sh2, sc2, g2 = [m.reshape(bsz, 1, d) for m in jnp.split(mod, 6, axis=-1)]

    o = 3 * rw
    seg = lambda a, b: w_in[:, a:b]
    w_rwkv = jnp.concatenate([
        seg(0, o),
        _pad_cols(seg(o, o + DECAY_LORA), LANE),
        _pad_cols(seg(o + DECAY_LORA, o + DECAY_LORA + ICLR_LORA), LANE),
        _pad_cols(seg(o + DECAY_LORA + ICLR_LORA, o + DECAY_LORA + ICLR_LORA + GATE_LORA),
                  2 * LANE)], axis=1).astype(BF16)
    rcols = o + DECAY_LORA + ICLR_LORA + GATE_LORA
    w_sb = w_in[:, rcols:].astype(BF16)
    mseg = lambda a, b: shift_mu[a:b][None, :]
    mu = jnp.concatenate([
        mseg(0, o),
        _pad_cols(mseg(o, o + DECAY_LORA), LANE),
        _pad_cols(mseg(o + DECAY_LORA, o + DECAY_LORA + ICLR_LORA), LANE),
        _pad_cols(mseg(o + DECAY_LORA + ICLR_LORA, rcols), 2 * LANE)], axis=1)

    x2 = x.reshape(t, d)
    p_rwkv = _normmod_matmul(x2, norm1_gain, sc1, sh1, w_rwkv, F32, seq, tm_in, 512)
    qkv = _normmod_matmul(x2, norm1_gain, sc1, sh1, w_sb, BF16, seq, tm_in, 512)

    vecs = jnp.stack([w0, a0, k_k, k_a, r_k.reshape(-1), ln_x_gain, ln_x_bias,
                      jnp.zeros_like(w0)])
    y_a = _rwkv_time_mix(p_rwkv.reshape(bsz, seq, -1), mu, vecs,
                         _pad_rows(w_decay_up, LANE), _pad_rows(w_iclr_up, LANE),
                         _pad_rows(w_gate_up, 2 * LANE), ts=rwkv_ts)
    y_b = _sb_attention(qkv.reshape(bsz, seq, -1), sb_norm_gain, sbw)

    w_r = _pad_cols(jnp.concatenate([w_router_group, w_router_expert], axis=1), LANE)
    b_r = _pad_cols(jnp.concatenate([b_router_group, b_router_expert])[None, :], LANE)
    w_o = w_out.astype(BF16)
    x1, h2, routed = _outproj_norm_router(
        x2, y_a.reshape(t, rw), y_b.reshape(t, sbw), w_o[:rw], w_o[rw:], g1, norm2_gain,
        sc2, sh2, w_r, b_r, seq, tm_out)

    gidx, sidx, row_w, block_e, next_e, n_used = _route(routed)
    y2 = _moe_experts(h2, gidx, sidx, row_w, block_e, next_e, n_used,
                      w_exp_gate, w_exp_up, w_exp_down)
    return x1, y2, g2


def kernel(x, c, w_ada, b_ada, norm1_gain, w_in, shift_mu, w0, w_decay_up, a0, w_iclr_up, w_gate_up, k_k, k_a, r_k, ln_x_gain, ln_x_bias, sb_norm_gain, w_out, norm2_gain, w_router_group, b_router_group, w_router_expert, b_router_expert, w_exp_gate, w_exp_up, w_exp_down, final_norm_gain):
    bsz, seq, d = x.shape
    assert w_ada.shape[0] == 1, "the final norm is fused into the single layer's last kernel"
    l = 0
    tiles = dict(rwkv_ts=min(256, seq), tm_in=min(1024, seq), tm_out=min(256, seq))
    mod = _ada_mod(c, w_ada[l], b_ada[l])
    x1, y2, g2 = _layer(
        x, mod, norm1_gain[l], w_in[l], shift_mu[l], w0[l], w_decay_up[l], a0[l],
        w_iclr_up[l], w_gate_up[l], k_k[l], k_a[l], r_k[l], ln_x_gain[l], ln_x_bias[l],
        sb_norm_gain[l], w_out[l], norm2_gain[l], w_router_group[l], b_router_group[l],
        w_router_expert[l], b_router_expert[l], w_exp_gate[l], w_exp_up[l],
        w_exp_down[l], **tiles)
    out = _final_norm(x1, y2, g2, final_norm_gain, seq, min(512, seq))
    return out.reshape(bsz, seq, d)
```
